```python
import math
import jax
import jax.numpy as jnp
from jax import lax
import numpy as np

D_MODEL = 1024
BATCH = 4
SEQ = 8192
DEPTH = 2
DEC_BATCH = 32
DEC_SEQ = 1
PAST_LEN = 16384
PAGE_SIZE = 128

N_EVEN = (DEPTH + 1) // 2
N_ODD = DEPTH // 2

GLA_HEADS = 4
GLA_DK = 64
GLA_DV = 128
GLA_RANK = 16
GLA_TAU = 16.0
GLA_CHUNK = 64
NSA_HEADS = 8
NSA_KV_HEADS = 2
NSA_GROUP = NSA_HEADS // NSA_KV_HEADS
NSA_DH = 64
NSA_BRANCHES = 3
NSA_BLOCK = 64
NSA_TOP_N = 16
NSA_WINDOW = 512
NSA_CMP_HID = 64
Q_BLOCK = 128
SEL_Q_BLOCK = 64
M_HEADS = 4
M_DQK = 128
M_DV = 256
M_CHUNK = 64
D_FF = 2816
CONV_W = 3

EPS = 1e-6
TINY = 1e-30

EVEN_SPLITS = (GLA_HEADS * GLA_DK, GLA_HEADS * GLA_DK, GLA_HEADS * GLA_DV, GLA_HEADS * GLA_DV, GLA_RANK,
               NSA_HEADS * NSA_DH, NSA_BRANCHES * 2 * NSA_KV_HEADS * NSA_DH, NSA_HEADS * NSA_BRANCHES)
ODD_SPLITS = (M_HEADS * M_DQK, M_HEADS * M_DQK, M_HEADS * M_DV, M_HEADS * M_DV, M_HEADS, M_HEADS)
MIX_EVEN = GLA_HEADS * GLA_DV + NSA_HEADS * NSA_DH
MIX_ODD = M_HEADS * M_DV

kernel_name = 'gla_nsa_mlstm_convffn_step'


def _rmsnorm(x, g):
    xf = x.astype(jnp.float32)
    y = xf * lax.rsqrt(jnp.mean(xf * xf, axis=-1, keepdims=True) + EPS)
    return (y * g.astype(jnp.float32)).astype(x.dtype)


def _head_rmsnorm(x, g):
    x = x.astype(jnp.float32)
    return x * lax.rsqrt(jnp.mean(x * x, axis=-1, keepdims=True) + EPS) * g.astype(jnp.float32)


def _split(z, sizes):
    cuts = [int(c) for c in np.cumsum(sizes)[:-1]]
    return jnp.split(z, cuts, axis=-1)


def _to_chunks(a, nc, c):
    b = a.shape[0]
    a = a.reshape((b, nc, c) + a.shape[2:])
    return a.transpose((1, 0, 3, 2) + tuple(range(4, a.ndim)))


def _from_chunks(a):
    nc, b, h, c, d = a.shape
    return a.transpose(1, 0, 3, 2, 4).reshape(b, nc * c, h, d)


def _sweep(fn, qb, *arrs):
    b, n = arrs[0].shape[:2]
    if n <= qb or n % qb:
        return fn(jnp.int32(0), *arrs)
    nb = n // qb
    blocks = tuple(a.reshape((b, nb, qb) + a.shape[2:]).swapaxes(0, 1) for a in arrs)
    out = lax.map(lambda xs: fn(xs[0] * qb, *xs[1:]), (jnp.arange(nb, dtype=jnp.int32),) + blocks)
    out = out.swapaxes(0, 1)
    return out.reshape((b, n) + out.shape[3:])


def _gla_chunked(q, k, v, log_a, s0):
    t = q.shape[1]
    c = GLA_CHUNK if t % GLA_CHUNK == 0 else t
    nc = t // c
    tri = jnp.tril(jnp.ones((c, c), dtype=bool))[:, :, None]

    def step(s, inp):
        qc, kc, vc, la = inp
        cum = jnp.cumsum(la, axis=2)
        o_inter = jnp.einsum('bhcd,bhde->bhce', qc * jnp.exp(cum), s)
        diff = cum[:, :, :, None, :] - cum[:, :, None, :, :]
        decay = jnp.exp(jnp.where(tri, diff, -jnp.inf))
        att = jnp.einsum('bhid,bhijd,bhjd->bhij', qc, decay, kc)
        o = o_inter + jnp.einsum('bhij,bhje->bhie', att, vc)
        last = cum[:, :, -1:, :]
        s_new = jnp.exp(last[:, :, 0, :])[..., None] * s + jnp.einsum('bhcd,bhce->bhde', kc * jnp.exp(last - cum), vc)
        return s_new, o

    s_fin, o = lax.scan(step, s0, (_to_chunks(q, nc, c), _to_chunks(k, nc, c), _to_chunks(v, nc, c), _to_chunks(log_a, nc, c)))
    return _from_chunks(o), s_fin


def _mlstm_chunked(q, k, v, ig, lf, c0, n0, m0):
    t = q.shape[1]
    L = M_CHUNK if t % M_CHUNK == 0 else t
    nc = t // L
    tri = jnp.tril(jnp.ones((L, L), dtype=bool))

    def step(carry, inp):
        cm, nv, m = carry
        qc, kc, vc, ic, fc = inp
        cum = jnp.cumsum(fc, axis=-1)
        dlog = jnp.where(tri, cum[..., :, None] - cum[..., None, :] + ic[..., None, :], -jnp.inf)
        inter = cum + m[..., None]
        mi = jnp.maximum(inter, jnp.max(dlog, axis=-1))
        w = jnp.exp(dlog - mi[..., None])
        wi = jnp.exp(inter - mi)
        s = jnp.einsum('bhid,bhjd->bhij', qc, kc) * w
        num = wi[..., None] * jnp.einsum('bhid,bhde->bhie', qc, cm) + jnp.einsum('bhij,bhje->bhie', s, vc)
        qn = wi * jnp.einsum('bhid,bhd->bhi', qc, nv) + jnp.sum(s, axis=-1)
        h = num / jnp.maximum(jnp.abs(qn), jnp.exp(-mi))[..., None]
        last = cum[..., -1]
        gl = last[..., None] - cum + ic
        m_new = jnp.maximum(last + m, jnp.max(gl, axis=-1))
        wj = jnp.exp(gl - m_new[..., None])
        keep = jnp.exp(last + m - m_new)
        c_new = keep[..., None, None] * cm + jnp.einsum('bhl,bhld,bhle->bhde', wj, kc, vc)
        n_new = keep[..., None] * nv + jnp.einsum('bhl,bhld->bhd', wj, kc)
        return (c_new, n_new, m_new), h

    (cf, nf, mf), h = lax.scan(step, (c0, n0, m0), (_to_chunks(q, nc, L), _to_chunks(k, nc, L), _to_chunks(v, nc, L),
                                                   _to_chunks(ig, nc, L), _to_chunks(lf, nc, L)))
    return _from_chunks(h), cf, nf, mf


def _nsa_compress(rows, pe, w1, w2):
    b, l = rows.shape[:2]
    n = l // NSA_BLOCK
    blk = rows[:, :n * NSA_BLOCK].reshape((b, n, NSA_BLOCK) + rows.shape[2:])
    blk = blk + pe.transpose(1, 0, 2)[:, :, None, :]
    hid = jax.nn.gelu(jnp.einsum('bnsckd,csde->bncke', blk, w1))
    return jnp.einsum('bncke,ced->bnckd', hid, w2)


def _nsa_cmp_attend(q, kv_cmp, tpos):
    n = kv_cmp.shape[1]
    kc = kv_cmp.astype(jnp.float32)
    s = jnp.einsum('btkgd,bnkd->btkgn', q, kc[:, :, 0])
    vis = (((jnp.arange(n) + 1) * NSA_BLOCK - 1)[None, :] <= tpos[:, None])[None, :, None, None, :]
    s = jnp.where(vis, s, -jnp.inf)
    m = jnp.max(s, axis=-1, keepdims=True)
    m = jnp.where(jnp.isfinite(m), m, 0.0)
    e = jnp.exp(s - m)
    p = e / jnp.maximum(jnp.sum(e, axis=-1, keepdims=True), TINY)
    return jnp.einsum('btkgn,bnkd->btkgd', p, kc[:, :, 1]), p


def _nsa_select(p, tpos, n_blocks):
    imp = jnp.sum(p, axis=3)
    imp = jnp.pad(imp, ((0, 0), (0, 0), (0, 0), (0, n_blocks - imp.shape[-1])))
    blk = jnp.arange(n_blocks)[None, :]
    cur = (tpos // NSA_BLOCK)[:, None]
    forced = (blk == 0) | (blk == cur) | (blk == cur - 1)
    allowed = blk <= cur
    score = jnp.where(forced[None, :, None, :], jnp.inf, jnp.where(allowed[None, :, None, :], imp, -jnp.inf))
    idx = lax.top_k(score, NSA_TOP_N)[1]
    valid = idx <= cur[None, :, :, None]
    return idx, valid


def _nsa_sel_block(qc, tq, idx, valid, gather):
    b, qn = qc.shape[:2]
    kvb = gather(idx).astype(jnp.float32)
    kpos = idx[..., None] * NSA_BLOCK + jnp.arange(NSA_BLOCK)
    ok = valid[..., None] & (kpos <= tq[None, :, None, None, None])
    s = jnp.einsum('bqkgd,bqknsd->bqkgns', qc, kvb[..., 0, :])
    s = jnp.where(ok[:, :, :, None], s, -jnp.inf).reshape(b, qn, NSA_KV_HEADS, NSA_GROUP, -1)
    p = jax.nn.softmax(s, axis=-1)
    vals = kvb[..., 1, :].reshape(b, qn, NSA_KV_HEADS, -1, NSA_DH)
    return jnp.einsum('bqkgm,bqkmd->bqkgd', p, vals)


def _nsa_win_block(qc, p0, kwp, kw_pos0):
    qn = qc.shape[1]
    span = NSA_WINDOW + qn
    seg = lax.dynamic_slice_in_dim(kwp, p0 - kw_pos0, span, axis=1).astype(jnp.float32)
    kpos = p0 - NSA_WINDOW + jnp.arange(span)
    tq = p0 + jnp.arange(qn)
    ok = (kpos[None, :] >= kw_pos0) & (kpos[None, :] <= tq[:, None]) & (tq[:, None] - kpos[None, :] < NSA_WINDOW)
    s = jnp.einsum('bqkgd,bmkd->bqkgm', qc, seg[:, :, 0])
    s = jnp.where(ok[None, :, None, None, :], s, -jnp.inf)
    p = jax.nn.softmax(s, axis=-1)
    return jnp.einsum('bqkgm,bmkd->bqkgd', p, seg[:, :, 1])


def _nsa(q, kv3, gates, cmp_pe, cmp_w1, cmp_w2, past):
    b, t = q.shape[:2]
    q = q.astype(jnp.float32).reshape(b, t, NSA_KV_HEADS, NSA_GROUP, NSA_DH) * NSA_DH ** -0.5
    kv_c, kv_s, kv_w = kv3[:, :, 0], kv3[:, :, 1], kv3[:, :, 2]
    bi = jnp.arange(b)[:, None, None, None, None, None]
    hi = jnp.arange(NSA_KV_HEADS)[None, None, :, None, None, None]
    kvi = jnp.arange(2)
    offs = jnp.arange(NSA_BLOCK)
    if past is None:
        q_pos0 = 0
        cmp_rows = kv_c
        n_blocks = max(-(-t // NSA_BLOCK), NSA_TOP_N)
        sel_rows = jnp.pad(kv_s, ((0, 0), (0, n_blocks * NSA_BLOCK - t), (0, 0), (0, 0), (0, 0)))

        def gather(idx):
            rows = idx[..., None] * NSA_BLOCK + offs
            return sel_rows[bi, rows[..., None], kvi, hi]

        win_rows, kw_pos0, n_keep = kv_w, 0, min(NSA_WINDOW, t)
    else:
        cmp_pool, sel_pool, win_buf, page_table = past
        q_pos0 = PAST_LEN
        past_c = cmp_pool[page_table].reshape((b, PAST_LEN) + cmp_pool.shape[2:])
        cmp_rows = jnp.concatenate([past_c, kv_c], axis=1)
        n_blocks = max(-(-(PAST_LEN + t) // NSA_BLOCK), NSA_TOP_N)
        n_past_blk = PAST_LEN // NSA_BLOCK
        n_new_blk = -(-t // NSA_BLOCK)
        new_rows = jnp.pad(kv_s, ((0, 0), (0, n_new_blk * NSA_BLOCK - t), (0, 0), (0, 0), (0, 0)))

        def gather(idx):
            start = jnp.minimum(idx, n_past_blk - 1) * NSA_BLOCK
            phys = page_table[bi[..., 0, 0], start // PAGE_SIZE]
            rows = (start % PAGE_SIZE)[..., None] + offs
            old = sel_pool[phys[..., None, None], rows[..., None], kvi, hi]
            nrows = jnp.clip(idx - n_past_blk, 0, n_new_blk - 1)[..., None] * NSA_BLOCK + offs
            new = new_rows[bi, nrows[..., None], kvi, hi]
            return jnp.where((idx < n_past_blk)[..., None, None, None], old, new)

        win_rows = jnp.concatenate([win_buf, kv_w], axis=1)
        n_keep = win_buf.shape[1]
        kw_pos0 = PAST_LEN - n_keep
    tpos = q_pos0 + jnp.arange(t)
    kv_cmp = _nsa_compress(cmp_rows, cmp_pe, cmp_w1, cmp_w2)
    o_cmp, p_cmp = _nsa_cmp_attend(q, kv_cmp, tpos)
    idx, valid = _nsa_select(p_cmp, tpos, n_blocks)

    def sel_fn(off, qc, ic, vc):
        tq = q_pos0 + off + jnp.arange(qc.shape[1])
        return _nsa_sel_block(qc, tq, ic, vc, gather)

    o_sel = _sweep(sel_fn, SEL_Q_BLOCK, q, idx, valid)
    kwp = jnp.pad(win_rows, ((0, 0), (NSA_WINDOW, 0), (0, 0), (0, 0), (0, 0)))

    def win_fn(off, qc):
        return _nsa_win_block(qc, q_pos0 + off, kwp, kw_pos0)

    o_win = _sweep(win_fn, Q_BLOCK, q)
    o = gates[..., 0:1] * o_cmp + gates[..., 1:2] * o_sel + gates[..., 2:3] * o_win
    return o.reshape(b, t, NSA_HEADS * NSA_DH), kv_c, kv_s, win_rows[:, win_rows.shape[1] - n_keep:]


def _even_mixer(xn, w_in, w_out, gla_w_a2, gla_b_a, gla_norm, cmp_pe, cmp_w1, cmp_w2, gate_b, gla_state, past):
    b, t, _ = xn.shape
    z = (xn @ w_in).astype(jnp.float32)
    gq, gk, gv, gg, ga, nq, nkv, ngate = _split(z, EVEN_SPLITS)
    gq = gq.reshape(b, t, GLA_HEADS, GLA_DK) * GLA_DK ** -0.5
    gk = gk.reshape(b, t, GLA_HEADS, GLA_DK)
    gv = gv.reshape(b, t, GLA_HEADS, GLA_DV)
    log_a = jax.nn.log_sigmoid(ga @ gla_w_a2.astype(jnp.float32) + gla_b_a).reshape(b, t, GLA_HEADS, GLA_DK) / GLA_TAU
    o_gla, s_new = _gla_chunked(gq, gk, gv, log_a, gla_state.astype(jnp.float32))
    o_gla = (_head_rmsnorm(o_gla, gla_norm) * jax.nn.silu(gg).reshape(b, t, GLA_HEADS, GLA_DV)).reshape(b, t, -1)
    kv3 = nkv.astype(xn.dtype).reshape(b, t, NSA_BRANCHES, 2, NSA_KV_HEADS, NSA_DH)
    gates = jax.nn.sigmoid(ngate + gate_b).reshape(b, t, NSA_KV_HEADS, NSA_GROUP, NSA_BRANCHES)
    o_nsa, cmp_new, sel_new, win_new = _nsa(nq.reshape(b, t, NSA_HEADS, NSA_DH), kv3, gates, cmp_pe, cmp_w1, cmp_w2, past)
    y = jnp.concatenate([o_gla, o_nsa], axis=-1).astype(xn.dtype) @ w_out
    return y, s_new, cmp_new, sel_new, win_new


def _odd_mixer(xn, w_in, w_out, b_i, b_f, m_norm, c0, n0, m0):
    b, t, _ = xn.shape
    z = (xn @ w_in).astype(jnp.float32)
    q, k, v, o, ig, fg = _split(z, ODD_SPLITS)
    q = q.reshape(b, t, M_HEADS, M_DQK)
    k = k.reshape(b, t, M_HEADS, M_DQK) * M_DQK ** -0.5
    v = v.reshape(b, t, M_HEADS, M_DV)
    h, cf, nf, mf = _mlstm_chunked(q, k, v, ig + b_i, jax.nn.log_sigmoid(fg + b_f),
                                   c0.astype(jnp.float32), n0.astype(jnp.float32), m0.astype(jnp.float32))
    h = _head_rmsnorm(h, m_norm.reshape(M_HEADS, M_DV)) * jax.nn.sigmoid(o).reshape(b, t, M_HEADS, M_DV)
    return h.reshape(b, t, -1).astype(xn.dtype) @ w_out, cf, nf, mf


def _conv_ffn(xn, w_up, conv_w, conv_b, w_down, conv_state):
    t = xn.shape[1]
    gpre, up = jnp.split(xn @ w_up, 2, axis=-1)
    full = jnp.concatenate([conv_state.astype(gpre.dtype), gpre], axis=1)
    acc = conv_b + full[:, 0:t] * conv_w[0]
    for j in range(1, CONV_W):
        acc = acc + full[:, j:j + t] * conv_w[j]
    h = jax.nn.gelu(acc) * up
    return h @ w_down, full[:, full.shape[1] - (CONV_W - 1):]


def setup_inputs(seed: int = 0) -> dict:
    key = jax.random.key(seed)
    ks = iter(jax.random.split(key, 48))

    def nrm(shape, scale=1.0):
        return jax.random.normal(next(ks), shape, jnp.float32) * scale

    def gain(shape):
        return 1.0 + nrm(shape, 0.02)

    n_pages = PAST_LEN // PAGE_SIZE
    n_phys = (5 * DEC_BATCH * n_pages + 3) // 4
    w_buf = min(NSA_WINDOW, PAST_LEN)
    kv_pool = (N_EVEN, n_phys, PAGE_SIZE, 2, NSA_KV_HEADS, NSA_DH)
    x_prompt = nrm((BATCH, SEQ, D_MODEL))
    x_sample = nrm((DEC_BATCH, DEC_SEQ, D_MODEL))
    cache_cmp_kv = nrm(kv_pool)
    cache_sel_kv = nrm(kv_pool)
    cache_win_kv = nrm((N_EVEN, DEC_BATCH, w_buf, 2, NSA_KV_HEADS, NSA_DH))
    state_gla = nrm((N_EVEN, DEC_BATCH, GLA_HEADS, GLA_DK, GLA_DV), 0.3)
    state_mlstm_c = nrm((N_ODD, DEC_BATCH, M_HEADS, M_DQK, M_DV), 0.3)
    state_mlstm_n = nrm((N_ODD, DEC_BATCH, M_HEADS, M_DQK), 0.3)
    state_mlstm_m = nrm((N_ODD, DEC_BATCH, M_HEADS), 1.0)
    state_ffn_conv = nrm((DEPTH, DEC_BATCH, CONV_W - 1, D_FF))
    page_table = jax.random.permutation(next(ks), n_phys)[:DEC_BATCH * n_pages].reshape(DEC_BATCH, n_pages).astype(jnp.int32)
    p_even = sum(EVEN_SPLITS)
    p_odd = sum(ODD_SPLITS)
    return {
        'x_prompt': x_prompt,
        'x_sample': x_sample,
        'cache_cmp_kv': cache_cmp_kv,
        'cache_sel_kv': cache_sel_kv,
        'cache_win_kv': cache_win_kv,
        'state_gla': state_gla,
        'state_mlstm_c': state_mlstm_c,
        'state_mlstm_n': state_mlstm_n,
        'state_mlstm_m': state_mlstm_m,
        'state_ffn_conv': state_ffn_conv,
        'page_table': page_table,
        'norm_mix': gain((DEPTH, D_MODEL)),
        'norm_ffn': gain((DEPTH, D_MODEL)),
        'norm_final': gain((D_MODEL,)),
        'even_w_in': nrm((N_EVEN, D_MODEL, p_even), D_MODEL ** -0.5),
        'even_w_out': nrm((N_EVEN, MIX_EVEN, D_MODEL), MIX_EVEN ** -0.5),
        'gla_w_a2': nrm((N_EVEN, GLA_RANK, GLA_HEADS * GLA_DK), GLA_RANK ** -0.5),
        'gla_b_a': nrm((N_EVEN, GLA_HEADS * GLA_DK), 0.1),
        'gla_norm': gain((N_EVEN, GLA_DV)),
        'nsa_cmp_pe': nrm((N_EVEN, 2, NSA_BLOCK, NSA_DH), 0.1),
        'nsa_cmp_w1': nrm((N_EVEN, 2, NSA_BLOCK, NSA_DH, NSA_CMP_HID), (NSA_BLOCK * NSA_DH) ** -0.5),
        'nsa_cmp_w2': nrm((N_EVEN, 2, NSA_CMP_HID, NSA_DH), NSA_CMP_HID ** -0.5),
        'nsa_gate_b': nrm((N_EVEN, NSA_HEADS * NSA_BRANCHES), 0.1),
        'odd_w_in': nrm((N_ODD, D_MODEL, p_odd), D_MODEL ** -0.5),
        'odd_w_out': nrm((N_ODD, MIX_ODD, D_MODEL), MIX_ODD ** -0.5),
        'mlstm_b_i': nrm((N_ODD, M_HEADS), 0.1),
        'mlstm_b_f': jnp.linspace(3.0, 6.0, M_HEADS, dtype=jnp.float32)[None, :] + nrm((N_ODD, M_HEADS), 0.1),
        'mlstm_norm': gain((N_ODD, MIX_ODD)),
        'ffn_w_up': nrm((DEPTH, D_MODEL, 2 * D_FF), D_MODEL ** -0.5),
        'ffn_conv_w': nrm((DEPTH, CONV_W, D_FF), CONV_W ** -0.5),
        'ffn_conv_b': nrm((DEPTH, D_FF), 0.02),
        'ffn_w_down': nrm((DEPTH, D_FF, D_MODEL), D_FF ** -0.5),
    }


def reference(x_prompt, x_sample, cache_cmp_kv, cache_sel_kv, cache_win_kv, state_gla, state_mlstm_c, state_mlstm_n,
              state_mlstm_m, state_ffn_conv, page_table, norm_mix, norm_ffn, norm_final, even_w_in, even_w_out,
              gla_w_a2, gla_b_a, gla_norm, nsa_cmp_pe, nsa_cmp_w1, nsa_cmp_w2, nsa_gate_b, odd_w_in, odd_w_out,
              mlstm_b_i, mlstm_b_f, mlstm_norm, ffn_w_up, ffn_conv_w, ffn_conv_b, ffn_w_down):
    f32 = jnp.float32
    xp, xs = x_prompt, x_sample
    bp = xp.shape[0]
    cmp_p, cmp_s, sel_p, sel_s, win_p, win_s, gla_p, gla_s = [], [], [], [], [], [], [], []
    mc_p, mc_s, mn_p, mn_s, mm_p, mm_s, cv_p, cv_s = [], [], [], [], [], [], [], []
    for l in range(DEPTH):
        if l % 2 == 0:
            e = l // 2
            prm = (even_w_in[e], even_w_out[e], gla_w_a2[e], gla_b_a[e], gla_norm[e],
                   nsa_cmp_pe[e], nsa_cmp_w1[e], nsa_cmp_w2[e], nsa_gate_b[e])
            d, s_, c_, k_, w_ = _even_mixer(_rmsnorm(xp, norm_mix[l]), *prm,
                                            jnp.zeros((bp, GLA_HEADS, GLA_DK, GLA_DV), f32), None)
            xp = xp + d
            gla_p.append(s_); cmp_p.append(c_); sel_p.append(k_); win_p.append(w_)
            d, s_, c_, k_, w_ = _even_mixer(_rmsnorm(xs, norm_mix[l]), *prm, state_gla[e],
                                            (cache_cmp_kv[e], cache_sel_kv[e], cache_win_kv[e], page_table))
            xs = xs + d
            gla_s.append(s_); cmp_s.append(c_); sel_s.append(k_); win_s.append(w_)
        else:
            o = l // 2
            prm = (odd_w_in[o], odd_w_out[o], mlstm_b_i[o], mlstm_b_f[o], mlstm_norm[o])
            d, c_, n_, m_ = _odd_mixer(_rmsnorm(xp, norm_mix[l]), *prm,
                                       jnp.zeros((bp, M_HEADS, M_DQK, M_DV), f32),
                                       jnp.zeros((bp, M_HEADS, M_DQK), f32), jnp.zeros((bp, M_HEADS), f32))
            xp = xp + d
            mc_p.append(c_); mn_p.append(n_); mm_p.append(m_)
            d, c_, n_, m_ = _odd_mixer(_rmsnorm(xs, norm_mix[l]), *prm,
                                       state_mlstm_c[o], state_mlstm_n[o], state_mlstm_m[o])
            xs = xs + d
            mc_s.append(c_); mn_s.append(n_); mm_s.append(m_)
        fp = (ffn_w_up[l], ffn_conv_w[l], ffn_conv_b[l], ffn_w_down[l])
        d, cv = _conv_ffn(_rmsnorm(xp, norm_ffn[l]), *fp, jnp.zeros((bp, CONV_W - 1, D_FF), xp.dtype))
        xp = xp + d
        cv_p.append(cv)
        d, cv = _conv_ffn(_rmsnorm(xs, norm_ffn[l]), *fp, state_ffn_conv[l])
        xs = xs + d
        cv_s.append(cv)
    y_prompt = _rmsnorm(xp, norm_final)
    y_sample = _rmsnorm(xs, norm_final)
    return (y_prompt, y_sample,
            jnp.stack(cmp_p), jnp.stack(cmp_s), jnp.stack(sel_p), jnp.stack(sel_s),
            jnp.stack(win_p), jnp.stack(win_s), jnp.stack(gla_p), jnp.stack(gla_s),
            jnp.stack(mc_p), jnp.stack(mc_s), jnp.stack(mn_p), jnp.stack(mn_s),
            jnp.stack(mm_p), jnp.stack(mm_s), jnp.stack(cv_p), jnp.stack(cv_s))
```

```python
import functools

import jax
import jax.numpy as jnp
from jax import lax
from jax.experimental import pallas as pl
from jax.experimental.pallas import tpu as pltpu

F32 = jnp.float32
BF16 = jnp.bfloat16

GLA_HEADS, GLA_DK, GLA_DV, GLA_RANK, GLA_TAU, GLA_CHUNK = 4, 64, 128, 16, 16.0, 64
NSA_HEADS, NSA_KV_HEADS, NSA_GROUP, NSA_DH = 8, 2, 4, 64
NSA_BRANCHES, NSA_BLOCK, NSA_TOP_N, NSA_WINDOW, NSA_CMP_HID = 3, 64, 16, 512, 64
M_HEADS, M_DQK, M_DV, M_CHUNK = 4, 128, 256, 64
CONV_W = 3
PAGE_SIZE = 128
EPS, TINY = 1e-6, 1e-30

LANE = 128
SUBLANE = 8
VMEM_BIG = 52 * 1024 * 1024
VMEM_MID = 40 * 1024 * 1024

MASK_BIG = 32768.0

NSA_KV_W = 2 * NSA_KV_HEADS * NSA_DH
GLA_Z_W = 2 * GLA_HEADS * GLA_DK + 2 * GLA_HEADS * GLA_DV
M_Z_W = 2 * M_HEADS * M_DQK + 2 * M_HEADS * M_DV
M_AUG = M_DV + LANE
GATE_OFF = GLA_RANK

_NN = (((1,), (0,)), ((), ()))
_NT = (((1,), (1,)), ((), ()))
_TN = (((0,), (0,)), ((), ()))


def _mm(a, b, dims=_NN):
    return lax.dot_general(a, b, dims, preferred_element_type=F32)


def _mm_bf(a, b, dims=_NN):
    return _mm(a.astype(BF16), b.astype(BF16), dims)


def _split_bf16(x, n):
    parts, r = [], x
    for _ in range(n):
        p = r.astype(BF16)
        parts.append(p)
        r = r - p.astype(F32)
    return parts


def _mm_sel(sel, x, dims=_NN, x_is_rhs=True):
    out = None
    for p in _split_bf16(x, 3):
        t = _mm(sel, p, dims) if x_is_rhs else _mm(p, sel, dims)
        out = t if out is None else out + t
    return out


def _mm_hp(a, b, dims=_NN):
    a1, a2 = _split_bf16(a, 2)
    b1, b2 = _split_bf16(b, 2)
    return _mm(a1, b1, dims) + (_mm(a1, b2, dims) + _mm(a2, b1, dims))


def _gelu(x):
    return 0.5 * x * (1.0 + jnp.tanh(0.7978845608028654 * (x + 0.044715 * (x * x * x))))


def _sigmoid(x):
    return 1.0 / (1.0 + jnp.exp(-x))


def _logsigmoid(x):
    return jnp.minimum(x, 0.0) - jnp.log(1.0 + jnp.exp(-jnp.abs(x)))


def _rms(x, g):
    return x * lax.rsqrt(jnp.mean(x * x, axis=-1, keepdims=True) + EPS) * g


def _iota(shape, dim):
    return lax.broadcasted_iota(jnp.int32, shape, dim)


def _params(n_axes, vmem=VMEM_MID):
    return pltpu.CompilerParams(dimension_semantics=("arbitrary",) * n_axes, vmem_limit_bytes=vmem)


def _resident(shape):
    nd = len(shape)
    return pl.BlockSpec(shape, lambda *_: (0,) * nd, pipeline_mode=pl.Buffered(1))


def _norm_proj_body(x_ref, g_ref, w_ref, *o_refs, widths):
    xb = _rms(x_ref[...], g_ref[...]).astype(BF16)
    off = 0
    for o_ref, n in zip(o_refs, widths):
        o_ref[...] = _mm(xb, w_ref[:, off:off + n])
        off += n


def _norm_proj(x, g, w, widths, tm):
    m, d = x.shape
    tm = min(tm, m)
    assert m % tm == 0 and sum(widths) == w.shape[1]
    return pl.pallas_call(
        functools.partial(_norm_proj_body, widths=tuple(widths)),
        grid=(m // tm,),
        in_specs=[pl.BlockSpec((tm, d), lambda i: (i, 0)), _resident((1, d)), _resident(w.shape)],
        out_specs=[pl.BlockSpec((tm, n), lambda i: (i, 0)) for n in widths],
        out_shape=[jax.ShapeDtypeStruct((m, n), F32) for n in widths],
        compiler_params=_params(1),
        name="norm_proj",
    )(x, g.reshape(1, d), w)


def _out_proj_body(x_ref, *refs, groups):
    h_refs, w_ref, o_ref = refs[:-2], refs[-2], refs[-1]
    acc = x_ref[...]
    i = off = 0
    for gsz in groups:
        h = h_refs[i][...]
        for j in range(1, gsz):
            h = h + h_refs[i + j][...]
        i += gsz
        n = h.shape[1]
        acc = acc + _mm(h.astype(BF16), w_ref[off:off + n, :])
        off += n
    o_ref[...] = acc


def _out_proj(x, hs, groups, w, tm):
    m, d = x.shape
    tm = min(tm, m)
    assert m % tm == 0
    return pl.pallas_call(
        functools.partial(_out_proj_body, groups=tuple(groups)),
        grid=(m // tm,),
        in_specs=[pl.BlockSpec((tm, d), lambda i: (i, 0))]
        + [pl.BlockSpec((tm, h.shape[1]), lambda i: (i, 0)) for h in hs]
        + [_resident(w.shape)],
        out_specs=pl.BlockSpec((tm, d), lambda i: (i, 0)),
        out_shape=jax.ShapeDtypeStruct((m, d), F32),
        compiler_params=_params(1),
        name="out_proj",
    )(x, *hs, w)


def _ffn_chunks(xn, resid, wup_ref, cw_ref, cb_ref, wd_ref, fc, prev_fn, keep_fn):
    f = cw_ref.shape[1]
    acc = resid
    for c in range(f // fc):
        sl = slice(c * fc, (c + 1) * fc)
        gp = _mm(xn, wup_ref[:, sl])
        up = _mm(xn, wup_ref[:, f + c * fc:f + (c + 1) * fc])
        g2, g1 = prev_fn(gp, sl)
        a = cb_ref[:, sl] + g2 * cw_ref[0:1, sl] + g1 * cw_ref[1:2, sl] + gp * cw_ref[2:3, sl]
        acc = acc + _mm((_gelu(a) * up).astype(BF16), wd_ref[sl, :])
        keep_fn(gp, sl)
    return acc


def _ffn_prompt_body(x_ref, g_ref, wup_ref, cw_ref, cb_ref, wd_ref, st_ref, gf_ref, y_ref, ns_ref, carry, *, final, fc):
    t = pl.program_id(1)
    tm = x_ref.shape[0]

    @pl.when(t == 0)
    def _():
        carry[...] = st_ref[0]

    x = x_ref[...]
    xn = _rms(x, g_ref[...]).astype(BF16)
    row = _iota((tm, fc), 0)

    def prev_fn(gp, sl):
        c0, c1 = carry[0:1, sl], carry[1:2, sl]
        g1 = jnp.where(row == 0, c1, pltpu.roll(gp, 1, 0))
        g2 = jnp.where(row == 0, c0, jnp.where(row == 1, c1, pltpu.roll(gp, 2, 0)))
        return g2, g1

    def keep_fn(gp, sl):
        carry[:, sl] = gp[tm - 2:tm, :]

    acc = _ffn_chunks(xn, x, wup_ref, cw_ref, cb_ref, wd_ref, fc, prev_fn, keep_fn)
    y_ref[...] = _rms(acc, gf_ref[...]) if final else acc

    @pl.when(t == pl.num_programs(1) - 1)
    def _():
        ns_ref[0] = carry[...]


def _ffn_prompt(x, bsz, g, wup, cw, cb, wd, st, gf, final, tm, fc=256):
    m, d = x.shape
    t = m // bsz
    tm = min(tm, t)
    f = cw.shape[1]
    assert t % tm == 0 and f % fc == 0
    nt = t // tm
    return pl.pallas_call(
        functools.partial(_ffn_prompt_body, final=final, fc=fc),
        grid=(bsz, nt),
        in_specs=[pl.BlockSpec((tm, d), lambda b, i: (b * nt + i, 0)), _resident((1, d)), _resident(wup.shape),
                  _resident(cw.shape), _resident((1, f)), _resident(wd.shape),
                  pl.BlockSpec((1, CONV_W - 1, f), lambda b, i: (b, 0, 0)), _resident((1, d))],
        out_specs=[pl.BlockSpec((tm, d), lambda b, i: (b * nt + i, 0)),
                   pl.BlockSpec((1, CONV_W - 1, f), lambda b, i: (b, 0, 0))],
        out_shape=[jax.ShapeDtypeStruct((m, d), F32), jax.ShapeDtypeStruct((bsz, CONV_W - 1, f), F32)],
        scratch_shapes=[pltpu.VMEM((CONV_W - 1, f), F32)],
        compiler_params=_params(2, VMEM_BIG),
        name="ffn_prompt",
    )(x, g.reshape(1, d), wup, cw, cb.reshape(1, f), wd, st, gf.reshape(1, d))


def _ffn_decode_body(x_ref, g_ref, wup_ref, cw_ref, cb_ref, wd_ref, s0_ref, s1_ref, gf_ref, y_ref, gp_ref, *, final, fc):
    x = x_ref[...]
    xn = _rms(x, g_ref[...]).astype(BF16)

    def prev_fn(gp, sl):
        return s0_ref[:, sl], s1_ref[:, sl]

    def keep_fn(gp, sl):
        gp_ref[:, sl] = gp

    acc = _ffn_chunks(xn, x, wup_ref, cw_ref, cb_ref, wd_ref, fc, prev_fn, keep_fn)
    y_ref[...] = _rms(acc, gf_ref[...]) if final else acc


def _ffn_decode(x, g, wup, cw, cb, wd, st, gf, final, fc=256):
    m, d = x.shape
    f = cw.shape[1]
    y, gp = pl.pallas_call(
        functools.partial(_ffn_decode_body, final=final, fc=fc),
        grid=(1,),
        in_specs=[_resident((m, d)), _resident((1, d)), _resident(wup.shape), _resident(cw.shape), _resident((1, f)),
                  _resident(wd.shape), _resident((m, f)), _resident((m, f)), _resident((1, d))],
        out_specs=[pl.BlockSpec((m, d), lambda i: (0, 0)), pl.BlockSpec((m, f), lambda i: (0, 0))],
        out_shape=[jax.ShapeDtypeStruct((m, d), F32), jax.ShapeDtypeStruct((m, f), F32)],
        compiler_params=_params(1, VMEM_BIG),
        name="ffn_decode",
    )(x, g.reshape(1, d), wup, cw, cb.reshape(1, f), wd, st[:, 0], st[:, 1], gf.reshape(1, d))
    return y, jnp.stack([st[:, 1], gp], axis=1)


def _gla_finish(o, gg, gn):
    return _rms(o, gn) * (gg * _sigmoid(gg))


def _gla_body(z_ref, zs_ref, wa_ref, ba_ref, gn_ref, s0_ref, o_ref, sfin_ref, s_scr):
    t = pl.program_id(1)
    c = z_ref.shape[0]
    hk = GLA_HEADS * GLA_DK

    @pl.when(t == 0)
    def _():
        s_scr[...] = s0_ref[0]

    la = _logsigmoid(_mm_hp(zs_ref[...], wa_ref[...]) + ba_ref[...]) * (1.0 / GLA_TAU)
    r, cidx = _iota((c, c), 0), _iota((c, c), 1)
    tri = r >= cidx
    cum = _mm_sel(jnp.where(tri, 1.0, 0.0).astype(BF16), la)
    last = cum[c - 1:c, :]
    eq, ek, ekl, el = jnp.exp(cum), jnp.exp(-cum), jnp.exp(last - cum), jnp.exp(last)
    gn = gn_ref[...]
    for h in range(GLA_HEADS):
        ks = slice(h * GLA_DK, (h + 1) * GLA_DK)
        vs = slice(2 * hk + h * GLA_DV, 2 * hk + (h + 1) * GLA_DV)
        gs = slice(2 * hk + GLA_HEADS * GLA_DV + h * GLA_DV, 2 * hk + GLA_HEADS * GLA_DV + (h + 1) * GLA_DV)
        q = z_ref[:, ks] * (GLA_DK ** -0.5)
        k = z_ref[:, hk + h * GLA_DK:hk + (h + 1) * GLA_DK]
        v = z_ref[:, vs]
        qt = (q * eq[:, ks]).astype(BF16)
        s_old = s_scr[h]
        att = jnp.where(tri, _mm(qt, (k * ek[:, ks]).astype(BF16), _NT), 0.0)
        vb = v.astype(BF16)
        o = _mm(qt, s_old.astype(BF16)) + _mm(att.astype(BF16), vb)
        ecol = jnp.sum(jnp.where(r == cidx, jnp.broadcast_to(el[:, ks], (c, c)), 0.0), axis=1, keepdims=True)
        s_scr[h] = ecol * s_old + _mm((k * ekl[:, ks]).astype(BF16), vb, _TN)
        o_ref[:, h * GLA_DV:(h + 1) * GLA_DV] = _gla_finish(o, z_ref[:, gs], gn)

    @pl.when(t == pl.num_programs(1) - 1)
    def _():
        sfin_ref[0] = s_scr[...]


def _gla_prompt(z, zs, wa, ba, gn, s0, bsz):
    m = z.shape[0]
    t = m // bsz
    c = GLA_CHUNK
    assert t % c == 0 and GLA_DK == c
    nc = t // c
    hk = GLA_HEADS * GLA_DK
    return pl.pallas_call(
        _gla_body,
        grid=(bsz, nc),
        in_specs=[pl.BlockSpec((c, GLA_Z_W), lambda b, i: (b * nc + i, 0)),
                  pl.BlockSpec((c, LANE), lambda b, i: (b * nc + i, 0)),
                  _resident((LANE, hk)), _resident((1, hk)), _resident((1, GLA_DV)),
                  pl.BlockSpec((1, GLA_HEADS, GLA_DK, GLA_DV), lambda b, i: (b, 0, 0, 0))],
        out_specs=[pl.BlockSpec((c, GLA_HEADS * GLA_DV), lambda b, i: (b * nc + i, 0)),
                   pl.BlockSpec((1, GLA_HEADS, GLA_DK, GLA_DV), lambda b, i: (b, 0, 0, 0))],
        out_shape=[jax.ShapeDtypeStruct((m, GLA_HEADS * GLA_DV), F32),
                   jax.ShapeDtypeStruct((bsz, GLA_HEADS, GLA_DK, GLA_DV), F32)],
        scratch_shapes=[pltpu.VMEM((GLA_HEADS, GLA_DK, GLA_DV), F32)],
        compiler_params=_params(2),
        name="gla_prompt",
    )(z, zs, wa, ba, gn, s0)


def _gla_decode_body(z_ref, zs_ref, wa_ref, ba_ref, gn_ref, s_ref, o_ref, sn_ref):
    hk = GLA_HEADS * GLA_DK
    rows = SUBLANE
    z = jnp.broadcast_to(z_ref[0], (rows, GLA_Z_W))
    ga = jnp.broadcast_to(zs_ref[0], (rows, LANE))
    la = _logsigmoid(_mm_hp(ga, wa_ref[...]) + ba_ref[...]) * (1.0 / GLA_TAU)
    ea = jnp.exp(la)
    r, cidx = _iota((GLA_DK, GLA_DK), 0), _iota((GLA_DK, GLA_DK), 1)
    row0 = _iota((rows, GLA_DK), 0) == 0
    gn = gn_ref[...]
    for h in range(GLA_HEADS):
        ks = slice(h * GLA_DK, (h + 1) * GLA_DK)
        q = z[:, ks] * (GLA_DK ** -0.5)
        k = z[:, hk + h * GLA_DK:hk + (h + 1) * GLA_DK]
        v = z[:, 2 * hk + h * GLA_DV:2 * hk + (h + 1) * GLA_DV]
        gg = z[:, 2 * hk + GLA_HEADS * GLA_DV + h * GLA_DV:2 * hk + GLA_HEADS * GLA_DV + (h + 1) * GLA_DV]
        s_old = s_ref[0, h]
        o = _mm_hp(q * ea[:, ks], s_old) + jnp.sum(q * k, axis=1, keepdims=True) * v
        ecol = jnp.sum(jnp.where(r == cidx, jnp.broadcast_to(ea[0:1, ks], (GLA_DK, GLA_DK)), 0.0), axis=1, keepdims=True)
        sn_ref[0, h] = ecol * s_old + _mm_hp(jnp.where(row0, k, 0.0), v, _TN)
        o_ref[0, :, h * GLA_DV:(h + 1) * GLA_DV] = _gla_finish(o, gg, gn)[0:1]


def _gla_decode(z, zs, wa, ba, gn, s0):
    b = z.shape[0]
    hk = GLA_HEADS * GLA_DK
    st = (1, GLA_HEADS, GLA_DK, GLA_DV)
    o, sn = pl.pallas_call(
        _gla_decode_body,
        grid=(b,),
        in_specs=[pl.BlockSpec((1, 1, GLA_Z_W), lambda i: (i, 0, 0)), pl.BlockSpec((1, 1, LANE), lambda i: (i, 0, 0)),
                  _resident((LANE, hk)), _resident((1, hk)), _resident((1, GLA_DV)),
                  pl.BlockSpec(st, lambda i: (i, 0, 0, 0))],
        out_specs=[pl.BlockSpec((1, 1, GLA_HEADS * GLA_DV), lambda i: (i, 0, 0)), pl.BlockSpec(st, lambda i: (i, 0, 0, 0))],
        out_shape=[jax.ShapeDtypeStruct((b, 1, GLA_HEADS * GLA_DV), F32), jax.ShapeDtypeStruct((b,) + st[1:], F32)],
        compiler_params=_params(1),
        name="gla_decode",
    )(z.reshape(b, 1, -1), zs.reshape(b, 1, -1), wa, ba, gn, s0)
    return o.reshape(b, -1), sn


def _mlstm_finish(hh, og, mn):
    return _rms(hh, mn) * _sigmoid(og)


def _mlstm_body(z_ref, zs_ref, bias_ref, mn_ref, c0_ref, m0_ref, h_ref, cfin_ref, mfin_ref, c_scr, m_scr):
    t = pl.program_id(1)
    L = z_ref.shape[0]
    hq = M_HEADS * M_DQK

    @pl.when(t == 0)
    def _():
        c_scr[...] = c0_ref[0]
        m_scr[...] = m0_ref[0]

    g = zs_ref[...] + bias_ref[...]
    lane = _iota((L, LANE), 1)
    gx = jnp.where((lane >= M_HEADS) & (lane < 2 * M_HEADS), _logsigmoid(g), g)
    r, cidx = _iota((L, L), 0), _iota((L, L), 1)
    tri = r >= cidx
    cum_c = _mm_sel(jnp.where(tri, 1.0, 0.0).astype(BF16), gx)
    rows = _mm_sel(jnp.where(r == cidx, 1.0, 0.0).astype(BF16), gx, _TN, x_is_rhs=False)
    cum_r = _mm_sel(jnp.where(r <= cidx, 1.0, 0.0).astype(BF16), gx, _TN, x_is_rhs=False)
    ones_col = jnp.where(_iota((L, LANE), 1) == 0, 1.0, 0.0).astype(BF16)
    m_all = m_scr[...]
    m_new_all = m_all
    for h in range(M_HEADS):
        q = z_ref[:, h * M_DQK:(h + 1) * M_DQK].astype(BF16)
        k = z_ref[:, hq + h * M_DQK:hq + (h + 1) * M_DQK] * (M_DQK ** -0.5)
        v = z_ref[:, 2 * hq + h * M_DV:2 * hq + (h + 1) * M_DV]
        og = z_ref[:, 2 * hq + M_HEADS * M_DV + h * M_DV:2 * hq + M_HEADS * M_DV + (h + 1) * M_DV]
        va = jnp.concatenate([v.astype(BF16), ones_col], axis=1)
        ic_r, ic_c = rows[h:h + 1, :], gx[:, h:h + 1]
        cr, cc = cum_r[M_HEADS + h:M_HEADS + h + 1, :], cum_c[:, M_HEADS + h:M_HEADS + h + 1]
        m_old = m_all[0:1, h:h + 1]
        dlog = jnp.where(tri, cc - cr + ic_r, -jnp.inf)
        inter = cc + m_old
        mi = jnp.maximum(inter, jnp.max(dlog, axis=1, keepdims=True))
        w = jnp.exp(dlog - mi)
        wi = jnp.exp(inter - mi)
        s = _mm(q, k.astype(BF16), _NT) * w
        c_old = c_scr[h]
        num = wi * _mm(q, c_old.astype(BF16)) + _mm(s.astype(BF16), va)
        qn = num[:, M_DV:M_DV + 1]
        hh = num[:, :M_DV] / jnp.maximum(jnp.abs(qn), jnp.exp(-mi))
        last = cc[L - 1:L, :]
        gl = last - cc + ic_c
        m_new = jnp.maximum(last + m_old, jnp.max(gl, axis=0, keepdims=True))
        wj = jnp.exp(gl - m_new)
        keep = jnp.exp(last + m_old - m_new)
        c_scr[h] = keep * c_old + _mm((wj * k).astype(BF16), va, _TN)
        m_new_all = jnp.where(_iota((1, LANE), 1) == h, m_new, m_new_all)
        h_ref[:, h * M_DV:(h + 1) * M_DV] = _mlstm_finish(hh, og, mn_ref[:, h * M_DV:(h + 1) * M_DV])
    m_scr[...] = m_new_all

    @pl.when(t == pl.num_programs(1) - 1)
    def _():
        cfin_ref[0] = c_scr[...]
        mfin_ref[0] = m_scr[...]


def _mlstm_prompt(z, zs, bias, mn, c0, m0, bsz):
    m = z.shape[0]
    t = m // bsz
    L = M_CHUNK
    assert t % L == 0
    nc = t // L
    st = (1, M_HEADS, M_DQK, M_AUG)
    return pl.pallas_call(
        _mlstm_body,
        grid=(bsz, nc),
        in_specs=[pl.BlockSpec((L, M_Z_W), lambda b, i: (b * nc + i, 0)), pl.BlockSpec((L, LANE), lambda b, i: (b * nc + i, 0)),
                  _resident((1, LANE)), _resident((1, M_HEADS * M_DV)),
                  pl.BlockSpec(st, lambda b, i: (b, 0, 0, 0)), pl.BlockSpec((1, 1, LANE), lambda b, i: (b, 0, 0))],
        out_specs=[pl.BlockSpec((L, M_HEADS * M_DV), lambda b, i: (b * nc + i, 0)),
                   pl.BlockSpec(st, lambda b, i: (b, 0, 0, 0)), pl.BlockSpec((1, 1, LANE), lambda b, i: (b, 0, 0))],
        out_shape=[jax.ShapeDtypeStruct((m, M_HEADS * M_DV), F32), jax.ShapeDtypeStruct((bsz,) + st[1:], F32),
                   jax.ShapeDtypeStruct((bsz, 1, LANE), F32)],
        scratch_shapes=[pltpu.VMEM(st[1:], F32), pltpu.VMEM((1, LANE), F32)],
        compiler_params=_params(2),
        name="mlstm_prompt",
    )(z, zs, bias, mn, c0, m0)


def _mlstm_decode_body(z_ref, zs_ref, bias_ref, mn_ref, c_ref, n_ref, m_ref, h_ref, cn_ref, nn_ref, mo_ref):
    hq = M_HEADS * M_DQK
    rows = SUBLANE
    z = jnp.broadcast_to(z_ref[0], (rows, M_Z_W))
    g = zs_ref[0] + bias_ref[...]
    m_all = m_ref[0]
    m_new_all = m_all
    row0 = _iota((rows, M_DQK), 0) == 0
    for h in range(M_HEADS):
        q = z[:, h * M_DQK:(h + 1) * M_DQK]
        k = z[:, hq + h * M_DQK:hq + (h + 1) * M_DQK] * (M_DQK ** -0.5)
        v = z[:, 2 * hq + h * M_DV:2 * hq + (h + 1) * M_DV]
        og = z[:, 2 * hq + M_HEADS * M_DV + h * M_DV:2 * hq + M_HEADS * M_DV + (h + 1) * M_DV]
        ic = g[:, h:h + 1]
        fl = _logsigmoid(g[:, M_HEADS + h:M_HEADS + h + 1])
        m_old = m_all[:, h:h + 1]
        c_old, n_old = c_ref[0, h], n_ref[0, h:h + 1, :]
        mi = jnp.maximum(fl + m_old, ic)
        w = jnp.exp(ic - mi)
        wi = jnp.exp(fl + m_old - mi)
        s = jnp.sum(q * k, axis=1, keepdims=True) * w
        num = wi * _mm_hp(q, c_old) + s * v
        qn = wi * jnp.sum(q * n_old, axis=1, keepdims=True) + s
        hh = num / jnp.maximum(jnp.abs(qn), jnp.exp(-mi))
        cn_ref[0, h] = wi * c_old + w * _mm_hp(jnp.where(row0, k, 0.0), v, _TN)
        nn_ref[0, h:h + 1, :] = wi * n_old + w * k[0:1]
        m_new_all = jnp.where(_iota((1, LANE), 1) == h, mi, m_new_all)
        h_ref[0, :, h * M_DV:(h + 1) * M_DV] = _mlstm_finish(hh, og, mn_ref[:, h * M_DV:(h + 1) * M_DV])[0:1]
    mo_ref[0] = m_new_all


def _mlstm_decode(z, zs, bias, mn, c0, n0, m0):
    b = z.shape[0]
    cs, ns = (1, M_HEADS, M_DQK, M_DV), (1, M_HEADS, M_DQK)
    m0p = jnp.pad(m0, ((0, 0), (0, LANE - M_HEADS))).reshape(b, 1, LANE)
    row3 = lambda n: pl.BlockSpec((1, 1, n), lambda i: (i, 0, 0))
    h, cn, nn, mo = pl.pallas_call(
        _mlstm_decode_body,
        grid=(b,),
        in_specs=[row3(M_Z_W), row3(LANE), _resident((1, LANE)), _resident((1, M_HEADS * M_DV)),
                  pl.BlockSpec(cs, lambda i: (i, 0, 0, 0)), pl.BlockSpec(ns, lambda i: (i, 0, 0)), row3(LANE)],
        out_specs=[row3(M_HEADS * M_DV), pl.BlockSpec(cs, lambda i: (i, 0, 0, 0)), pl.BlockSpec(ns, lambda i: (i, 0, 0)), row3(LANE)],
        out_shape=[jax.ShapeDtypeStruct((b, 1, M_HEADS * M_DV), F32), jax.ShapeDtypeStruct((b,) + cs[1:], F32),
                   jax.ShapeDtypeStruct((b,) + ns[1:], F32), jax.ShapeDtypeStruct((b, 1, LANE), F32)],
        compiler_params=_params(1),
        name="mlstm_decode",
    )(z.reshape(b, 1, -1), zs.reshape(b, 1, -1), bias, mn, c0, n0, m0p)
    return h.reshape(b, -1), cn, nn, mo[:, 0, :M_HEADS]


def _compress_dense_body(x_ref, pe_ref, w1_ref, w2_ref, o_ref, acc):
    kk = pl.program_id(1)

    @pl.when(kk == 0)
    def _():
        acc[...] = jnp.zeros_like(acc)

    acc[...] += _mm((x_ref[...] + pe_ref[...]).astype(BF16), w1_ref[...])

    @pl.when(kk == pl.num_programs(1) - 1)
    def _():
        o_ref[...] = _mm(_gelu(acc[...]).astype(BF16), w2_ref[...])


def _compress_dense(x, pe, w1, w2, tk=2048):
    r, kdim = x.shape
    tr = r if r <= 512 else 512
    assert r % tr == 0 and kdim % tk == 0
    return pl.pallas_call(
        _compress_dense_body,
        grid=(r // tr, kdim // tk),
        in_specs=[pl.BlockSpec((tr, tk), lambda i, k: (i, k)), pl.BlockSpec((1, tk), lambda i, k: (0, k)),
                  pl.BlockSpec((tk, NSA_KV_W), lambda i, k: (k, 0)), _resident(w2.shape)],
        out_specs=pl.BlockSpec((tr, NSA_KV_W), lambda i, k: (i, 0)),
        out_shape=jax.ShapeDtypeStruct((r, NSA_KV_W), F32),
        scratch_shapes=[pltpu.VMEM((tr, NSA_KV_W), F32)],
        compiler_params=_params(2),
        name="nsa_compress",
    )(x, pe, w1, w2)


_PAGES_PER_STEP = 8


def _compress_paged_body(pt_ref, *refs, kc):
    page_refs = refs[:_PAGES_PER_STEP]
    pe_ref, w1_ref, w2_ref, o_ref, xs = refs[_PAGES_PER_STEP:]
    g = pl.program_id(1)
    for p, pr in enumerate(page_refs):
        xs[pl.ds(g * _PAGES_PER_STEP + p, 1), :] = pr[0]

    @pl.when(g == pl.num_programs(1) - 1)
    def _():
        kdim = pe_ref.shape[1]
        for half in range(PAGE_SIZE // NSA_BLOCK):
            def step(i, acc):
                c0 = pl.multiple_of(i * kc, kc)
                xb = (xs[:, pl.ds(half * kdim + c0, kc)] + pe_ref[:, pl.ds(c0, kc)]).astype(BF16)
                return acc + _mm(xb, w1_ref[pl.ds(c0, kc), :])
            hid = lax.fori_loop(0, kdim // kc, step, jnp.zeros((xs.shape[0], NSA_KV_W), F32))
            o_ref[:, half * NSA_KV_W:(half + 1) * NSA_KV_W] = _mm(_gelu(hid).astype(BF16), w2_ref[...])


def _compress_paged(pool, page_table, pe, w1, w2, kc=2048):
    b, n_pages = page_table.shape
    width = pool.shape[1]
    pps = _PAGES_PER_STEP
    assert n_pages % pps == 0
    pool3 = pool.reshape(pool.shape[0], 1, width)
    blocks_per_page = PAGE_SIZE // NSA_BLOCK

    def page_spec(p):
        return pl.BlockSpec((1, 1, width), lambda i, g, pt: (pt[i, g * pps + p], 0, 0))

    grid_spec = pltpu.PrefetchScalarGridSpec(
        num_scalar_prefetch=1,
        grid=(b, n_pages // pps),
        in_specs=[page_spec(p) for p in range(pps)]
        + [pl.BlockSpec(pe.shape, lambda i, g, pt: (0, 0), pipeline_mode=pl.Buffered(1)),
           pl.BlockSpec(w1.shape, lambda i, g, pt: (0, 0), pipeline_mode=pl.Buffered(1)),
           pl.BlockSpec(w2.shape, lambda i, g, pt: (0, 0), pipeline_mode=pl.Buffered(1))],
        out_specs=pl.BlockSpec((n_pages, blocks_per_page * NSA_KV_W), lambda i, g, pt: (i, 0)),
        scratch_shapes=[pltpu.VMEM((n_pages, width), F32)],
    )
    return pl.pallas_call(
        functools.partial(_compress_paged_body, kc=kc),
        grid_spec=grid_spec,
        out_shape=jax.ShapeDtypeStruct((b * n_pages, blocks_per_page * NSA_KV_W), F32),
        compiler_params=_params(2, VMEM_BIG),
        name="nsa_compress_paged",
    )(page_table, *([pool3] * pps), pe, w1, w2)


def _gate_cols(zs_ref, gb_ref):
    return _sigmoid(zs_ref[...] + gb_ref[...])


def _cmp_sel_body(q_ref, kvc_ref, zs_ref, gb_ref, ocmp_ref, nm_ref, sc_scr):
    qi = pl.program_id(1)
    tq = q_ref.shape[0]
    nbp = kvc_ref.shape[0]
    t0 = qi * tq
    tpos = t0 + _iota((nbp, tq), 1)
    blk = _iota((nbp, tq), 0)
    cur = jnp.right_shift(tpos, NSA_BLOCK.bit_length() - 1)
    vis = blk * NSA_BLOCK + (NSA_BLOCK - 1) <= tpos
    forced = (blk == 0) | (blk == cur) | (blk == cur - 1)
    allowed = blk <= cur
    gates = _gate_cols(zs_ref, gb_ref)
    eye = jnp.where(_iota((nbp, nbp), 0) == _iota((nbp, nbp), 1), 1.0, 0.0).astype(BF16)
    n_live = (t0 + tq - 1) // NSA_BLOCK + 1
    for kh in range(NSA_KV_HEADS):
        kc = kvc_ref[:, kh * NSA_DH:(kh + 1) * NSA_DH].astype(BF16)
        vc = kvc_ref[:, NSA_KV_HEADS * NSA_DH + kh * NSA_DH:NSA_KV_HEADS * NSA_DH + (kh + 1) * NSA_DH].astype(BF16)
        imp = jnp.zeros((nbp, tq), F32)
        for gi in range(NSA_GROUP):
            h = kh * NSA_GROUP + gi
            q = (q_ref[:, h * NSA_DH:(h + 1) * NSA_DH] * (NSA_DH ** -0.5)).astype(BF16)
            s = jnp.where(vis, _mm(kc, q, _NT), -jnp.inf)
            m = jnp.max(s, axis=0, keepdims=True)
            m = jnp.where(m == -jnp.inf, 0.0, m)
            e = jnp.exp(s - m)
            p = e / jnp.maximum(jnp.sum(e, axis=0, keepdims=True), TINY)
            imp = imp + p
            o = _mm(p.astype(BF16), vc, _TN)
            gc = GATE_OFF + h * NSA_BRANCHES
            ocmp_ref[:, h * NSA_DH:(h + 1) * NSA_DH] = o * gates[:, gc:gc + 1]
        score = jnp.where(forced, jnp.inf, jnp.where(allowed, imp, -jnp.inf))
        sc_scr[...] = score

        def rank_step(j, cnt):
            row = sc_scr[pl.ds(j, 1), :]
            tie = jnp.where(j < blk, 1.0, 0.0)
            return cnt + jnp.where(row > score, 1.0, jnp.where(row == score, tie, 0.0))

        cnt = lax.fori_loop(0, n_live, rank_step, jnp.zeros((nbp, tq), F32))
        nm_t = jnp.where(allowed, jnp.where(cnt < NSA_TOP_N, 0.0, -MASK_BIG), -MASK_BIG).astype(BF16)
        nm_ref[:, kh * nbp:(kh + 1) * nbp] = _mm(nm_t, eye, _TN).astype(BF16)


def _cmp_sel(q, kvc, zs, gb, bsz, nbp, tq):
    m = q.shape[0]
    t = m // bsz
    tq = min(tq, t)
    nt = t // tq
    hd = NSA_HEADS * NSA_DH
    return pl.pallas_call(
        _cmp_sel_body,
        grid=(bsz, nt),
        in_specs=[pl.BlockSpec((tq, hd), lambda b, i: (b * nt + i, 0)), pl.BlockSpec((nbp, NSA_KV_W), lambda b, i: (b, 0)),
                  pl.BlockSpec((tq, LANE), lambda b, i: (b * nt + i, 0)), _resident((1, LANE))],
        out_specs=[pl.BlockSpec((tq, hd), lambda b, i: (b * nt + i, 0)),
                   pl.BlockSpec((tq, NSA_KV_HEADS * nbp), lambda b, i: (b * nt + i, 0))],
        out_shape=[jax.ShapeDtypeStruct((m, hd), F32), jax.ShapeDtypeStruct((m, NSA_KV_HEADS * nbp), BF16)],
        scratch_shapes=[pltpu.VMEM((nbp, tq), F32)],
        compiler_params=_params(2),
        name="nsa_cmp_select",
    )(q, kvc, zs, gb)


def _sel_attn_body(q_ref, nm_ref, ka_ref, v_ref, zs_ref, gb_ref, o_ref, m_scr, l_scr, acc_scr, *, tk):
    qi = pl.program_id(1)
    tq = q_ref.shape[0]
    nbp = nm_ref.shape[1] // NSA_KV_HEADS
    rows = NSA_GROUP * tq
    t0 = qi * tq
    n_before = t0 // tk
    gates = _gate_cols(zs_ref, gb_ref)
    for kh in range(NSA_KV_HEADS):
        nm = nm_ref[:, kh * nbp:(kh + 1) * nbp]
        qa = jnp.concatenate(
            [jnp.concatenate([nm, (q_ref[:, h * NSA_DH:(h + 1) * NSA_DH] * (NSA_DH ** -0.5)).astype(BF16)], axis=1)
             for h in range(kh * NSA_GROUP, (kh + 1) * NSA_GROUP)], axis=0)
        m_scr[...] = jnp.full((rows, 1), -jnp.inf, F32)
        l_scr[...] = jnp.zeros((rows, 1), F32)
        acc_scr[...] = jnp.zeros((rows, NSA_DH), F32)

        def tile(kt, causal):
            k0 = pl.multiple_of(kt * tk, tk)
            s = _mm(qa, ka_ref[0, kh, pl.ds(k0, tk), :], _NT)
            if causal:
                kpos = k0 + _iota((rows, tk), 1)
                tpos = t0 + jnp.bitwise_and(_iota((rows, tk), 0), tq - 1)
                s = jnp.where(kpos <= tpos, s, -jnp.inf)
            m_old = m_scr[...]
            m_new = jnp.maximum(m_old, jnp.max(s, axis=1, keepdims=True))
            alpha = jnp.exp(m_old - m_new)
            p = jnp.exp(s - m_new)
            l_scr[...] = alpha * l_scr[...] + jnp.sum(p, axis=1, keepdims=True)
            acc_scr[...] = alpha * acc_scr[...] + _mm(p.astype(BF16), v_ref[0, kh, pl.ds(k0, tk), :])
            m_scr[...] = m_new

        def past_tile(kt, carry):
            tile(kt, False)
            return carry

        lax.fori_loop(0, n_before, past_tile, 0)
        tile(n_before, True)
        o = acc_scr[...] / l_scr[...]
        for gi in range(NSA_GROUP):
            h = kh * NSA_GROUP + gi
            gc = GATE_OFF + h * NSA_BRANCHES + 1
            o_ref[:, h * NSA_DH:(h + 1) * NSA_DH] = o[gi * tq:(gi + 1) * tq] * gates[:, gc:gc + 1]


def _sel_attn(q, nm, kaug, vs, zs, gb, bsz, tq, tk):
    m = q.shape[0]
    t = m // bsz
    tq, tk = min(tq, t), min(tk, t)
    assert t % tq == 0 and t % tk == 0 and tk % tq == 0 and tq & (tq - 1) == 0
    nt = t // tq
    hd = NSA_HEADS * NSA_DH
    rows = NSA_GROUP * tq
    return pl.pallas_call(
        functools.partial(_sel_attn_body, tk=tk),
        grid=(bsz, nt),
        in_specs=[pl.BlockSpec((tq, hd), lambda b, i: (b * nt + i, 0)), pl.BlockSpec((tq, nm.shape[1]), lambda b, i: (b * nt + i, 0)),
                  pl.BlockSpec((1,) + kaug.shape[1:], lambda b, i: (b, 0, 0, 0)),
                  pl.BlockSpec((1,) + vs.shape[1:], lambda b, i: (b, 0, 0, 0)),
                  pl.BlockSpec((tq, LANE), lambda b, i: (b * nt + i, 0)), _resident((1, LANE))],
        out_specs=pl.BlockSpec((tq, hd), lambda b, i: (b * nt + i, 0)),
        out_shape=jax.ShapeDtypeStruct((m, hd), F32),
        scratch_shapes=[pltpu.VMEM((rows, 1), F32), pltpu.VMEM((rows, 1), F32), pltpu.VMEM((rows, NSA_DH), F32)],
        compiler_params=_params(2, VMEM_BIG),
        name="nsa_selected",
    )(q, nm, kaug, vs, zs, gb)


def _win_attn_body(q_ref, k_ref, v_ref, zs_ref, gb_ref, o_ref, *, span):
    qi = pl.program_id(1)
    tq = q_ref.shape[0]
    rows = NSA_GROUP * tq
    t0 = qi * tq
    start = pl.multiple_of(jnp.maximum(t0 + tq - span, 0), tq)
    gates = _gate_cols(zs_ref, gb_ref)
    kpos = start + _iota((rows, span), 1)
    tpos = t0 + jnp.bitwise_and(_iota((rows, span), 0), tq - 1)
    ok = (kpos <= tpos) & (tpos - kpos < NSA_WINDOW)
    for kh in range(NSA_KV_HEADS):
        qa = jnp.concatenate([(q_ref[:, h * NSA_DH:(h + 1) * NSA_DH] * (NSA_DH ** -0.5)).astype(BF16)
                              for h in range(kh * NSA_GROUP, (kh + 1) * NSA_GROUP)], axis=0)
        s = jnp.where(ok, _mm(qa, k_ref[0, kh, pl.ds(start, span), :], _NT), -jnp.inf)
        e = jnp.exp(s - jnp.max(s, axis=1, keepdims=True))
        o = _mm(e.astype(BF16), v_ref[0, kh, pl.ds(start, span), :]) / jnp.sum(e, axis=1, keepdims=True)
        for gi in range(NSA_GROUP):
            h = kh * NSA_GROUP + gi
            gc = GATE_OFF + h * NSA_BRANCHES + 2
            o_ref[:, h * NSA_DH:(h + 1) * NSA_DH] = o[gi * tq:(gi + 1) * tq] * gates[:, gc:gc + 1]


def _win_attn(q, kw, vw, zs, gb, bsz, tq):
    m = q.shape[0]
    t = m // bsz
    tq = min(tq, t)
    span = min(NSA_WINDOW + tq, t)
    assert t % tq == 0 and tq & (tq - 1) == 0
    nt = t // tq
    hd = NSA_HEADS * NSA_DH
    return pl.pallas_call(
        functools.partial(_win_attn_body, span=span),
        grid=(bsz, nt),
        in_specs=[pl.BlockSpec((tq, hd), lambda b, i: (b * nt + i, 0)),
                  pl.BlockSpec((1,) + kw.shape[1:], lambda b, i: (b, 0, 0, 0)),
                  pl.BlockSpec((1,) + vw.shape[1:], lambda b, i: (b, 0, 0, 0)),
                  pl.BlockSpec((tq, LANE), lambda b, i: (b * nt + i, 0)), _resident((1, LANE))],
        out_specs=pl.BlockSpec((tq, hd), lambda b, i: (b * nt + i, 0)),
        out_shape=jax.ShapeDtypeStruct((m, hd), F32),
        compiler_params=_params(2),
        name="nsa_window",
    )(q, kw, vw, zs, gb)


def _decode_forced(n_past_blk):
    cur = n_past_blk
    return sorted({0, cur - 1, cur} - {-1})


def _cmp_decode_body(q_ref, kvc_ref, gz_ref, gb_ref, o_ref, idx_ref, *, past, n_pick):
    nb = kvc_ref.shape[0]
    lane = _iota((NSA_HEADS, nb), 1)
    hrow = _iota((NSA_HEADS, nb), 0)
    vis = lane * NSA_BLOCK + (NSA_BLOCK - 1) <= past
    q = (q_ref[0] * (NSA_DH ** -0.5)).astype(BF16)
    gates = _sigmoid(gz_ref[0] + gb_ref[...])
    cur = past // NSA_BLOCK
    o_all = jnp.zeros((NSA_HEADS, NSA_DH), F32)
    idx_all = jnp.zeros((SUBLANE, LANE), F32)
    orow = _iota((NSA_HEADS, NSA_DH), 0)
    slot_r, slot_c = _iota((SUBLANE, LANE), 0), _iota((SUBLANE, LANE), 1)
    l1 = _iota((1, nb), 1)
    l1f = l1.astype(F32)
    forced = (l1 == 0) | (l1 == cur) | (l1 == cur - 1)
    for kh in range(NSA_KV_HEADS):
        kc = kvc_ref[:, kh * NSA_DH:(kh + 1) * NSA_DH].astype(BF16)
        vc = kvc_ref[:, NSA_KV_HEADS * NSA_DH + kh * NSA_DH:NSA_KV_HEADS * NSA_DH + (kh + 1) * NSA_DH].astype(BF16)
        s = jnp.where(vis, _mm(q, kc, _NT), -jnp.inf)
        m = jnp.max(s, axis=1, keepdims=True)
        m = jnp.where(m == -jnp.inf, 0.0, m)
        e = jnp.exp(s - m)
        p = e / jnp.maximum(jnp.sum(e, axis=1, keepdims=True), TINY)
        mine = (hrow >= kh * NSA_GROUP) & (hrow < (kh + 1) * NSA_GROUP)
        o_all = jnp.where((orow >= kh * NSA_GROUP) & (orow < (kh + 1) * NSA_GROUP), _mm(p.astype(BF16), vc), o_all)
        imp = jnp.sum(jnp.where(mine, p, 0.0), axis=0, keepdims=True)
        score = jnp.where(forced, -jnp.inf, imp)
        for r in range(n_pick):
            mx = jnp.max(score, axis=1, keepdims=True)
            pick = jnp.min(jnp.where(score == mx, l1f, float(nb)), axis=1, keepdims=True)
            score = jnp.where(l1f == pick, -jnp.inf, score)
            idx_all = jnp.where((slot_r == kh) & (slot_c == r), pick, idx_all)
    o_ref[0] = o_all * gates[:, 0:1]
    idx_ref[0] = idx_all.astype(jnp.int32)


def _cmp_decode(qh, kvc, gz, gb3, past, n_pick):
    b = qh.shape[0]
    nb = kvc.shape[0] // b
    return pl.pallas_call(
        functools.partial(_cmp_decode_body, past=past, n_pick=n_pick),
        grid=(b,),
        in_specs=[pl.BlockSpec((1, NSA_HEADS, NSA_DH), lambda i: (i, 0, 0)), pl.BlockSpec((nb, NSA_KV_W), lambda i: (i, 0)),
                  pl.BlockSpec((1, NSA_HEADS, NSA_BRANCHES), lambda i: (i, 0, 0)), _resident((NSA_HEADS, NSA_BRANCHES))],
        out_specs=[pl.BlockSpec((1, NSA_HEADS, NSA_DH), lambda i: (i, 0, 0)), pl.BlockSpec((1, SUBLANE, LANE), lambda i: (i, 0, 0))],
        out_shape=[jax.ShapeDtypeStruct((b, NSA_HEADS, NSA_DH), F32), jax.ShapeDtypeStruct((b, SUBLANE, LANE), jnp.int32)],
        compiler_params=_params(1),
        name="nsa_cmp_decode",
    )(qh, kvc, gz, gb3)


def _selwin_decode_body(rows_ref, q_ref, ns_ref, nw_ref, win_ref, *refs, n_shared, n_own, first_win_row):
    n_blk = n_shared + NSA_KV_HEADS * n_own
    blk_refs = refs[:n_blk]
    gz_ref, gb_ref, o_ref = refs[n_blk:]
    q = q_ref[0] * (NSA_DH ** -0.5)
    qb = q.astype(BF16)
    gates = _sigmoid(gz_ref[0] + gb_ref[...])
    hrow = _iota((NSA_HEADS, NSA_DH), 0)
    n_win = win_ref.shape[1]
    wlane = _iota((NSA_HEADS, n_win), 1)
    o_all = jnp.zeros((NSA_HEADS, NSA_DH), F32)
    voff = NSA_KV_HEADS * NSA_DH
    for kh in range(NSA_KV_HEADS):
        ksl = slice(kh * NSA_DH, (kh + 1) * NSA_DH)
        vsl = slice(voff + kh * NSA_DH, voff + (kh + 1) * NSA_DH)
        mine = blk_refs[:n_shared] + blk_refs[n_shared + kh * n_own:n_shared + (kh + 1) * n_own]
        kcat = jnp.concatenate([r[0, :, ksl] for r in mine], axis=0).astype(BF16)
        vcat = jnp.concatenate([r[0, :, vsl] for r in mine], axis=0).astype(BF16)
        s = _mm(qb, kcat, _NT)
        kn, vn = ns_ref[0, :, ksl], ns_ref[0, :, vsl]
        sn = jnp.sum(q * kn, axis=1, keepdims=True)
        m = jnp.maximum(jnp.max(s, axis=1, keepdims=True), sn)
        e, en = jnp.exp(s - m), jnp.exp(sn - m)
        o_sel = (_mm(e.astype(BF16), vcat) + en * vn) / (jnp.sum(e, axis=1, keepdims=True) + en)
        sw = jnp.where(wlane >= first_win_row, _mm(qb, win_ref[0, :, ksl].astype(BF16), _NT), -jnp.inf)
        kwn, vwn = nw_ref[0, :, ksl], nw_ref[0, :, vsl]
        swn = jnp.sum(q * kwn, axis=1, keepdims=True)
        mw = jnp.maximum(jnp.max(sw, axis=1, keepdims=True), swn)
        ew, ewn = jnp.exp(sw - mw), jnp.exp(swn - mw)
        o_win = (_mm(ew.astype(BF16), win_ref[0, :, vsl].astype(BF16)) + ewn * vwn) / (jnp.sum(ew, axis=1, keepdims=True) + ewn)
        keep = (hrow >= kh * NSA_GROUP) & (hrow < (kh + 1) * NSA_GROUP)
        o_all = jnp.where(keep, gates[:, 1:2] * o_sel + gates[:, 2:3] * o_win, o_all)
    o_ref[0] = o_all


def _selwin_decode(rows, qh, new_s, new_w, win, pool_blk, gz, gb3, n_shared, n_own, first_win_row):
    b = qh.shape[0]
    n_blk = rows.shape[1]
    row3 = lambda n: pl.BlockSpec((1, 1, n), lambda i, r: (i, 0, 0))

    def blk_spec(s):
        return pl.BlockSpec((1, NSA_BLOCK, NSA_KV_W), lambda i, r: (r[i, s], 0, 0))

    grid_spec = pltpu.PrefetchScalarGridSpec(
        num_scalar_prefetch=1,
        grid=(b,),
        in_specs=[pl.BlockSpec((1, NSA_HEADS, NSA_DH), lambda i, r: (i, 0, 0)), row3(NSA_KV_W), row3(NSA_KV_W),
                  pl.BlockSpec((1,) + win.shape[1:], lambda i, r: (i, 0, 0))]
        + [blk_spec(s) for s in range(n_blk)]
        + [pl.BlockSpec((1, NSA_HEADS, NSA_BRANCHES), lambda i, r: (i, 0, 0)),
           pl.BlockSpec((NSA_HEADS, NSA_BRANCHES), lambda i, r: (0, 0))],
        out_specs=pl.BlockSpec((1, NSA_HEADS, NSA_DH), lambda i, r: (i, 0, 0)),
    )
    return pl.pallas_call(
        functools.partial(_selwin_decode_body, n_shared=n_shared, n_own=n_own, first_win_row=first_win_row),
        grid_spec=grid_spec,
        out_shape=jax.ShapeDtypeStruct((b, NSA_HEADS, NSA_DH), F32),
        compiler_params=_params(1),
        name="nsa_selwin_decode",
    )(rows, qh, new_s.reshape(b, 1, -1), new_w.reshape(b, 1, -1), win, *([pool_blk] * n_blk), gz, gb3)


def _pad_cols(w, n):
    return jnp.pad(w, ((0, 0), (0, n - w.shape[1])))


def _even_w_in(w):
    sizes = (GLA_HEADS * GLA_DK, GLA_HEADS * GLA_DK, GLA_HEADS * GLA_DV, GLA_HEADS * GLA_DV, GLA_RANK,
             NSA_HEADS * NSA_DH, NSA_BRANCHES * NSA_KV_W, NSA_HEADS * NSA_BRANCHES)
    cuts = [0]
    for s in sizes:
        cuts.append(cuts[-1] + s)
    gq, gk, gv, gg, ga, nq, nkv, ng = (w[:, cuts[i]:cuts[i + 1]] for i in range(len(sizes)))
    small = _pad_cols(jnp.concatenate([ga, ng], axis=1), LANE)
    return jnp.concatenate([gq, gk, gv, gg, nq, nkv, small], axis=1).astype(BF16)


_EVEN_WIDTHS = (GLA_Z_W, NSA_HEADS * NSA_DH, NSA_KV_W, NSA_KV_W, NSA_KV_W, LANE)


def _odd_w_in(w):
    main = M_Z_W
    return jnp.concatenate([w[:, :main], _pad_cols(w[:, main:], LANE)], axis=1).astype(BF16)


_ODD_WIDTHS = (M_Z_W, LANE)


def _compress_weights(pe, w1, w2):
    eye_k = jnp.eye(NSA_KV_HEADS, dtype=F32)
    eye_c = jnp.eye(2, dtype=F32)
    w1big = jnp.einsum("csde,cx,ky->sckdxye", w1, eye_c, eye_k).reshape(NSA_BLOCK * NSA_KV_W, NSA_KV_W)
    w2big = jnp.einsum("ced,cx,ky->ckexyd", w2, eye_c, eye_k).reshape(NSA_KV_W, NSA_KV_W)
    pe_flat = jnp.broadcast_to(pe.transpose(1, 0, 2)[:, :, None, :], (NSA_BLOCK, 2, NSA_KV_HEADS, NSA_DH)).reshape(1, -1)
    return pe_flat, w1big.astype(BF16), w2big.astype(BF16)


def _gate_bias_row(gb):
    return jnp.pad(gb, (GATE_OFF, LANE - GATE_OFF - gb.shape[0])).reshape(1, LANE)


def _heads_major(kv, which):
    b, t, _ = kv.shape
    return kv.reshape(b, t, 2, NSA_KV_HEADS, NSA_DH)[:, :, which].transpose(0, 2, 1, 3).astype(BF16)


def _even_layer_prompt(x, bsz, g, prm):
    m = x.shape[0]
    t = m // bsz
    z, nq, kvc, kvs, kvw, zs = _norm_proj(x, g, prm["w_in"], _EVEN_WIDTHS, 256)
    o_gla, s_fin = _gla_prompt(z, zs, prm["wa"], prm["ba"], prm["gn"],
                               jnp.zeros((bsz, GLA_HEADS, GLA_DK, GLA_DV), F32), bsz)
    nb = t // NSA_BLOCK
    nbp = -(-nb // LANE) * LANE
    kvcmp = _compress_dense(kvc.reshape(bsz * nb, NSA_BLOCK * NSA_KV_W), prm["pe"], prm["w1"], prm["w2"])
    kvcmp = jnp.pad(kvcmp.reshape(bsz, nb, NSA_KV_W), ((0, 0), (0, nbp - nb), (0, 0))).reshape(bsz * nbp, NSA_KV_W)
    o_cmp, nm = _cmp_sel(nq, kvcmp, zs, prm["gb"], bsz, nbp, 128)
    onehot = (jnp.arange(t)[:, None] // NSA_BLOCK == jnp.arange(nbp)[None, :]).astype(BF16)
    ks = _heads_major(kvs.reshape(bsz, t, -1), 0)
    kaug = jnp.concatenate([jnp.broadcast_to(onehot, (bsz, NSA_KV_HEADS, t, nbp)), ks], axis=-1)
    o_sel = _sel_attn(nq, nm, kaug, _heads_major(kvs.reshape(bsz, t, -1), 1), zs, prm["gb"], bsz, 128, 512)
    kw3 = kvw.reshape(bsz, t, -1)
    o_win = _win_attn(nq, _heads_major(kw3, 0), _heads_major(kw3, 1), zs, prm["gb"], bsz, 128)
    y = _out_proj(x, [o_gla, o_cmp, o_sel, o_win], (1, 3), prm["w_out"], 512)
    kv_shape = (bsz, t, 2, NSA_KV_HEADS, NSA_DH)
    n_keep = min(NSA_WINDOW, t)
    return y, s_fin, kvc.reshape(kv_shape), kvs.reshape(kv_shape), kvw.reshape(kv_shape)[:, t - n_keep:]


def _even_layer_sample(x, g, prm, gla_state, cmp_pool, sel_pool, win_buf, page_table):
    b = x.shape[0]
    n_pages = page_table.shape[1]
    past = n_pages * PAGE_SIZE
    n_past_blk = past // NSA_BLOCK
    z, nq, kvc, kvs, kvw, zs = _norm_proj(x, g, prm["w_in"], _EVEN_WIDTHS, b)
    o_gla, s_new = _gla_decode(z, zs, prm["wa"], prm["ba"], prm["gn"], gla_state)
    n_phys = cmp_pool.shape[0]
    kvcmp = _compress_paged(cmp_pool.reshape(n_phys, -1), page_table, prm["pe"], prm["w1"], prm["w2"])
    kvcmp = kvcmp.reshape(b * n_past_blk, NSA_KV_W)
    qh = nq.reshape(b, NSA_HEADS, NSA_DH)
    gz = zs[:, GATE_OFF:GATE_OFF + NSA_HEADS * NSA_BRANCHES].reshape(b, NSA_HEADS, NSA_BRANCHES)
    forced = _decode_forced(n_past_blk)
    n_pick = NSA_TOP_N - len(forced)
    assert n_past_blk - len(forced) + 1 >= n_pick
    o_cmp, idx = _cmp_decode(qh, kvcmp, gz, prm["gb3"], past, n_pick)
    shared = [f for f in forced if f < n_past_blk]
    logical = jnp.concatenate([jnp.broadcast_to(jnp.asarray(shared, jnp.int32), (b, len(shared))),
                               idx[:, :NSA_KV_HEADS, :n_pick].reshape(b, NSA_KV_HEADS * n_pick)], axis=1)
    per_page = PAGE_SIZE // NSA_BLOCK
    phys = jnp.take_along_axis(page_table, logical // per_page, axis=1) * per_page + logical % per_page
    win_keep = win_buf.shape[1]
    first_win_row = max(win_keep - NSA_WINDOW + 1, 0)
    o_sw = _selwin_decode(phys.astype(jnp.int32), qh, kvs, kvw, win_buf.reshape(b, win_keep, NSA_KV_W),
                          sel_pool.reshape(n_phys * per_page, NSA_BLOCK, NSA_KV_W), gz, prm["gb3"],
                          len(shared), n_pick, first_win_row)
    hd = NSA_HEADS * NSA_DH
    y = _out_proj(x, [o_gla, o_cmp.reshape(b, hd), o_sw.reshape(b, hd)], (1, 2), prm["w_out"], b)
    kv_shape = (b, 1, 2, NSA_KV_HEADS, NSA_DH)
    win_new = jnp.concatenate([win_buf, kvw.reshape(kv_shape)], axis=1)[:, 1:] if win_keep else win_buf
    return y, s_new, kvc.reshape(kv_shape), kvs.reshape(kv_shape), win_new


def _odd_layer_prompt(x, bsz, g, prm):
    z, zs = _norm_proj(x, g, prm["w_in"], _ODD_WIDTHS, 256)
    h, c_aug, m_fin = _mlstm_prompt(z, zs, prm["bias"], prm["mn"], jnp.zeros((bsz, M_HEADS, M_DQK, M_AUG), F32),
                                    jnp.zeros((bsz, 1, LANE), F32), bsz)
    y = _out_proj(x, [h], (1,), prm["w_out"], 512)
    return y, c_aug[..., :M_DV], c_aug[..., M_DV], m_fin[:, 0, :M_HEADS]


def _odd_layer_sample(x, g, prm, c0, n0, m0):
    b = x.shape[0]
    z, zs = _norm_proj(x, g, prm["w_in"], _ODD_WIDTHS, b)
    h, cn, nn, mn = _mlstm_decode(z, zs, prm["bias"], prm["mn"], c0, n0, m0)
    return _out_proj(x, [h], (1,), prm["w_out"], b), cn, nn, mn


def kernel(x_prompt, x_sample, cache_cmp_kv, cache_sel_kv, cache_win_kv, state_gla, state_mlstm_c, state_mlstm_n, state_mlstm_m, state_ffn_conv, page_table, norm_mix, norm_ffn, norm_final, even_w_in, even_w_out, gla_w_a2, gla_b_a, gla_norm, nsa_cmp_pe, nsa_cmp_w1, nsa_cmp_w2, nsa_gate_b, odd_w_in, odd_w_out, mlstm_b_i, mlstm_b_f, mlstm_norm, ffn_w_up, ffn_conv_w, ffn_conv_b, ffn_w_down):
    bp, t, d = x_prompt.shape
    bs = x_sample.shape[0]
    assert x_sample.shape[1] == 1
    depth = norm_mix.shape[0]
    f = ffn_conv_w.shape[2]
    xp = x_prompt.reshape(bp * t, d)
    xs = x_sample.reshape(bs, d)
    outs = {k: [] for k in ("cmp_p", "cmp_s", "sel_p", "sel_s", "win_p", "win_s", "gla_p", "gla_s",
                            "mc_p", "mc_s", "mn_p", "mn_s", "mm_p", "mm_s", "cv_p", "cv_s")}
    for l in range(depth):
        if l % 2 == 0:
            e = l // 2
            pe, w1, w2 = _compress_weights(nsa_cmp_pe[e], nsa_cmp_w1[e], nsa_cmp_w2[e])
            prm = dict(w_in=_even_w_in(even_w_in[e]), w_out=even_w_out[e].astype(BF16),
                       wa=jnp.pad(gla_w_a2[e], ((0, LANE - GLA_RANK), (0, 0))), ba=gla_b_a[e].reshape(1, -1),
                       gn=gla_norm[e].reshape(1, -1), pe=pe, w1=w1, w2=w2, gb=_gate_bias_row(nsa_gate_b[e]),
                       gb3=nsa_gate_b[e].reshape(NSA_HEADS, NSA_BRANCHES))
            xp, s_, c_, k_, w_ = _even_layer_prompt(xp, bp, norm_mix[l], prm)
            outs["gla_p"].append(s_); outs["cmp_p"].append(c_); outs["sel_p"].append(k_); outs["win_p"].append(w_)
            xs, s_, c_, k_, w_ = _even_layer_sample(xs, norm_mix[l], prm, state_gla[e], cache_cmp_kv[e], cache_sel_kv[e],
                                                    cache_win_kv[e], page_table)
            outs["gla_s"].append(s_); outs["cmp_s"].append(c_); outs["sel_s"].append(k_); outs["win_s"].append(w_)
        else:
            o = l // 2
            bias = jnp.pad(jnp.concatenate([mlstm_b_i[o], mlstm_b_f[o]]), (0, LANE - 2 * M_HEADS)).reshape(1, LANE)
            prm = dict(w_in=_odd_w_in(odd_w_in[o]), w_out=odd_w_out[o].astype(BF16), bias=bias, mn=mlstm_norm[o].reshape(1, -1))
            xp, c_, n_, m_ = _odd_layer_prompt(xp, bp, norm_mix[l], prm)
            outs["mc_p"].append(c_); outs["mn_p"].append(n_); outs["mm_p"].append(m_)
            xs, c_, n_, m_ = _odd_layer_sample(xs, norm_mix[l], prm, state_mlstm_c[o], state_mlstm_n[o], state_mlstm_m[o])
            outs["mc_s"].append(c_); outs["mn_s"].append(n_); outs["mm_s"].append(m_)
        final = l == depth - 1
        wup, wd = ffn_w_up[l].astype(BF16), ffn_w_down[l].astype(BF16)
        xp, cv = _ffn_prompt(xp, bp, norm_ffn[l], wup, ffn_conv_w[l], ffn_conv_b[l], wd,
                             jnp.zeros((bp, CONV_W - 1, f), F32), norm_final, final, 512)
        outs["cv_p"].append(cv)
        xs, cv = _ffn_decode(xs, norm_ffn[l], wup, ffn_conv_w[l], ffn_conv_b[l], wd, state_ffn_conv[l], norm_final, final)
        outs["cv_s"].append(cv)
    st = lambda k: jnp.stack(outs[k])
    return (xp.reshape(bp, t, d), xs.reshape(bs, 1, d),
            st("cmp_p"), st("cmp_s"), st("sel_p"), st("sel_s"), st("win_p"), st("win_s"), st("gla_p"), st("gla_s"),
            st("mc_p"), st("mc_s"), st("mn_p"), st("mn_s"), st("mm_p"), st("mm_s"), st("cv_p"), st("cv_s"))
```

```python
import functools

import jax
import jax.numpy as jnp
from jax import lax
from jax.experimental import pallas as pl
from jax.experimental.pallas import tpu as pltpu

F32 = jnp.float32
BF16 = jnp.bfloat16

GLA_HEADS, GLA_DK, GLA_DV, GLA_RANK, GLA_TAU, GLA_CHUNK = 4, 64, 128, 16, 16.0, 64
NSA_HEADS, NSA_KV_HEADS, NSA_GROUP, NSA_DH = 8, 2, 4, 64
NSA_BRANCHES, NSA_BLOCK, NSA_TOP_N, NSA_WINDOW, NSA_CMP_HID = 3, 64, 16, 512, 64
M_HEADS, M_DQK, M_DV, M_CHUNK = 4, 128, 256, 64
CONV_W = 3
PAGE_SIZE = 128
EPS, TINY = 1e-6, 1e-30

LANE = 128
SUBLANE = 8
VMEM_BIG = 52 * 1024 * 1024
VMEM_MID = 40 * 1024 * 1024

MASK_BIG = 32768.0

NSA_KV_W = 2 * NSA_KV_HEADS * NSA_DH
GLA_Z_W = 2 * GLA_HEADS * GLA_DK + 2 * GLA_HEADS * GLA_DV
M_Z_W = 2 * M_HEADS * M_DQK + 2 * M_HEADS * M_DV
M_AUG = M_DV + LANE
GATE_OFF = GLA_RANK
_BLOCKS_PER_PAGE = PAGE_SIZE // NSA_BLOCK
_PAGE_SHIFT = _BLOCKS_PER_PAGE.bit_length() - 1

_NN = (((1,), (0,)), ((), ()))
_NT = (((1,), (1,)), ((), ()))
_TN = (((0,), (0,)), ((), ()))


def _mm(a, b, dims=_NN):
    return lax.dot_general(a, b, dims, preferred_element_type=F32)


def _mm_bf(a, b, dims=_NN):
    return _mm(a.astype(BF16), b.astype(BF16), dims)


def _split_bf16(x, n):
    parts, r = [], x
    for _ in range(n):
        p = r.astype(BF16)
        parts.append(p)
        r = r - p.astype(F32)
    return parts


def _mm_sel(sel, x, dims=_NN, x_is_rhs=True):
    out = None
    for p in _split_bf16(x, 3):
        t = _mm(sel, p, dims) if x_is_rhs else _mm(p, sel, dims)
        out = t if out is None else out + t
    return out


def _mm_hp(a, b, dims=_NN):
    a1, a2 = _split_bf16(a, 2)
    b1, b2 = _split_bf16(b, 2)
    return _mm(a1, b1, dims) + (_mm(a1, b2, dims) + _mm(a2, b1, dims))


def _gelu(x):
    return 0.5 * x * (1.0 + jnp.tanh(0.7978845608028654 * (x + 0.044715 * (x * x * x))))


def _sigmoid(x):
    return 1.0 / (1.0 + jnp.exp(-x))


def _logsigmoid(x):
    return jnp.minimum(x, 0.0) - jnp.log(1.0 + jnp.exp(-jnp.abs(x)))


def _rms(x, g):
    return x * lax.rsqrt(jnp.mean(x * x, axis=-1, keepdims=True) + EPS) * g


def _iota(shape, dim):
    return lax.broadcasted_iota(jnp.int32, shape, dim)


def _params(n_axes, vmem=VMEM_MID):
    return pltpu.CompilerParams(dimension_semantics=("arbitrary",) * n_axes, vmem_limit_bytes=vmem)


def _resident(shape):
    nd = len(shape)
    return pl.BlockSpec(shape, lambda *_: (0,) * nd, pipeline_mode=pl.Buffered(1))


def _norm_proj_body(x_ref, g_ref, w_ref, *o_refs, widths):
    xb = _rms(x_ref[...], g_ref[...]).astype(BF16)
    off = 0
    for o_ref, n in zip(o_refs, widths):
        o_ref[...] = _mm(xb, w_ref[:, off:off + n])
        off += n


def _norm_proj(x, g, w, widths, tm):
    m, d = x.shape
    tm = min(tm, m)
    assert m % tm == 0 and sum(widths) == w.shape[1]
    return pl.pallas_call(
        functools.partial(_norm_proj_body, widths=tuple(widths)),
        grid=(m // tm,),
        in_specs=[pl.BlockSpec((tm, d), lambda i: (i, 0)), _resident((1, d)), _resident(w.shape)],
        out_specs=[pl.BlockSpec((tm, n), lambda i: (i, 0)) for n in widths],
        out_shape=[jax.ShapeDtypeStruct((m, n), F32) for n in widths],
        compiler_params=_params(1),
        name="norm_proj",
    )(x, g.reshape(1, d), w)


def _out_proj_body(x_ref, *refs, groups):
    h_refs, w_ref, o_ref = refs[:-2], refs[-2], refs[-1]
    acc = x_ref[...]
    i = off = 0
    for gsz in groups:
        h = h_refs[i][...]
        for j in range(1, gsz):
            h = h + h_refs[i + j][...]
        i += gsz
        n = h.shape[1]
        acc = acc + _mm(h.astype(BF16), w_ref[off:off + n, :])
        off += n
    o_ref[...] = acc


def _out_proj(x, hs, groups, w, tm):
    m, d = x.shape
    tm = min(tm, m)
    assert m % tm == 0
    return pl.pallas_call(
        functools.partial(_out_proj_body, groups=tuple(groups)),
        grid=(m // tm,),
        in_specs=[pl.BlockSpec((tm, d), lambda i: (i, 0))]
        + [pl.BlockSpec((tm, h.shape[1]), lambda i: (i, 0)) for h in hs]
        + [_resident(w.shape)],
        out_specs=pl.BlockSpec((tm, d), lambda i: (i, 0)),
        out_shape=jax.ShapeDtypeStruct((m, d), F32),
        compiler_params=_params(1),
        name="out_proj",
    )(x, *hs, w)


def _ffn_chunks(xn, resid, wup_ref, cw_ref, cb_ref, wd_ref, fc, prev_fn, keep_fn):
    f = cw_ref.shape[1]
    acc = resid
    for c in range(f // fc):
        sl = slice(c * fc, (c + 1) * fc)
        gp = _mm(xn, wup_ref[:, sl])
        up = _mm(xn, wup_ref[:, f + c * fc:f + (c + 1) * fc])
        g2, g1 = prev_fn(gp, sl)
        a = cb_ref[:, sl] + g2 * cw_ref[0:1, sl] + g1 * cw_ref[1:2, sl] + gp * cw_ref[2:3, sl]
        acc = acc + _mm((_gelu(a) * up).astype(BF16), wd_ref[sl, :])
        keep_fn(gp, sl)
    return acc


def _ffn_prompt_body(x_ref, g_ref, wup_ref, cw_ref, cb_ref, wd_ref, st_ref, gf_ref, y_ref, ns_ref, carry, *, final, fc):
    t = pl.program_id(1)
    tm = x_ref.shape[0]

    @pl.when(t == 0)
    def _():
        carry[...] = st_ref[0]

    x = x_ref[...]
    xn = _rms(x, g_ref[...]).astype(BF16)
    row = _iota((tm, fc), 0)

    def prev_fn(gp, sl):
        c0, c1 = carry[0:1, sl], carry[1:2, sl]
        g1 = jnp.where(row == 0, c1, pltpu.roll(gp, 1, 0))
        g2 = jnp.where(row == 0, c0, jnp.where(row == 1, c1, pltpu.roll(gp, 2, 0)))
        return g2, g1

    def keep_fn(gp, sl):
        carry[:, sl] = gp[tm - 2:tm, :]

    acc = _ffn_chunks(xn, x, wup_ref, cw_ref, cb_ref, wd_ref, fc, prev_fn, keep_fn)
    y_ref[...] = _rms(acc, gf_ref[...]) if final else acc

    @pl.when(t == pl.num_programs(1) - 1)
    def _():
        ns_ref[0] = carry[...]


def _ffn_prompt(x, bsz, g, wup, cw, cb, wd, st, gf, final, tm, fc=256):
    m, d = x.shape
    t = m // bsz
    tm = min(tm, t)
    f = cw.shape[1]
    assert t % tm == 0 and f % fc == 0
    nt = t // tm
    return pl.pallas_call(
        functools.partial(_ffn_prompt_body, final=final, fc=fc),
        grid=(bsz, nt),
        in_specs=[pl.BlockSpec((tm, d), lambda b, i: (b * nt + i, 0)), _resident((1, d)), _resident(wup.shape),
                  _resident(cw.shape), _resident((1, f)), _resident(wd.shape),
                  pl.BlockSpec((1, CONV_W - 1, f), lambda b, i: (b, 0, 0)), _resident((1, d))],
        out_specs=[pl.BlockSpec((tm, d), lambda b, i: (b * nt + i, 0)),
                   pl.BlockSpec((1, CONV_W - 1, f), lambda b, i: (b, 0, 0))],
        out_shape=[jax.ShapeDtypeStruct((m, d), F32), jax.ShapeDtypeStruct((bsz, CONV_W - 1, f), F32)],
        scratch_shapes=[pltpu.VMEM((CONV_W - 1, f), F32)],
        compiler_params=_params(2, VMEM_BIG),
        name="ffn_prompt",
    )(x, g.reshape(1, d), wup, cw, cb.reshape(1, f), wd, st, gf.reshape(1, d))


def _ffn_decode_body(x_ref, g_ref, wup_ref, cw_ref, cb_ref, wd_ref, s0_ref, s1_ref, gf_ref, y_ref, gp_ref, *, final, fc):
    x = x_ref[...]
    xn = _rms(x, g_ref[...]).astype(BF16)

    def prev_fn(gp, sl):
        return s0_ref[:, sl], s1_ref[:, sl]

    def keep_fn(gp, sl):
        gp_ref[:, sl] = gp

    acc = _ffn_chunks(xn, x, wup_ref, cw_ref, cb_ref, wd_ref, fc, prev_fn, keep_fn)
    y_ref[...] = _rms(acc, gf_ref[...]) if final else acc


def _ffn_decode(x, g, wup, cw, cb, wd, st, gf, final, fc=256):
    m, d = x.shape
    f = cw.shape[1]
    y, gp = pl.pallas_call(
        functools.partial(_ffn_decode_body, final=final, fc=fc),
        grid=(1,),
        in_specs=[_resident((m, d)), _resident((1, d)), _resident(wup.shape), _resident(cw.shape), _resident((1, f)),
                  _resident(wd.shape), _resident((m, f)), _resident((m, f)), _resident((1, d))],
        out_specs=[pl.BlockSpec((m, d), lambda i: (0, 0)), pl.BlockSpec((m, f), lambda i: (0, 0))],
        out_shape=[jax.ShapeDtypeStruct((m, d), F32), jax.ShapeDtypeStruct((m, f), F32)],
        compiler_params=_params(1, VMEM_BIG),
        name="ffn_decode",
    )(x, g.reshape(1, d), wup, cw, cb.reshape(1, f), wd, st[:, 0], st[:, 1], gf.reshape(1, d))
    return y, jnp.stack([st[:, 1], gp], axis=1)


def _gla_finish(o, gg, gn):
    return _rms(o, gn) * (gg * _sigmoid(gg))


def _gla_body(z_ref, zs_ref, wa_ref, ba_ref, gn_ref, s0_ref, o_ref, sfin_ref, s_scr):
    t = pl.program_id(1)
    c = z_ref.shape[0]
    hk = GLA_HEADS * GLA_DK

    @pl.when(t == 0)
    def _():
        s_scr[...] = s0_ref[0]

    la = _logsigmoid(_mm_hp(zs_ref[...], wa_ref[...]) + ba_ref[...]) * (1.0 / GLA_TAU)
    r, cidx = _iota((c, c), 0), _iota((c, c), 1)
    tri = r >= cidx
    cum = _mm_sel(jnp.where(tri, 1.0, 0.0).astype(BF16), la)
    last = cum[c - 1:c, :]
    eq, ek, ekl, el = jnp.exp(cum), jnp.exp(-cum), jnp.exp(last - cum), jnp.exp(last)
    gn = gn_ref[...]
    for h in range(GLA_HEADS):
        ks = slice(h * GLA_DK, (h + 1) * GLA_DK)
        vs = slice(2 * hk + h * GLA_DV, 2 * hk + (h + 1) * GLA_DV)
        gs = slice(2 * hk + GLA_HEADS * GLA_DV + h * GLA_DV, 2 * hk + GLA_HEADS * GLA_DV + (h + 1) * GLA_DV)
        q = z_ref[:, ks] * (GLA_DK ** -0.5)
        k = z_ref[:, hk + h * GLA_DK:hk + (h + 1) * GLA_DK]
        v = z_ref[:, vs]
        qt = (q * eq[:, ks]).astype(BF16)
        s_old = s_scr[h]
        att = jnp.where(tri, _mm(qt, (k * ek[:, ks]).astype(BF16), _NT), 0.0)
        vb = v.astype(BF16)
        o = _mm(qt, s_old.astype(BF16)) + _mm(att.astype(BF16), vb)
        ecol = jnp.sum(jnp.where(r == cidx, jnp.broadcast_to(el[:, ks], (c, c)), 0.0), axis=1, keepdims=True)
        s_scr[h] = ecol * s_old + _mm((k * ekl[:, ks]).astype(BF16), vb, _TN)
        o_ref[:, h * GLA_DV:(h + 1) * GLA_DV] = _gla_finish(o, z_ref[:, gs], gn)

    @pl.when(t == pl.num_programs(1) - 1)
    def _():
        sfin_ref[0] = s_scr[...]


def _gla_prompt(z, zs, wa, ba, gn, s0, bsz):
    m = z.shape[0]
    t = m // bsz
    c = GLA_CHUNK
    assert t % c == 0 and GLA_DK == c
    nc = t // c
    hk = GLA_HEADS * GLA_DK
    return pl.pallas_call(
        _gla_body,
        grid=(bsz, nc),
        in_specs=[pl.BlockSpec((c, GLA_Z_W), lambda b, i: (b * nc + i, 0)),
                  pl.BlockSpec((c, LANE), lambda b, i: (b * nc + i, 0)),
                  _resident((LANE, hk)), _resident((1, hk)), _resident((1, GLA_DV)),
                  pl.BlockSpec((1, GLA_HEADS, GLA_DK, GLA_DV), lambda b, i: (b, 0, 0, 0))],
        out_specs=[pl.BlockSpec((c, GLA_HEADS * GLA_DV), lambda b, i: (b * nc + i, 0)),
                   pl.BlockSpec((1, GLA_HEADS, GLA_DK, GLA_DV), lambda b, i: (b, 0, 0, 0))],
        out_shape=[jax.ShapeDtypeStruct((m, GLA_HEADS * GLA_DV), F32),
                   jax.ShapeDtypeStruct((bsz, GLA_HEADS, GLA_DK, GLA_DV), F32)],
        scratch_shapes=[pltpu.VMEM((GLA_HEADS, GLA_DK, GLA_DV), F32)],
        compiler_params=_params(2),
        name="gla_prompt",
    )(z, zs, wa, ba, gn, s0)


def _gla_decode_body(z_ref, zs_ref, wa_ref, ba_ref, gn_ref, s_ref, o_ref, sn_ref):
    hk = GLA_HEADS * GLA_DK
    rows = SUBLANE
    z = jnp.broadcast_to(z_ref[0], (rows, GLA_Z_W))
    ga = jnp.broadcast_to(zs_ref[0], (rows, LANE))
    la = _logsigmoid(_mm_hp(ga, wa_ref[...]) + ba_ref[...]) * (1.0 / GLA_TAU)
    ea = jnp.exp(la)
    r, cidx = _iota((GLA_DK, GLA_DK), 0), _iota((GLA_DK, GLA_DK), 1)
    row0 = _iota((rows, GLA_DK), 0) == 0
    gn = gn_ref[...]
    for h in range(GLA_HEADS):
        ks = slice(h * GLA_DK, (h + 1) * GLA_DK)
        q = z[:, ks] * (GLA_DK ** -0.5)
        k = z[:, hk + h * GLA_DK:hk + (h + 1) * GLA_DK]
        v = z[:, 2 * hk + h * GLA_DV:2 * hk + (h + 1) * GLA_DV]
        gg = z[:, 2 * hk + GLA_HEADS * GLA_DV + h * GLA_DV:2 * hk + GLA_HEADS * GLA_DV + (h + 1) * GLA_DV]
        s_old = s_ref[0, h]
        o = _mm_hp(q * ea[:, ks], s_old) + jnp.sum(q * k, axis=1, keepdims=True) * v
        ecol = jnp.sum(jnp.where(r == cidx, jnp.broadcast_to(ea[0:1, ks], (GLA_DK, GLA_DK)), 0.0), axis=1, keepdims=True)
        sn_ref[0, h] = ecol * s_old + _mm_hp(jnp.where(row0, k, 0.0), v, _TN)
        o_ref[0, :, h * GLA_DV:(h + 1) * GLA_DV] = _gla_finish(o, gg, gn)[0:1]


def _gla_decode(z, zs, wa, ba, gn, s0):
    b = z.shape[0]
    hk = GLA_HEADS * GLA_DK
    st = (1, GLA_HEADS, GLA_DK, GLA_DV)
    o, sn = pl.pallas_call(
        _gla_decode_body,
        grid=(b,),
        in_specs=[pl.BlockSpec((1, 1, GLA_Z_W), lambda i: (i, 0, 0)), pl.BlockSpec((1, 1, LANE), lambda i: (i, 0, 0)),
                  _resident((LANE, hk)), _resident((1, hk)), _resident((1, GLA_DV)),
                  pl.BlockSpec(st, lambda i: (i, 0, 0, 0))],
        out_specs=[pl.BlockSpec((1, 1, GLA_HEADS * GLA_DV), lambda i: (i, 0, 0)), pl.BlockSpec(st, lambda i: (i, 0, 0, 0))],
        out_shape=[jax.ShapeDtypeStruct((b, 1, GLA_HEADS * GLA_DV), F32), jax.ShapeDtypeStruct((b,) + st[1:], F32)],
        compiler_params=_params(1),
        name="gla_decode",
    )(z.reshape(b, 1, -1), zs.reshape(b, 1, -1), wa, ba, gn, s0)
    return o.reshape(b, -1), sn


def _mlstm_finish(hh, og, mn):
    return _rms(hh, mn) * _sigmoid(og)


def _mlstm_body(z_ref, zs_ref, bias_ref, mn_ref, c0_ref, m0_ref, h_ref, cfin_ref, mfin_ref, c_scr, m_scr):
    t = pl.program_id(1)
    L = z_ref.shape[0]
    hq = M_HEADS * M_DQK

    @pl.when(t == 0)
    def _():
        c_scr[...] = c0_ref[0]
        m_scr[...] = m0_ref[0]

    g = zs_ref[...] + bias_ref[...]
    lane = _iota((L, LANE), 1)
    gx = jnp.where((lane >= M_HEADS) & (lane < 2 * M_HEADS), _logsigmoid(g), g)
    r, cidx = _iota((L, L), 0), _iota((L, L), 1)
    tri = r >= cidx
    cum_c = _mm_sel(jnp.where(tri, 1.0, 0.0).astype(BF16), gx)
    rows = _mm_sel(jnp.where(r == cidx, 1.0, 0.0).astype(BF16), gx, _TN, x_is_rhs=False)
    cum_r = _mm_sel(jnp.where(r <= cidx, 1.0, 0.0).astype(BF16), gx, _TN, x_is_rhs=False)
    ones_col = jnp.where(_iota((L, LANE), 1) == 0, 1.0, 0.0).astype(BF16)
    m_all = m_scr[...]
    m_new_all = m_all
    for h in range(M_HEADS):
        q = z_ref[:, h * M_DQK:(h + 1) * M_DQK].astype(BF16)
        k = z_ref[:, hq + h * M_DQK:hq + (h + 1) * M_DQK] * (M_DQK ** -0.5)
        v = z_ref[:, 2 * hq + h * M_DV:2 * hq + (h + 1) * M_DV]
        og = z_ref[:, 2 * hq + M_HEADS * M_DV + h * M_DV:2 * hq + M_HEADS * M_DV + (h + 1) * M_DV]
        va = jnp.concatenate([v.astype(BF16), ones_col], axis=1)
        ic_r, ic_c = rows[h:h + 1, :], gx[:, h:h + 1]
        cr, cc = cum_r[M_HEADS + h:M_HEADS + h + 1, :], cum_c[:, M_HEADS + h:M_HEADS + h + 1]
        m_old = m_all[0:1, h:h + 1]
        dlog = jnp.where(tri, cc - cr + ic_r, -jnp.inf)
        inter = cc + m_old
        mi = jnp.maximum(inter, jnp.max(dlog, axis=1, keepdims=True))
        w = jnp.exp(dlog - mi)
        wi = jnp.exp(inter - mi)
        s = _mm(q, k.astype(BF16), _NT) * w
        c_old = c_scr[h]
        num = wi * _mm(q, c_old.astype(BF16)) + _mm(s.astype(BF16), va)
        qn = num[:, M_DV:M_DV + 1]
        hh = num[:, :M_DV] / jnp.maximum(jnp.abs(qn), jnp.exp(-mi))
        last = cc[L - 1:L, :]
        gl = last - cc + ic_c
        m_new = jnp.maximum(last + m_old, jnp.max(gl, axis=0, keepdims=True))
        wj = jnp.exp(gl - m_new)
        keep = jnp.exp(last + m_old - m_new)
        c_scr[h] = keep * c_old + _mm((wj * k).astype(BF16), va, _TN)
        m_new_all = jnp.where(_iota((1, LANE), 1) == h, m_new, m_new_all)
        h_ref[:, h * M_DV:(h + 1) * M_DV] = _mlstm_finish(hh, og, mn_ref[:, h * M_DV:(h + 1) * M_DV])
    m_scr[...] = m_new_all

    @pl.when(t == pl.num_programs(1) - 1)
    def _():
        cfin_ref[0] = c_scr[...]
        mfin_ref[0] = m_scr[...]


def _mlstm_prompt(z, zs, bias, mn, c0, m0, bsz):
    m = z.shape[0]
    t = m // bsz
    L = M_CHUNK
    assert t % L == 0
    nc = t // L
    st = (1, M_HEADS, M_DQK, M_AUG)
    return pl.pallas_call(
        _mlstm_body,
        grid=(bsz, nc),
        in_specs=[pl.BlockSpec((L, M_Z_W), lambda b, i: (b * nc + i, 0)), pl.BlockSpec((L, LANE), lambda b, i: (b * nc + i, 0)),
                  _resident((1, LANE)), _resident((1, M_HEADS * M_DV)),
                  pl.BlockSpec(st, lambda b, i: (b, 0, 0, 0)), pl.BlockSpec((1, 1, LANE), lambda b, i: (b, 0, 0))],
        out_specs=[pl.BlockSpec((L, M_HEADS * M_DV), lambda b, i: (b * nc + i, 0)),
                   pl.BlockSpec(st, lambda b, i: (b, 0, 0, 0)), pl.BlockSpec((1, 1, LANE), lambda b, i: (b, 0, 0))],
        out_shape=[jax.ShapeDtypeStruct((m, M_HEADS * M_DV), F32), jax.ShapeDtypeStruct((bsz,) + st[1:], F32),
                   jax.ShapeDtypeStruct((bsz, 1, LANE), F32)],
        scratch_shapes=[pltpu.VMEM(st[1:], F32), pltpu.VMEM((1, LANE), F32)],
        compiler_params=_params(2),
        name="mlstm_prompt",
    )(z, zs, bias, mn, c0, m0)


def _mlstm_decode_body(z_ref, zs_ref, bias_ref, mn_ref, c_ref, n_ref, m_ref, h_ref, cn_ref, nn_ref, mo_ref):
    hq = M_HEADS * M_DQK
    rows = SUBLANE
    z = jnp.broadcast_to(z_ref[0], (rows, M_Z_W))
    g = zs_ref[0] + bias_ref[...]
    m_all = m_ref[0]
    m_new_all = m_all
    row0 = _iota((rows, M_DQK), 0) == 0
    for h in range(M_HEADS):
        q = z[:, h * M_DQK:(h + 1) * M_DQK]
        k = z[:, hq + h * M_DQK:hq + (h + 1) * M_DQK] * (M_DQK ** -0.5)
        v = z[:, 2 * hq + h * M_DV:2 * hq + (h + 1) * M_DV]
        og = z[:, 2 * hq + M_HEADS * M_DV + h * M_DV:2 * hq + M_HEADS * M_DV + (h + 1) * M_DV]
        ic = g[:, h:h + 1]
        fl = _logsigmoid(g[:, M_HEADS + h:M_HEADS + h + 1])
        m_old = m_all[:, h:h + 1]
        c_old, n_old = c_ref[0, h], n_ref[0, h:h + 1, :]
        mi = jnp.maximum(fl + m_old, ic)
        w = jnp.exp(ic - mi)
        wi = jnp.exp(fl + m_old - mi)
        s = jnp.sum(q * k, axis=1, keepdims=True) * w
        num = wi * _mm_hp(q, c_old) + s * v
        qn = wi * jnp.sum(q * n_old, axis=1, keepdims=True) + s
        hh = num / jnp.maximum(jnp.abs(qn), jnp.exp(-mi))
        cn_ref[0, h] = wi * c_old + w * _mm_hp(jnp.where(row0, k, 0.0), v, _TN)
        nn_ref[0, h:h + 1, :] = wi * n_old + w * k[0:1]
        m_new_all = jnp.where(_iota((1, LANE), 1) == h, mi, m_new_all)
        h_ref[0, :, h * M_DV:(h + 1) * M_DV] = _mlstm_finish(hh, og, mn_ref[:, h * M_DV:(h + 1) * M_DV])[0:1]
    mo_ref[0] = m_new_all


def _mlstm_decode(z, zs, bias, mn, c0, n0, m0):
    b = z.shape[0]
    cs, ns = (1, M_HEADS, M_DQK, M_DV), (1, M_HEADS, M_DQK)
    m0p = jnp.pad(m0, ((0, 0), (0, LANE - M_HEADS))).reshape(b, 1, LANE)
    row3 = lambda n: pl.BlockSpec((1, 1, n), lambda i: (i, 0, 0))
    h, cn, nn, mo = pl.pallas_call(
        _mlstm_decode_body,
        grid=(b,),
        in_specs=[row3(M_Z_W), row3(LANE), _resident((1, LANE)), _resident((1, M_HEADS * M_DV)),
                  pl.BlockSpec(cs, lambda i: (i, 0, 0, 0)), pl.BlockSpec(ns, lambda i: (i, 0, 0)), row3(LANE)],
        out_specs=[row3(M_HEADS * M_DV), pl.BlockSpec(cs, lambda i: (i, 0, 0, 0)), pl.BlockSpec(ns, lambda i: (i, 0, 0)), row3(LANE)],
        out_shape=[jax.ShapeDtypeStruct((b, 1, M_HEADS * M_DV), F32), jax.ShapeDtypeStruct((b,) + cs[1:], F32),
                   jax.ShapeDtypeStruct((b,) + ns[1:], F32), jax.ShapeDtypeStruct((b, 1, LANE), F32)],
        compiler_params=_params(1),
        name="mlstm_decode",
    )(z.reshape(b, 1, -1), zs.reshape(b, 1, -1), bias, mn, c0, n0, m0p)
    return h.reshape(b, -1), cn, nn, mo[:, 0, :M_HEADS]


def _compress_dense_body(x_ref, pe_ref, w1_ref, w2_ref, o_ref, acc):
    kk = pl.program_id(1)

    @pl.when(kk == 0)
    def _():
        acc[...] = jnp.zeros_like(acc)

    acc[...] += _mm((x_ref[...] + pe_ref[...]).astype(BF16), w1_ref[...])

    @pl.when(kk == pl.num_programs(1) - 1)
    def _():
        o_ref[...] = _mm(_gelu(acc[...]).astype(BF16), w2_ref[...])


def _compress_dense(x, pe, w1, w2, tk=2048):
    r, kdim = x.shape
    tr = r if r <= 512 else 512
    assert r % tr == 0 and kdim % tk == 0
    return pl.pallas_call(
        _compress_dense_body,
        grid=(r // tr, kdim // tk),
        in_specs=[pl.BlockSpec((tr, tk), lambda i, k: (i, k)), pl.BlockSpec((1, tk), lambda i, k: (0, k)),
                  pl.BlockSpec((tk, NSA_KV_W), lambda i, k: (k, 0)), _resident(w2.shape)],
        out_specs=pl.BlockSpec((tr, NSA_KV_W), lambda i, k: (i, 0)),
        out_shape=jax.ShapeDtypeStruct((r, NSA_KV_W), F32),
        scratch_shapes=[pltpu.VMEM((tr, NSA_KV_W), F32)],
        compiler_params=_params(2),
        name="nsa_compress",
    )(x, pe, w1, w2)


_PAGES_PER_STEP = 8


_D_PER_STEP = 2


def _compress_paged_body(pt_ref, *refs, n_pages):
    page_refs = refs[:_PAGES_PER_STEP]
    pe_ref, w1_ref, w2_ref, o_ref, xs = refs[_PAGES_PER_STEP:]
    g = pl.program_id(1)
    for p, pr in enumerate(page_refs):
        r0 = pl.multiple_of((g * _PAGES_PER_STEP + p) * NSA_KV_W, NSA_KV_W)
        xs[pl.ds(r0, NSA_KV_W), :] = pr[0]

    @pl.when(g == pl.num_programs(1) - 1)
    def _():
        kvd = NSA_KV_HEADS * NSA_DH
        for c in range(2):
            def step(j, acc):
                parts = []
                for dd in range(_D_PER_STEP):
                    d = j * _D_PER_STEP + dd
                    pe_row = pe_ref[pl.ds(c * NSA_DH + d, 1), :]
                    rows = [xs[pl.ds(c * kvd + k * NSA_DH + d, n_pages, stride=NSA_KV_W), :] + pe_row
                            for k in range(NSA_KV_HEADS)]
                    parts.append(jnp.concatenate(rows, axis=0).astype(BF16))
                return acc + _mm(jnp.concatenate(parts, axis=1), w1_ref[c, j])
            hid = lax.fori_loop(0, NSA_DH // _D_PER_STEP, step, jnp.zeros((NSA_KV_HEADS * n_pages, PAGE_SIZE), F32))
            out = _mm(_gelu(hid).astype(BF16), w2_ref[c])
            o_ref[c * NSA_KV_HEADS * n_pages:(c + 1) * NSA_KV_HEADS * n_pages, :] = out


def _compress_paged(pool_t, page_table, pe_t, w1_t, w2_t):
    b, n_pages = page_table.shape
    pps = _PAGES_PER_STEP
    assert n_pages % pps == 0 and _BLOCKS_PER_PAGE == 2
    rows_out = 2 * NSA_KV_HEADS * n_pages

    def page_spec(p):
        return pl.BlockSpec((1, NSA_KV_W, PAGE_SIZE), lambda i, g, pt: (pt[i, g * pps + p], 0, 0))

    def fixed(a):
        nd = a.ndim
        return pl.BlockSpec(a.shape, lambda i, g, pt: (0,) * nd, pipeline_mode=pl.Buffered(1))

    grid_spec = pltpu.PrefetchScalarGridSpec(
        num_scalar_prefetch=1,
        grid=(b, n_pages // pps),
        in_specs=[page_spec(p) for p in range(pps)] + [fixed(pe_t), fixed(w1_t), fixed(w2_t)],
        out_specs=pl.BlockSpec((rows_out, PAGE_SIZE), lambda i, g, pt: (i, 0)),
        scratch_shapes=[pltpu.VMEM((n_pages * NSA_KV_W, PAGE_SIZE), F32)],
    )
    return pl.pallas_call(
        functools.partial(_compress_paged_body, n_pages=n_pages),
        grid_spec=grid_spec,
        out_shape=jax.ShapeDtypeStruct((b * rows_out, PAGE_SIZE), F32),
        compiler_params=_params(2, VMEM_BIG),
        name="nsa_compress_paged",
    )(page_table, *([pool_t] * pps), pe_t, w1_t, w2_t)


def _gate_cols(zs_ref, gb_ref):
    return _sigmoid(zs_ref[...] + gb_ref[...])


def _cmp_sel_body(q_ref, kvc_ref, zs_ref, gb_ref, ocmp_ref, nm_ref, sc_scr):
    qi = pl.program_id(1)
    tq = q_ref.shape[0]
    nbp = kvc_ref.shape[0]
    t0 = qi * tq
    tpos = t0 + _iota((nbp, tq), 1)
    blk = _iota((nbp, tq), 0)
    cur = jnp.right_shift(tpos, NSA_BLOCK.bit_length() - 1)
    vis = blk * NSA_BLOCK + (NSA_BLOCK - 1) <= tpos
    forced = (blk == 0) | (blk == cur) | (blk == cur - 1)
    allowed = blk <= cur
    gates = _gate_cols(zs_ref, gb_ref)
    eye = jnp.where(_iota((nbp, nbp), 0) == _iota((nbp, nbp), 1), 1.0, 0.0).astype(BF16)
    n_live = (t0 + tq - 1) // NSA_BLOCK + 1
    for kh in range(NSA_KV_HEADS):
        kc = kvc_ref[:, kh * NSA_DH:(kh + 1) * NSA_DH].astype(BF16)
        vc = kvc_ref[:, NSA_KV_HEADS * NSA_DH + kh * NSA_DH:NSA_KV_HEADS * NSA_DH + (kh + 1) * NSA_DH].astype(BF16)
        imp = jnp.zeros((nbp, tq), F32)
        for gi in range(NSA_GROUP):
            h = kh * NSA_GROUP + gi
            q = (q_ref[:, h * NSA_DH:(h + 1) * NSA_DH] * (NSA_DH ** -0.5)).astype(BF16)
            s = jnp.where(vis, _mm(kc, q, _NT), -jnp.inf)
            m = jnp.max(s, axis=0, keepdims=True)
            m = jnp.where(m == -jnp.inf, 0.0, m)
            e = jnp.exp(s - m)
            p = e / jnp.maximum(jnp.sum(e, axis=0, keepdims=True), TINY)
            imp = imp + p
            o = _mm(p.astype(BF16), vc, _TN)
            gc = GATE_OFF + h * NSA_BRANCHES
            ocmp_ref[:, h * NSA_DH:(h + 1) * NSA_DH] = o * gates[:, gc:gc + 1]
        score = jnp.where(forced, jnp.inf, jnp.where(allowed, imp, -jnp.inf))
        sc_scr[...] = score

        def rank_step(j, cnt):
            row = sc_scr[pl.ds(j, 1), :]
            tie = jnp.where(j < blk, 1.0, 0.0)
            return cnt + jnp.where(row > score, 1.0, jnp.where(row == score, tie, 0.0))

        cnt = lax.fori_loop(0, n_live, rank_step, jnp.zeros((nbp, tq), F32))
        nm_t = jnp.where(allowed, jnp.where(cnt < NSA_TOP_N, 0.0, -MASK_BIG), -MASK_BIG).astype(BF16)
        nm_ref[:, kh * nbp:(kh + 1) * nbp] = _mm(nm_t, eye, _TN).astype(BF16)


def _cmp_sel(q, kvc, zs, gb, bsz, nbp, tq):
    m = q.shape[0]
    t = m // bsz
    tq = min(tq, t)
    nt = t // tq
    hd = NSA_HEADS * NSA_DH
    return pl.pallas_call(
        _cmp_sel_body,
        grid=(bsz, nt),
        in_specs=[pl.BlockSpec((tq, hd), lambda b, i: (b * nt + i, 0)), pl.BlockSpec((nbp, NSA_KV_W), lambda b, i: (b, 0)),
                  pl.BlockSpec((tq, LANE), lambda b, i: (b * nt + i, 0)), _resident((1, LANE))],
        out_specs=[pl.BlockSpec((tq, hd), lambda b, i: (b * nt + i, 0)),
                   pl.BlockSpec((tq, NSA_KV_HEADS * nbp), lambda b, i: (b * nt + i, 0))],
        out_shape=[jax.ShapeDtypeStruct((m, hd), F32), jax.ShapeDtypeStruct((m, NSA_KV_HEADS * nbp), BF16)],
        scratch_shapes=[pltpu.VMEM((nbp, tq), F32)],
        compiler_params=_params(2),
        name="nsa_cmp_select",
    )(q, kvc, zs, gb)


def _sel_attn_body(q_ref, nm_ref, ka_ref, v_ref, zs_ref, gb_ref, o_ref, m_scr, l_scr, acc_scr, *, tk):
    qi = pl.program_id(1)
    tq = q_ref.shape[0]
    nbp = nm_ref.shape[1] // NSA_KV_HEADS
    rows = NSA_GROUP * tq
    t0 = qi * tq
    n_before = t0 // tk
    gates = _gate_cols(zs_ref, gb_ref)
    for kh in range(NSA_KV_HEADS):
        nm = nm_ref[:, kh * nbp:(kh + 1) * nbp]
        qa = jnp.concatenate(
            [jnp.concatenate([nm, (q_ref[:, h * NSA_DH:(h + 1) * NSA_DH] * (NSA_DH ** -0.5)).astype(BF16)], axis=1)
             for h in range(kh * NSA_GROUP, (kh + 1) * NSA_GROUP)], axis=0)
        m_scr[...] = jnp.full((rows, 1), -jnp.inf, F32)
        l_scr[...] = jnp.zeros((rows, 1), F32)
        acc_scr[...] = jnp.zeros((rows, NSA_DH), F32)

        def tile(kt, causal):
            k0 = pl.multiple_of(kt * tk, tk)
            s = _mm(qa, ka_ref[0, kh, pl.ds(k0, tk), :], _NT)
            if causal:
                kpos = k0 + _iota((rows, tk), 1)
                tpos = t0 + jnp.bitwise_and(_iota((rows, tk), 0), tq - 1)
                s = jnp.where(kpos <= tpos, s, -jnp.inf)
            m_old = m_scr[...]
            m_new = jnp.maximum(m_old, jnp.max(s, axis=1, keepdims=True))
            alpha = jnp.exp(m_old - m_new)
            p = jnp.exp(s - m_new)
            l_scr[...] = alpha * l_scr[...] + jnp.sum(p, axis=1, keepdims=True)
            acc_scr[...] = alpha * acc_scr[...] + _mm(p.astype(BF16), v_ref[0, kh, pl.ds(k0, tk), :])
            m_scr[...] = m_new

        def past_tile(kt, carry):
            tile(kt, False)
            return carry

        lax.fori_loop(0, n_before, past_tile, 0)
        tile(n_before, True)
        o = acc_scr[...] / l_scr[...]
        for gi in range(NSA_GROUP):
            h = kh * NSA_GROUP + gi
            gc = GATE_OFF + h * NSA_BRANCHES + 1
            o_ref[:, h * NSA_DH:(h + 1) * NSA_DH] = o[gi * tq:(gi + 1) * tq] * gates[:, gc:gc + 1]


def _sel_attn(q, nm, kaug, vs, zs, gb, bsz, tq, tk):
    m = q.shape[0]
    t = m // bsz
    tq, tk = min(tq, t), min(tk, t)
    assert t % tq == 0 and t % tk == 0 and tk % tq == 0 and tq & (tq - 1) == 0
    nt = t // tq
    hd = NSA_HEADS * NSA_DH
    rows = NSA_GROUP * tq
    return pl.pallas_call(
        functools.partial(_sel_attn_body, tk=tk),
        grid=(bsz, nt),
        in_specs=[pl.BlockSpec((tq, hd), lambda b, i: (b * nt + i, 0)), pl.BlockSpec((tq, nm.shape[1]), lambda b, i: (b * nt + i, 0)),
                  pl.BlockSpec((1,) + kaug.shape[1:], lambda b, i: (b, 0, 0, 0)),
                  pl.BlockSpec((1,) + vs.shape[1:], lambda b, i: (b, 0, 0, 0)),
                  pl.BlockSpec((tq, LANE), lambda b, i: (b * nt + i, 0)), _resident((1, LANE))],
        out_specs=pl.BlockSpec((tq, hd), lambda b, i: (b * nt + i, 0)),
        out_shape=jax.ShapeDtypeStruct((m, hd), F32),
        scratch_shapes=[pltpu.VMEM((rows, 1), F32), pltpu.VMEM((rows, 1), F32), pltpu.VMEM((rows, NSA_DH), F32)],
        compiler_params=_params(2, VMEM_BIG),
        name="nsa_selected",
    )(q, nm, kaug, vs, zs, gb)


def _win_attn_body(q_ref, k_ref, v_ref, zs_ref, gb_ref, o_ref, *, span):
    qi = pl.program_id(1)
    tq = q_ref.shape[0]
    rows = NSA_GROUP * tq
    t0 = qi * tq
    start = pl.multiple_of(jnp.maximum(t0 + tq - span, 0), tq)
    gates = _gate_cols(zs_ref, gb_ref)
    kpos = start + _iota((rows, span), 1)
    tpos = t0 + jnp.bitwise_and(_iota((rows, span), 0), tq - 1)
    ok = (kpos <= tpos) & (tpos - kpos < NSA_WINDOW)
    for kh in range(NSA_KV_HEADS):
        qa = jnp.concatenate([(q_ref[:, h * NSA_DH:(h + 1) * NSA_DH] * (NSA_DH ** -0.5)).astype(BF16)
                              for h in range(kh * NSA_GROUP, (kh + 1) * NSA_GROUP)], axis=0)
        s = jnp.where(ok, _mm(qa, k_ref[0, kh, pl.ds(start, span), :], _NT), -jnp.inf)
        e = jnp.exp(s - jnp.max(s, axis=1, keepdims=True))
        o = _mm(e.astype(BF16), v_ref[0, kh, pl.ds(start, span), :]) / jnp.sum(e, axis=1, keepdims=True)
        for gi in range(NSA_GROUP):
            h = kh * NSA_GROUP + gi
            gc = GATE_OFF + h * NSA_BRANCHES + 2
            o_ref[:, h * NSA_DH:(h + 1) * NSA_DH] = o[gi * tq:(gi + 1) * tq] * gates[:, gc:gc + 1]


def _win_attn(q, kw, vw, zs, gb, bsz, tq):
    m = q.shape[0]
    t = m // bsz
    tq = min(tq, t)
    span = min(NSA_WINDOW + tq, t)
    assert t % tq == 0 and tq & (tq - 1) == 0
    nt = t // tq
    hd = NSA_HEADS * NSA_DH
    return pl.pallas_call(
        functools.partial(_win_attn_body, span=span),
        grid=(bsz, nt),
        in_specs=[pl.BlockSpec((tq, hd), lambda b, i: (b * nt + i, 0)),
                  pl.BlockSpec((1,) + kw.shape[1:], lambda b, i: (b, 0, 0, 0)),
                  pl.BlockSpec((1,) + vw.shape[1:], lambda b, i: (b, 0, 0, 0)),
                  pl.BlockSpec((tq, LANE), lambda b, i: (b * nt + i, 0)), _resident((1, LANE))],
        out_specs=pl.BlockSpec((tq, hd), lambda b, i: (b * nt + i, 0)),
        out_shape=jax.ShapeDtypeStruct((m, hd), F32),
        compiler_params=_params(2),
        name="nsa_window",
    )(q, kw, vw, zs, gb)


def _decode_forced(n_past_blk):
    cur = n_past_blk
    return sorted({0, cur - 1, cur} - {-1})


def _cmp_decode_body(q_ref, kvc_ref, gz_ref, gb_ref, o_ref, idx_ref, *, past, n_pick):
    n_pages = kvc_ref.shape[0] // (2 * NSA_KV_HEADS)
    nb = _BLOCKS_PER_PAGE * n_pages

    def block_id(shape):
        pos = _iota(shape, 1)
        page = jnp.where(pos >= n_pages, pos - n_pages, pos)
        return page * _BLOCKS_PER_PAGE + jnp.where(pos >= n_pages, 1, 0)

    lane = block_id((NSA_HEADS, nb))
    hrow = _iota((NSA_HEADS, nb), 0)
    vis = lane * NSA_BLOCK + (NSA_BLOCK - 1) <= past
    q = (q_ref[0] * (NSA_DH ** -0.5)).astype(BF16)
    gates = _sigmoid(gz_ref[0] + gb_ref[...])
    cur = past // NSA_BLOCK
    o_all = jnp.zeros((NSA_HEADS, NSA_DH), F32)
    idx_all = jnp.zeros((SUBLANE, LANE), F32)
    orow = _iota((NSA_HEADS, NSA_DH), 0)
    slot_r, slot_c = _iota((SUBLANE, LANE), 0), _iota((SUBLANE, LANE), 1)
    l1 = block_id((1, nb))
    l1f = l1.astype(F32)
    forced = (l1 == 0) | (l1 == cur) | (l1 == cur - 1)
    for kh in range(NSA_KV_HEADS):
        kc = kvc_ref[kh * n_pages:(kh + 1) * n_pages, :].astype(BF16)
        vc = kvc_ref[(NSA_KV_HEADS + kh) * n_pages:(NSA_KV_HEADS + kh + 1) * n_pages, :].astype(BF16)
        s = jnp.concatenate([_mm(q, kc[:, j * NSA_DH:(j + 1) * NSA_DH], _NT) for j in range(_BLOCKS_PER_PAGE)], axis=1)
        s = jnp.where(vis, s, -jnp.inf)
        m = jnp.max(s, axis=1, keepdims=True)
        m = jnp.where(m == -jnp.inf, 0.0, m)
        e = jnp.exp(s - m)
        p = e / jnp.maximum(jnp.sum(e, axis=1, keepdims=True), TINY)
        mine = (hrow >= kh * NSA_GROUP) & (hrow < (kh + 1) * NSA_GROUP)
        pb = p.astype(BF16)
        o_kh = _mm(pb[:, :n_pages], vc[:, :NSA_DH])
        for j in range(1, _BLOCKS_PER_PAGE):
            o_kh = o_kh + _mm(pb[:, j * n_pages:(j + 1) * n_pages], vc[:, j * NSA_DH:(j + 1) * NSA_DH])
        o_all = jnp.where((orow >= kh * NSA_GROUP) & (orow < (kh + 1) * NSA_GROUP), o_kh, o_all)
        imp = jnp.sum(jnp.where(mine, p, 0.0), axis=0, keepdims=True)
        score = jnp.where(forced, -jnp.inf, imp)
        for r in range(n_pick):
            mx = jnp.max(score, axis=1, keepdims=True)
            pick = jnp.min(jnp.where(score == mx, l1f, float(nb)), axis=1, keepdims=True)
            score = jnp.where(l1f == pick, -jnp.inf, score)
            idx_all = jnp.where((slot_r == kh) & (slot_c == r), pick, idx_all)
    o_ref[0] = o_all * gates[:, 0:1]
    idx_ref[0] = idx_all.astype(jnp.int32)


def _cmp_decode(qh, kvc, gz, gb3, past, n_pick):
    b = qh.shape[0]
    rows = kvc.shape[0] // b
    return pl.pallas_call(
        functools.partial(_cmp_decode_body, past=past, n_pick=n_pick),
        grid=(b,),
        in_specs=[pl.BlockSpec((1, NSA_HEADS, NSA_DH), lambda i: (i, 0, 0)), pl.BlockSpec((rows, kvc.shape[1]), lambda i: (i, 0)),
                  pl.BlockSpec((1, NSA_HEADS, NSA_BRANCHES), lambda i: (i, 0, 0)), _resident((NSA_HEADS, NSA_BRANCHES))],
        out_specs=[pl.BlockSpec((1, NSA_HEADS, NSA_DH), lambda i: (i, 0, 0)), pl.BlockSpec((1, SUBLANE, LANE), lambda i: (i, 0, 0))],
        out_shape=[jax.ShapeDtypeStruct((b, NSA_HEADS, NSA_DH), F32), jax.ShapeDtypeStruct((b, SUBLANE, LANE), jnp.int32)],
        compiler_params=_params(1),
        name="nsa_cmp_decode",
    )(qh, kvc, gz, gb3)


def _selwin_decode_body(info_ref, q_ref, ns_ref, nw_ref, win_ref, *refs, n_shared, n_own, first_win_row):
    n_blk = n_shared + NSA_KV_HEADS * n_own
    page_refs = refs[:n_blk]
    gz_ref, gb_ref, o_ref, wout_ref = refs[n_blk:]
    i = pl.program_id(0)
    q = q_ref[0] * (NSA_DH ** -0.5)
    qb = q.astype(BF16)
    gates = _sigmoid(gz_ref[0] + gb_ref[...])
    hrow = _iota((NSA_HEADS, NSA_DH), 0)
    n_win = win_ref.shape[2]
    wlane = _iota((NSA_HEADS, n_win), 1)
    pblk = jnp.right_shift(_iota((NSA_HEADS, PAGE_SIZE), 1), NSA_BLOCK.bit_length() - 1)
    o_all = jnp.zeros((NSA_HEADS, NSA_DH), F32)
    voff = NSA_KV_HEADS * NSA_DH
    for kh in range(NSA_KV_HEADS):
        ksl = slice(kh * NSA_DH, (kh + 1) * NSA_DH)
        vsl = slice(voff + kh * NSA_DH, voff + (kh + 1) * NSA_DH)
        slots = list(range(n_shared)) + list(range(n_shared + kh * n_own, n_shared + (kh + 1) * n_own))
        s_parts = []
        for sl in slots:
            blk_in_page = jnp.bitwise_and(info_ref[i, sl], _BLOCKS_PER_PAGE - 1)
            sp = _mm(qb, page_refs[sl][0, ksl, :].astype(BF16))
            s_parts.append(jnp.where(pblk == blk_in_page, sp, -jnp.inf))
        kn, vn = ns_ref[0, :, ksl], ns_ref[0, :, vsl]
        sn = jnp.sum(q * kn, axis=1, keepdims=True)
        m = sn
        for sp in s_parts:
            m = jnp.maximum(m, jnp.max(sp, axis=1, keepdims=True))
        en = jnp.exp(sn - m)
        l, acc = en, en * vn
        for sl, sp in zip(slots, s_parts):
            e = jnp.exp(sp - m)
            l = l + jnp.sum(e, axis=1, keepdims=True)
            acc = acc + _mm(e.astype(BF16), page_refs[sl][0, vsl, :].astype(BF16), _NT)
        o_sel = acc / l
        sw = jnp.where(wlane >= first_win_row, _mm(qb, win_ref[0, ksl, :].astype(BF16)), -jnp.inf)
        kwn, vwn = nw_ref[0, :, ksl], nw_ref[0, :, vsl]
        swn = jnp.sum(q * kwn, axis=1, keepdims=True)
        mw = jnp.maximum(jnp.max(sw, axis=1, keepdims=True), swn)
        ew, ewn = jnp.exp(sw - mw), jnp.exp(swn - mw)
        o_win = (_mm(ew.astype(BF16), win_ref[0, vsl, :].astype(BF16), _NT) + ewn * vwn) / (jnp.sum(ew, axis=1, keepdims=True) + ewn)
        keep = (hrow >= kh * NSA_GROUP) & (hrow < (kh + 1) * NSA_GROUP)
        o_all = jnp.where(keep, gates[:, 1:2] * o_sel + gates[:, 2:3] * o_win, o_all)
    o_ref[0] = o_all
    r2, c2 = _iota((NSA_KV_W, NSA_KV_W), 0), _iota((NSA_KV_W, NSA_KV_W), 1)
    new_col = jnp.sum(jnp.where(r2 == c2, jnp.broadcast_to(nw_ref[0], (NSA_KV_W, NSA_KV_W)), 0.0), axis=1, keepdims=True)
    shifted = pltpu.roll(win_ref[0], n_win - 1, 1)
    wout_ref[0] = jnp.where(_iota((NSA_KV_W, n_win), 1) == n_win - 1, new_col, shifted)


def _selwin_decode(info, qh, new_s, new_w, win_t, pool_t, gz, gb3, n_shared, n_own, first_win_row):
    b = qh.shape[0]
    n_blk = info.shape[1]
    row3 = lambda n: pl.BlockSpec((1, 1, n), lambda i, r: (i, 0, 0))

    def page_spec(s):
        return pl.BlockSpec((1, NSA_KV_W, PAGE_SIZE), lambda i, r: (jnp.right_shift(r[i, s], _PAGE_SHIFT), 0, 0))

    win_spec = pl.BlockSpec((1,) + win_t.shape[1:], lambda i, r: (i, 0, 0))
    grid_spec = pltpu.PrefetchScalarGridSpec(
        num_scalar_prefetch=1,
        grid=(b,),
        in_specs=[pl.BlockSpec((1, NSA_HEADS, NSA_DH), lambda i, r: (i, 0, 0)), row3(NSA_KV_W), row3(NSA_KV_W), win_spec]
        + [page_spec(s) for s in range(n_blk)]
        + [pl.BlockSpec((1, NSA_HEADS, NSA_BRANCHES), lambda i, r: (i, 0, 0)),
           pl.BlockSpec((NSA_HEADS, NSA_BRANCHES), lambda i, r: (0, 0))],
        out_specs=[pl.BlockSpec((1, NSA_HEADS, NSA_DH), lambda i, r: (i, 0, 0)), win_spec],
    )
    return pl.pallas_call(
        functools.partial(_selwin_decode_body, n_shared=n_shared, n_own=n_own, first_win_row=first_win_row),
        grid_spec=grid_spec,
        out_shape=[jax.ShapeDtypeStruct((b, NSA_HEADS, NSA_DH), F32), jax.ShapeDtypeStruct(win_t.shape, F32)],
        compiler_params=_params(1),
        name="nsa_selwin_decode",
    )(info, qh, new_s.reshape(b, 1, -1), new_w.reshape(b, 1, -1), win_t, *([pool_t] * n_blk), gz, gb3)


def _feature_major(cache):
    n, rows = cache.shape[:2]
    return jnp.transpose(cache, (0, 2, 3, 4, 1)).reshape(n, NSA_KV_W, rows)


def _pad_cols(w, n):
    return jnp.pad(w, ((0, 0), (0, n - w.shape[1])))


def _even_w_in(w):
    sizes = (GLA_HEADS * GLA_DK, GLA_HEADS * GLA_DK, GLA_HEADS * GLA_DV, GLA_HEADS * GLA_DV, GLA_RANK,
             NSA_HEADS * NSA_DH, NSA_BRANCHES * NSA_KV_W, NSA_HEADS * NSA_BRANCHES)
    cuts = [0]
    for s in sizes:
        cuts.append(cuts[-1] + s)
    gq, gk, gv, gg, ga, nq, nkv, ng = (w[:, cuts[i]:cuts[i + 1]] for i in range(len(sizes)))
    small = _pad_cols(jnp.concatenate([ga, ng], axis=1), LANE)
    return jnp.concatenate([gq, gk, gv, gg, nq, nkv, small], axis=1).astype(BF16)


_EVEN_WIDTHS = (GLA_Z_W, NSA_HEADS * NSA_DH, NSA_KV_W, NSA_KV_W, NSA_KV_W, LANE)


def _odd_w_in(w):
    main = M_Z_W
    return jnp.concatenate([w[:, :main], _pad_cols(w[:, main:], LANE)], axis=1).astype(BF16)


_ODD_WIDTHS = (M_Z_W, LANE)


def _compress_weights(pe, w1, w2):
    eye_k = jnp.eye(NSA_KV_HEADS, dtype=F32)
    eye_c = jnp.eye(2, dtype=F32)
    w1big = jnp.einsum("csde,cx,ky->sckdxye", w1, eye_c, eye_k).reshape(NSA_BLOCK * NSA_KV_W, NSA_KV_W)
    w2big = jnp.einsum("ced,cx,ky->ckexyd", w2, eye_c, eye_k).reshape(NSA_KV_W, NSA_KV_W)
    pe_flat = jnp.broadcast_to(pe.transpose(1, 0, 2)[:, :, None, :], (NSA_BLOCK, 2, NSA_KV_HEADS, NSA_DH)).reshape(1, -1)
    return pe_flat, w1big.astype(BF16), w2big.astype(BF16)


def _compress_weights_paged(pe, w1, w2):
    eye_b = jnp.eye(_BLOCKS_PER_PAGE, dtype=F32)
    pe_t = jnp.tile(pe.transpose(0, 2, 1), (1, 1, _BLOCKS_PER_PAGE)).reshape(2 * NSA_DH, PAGE_SIZE)
    w1_t = jnp.einsum("csde,hx->cdhsxe", w1, eye_b).reshape(2, NSA_DH // _D_PER_STEP, _D_PER_STEP * PAGE_SIZE,
                                                            _BLOCKS_PER_PAGE * NSA_CMP_HID)
    w2_t = jnp.einsum("ced,hx->chexd", w2, eye_b).reshape(2, _BLOCKS_PER_PAGE * NSA_CMP_HID, _BLOCKS_PER_PAGE * NSA_DH)
    return pe_t, w1_t.astype(BF16), w2_t.astype(BF16)


def _gate_bias_row(gb):
    return jnp.pad(gb, (GATE_OFF, LANE - GATE_OFF - gb.shape[0])).reshape(1, LANE)


def _heads_major(kv, which):
    b, t, _ = kv.shape
    return kv.reshape(b, t, 2, NSA_KV_HEADS, NSA_DH)[:, :, which].transpose(0, 2, 1, 3).astype(BF16)


def _even_layer_prompt(x, bsz, g, prm):
    m = x.shape[0]
    t = m // bsz
    z, nq, kvc, kvs, kvw, zs = _norm_proj(x, g, prm["w_in"], _EVEN_WIDTHS, 256)
    o_gla, s_fin = _gla_prompt(z, zs, prm["wa"], prm["ba"], prm["gn"],
                               jnp.zeros((bsz, GLA_HEADS, GLA_DK, GLA_DV), F32), bsz)
    nb = t // NSA_BLOCK
    nbp = -(-nb // LANE) * LANE
    kvcmp = _compress_dense(kvc.reshape(bsz * nb, NSA_BLOCK * NSA_KV_W), prm["pe"], prm["w1"], prm["w2"])
    kvcmp = jnp.pad(kvcmp.reshape(bsz, nb, NSA_KV_W), ((0, 0), (0, nbp - nb), (0, 0))).reshape(bsz * nbp, NSA_KV_W)
    o_cmp, nm = _cmp_sel(nq, kvcmp, zs, prm["gb"], bsz, nbp, 128)
    onehot = (jnp.arange(t)[:, None] // NSA_BLOCK == jnp.arange(nbp)[None, :]).astype(BF16)
    ks = _heads_major(kvs.reshape(bsz, t, -1), 0)
    kaug = jnp.concatenate([jnp.broadcast_to(onehot, (bsz, NSA_KV_HEADS, t, nbp)), ks], axis=-1)
    o_sel = _sel_attn(nq, nm, kaug, _heads_major(kvs.reshape(bsz, t, -1), 1), zs, prm["gb"], bsz, 128, 512)
    kw3 = kvw.reshape(bsz, t, -1)
    o_win = _win_attn(nq, _heads_major(kw3, 0), _heads_major(kw3, 1), zs, prm["gb"], bsz, 128)
    y = _out_proj(x, [o_gla, o_cmp, o_sel, o_win], (1, 3), prm["w_out"], 512)
    kv_shape = (bsz, t, 2, NSA_KV_HEADS, NSA_DH)
    n_keep = min(NSA_WINDOW, t)
    return y, s_fin, kvc.reshape(kv_shape), kvs.reshape(kv_shape), kvw.reshape(kv_shape)[:, t - n_keep:]


def _even_layer_sample(x, g, prm, gla_state, cmp_pool, sel_pool, win_buf, page_table):
    b = x.shape[0]
    n_pages = page_table.shape[1]
    past = n_pages * PAGE_SIZE
    n_past_blk = past // NSA_BLOCK
    z, nq, kvc, kvs, kvw, zs = _norm_proj(x, g, prm["w_in"], _EVEN_WIDTHS, b)
    o_gla, s_new = _gla_decode(z, zs, prm["wa"], prm["ba"], prm["gn"], gla_state)
    kvcmp = _compress_paged(_feature_major(cmp_pool), page_table, prm["pe_t"], prm["w1_t"], prm["w2_t"])
    qh = nq.reshape(b, NSA_HEADS, NSA_DH)
    gz = zs[:, GATE_OFF:GATE_OFF + NSA_HEADS * NSA_BRANCHES].reshape(b, NSA_HEADS, NSA_BRANCHES)
    forced = _decode_forced(n_past_blk)
    n_pick = NSA_TOP_N - len(forced)
    assert n_past_blk - len(forced) + 1 >= n_pick
    o_cmp, idx = _cmp_decode(qh, kvcmp, gz, prm["gb3"], past, n_pick)
    shared = [f for f in forced if f < n_past_blk]
    logical = jnp.concatenate([jnp.broadcast_to(jnp.asarray(shared, jnp.int32), (b, len(shared))),
                               idx[:, :NSA_KV_HEADS, :n_pick].reshape(b, NSA_KV_HEADS * n_pick)], axis=1)
    per_page = _BLOCKS_PER_PAGE
    phys = jnp.take_along_axis(page_table, logical // per_page, axis=1) * per_page + logical % per_page
    win_keep = win_buf.shape[1]
    assert win_keep >= 1
    first_win_row = max(win_keep - NSA_WINDOW + 1, 0)
    o_sw, win_new_t = _selwin_decode(phys.astype(jnp.int32), qh, kvs, kvw, _feature_major(win_buf), _feature_major(sel_pool),
                                     gz, prm["gb3"], len(shared), n_pick, first_win_row)
    hd = NSA_HEADS * NSA_DH
    y = _out_proj(x, [o_gla, o_cmp.reshape(b, hd), o_sw.reshape(b, hd)], (1, 2), prm["w_out"], b)
    kv_shape = (b, 1, 2, NSA_KV_HEADS, NSA_DH)
    win_new = jnp.transpose(win_new_t.reshape(b, 2, NSA_KV_HEADS, NSA_DH, win_keep), (0, 4, 1, 2, 3))
    return y, s_new, kvc.reshape(kv_shape), kvs.reshape(kv_shape), win_new


def _odd_layer_prompt(x, bsz, g, prm):
    z, zs = _norm_proj(x, g, prm["w_in"], _ODD_WIDTHS, 256)
    h, c_aug, m_fin = _mlstm_prompt(z, zs, prm["bias"], prm["mn"], jnp.zeros((bsz, M_HEADS, M_DQK, M_AUG), F32),
                                    jnp.zeros((bsz, 1, LANE), F32), bsz)
    y = _out_proj(x, [h], (1,), prm["w_out"], 512)
    return y, c_aug[..., :M_DV], c_aug[..., M_DV], m_fin[:, 0, :M_HEADS]


def _odd_layer_sample(x, g, prm, c0, n0, m0):
    b = x.shape[0]
    z, zs = _norm_proj(x, g, prm["w_in"], _ODD_WIDTHS, b)
    h, cn, nn, mn = _mlstm_decode(z, zs, prm["bias"], prm["mn"], c0, n0, m0)
    return _out_proj(x, [h], (1,), prm["w_out"], b), cn, nn, mn


def kernel(x_prompt, x_sample, cache_cmp_kv, cache_sel_kv, cache_win_kv, state_gla, state_mlstm_c, state_mlstm_n, state_mlstm_m, state_ffn_conv, page_table, norm_mix, norm_ffn, norm_final, even_w_in, even_w_out, gla_w_a2, gla_b_a, gla_norm, nsa_cmp_pe, nsa_cmp_w1, nsa_cmp_w2, nsa_gate_b, odd_w_in, odd_w_out, mlstm_b_i, mlstm_b_f, mlstm_norm, ffn_w_up, ffn_conv_w, ffn_conv_b, ffn_w_down):
    bp, t, d = x_prompt.shape
    bs = x_sample.shape[0]
    assert x_sample.shape[1] == 1
    depth = norm_mix.shape[0]
    f = ffn_conv_w.shape[2]
    xp = x_prompt.reshape(bp * t, d)
    xs = x_sample.reshape(bs, d)
    outs = {k: [] for k in ("cmp_p", "cmp_s", "sel_p", "sel_s", "win_p", "win_s", "gla_p", "gla_s",
                            "mc_p", "mc_s", "mn_p", "mn_s", "mm_p", "mm_s", "cv_p", "cv_s")}
    for l in range(depth):
        if l % 2 == 0:
            e = l // 2
            pe, w1, w2 = _compress_weights(nsa_cmp_pe[e], nsa_cmp_w1[e], nsa_cmp_w2[e])
            pe_t, w1_t, w2_t = _compress_weights_paged(nsa_cmp_pe[e], nsa_cmp_w1[e], nsa_cmp_w2[e])
            prm = dict(pe_t=pe_t, w1_t=w1_t, w2_t=w2_t,w_in=_even_w_in(even_w_in[e]), w_out=even_w_out[e].astype(BF16),
                       wa=jnp.pad(gla_w_a2[e], ((0, LANE - GLA_RANK), (0, 0))), ba=gla_b_a[e].reshape(1, -1),
                       gn=gla_norm[e].reshape(1, -1), pe=pe, w1=w1, w2=w2, gb=_gate_bias_row(nsa_gate_b[e]),
                       gb3=nsa_gate_b[e].reshape(NSA_HEADS, NSA_BRANCHES))
            xp, s_, c_, k_, w_ = _even_layer_prompt(xp, bp, norm_mix[l], prm)
            outs["gla_p"].append(s_); outs["cmp_p"].append(c_); outs["sel_p"].append(k_); outs["win_p"].append(w_)
            xs, s_, c_, k_, w_ = _even_layer_sample(xs, norm_mix[l], prm, state_gla[e], cache_cmp_kv[e], cache_sel_kv[e],
                                                    cache_win_kv[e], page_table)
            outs["gla_s"].append(s_); outs["cmp_s"].append(c_); outs["sel_s"].append(k_); outs["win_s"].append(w_)
        else:
            o = l // 2
            bias = jnp.pad(jnp.concatenate([mlstm_b_i[o], mlstm_b_f[o]]), (0, LANE - 2 * M_HEADS)).reshape(1, LANE)
            prm = dict(w_in=_odd_w_in(odd_w_in[o]), w_out=odd_w_out[o].astype(BF16), bias=bias, mn=mlstm_norm[o].reshape(1, -1))
            xp, c_, n_, m_ = _odd_layer_prompt(xp, bp, norm_mix[l], prm)
            outs["mc_p"].append(c_); outs["mn_p"].append(n_); outs["mm_p"].append(m_)
            xs, c_, n_, m_ = _odd_layer_sample(xs, norm_mix[l], prm, state_mlstm_c[o], state_mlstm_n[o], state_mlstm_m[o])
            outs["mc_s"].append(c_); outs["mn_s"].append(n_); outs["mm_s"].append(m_)
        final = l == depth - 1
        wup, wd = ffn_w_up[l].astype(BF16), ffn_w_down[l].astype(BF16)
        xp, cv = _ffn_prompt(xp, bp, norm_ffn[l], wup, ffn_conv_w[l], ffn_conv_b[l], wd,
                             jnp.zeros((bp, CONV_W - 1, f), F32), norm_final, final, 512)
        outs["cv_p"].append(cv)
        xs, cv = _ffn_decode(xs, norm_ffn[l], wup, ffn_conv_w[l], ffn_conv_b[l], wd, state_ffn_conv[l], norm_final, final)
        outs["cv_s"].append(cv)
    st = lambda k: jnp.stack(outs[k])
    return (xp.reshape(bp, t, d), xs.reshape(bs, 1, d),
            st("cmp_p"), st("cmp_s"), st("sel_p"), st("sel_s"), st("win_p"), st("win_s"), st("gla_p"), st("gla_s"),
            st("mc_p"), st("mc_s"), st("mn_p"), st("mn_s"), st("mm_p"), st("mm_s"), st("cv_p"), st("cv_s"))
```

```python
import functools

import jax
import jax.numpy as jnp
from jax import lax
from jax.experimental import pallas as pl
from jax.experimental.pallas import tpu as pltpu

F32 = jnp.float32
BF16 = jnp.bfloat16

GLA_HEADS, GLA_DK, GLA_DV, GLA_RANK, GLA_TAU, GLA_CHUNK = 4, 64, 128, 16, 16.0, 64
NSA_HEADS, NSA_KV_HEADS, NSA_GROUP, NSA_DH = 8, 2, 4, 64
NSA_BRANCHES, NSA_BLOCK, NSA_TOP_N, NSA_WINDOW, NSA_CMP_HID = 3, 64, 16, 512, 64
M_HEADS, M_DQK, M_DV, M_CHUNK = 4, 128, 256, 64
CONV_W = 3
PAGE_SIZE = 128
EPS, TINY = 1e-6, 1e-30

LANE = 128
SUBLANE = 8
VMEM_BIG = 52 * 1024 * 1024
VMEM_MID = 40 * 1024 * 1024

MASK_BIG = 32768.0

NSA_KV_W = 2 * NSA_KV_HEADS * NSA_DH
GLA_Z_W = 2 * GLA_HEADS * GLA_DK + 2 * GLA_HEADS * GLA_DV
M_Z_W = 2 * M_HEADS * M_DQK + 2 * M_HEADS * M_DV
M_AUG = M_DV + LANE
GATE_OFF = GLA_RANK
TM_PROJ = 256
TM_OUT = 512
TM_FFN = 512
TQ_CMP = 512
TQ_ATTN = 128
TK_SEL = 512
_BLOCKS_PER_PAGE = PAGE_SIZE // NSA_BLOCK
_PAGE_SHIFT = _BLOCKS_PER_PAGE.bit_length() - 1

_NN = (((1,), (0,)), ((), ()))
_NT = (((1,), (1,)), ((), ()))
_TN = (((0,), (0,)), ((), ()))


def _mm(a, b, dims=_NN):
    return lax.dot_general(a, b, dims, preferred_element_type=F32)


def _mm_bf(a, b, dims=_NN):
    return _mm(a.astype(BF16), b.astype(BF16), dims)


def _split_bf16(x, n):
    parts, r = [], x
    for _ in range(n):
        p = r.astype(BF16)
        parts.append(p)
        r = r - p.astype(F32)
    return parts


def _mm_sel(sel, x, dims=_NN, x_is_rhs=True):
    out = None
    for p in _split_bf16(x, 3):
        t = _mm(sel, p, dims) if x_is_rhs else _mm(p, sel, dims)
        out = t if out is None else out + t
    return out


def _mm_hp(a, b, dims=_NN):
    a1, a2 = _split_bf16(a, 2)
    b1, b2 = _split_bf16(b, 2)
    return _mm(a1, b1, dims) + (_mm(a1, b2, dims) + _mm(a2, b1, dims))


def _gelu(x):
    return 0.5 * x * (1.0 + jnp.tanh(0.7978845608028654 * (x + 0.044715 * (x * x * x))))


def _sigmoid(x):
    return 1.0 / (1.0 + jnp.exp(-x))


def _logsigmoid(x):
    return jnp.minimum(x, 0.0) - jnp.log(1.0 + jnp.exp(-jnp.abs(x)))


def _rms(x, g):
    return x * lax.rsqrt(jnp.mean(x * x, axis=-1, keepdims=True) + EPS) * g


def _iota(shape, dim):
    return lax.broadcasted_iota(jnp.int32, shape, dim)


def _params(n_axes, vmem=VMEM_MID):
    return pltpu.CompilerParams(dimension_semantics=("arbitrary",) * n_axes, vmem_limit_bytes=vmem)


def _resident(shape):
    nd = len(shape)
    return pl.BlockSpec(shape, lambda *_: (0,) * nd, pipeline_mode=pl.Buffered(1))


def _norm_proj_body(x_ref, g_ref, w_ref, *refs, widths, t_widths):
    xb = _rms(x_ref[...], g_ref[...]).astype(BF16)
    o_refs = refs[1:] if t_widths else refs
    off = 0
    for o_ref, n in zip(o_refs[:len(widths)], widths):
        o_ref[...] = _mm(xb, w_ref[:, off:off + n])
        off += n
    off = 0
    for o_ref, n in zip(o_refs[len(widths):], t_widths):
        o_ref[0] = _mm(refs[0][off:off + n, :], xb, _NT)
        off += n


def _norm_proj(x, g, w, widths, tm, wt=None, t_widths=(), bsz=1):
    m, d = x.shape
    tm = min(tm, m)
    nt = m // bsz // tm
    assert m % tm == 0 and sum(widths) == w.shape[1] and (m // bsz) % tm == 0
    t_in = [_resident(wt.shape)] if t_widths else []
    t_args = [wt] if t_widths else []
    return pl.pallas_call(
        functools.partial(_norm_proj_body, widths=tuple(widths), t_widths=tuple(t_widths)),
        grid=(m // tm,),
        in_specs=[pl.BlockSpec((tm, d), lambda i: (i, 0)), _resident((1, d)), _resident(w.shape)] + t_in,
        out_specs=[pl.BlockSpec((tm, n), lambda i: (i, 0)) for n in widths]
        + [pl.BlockSpec((1, n, tm), lambda i: (i // nt, 0, i % nt)) for n in t_widths],
        out_shape=[jax.ShapeDtypeStruct((m, n), F32) for n in widths]
        + [jax.ShapeDtypeStruct((bsz, n, m // bsz), F32) for n in t_widths],
        compiler_params=_params(1),
        name="norm_proj",
    )(x, g.reshape(1, d), w, *t_args)


def _out_proj_body(x_ref, *refs, groups, n_t):
    h_refs, w_ref, o_ref = refs[:-2], refs[-2], refs[-1]
    acc = x_ref[...]
    i = off = 0
    for gsz in groups:
        h = h_refs[i][...]
        for j in range(1, gsz):
            h = h + h_refs[i + j][...]
        i += gsz
        n = h.shape[1]
        acc = acc + _mm(h.astype(BF16), w_ref[off:off + n, :])
        off += n
    if n_t:
        ht = h_refs[i][0]
        for j in range(1, n_t):
            ht = ht + h_refs[i + j][0]
        acc = acc + _mm(ht.astype(BF16), w_ref[off:off + ht.shape[0], :], _TN)
    o_ref[...] = acc


def _out_proj(x, hs, groups, w, tm, t_hs=(), bsz=1):
    m, d = x.shape
    tm = min(tm, m)
    nt = m // bsz // tm
    assert m % tm == 0 and (m // bsz) % tm == 0
    return pl.pallas_call(
        functools.partial(_out_proj_body, groups=tuple(groups), n_t=len(t_hs)),
        grid=(m // tm,),
        in_specs=[pl.BlockSpec((tm, d), lambda i: (i, 0))]
        + [pl.BlockSpec((tm, h.shape[1]), lambda i: (i, 0)) for h in hs]
        + [pl.BlockSpec((1, h.shape[1], tm), lambda i: (i // nt, 0, i % nt)) for h in t_hs]
        + [_resident(w.shape)],
        out_specs=pl.BlockSpec((tm, d), lambda i: (i, 0)),
        out_shape=jax.ShapeDtypeStruct((m, d), F32),
        compiler_params=_params(1),
        name="out_proj",
    )(x, *hs, *t_hs, w)


def _ffn_chunks(xn, resid, wup_ref, cw_ref, cb_ref, wd_ref, fc, prev_fn, keep_fn):
    f = cw_ref.shape[1]
    acc = resid
    for c in range(f // fc):
        sl = slice(c * fc, (c + 1) * fc)
        gp = _mm(xn, wup_ref[:, sl])
        up = _mm(xn, wup_ref[:, f + c * fc:f + (c + 1) * fc])
        g2, g1 = prev_fn(gp, sl)
        a = cb_ref[:, sl] + g2 * cw_ref[0:1, sl] + g1 * cw_ref[1:2, sl] + gp * cw_ref[2:3, sl]
        acc = acc + _mm((_gelu(a) * up).astype(BF16), wd_ref[sl, :])
        keep_fn(gp, sl)
    return acc


def _ffn_prompt_body(x_ref, g_ref, wup_ref, cw_ref, cb_ref, wd_ref, st_ref, gf_ref, y_ref, ns_ref, carry, *, final, fc):
    t = pl.program_id(1)
    tm = x_ref.shape[0]

    @pl.when(t == 0)
    def _():
        carry[...] = st_ref[0]

    x = x_ref[...]
    xn = _rms(x, g_ref[...]).astype(BF16)
    row = _iota((tm, fc), 0)

    def prev_fn(gp, sl):
        c0, c1 = carry[0:1, sl], carry[1:2, sl]
        g1 = jnp.where(row == 0, c1, pltpu.roll(gp, 1, 0))
        g2 = jnp.where(row == 0, c0, jnp.where(row == 1, c1, pltpu.roll(gp, 2, 0)))
        return g2, g1

    def keep_fn(gp, sl):
        carry[:, sl] = gp[tm - 2:tm, :]

    acc = _ffn_chunks(xn, x, wup_ref, cw_ref, cb_ref, wd_ref, fc, prev_fn, keep_fn)
    y_ref[...] = _rms(acc, gf_ref[...]) if final else acc

    @pl.when(t == pl.num_programs(1) - 1)
    def _():
        ns_ref[0] = carry[...]


def _ffn_prompt(x, bsz, g, wup, cw, cb, wd, st, gf, final, tm, fc=256):
    m, d = x.shape
    t = m // bsz
    tm = min(tm, t)
    f = cw.shape[1]
    assert t % tm == 0 and f % fc == 0
    nt = t // tm
    return pl.pallas_call(
        functools.partial(_ffn_prompt_body, final=final, fc=fc),
        grid=(bsz, nt),
        in_specs=[pl.BlockSpec((tm, d), lambda b, i: (b * nt + i, 0)), _resident((1, d)), _resident(wup.shape),
                  _resident(cw.shape), _resident((1, f)), _resident(wd.shape),
                  pl.BlockSpec((1, CONV_W - 1, f), lambda b, i: (b, 0, 0)), _resident((1, d))],
        out_specs=[pl.BlockSpec((tm, d), lambda b, i: (b * nt + i, 0)),
                   pl.BlockSpec((1, CONV_W - 1, f), lambda b, i: (b, 0, 0))],
        out_shape=[jax.ShapeDtypeStruct((m, d), F32), jax.ShapeDtypeStruct((bsz, CONV_W - 1, f), F32)],
        scratch_shapes=[pltpu.VMEM((CONV_W - 1, f), F32)],
        compiler_params=_params(2, VMEM_BIG),
        name="ffn_prompt",
    )(x, g.reshape(1, d), wup, cw, cb.reshape(1, f), wd, st, gf.reshape(1, d))


def _ffn_decode_body(x_ref, g_ref, wup_ref, cw_ref, cb_ref, wd_ref, s0_ref, s1_ref, gf_ref, y_ref, gp_ref, *, final, fc):
    x = x_ref[...]
    xn = _rms(x, g_ref[...]).astype(BF16)

    def prev_fn(gp, sl):
        return s0_ref[:, sl], s1_ref[:, sl]

    def keep_fn(gp, sl):
        gp_ref[:, sl] = gp

    acc = _ffn_chunks(xn, x, wup_ref, cw_ref, cb_ref, wd_ref, fc, prev_fn, keep_fn)
    y_ref[...] = _rms(acc, gf_ref[...]) if final else acc


def _ffn_decode(x, g, wup, cw, cb, wd, st, gf, final, fc=256):
    m, d = x.shape
    f = cw.shape[1]
    y, gp = pl.pallas_call(
        functools.partial(_ffn_decode_body, final=final, fc=fc),
        grid=(1,),
        in_specs=[_resident((m, d)), _resident((1, d)), _resident(wup.shape), _resident(cw.shape), _resident((1, f)),
                  _resident(wd.shape), _resident((m, f)), _resident((m, f)), _resident((1, d))],
        out_specs=[pl.BlockSpec((m, d), lambda i: (0, 0)), pl.BlockSpec((m, f), lambda i: (0, 0))],
        out_shape=[jax.ShapeDtypeStruct((m, d), F32), jax.ShapeDtypeStruct((m, f), F32)],
        compiler_params=_params(1, VMEM_BIG),
        name="ffn_decode",
    )(x, g.reshape(1, d), wup, cw, cb.reshape(1, f), wd, st[:, 0], st[:, 1], gf.reshape(1, d))
    return y, jnp.stack([st[:, 1], gp], axis=1)


def _gla_finish(o, gg, gn):
    return _rms(o, gn) * (gg * _sigmoid(gg))


def _gla_body(z_ref, zs_ref, wa_ref, ba_ref, gn_ref, s0_ref, o_ref, sfin_ref, s_scr):
    t = pl.program_id(1)

    @pl.when(t == 0)
    def _():
        s_scr[...] = s0_ref[...]

    _round_robin([_gla_chunk(z_ref.at[bi], zs_ref.at[bi], wa_ref, ba_ref, gn_ref, o_ref.at[bi], s_scr.at[bi])
                  for bi in range(z_ref.shape[0])])

    @pl.when(t == pl.num_programs(1) - 1)
    def _():
        sfin_ref[...] = s_scr[...]


def _round_robin(chains):
    chains = list(chains)
    while chains:
        chains = [c for c in chains if next(c, _DONE) is not _DONE]


_DONE = object()


def _gla_chunk(z_ref, zs_ref, wa_ref, ba_ref, gn_ref, o_ref, s_scr):
    c = z_ref.shape[0]
    hk = GLA_HEADS * GLA_DK
    la = _logsigmoid(_mm_hp(zs_ref[...], wa_ref[...]) + ba_ref[...]) * (1.0 / GLA_TAU)
    yield
    r, cidx = _iota((c, c), 0), _iota((c, c), 1)
    tri = r >= cidx
    cum = _mm_sel(jnp.where(tri, 1.0, 0.0).astype(BF16), la)
    yield
    last = cum[c - 1:c, :]
    eq, ek, ekl, el = jnp.exp(cum), jnp.exp(-cum), jnp.exp(last - cum), jnp.exp(last)
    gn = gn_ref[...]
    for h in range(GLA_HEADS):
        ks = slice(h * GLA_DK, (h + 1) * GLA_DK)
        vs = slice(2 * hk + h * GLA_DV, 2 * hk + (h + 1) * GLA_DV)
        gs = slice(2 * hk + GLA_HEADS * GLA_DV + h * GLA_DV, 2 * hk + GLA_HEADS * GLA_DV + (h + 1) * GLA_DV)
        q = z_ref[:, ks] * (GLA_DK ** -0.5)
        k = z_ref[:, hk + h * GLA_DK:hk + (h + 1) * GLA_DK]
        v = z_ref[:, vs]
        qt = (q * eq[:, ks]).astype(BF16)
        s_old = s_scr[h]
        att = jnp.where(tri, _mm(qt, (k * ek[:, ks]).astype(BF16), _NT), 0.0)
        vb = v.astype(BF16)
        o_inter = _mm(qt, s_old.astype(BF16))
        kv_new = _mm((k * ekl[:, ks]).astype(BF16), vb, _TN)
        yield
        o = o_inter + _mm(att.astype(BF16), vb)
        ecol = jnp.sum(jnp.where(r == cidx, jnp.broadcast_to(el[:, ks], (c, c)), 0.0), axis=1, keepdims=True)
        s_scr[h] = ecol * s_old + kv_new
        yield
        o_ref[:, h * GLA_DV:(h + 1) * GLA_DV] = _gla_finish(o, z_ref[:, gs], gn)


def _seq_group(bsz):
    return next(n for n in (4, 2, 1) if bsz % n == 0)


def _gla_prompt(z, zs, wa, ba, gn, s0, bsz):
    m = z.shape[0]
    t = m // bsz
    c = GLA_CHUNK
    assert t % c == 0 and GLA_DK == c
    nc = t // c
    hk = GLA_HEADS * GLA_DK
    nb = _seq_group(bsz)
    st = (nb, GLA_HEADS, GLA_DK, GLA_DV)
    tok = lambda n: pl.BlockSpec((nb, c, n), lambda b, i: (b, i, 0))
    o, s_fin = pl.pallas_call(
        _gla_body,
        grid=(bsz // nb, nc),
        in_specs=[tok(GLA_Z_W), tok(LANE), _resident((LANE, hk)), _resident((1, hk)), _resident((1, GLA_DV)),
                  pl.BlockSpec(st, lambda b, i: (b, 0, 0, 0))],
        out_specs=[tok(GLA_HEADS * GLA_DV), pl.BlockSpec(st, lambda b, i: (b, 0, 0, 0))],
        out_shape=[jax.ShapeDtypeStruct((bsz, t, GLA_HEADS * GLA_DV), F32),
                   jax.ShapeDtypeStruct((bsz, GLA_HEADS, GLA_DK, GLA_DV), F32)],
        scratch_shapes=[pltpu.VMEM(st, F32)],
        compiler_params=_params(2),
        name="gla_prompt",
    )(z.reshape(bsz, t, -1), zs.reshape(bsz, t, -1), wa, ba, gn, s0)
    return o.reshape(m, -1), s_fin


def _gla_decode_body(z_ref, zs_ref, wa_ref, ba_ref, gn_ref, s_ref, o_ref, sn_ref):
    hk = GLA_HEADS * GLA_DK
    rows = SUBLANE
    z = jnp.broadcast_to(z_ref[0], (rows, GLA_Z_W))
    ga = jnp.broadcast_to(zs_ref[0], (rows, LANE))
    la = _logsigmoid(_mm_hp(ga, wa_ref[...]) + ba_ref[...]) * (1.0 / GLA_TAU)
    ea = jnp.exp(la)
    r, cidx = _iota((GLA_DK, GLA_DK), 0), _iota((GLA_DK, GLA_DK), 1)
    row0 = _iota((rows, GLA_DK), 0) == 0
    gn = gn_ref[...]
    for h in range(GLA_HEADS):
        ks = slice(h * GLA_DK, (h + 1) * GLA_DK)
        q = z[:, ks] * (GLA_DK ** -0.5)
        k = z[:, hk + h * GLA_DK:hk + (h + 1) * GLA_DK]
        v = z[:, 2 * hk + h * GLA_DV:2 * hk + (h + 1) * GLA_DV]
        gg = z[:, 2 * hk + GLA_HEADS * GLA_DV + h * GLA_DV:2 * hk + GLA_HEADS * GLA_DV + (h + 1) * GLA_DV]
        s_old = s_ref[0, h]
        o = _mm_hp(q * ea[:, ks], s_old) + jnp.sum(q * k, axis=1, keepdims=True) * v
        ecol = jnp.sum(jnp.where(r == cidx, jnp.broadcast_to(ea[0:1, ks], (GLA_DK, GLA_DK)), 0.0), axis=1, keepdims=True)
        sn_ref[0, h] = ecol * s_old + _mm_hp(jnp.where(row0, k, 0.0), v, _TN)
        o_ref[0, :, h * GLA_DV:(h + 1) * GLA_DV] = _gla_finish(o, gg, gn)[0:1]


def _gla_decode(z, zs, wa, ba, gn, s0):
    b = z.shape[0]
    hk = GLA_HEADS * GLA_DK
    st = (1, GLA_HEADS, GLA_DK, GLA_DV)
    o, sn = pl.pallas_call(
        _gla_decode_body,
        grid=(b,),
        in_specs=[pl.BlockSpec((1, 1, GLA_Z_W), lambda i: (i, 0, 0)), pl.BlockSpec((1, 1, LANE), lambda i: (i, 0, 0)),
                  _resident((LANE, hk)), _resident((1, hk)), _resident((1, GLA_DV)),
                  pl.BlockSpec(st, lambda i: (i, 0, 0, 0))],
        out_specs=[pl.BlockSpec((1, 1, GLA_HEADS * GLA_DV), lambda i: (i, 0, 0)), pl.BlockSpec(st, lambda i: (i, 0, 0, 0))],
        out_shape=[jax.ShapeDtypeStruct((b, 1, GLA_HEADS * GLA_DV), F32), jax.ShapeDtypeStruct((b,) + st[1:], F32)],
        compiler_params=_params(1),
        name="gla_decode",
    )(z.reshape(b, 1, -1), zs.reshape(b, 1, -1), wa, ba, gn, s0)
    return o.reshape(b, -1), sn


def _mlstm_finish(hh, og, mn):
    return _rms(hh, mn) * _sigmoid(og)


def _mlstm_body(z_ref, zs_ref, bias_ref, mn_ref, c0_ref, m0_ref, h_ref, cfin_ref, mfin_ref, c_scr, m_scr):
    t = pl.program_id(1)

    @pl.when(t == 0)
    def _():
        c_scr[...] = c0_ref[...]
        m_scr[...] = m0_ref[...]

    _round_robin([_mlstm_chunk(z_ref.at[bi], zs_ref.at[bi], bias_ref, mn_ref, h_ref.at[bi], c_scr.at[bi], m_scr.at[bi])
                  for bi in range(z_ref.shape[0])])

    @pl.when(t == pl.num_programs(1) - 1)
    def _():
        cfin_ref[...] = c_scr[...]
        mfin_ref[...] = m_scr[...]


def _mlstm_chunk(z_ref, zs_ref, bias_ref, mn_ref, h_ref, c_scr, m_scr):
    L = z_ref.shape[0]
    hq = M_HEADS * M_DQK
    g = zs_ref[...] + bias_ref[...]
    lane = _iota((L, LANE), 1)
    gx = jnp.where((lane >= M_HEADS) & (lane < 2 * M_HEADS), _logsigmoid(g), g)
    r, cidx = _iota((L, L), 0), _iota((L, L), 1)
    tri = r >= cidx
    cum_c = _mm_sel(jnp.where(tri, 1.0, 0.0).astype(BF16), gx)
    rows = _mm_sel(jnp.where(r == cidx, 1.0, 0.0).astype(BF16), gx, _TN, x_is_rhs=False)
    cum_r = _mm_sel(jnp.where(r <= cidx, 1.0, 0.0).astype(BF16), gx, _TN, x_is_rhs=False)
    yield
    ones_col = jnp.where(_iota((L, LANE), 1) == 0, 1.0, 0.0).astype(BF16)
    m_all = m_scr[...]
    m_new_all = m_all
    for h in range(M_HEADS):
        q = z_ref[:, h * M_DQK:(h + 1) * M_DQK].astype(BF16)
        k = z_ref[:, hq + h * M_DQK:hq + (h + 1) * M_DQK] * (M_DQK ** -0.5)
        v = z_ref[:, 2 * hq + h * M_DV:2 * hq + (h + 1) * M_DV]
        og = z_ref[:, 2 * hq + M_HEADS * M_DV + h * M_DV:2 * hq + M_HEADS * M_DV + (h + 1) * M_DV]
        va = jnp.concatenate([v.astype(BF16), ones_col], axis=1)
        ic_r, ic_c = rows[h:h + 1, :], gx[:, h:h + 1]
        cr, cc = cum_r[M_HEADS + h:M_HEADS + h + 1, :], cum_c[:, M_HEADS + h:M_HEADS + h + 1]
        m_old = m_all[0:1, h:h + 1]
        dlog = jnp.where(tri, cc - cr + ic_r, -jnp.inf)
        inter = cc + m_old
        mi = jnp.maximum(inter, jnp.max(dlog, axis=1, keepdims=True))
        w = jnp.exp(dlog - mi)
        wi = jnp.exp(inter - mi)
        qk = _mm(q, k.astype(BF16), _NT)
        c_old = c_scr[h]
        qc = _mm(q, c_old.astype(BF16))
        yield
        s = qk * w
        num = wi * qc + _mm(s.astype(BF16), va)
        yield
        qn = num[:, M_DV:M_DV + 1]
        hh = num[:, :M_DV] / jnp.maximum(jnp.abs(qn), jnp.exp(-mi))
        last = cc[L - 1:L, :]
        gl = last - cc + ic_c
        m_new = jnp.maximum(last + m_old, jnp.max(gl, axis=0, keepdims=True))
        wj = jnp.exp(gl - m_new)
        keep = jnp.exp(last + m_old - m_new)
        c_scr[h] = keep * c_old + _mm((wj * k).astype(BF16), va, _TN)
        yield
        m_new_all = jnp.where(_iota((1, LANE), 1) == h, m_new, m_new_all)
        h_ref[:, h * M_DV:(h + 1) * M_DV] = _mlstm_finish(hh, og, mn_ref[:, h * M_DV:(h + 1) * M_DV])
    m_scr[...] = m_new_all


def _mlstm_prompt(z, zs, bias, mn, c0, m0, bsz):
    m = z.shape[0]
    t = m // bsz
    L = M_CHUNK
    assert t % L == 0
    nc = t // L
    nb = _seq_group(bsz)
    st, ms = (nb, M_HEADS, M_DQK, M_AUG), (nb, 1, LANE)
    tok = lambda n: pl.BlockSpec((nb, L, n), lambda b, i: (b, i, 0))
    h, c_fin, m_fin = pl.pallas_call(
        _mlstm_body,
        grid=(bsz // nb, nc),
        in_specs=[tok(M_Z_W), tok(LANE), _resident((1, LANE)), _resident((1, M_HEADS * M_DV)),
                  pl.BlockSpec(st, lambda b, i: (b, 0, 0, 0)), pl.BlockSpec(ms, lambda b, i: (b, 0, 0))],
        out_specs=[tok(M_HEADS * M_DV), pl.BlockSpec(st, lambda b, i: (b, 0, 0, 0)), pl.BlockSpec(ms, lambda b, i: (b, 0, 0))],
        out_shape=[jax.ShapeDtypeStruct((bsz, t, M_HEADS * M_DV), F32), jax.ShapeDtypeStruct((bsz,) + st[1:], F32),
                   jax.ShapeDtypeStruct((bsz, 1, LANE), F32)],
        scratch_shapes=[pltpu.VMEM(st, F32), pltpu.VMEM(ms, F32)],
        compiler_params=_params(2),
        name="mlstm_prompt",
    )(z.reshape(bsz, t, -1), zs.reshape(bsz, t, -1), bias, mn, c0, m0)
    return h.reshape(m, -1), c_fin, m_fin


def _mlstm_decode_body(z_ref, zs_ref, bias_ref, mn_ref, c_ref, n_ref, m_ref, h_ref, cn_ref, nn_ref, mo_ref):
    hq = M_HEADS * M_DQK
    rows = SUBLANE
    z = jnp.broadcast_to(z_ref[0], (rows, M_Z_W))
    g = zs_ref[0] + bias_ref[...]
    m_all = m_ref[0]
    m_new_all = m_all
    row0 = _iota((rows, M_DQK), 0) == 0
    for h in range(M_HEADS):
        q = z[:, h * M_DQK:(h + 1) * M_DQK]
        k = z[:, hq + h * M_DQK:hq + (h + 1) * M_DQK] * (M_DQK ** -0.5)
        v = z[:, 2 * hq + h * M_DV:2 * hq + (h + 1) * M_DV]
        og = z[:, 2 * hq + M_HEADS * M_DV + h * M_DV:2 * hq + M_HEADS * M_DV + (h + 1) * M_DV]
        ic = g[:, h:h + 1]
        fl = _logsigmoid(g[:, M_HEADS + h:M_HEADS + h + 1])
        m_old = m_all[:, h:h + 1]
        c_old, n_old = c_ref[0, h], n_ref[0, h:h + 1, :]
        mi = jnp.maximum(fl + m_old, ic)
        w = jnp.exp(ic - mi)
        wi = jnp.exp(fl + m_old - mi)
        s = jnp.sum(q * k, axis=1, keepdims=True) * w
        num = wi * _mm_hp(q, c_old) + s * v
        qn = wi * jnp.sum(q * n_old, axis=1, keepdims=True) + s
        hh = num / jnp.maximum(jnp.abs(qn), jnp.exp(-mi))
        cn_ref[0, h] = wi * c_old + w * _mm_hp(jnp.where(row0, k, 0.0), v, _TN)
        nn_ref[0, h:h + 1, :] = wi * n_old + w * k[0:1]
        m_new_all = jnp.where(_iota((1, LANE), 1) == h, mi, m_new_all)
        h_ref[0, :, h * M_DV:(h + 1) * M_DV] = _mlstm_finish(hh, og, mn_ref[:, h * M_DV:(h + 1) * M_DV])[0:1]
    mo_ref[0] = m_new_all


def _mlstm_decode(z, zs, bias, mn, c0, n0, m0):
    b = z.shape[0]
    cs, ns = (1, M_HEADS, M_DQK, M_DV), (1, M_HEADS, M_DQK)
    m0p = jnp.pad(m0, ((0, 0), (0, LANE - M_HEADS))).reshape(b, 1, LANE)
    row3 = lambda n: pl.BlockSpec((1, 1, n), lambda i: (i, 0, 0))
    h, cn, nn, mo = pl.pallas_call(
        _mlstm_decode_body,
        grid=(b,),
        in_specs=[row3(M_Z_W), row3(LANE), _resident((1, LANE)), _resident((1, M_HEADS * M_DV)),
                  pl.BlockSpec(cs, lambda i: (i, 0, 0, 0)), pl.BlockSpec(ns, lambda i: (i, 0, 0)), row3(LANE)],
        out_specs=[row3(M_HEADS * M_DV), pl.BlockSpec(cs, lambda i: (i, 0, 0, 0)), pl.BlockSpec(ns, lambda i: (i, 0, 0)), row3(LANE)],
        out_shape=[jax.ShapeDtypeStruct((b, 1, M_HEADS * M_DV), F32), jax.ShapeDtypeStruct((b,) + cs[1:], F32),
                   jax.ShapeDtypeStruct((b,) + ns[1:], F32), jax.ShapeDtypeStruct((b, 1, LANE), F32)],
        compiler_params=_params(1),
        name="mlstm_decode",
    )(z.reshape(b, 1, -1), zs.reshape(b, 1, -1), bias, mn, c0, n0, m0p)
    return h.reshape(b, -1), cn, nn, mo[:, 0, :M_HEADS]


def _compress_dense_body(x_ref, pe_ref, w1_ref, w2_ref, o_ref, acc):
    kk = pl.program_id(1)

    @pl.when(kk == 0)
    def _():
        acc[...] = jnp.zeros_like(acc)

    acc[...] += _mm((x_ref[...] + pe_ref[...]).astype(BF16), w1_ref[...])

    @pl.when(kk == pl.num_programs(1) - 1)
    def _():
        o_ref[...] = _mm(_gelu(acc[...]).astype(BF16), w2_ref[...])


def _compress_dense(x, pe, w1, w2, tk=2048):
    r, kdim = x.shape
    tr = r if r <= 512 else 512
    assert r % tr == 0 and kdim % tk == 0
    return pl.pallas_call(
        _compress_dense_body,
        grid=(r // tr, kdim // tk),
        in_specs=[pl.BlockSpec((tr, tk), lambda i, k: (i, k)), pl.BlockSpec((1, tk), lambda i, k: (0, k)),
                  pl.BlockSpec((tk, NSA_KV_W), lambda i, k: (k, 0)), _resident(w2.shape)],
        out_specs=pl.BlockSpec((tr, NSA_KV_W), lambda i, k: (i, 0)),
        out_shape=jax.ShapeDtypeStruct((r, NSA_KV_W), F32),
        scratch_shapes=[pltpu.VMEM((tr, NSA_KV_W), F32)],
        compiler_params=_params(2),
        name="nsa_compress",
    )(x, pe, w1, w2)


_PAGES_PER_STEP = 8


_D_PER_STEP = 2


def _compress_paged_body(pt_ref, *refs, n_pages):
    page_refs = refs[:_PAGES_PER_STEP]
    pe_ref, w1_ref, w2_ref, o_ref, xs = refs[_PAGES_PER_STEP:]
    g = pl.program_id(1)
    for p, pr in enumerate(page_refs):
        r0 = pl.multiple_of((g * _PAGES_PER_STEP + p) * NSA_KV_W, NSA_KV_W)
        xs[pl.ds(r0, NSA_KV_W), :] = pr[0]

    @pl.when(g == pl.num_programs(1) - 1)
    def _():
        kvd = NSA_KV_HEADS * NSA_DH
        for c in range(2):
            def step(j, acc):
                parts = []
                for dd in range(_D_PER_STEP):
                    d = j * _D_PER_STEP + dd
                    pe_row = pe_ref[pl.ds(c * NSA_DH + d, 1), :]
                    rows = [xs[pl.ds(c * kvd + k * NSA_DH + d, n_pages, stride=NSA_KV_W), :] + pe_row
                            for k in range(NSA_KV_HEADS)]
                    parts.append(jnp.concatenate(rows, axis=0).astype(BF16))
                return acc + _mm(jnp.concatenate(parts, axis=1), w1_ref[c, j])
            hid = lax.fori_loop(0, NSA_DH // _D_PER_STEP, step, jnp.zeros((NSA_KV_HEADS * n_pages, PAGE_SIZE), F32))
            out = _mm(_gelu(hid).astype(BF16), w2_ref[c])
            o_ref[c * NSA_KV_HEADS * n_pages:(c + 1) * NSA_KV_HEADS * n_pages, :] = out


def _compress_paged(pool_t, page_table, pe_t, w1_t, w2_t):
    b, n_pages = page_table.shape
    pps = _PAGES_PER_STEP
    assert n_pages % pps == 0 and _BLOCKS_PER_PAGE == 2
    rows_out = 2 * NSA_KV_HEADS * n_pages

    def page_spec(p):
        return pl.BlockSpec((1, NSA_KV_W, PAGE_SIZE), lambda i, g, pt: (pt[i, g * pps + p], 0, 0))

    def fixed(a):
        nd = a.ndim
        return pl.BlockSpec(a.shape, lambda i, g, pt: (0,) * nd, pipeline_mode=pl.Buffered(1))

    grid_spec = pltpu.PrefetchScalarGridSpec(
        num_scalar_prefetch=1,
        grid=(b, n_pages // pps),
        in_specs=[page_spec(p) for p in range(pps)] + [fixed(pe_t), fixed(w1_t), fixed(w2_t)],
        out_specs=pl.BlockSpec((rows_out, PAGE_SIZE), lambda i, g, pt: (i, 0)),
        scratch_shapes=[pltpu.VMEM((n_pages * NSA_KV_W, PAGE_SIZE), F32)],
    )
    return pl.pallas_call(
        functools.partial(_compress_paged_body, n_pages=n_pages),
        grid_spec=grid_spec,
        out_shape=jax.ShapeDtypeStruct((b * rows_out, PAGE_SIZE), F32),
        compiler_params=_params(2, VMEM_BIG),
        name="nsa_compress_paged",
    )(page_table, *([pool_t] * pps), pe_t, w1_t, w2_t)


def _gate_rows(zs_ref, gb_ref):
    return _sigmoid(zs_ref[0] + gb_ref[...])


def _gate_row(gates, head, branch):
    r = GATE_OFF + head * NSA_BRANCHES + branch
    return gates[r:r + 1, :]


def _cmp_sel_body(q_ref, kvc_ref, zs_ref, gb_ref, ocmp_ref, nm_ref, sc_scr, thr_scr):
    qi = pl.program_id(1)
    tq = q_ref.shape[2]
    nbp = kvc_ref.shape[0]
    t0 = qi * tq
    tpos = t0 + _iota((nbp, tq), 1)
    blk = _iota((nbp, tq), 0)
    cur = jnp.right_shift(tpos, NSA_BLOCK.bit_length() - 1)
    vis = blk * NSA_BLOCK + (NSA_BLOCK - 1) <= tpos
    forced = (blk == 0) | (blk == cur) | (blk == cur - 1)
    allowed = blk <= cur
    gates = _gate_rows(zs_ref, gb_ref)
    for kh in range(NSA_KV_HEADS):
        kc = kvc_ref[:, kh * NSA_DH:(kh + 1) * NSA_DH].astype(BF16)
        vc = kvc_ref[:, NSA_KV_HEADS * NSA_DH + kh * NSA_DH:NSA_KV_HEADS * NSA_DH + (kh + 1) * NSA_DH].astype(BF16)
        imp = jnp.zeros((nbp, tq), F32)
        for gi in range(NSA_GROUP):
            h = kh * NSA_GROUP + gi
            q = (q_ref[0, h * NSA_DH:(h + 1) * NSA_DH, :] * (NSA_DH ** -0.5)).astype(BF16)
            s = jnp.where(vis, _mm(kc, q), -jnp.inf)
            m = jnp.max(s, axis=0, keepdims=True)
            m = jnp.where(m == -jnp.inf, 0.0, m)
            e = jnp.exp(s - m)
            p = e / jnp.maximum(jnp.sum(e, axis=0, keepdims=True), TINY)
            imp = imp + p
            ocmp_ref[0, h * NSA_DH:(h + 1) * NSA_DH, :] = _mm(vc, p.astype(BF16), _TN) * _gate_row(gates, h, 0)
        score = jnp.where(forced, jnp.inf, jnp.where(allowed, imp, -jnp.inf))
        key = lax.bitcast_convert_type(score, jnp.int32)
        for lc in range(tq // LANE):
            ls = slice(lc * LANE, (lc + 1) * LANE)
            n_live = (t0 + (lc + 1) * LANE - 1) // NSA_BLOCK + 1
            sc_scr[...] = key[:, ls]
            thr_scr[...] = key[:, ls]

            def rank_step(i, cnt):
                for j in (2 * i, 2 * i + 1):
                    row = sc_scr[pl.ds(j, 1), :]
                    thr_scr[pl.ds(j, 1), :] = row + 1
                    cnt = cnt + jnp.where(row >= thr_scr[...], 1, 0)
                return cnt

            cnt = lax.fori_loop(0, n_live // 2, rank_step, jnp.zeros((nbp, LANE), jnp.int32))
            nm_ref[0, kh * nbp:(kh + 1) * nbp, ls] = jnp.where(
                allowed[:, ls], jnp.where(cnt < NSA_TOP_N, 0.0, -MASK_BIG), -MASK_BIG).astype(BF16)


def _cmp_sel(q_t, kvc, zs_t, gb_col, nbp, tq):
    bsz, hd, t = q_t.shape
    tq = min(tq, t)
    nt = t // tq
    tok = lambda n: pl.BlockSpec((1, n, tq), lambda b, i: (b, 0, i))
    return pl.pallas_call(
        _cmp_sel_body,
        grid=(bsz, nt),
        in_specs=[tok(hd), pl.BlockSpec((nbp, NSA_KV_W), lambda b, i: (b, 0)), tok(LANE), _resident((LANE, 1))],
        out_specs=[tok(hd), tok(NSA_KV_HEADS * nbp)],
        out_shape=[jax.ShapeDtypeStruct((bsz, hd, t), F32), jax.ShapeDtypeStruct((bsz, NSA_KV_HEADS * nbp, t), BF16)],
        scratch_shapes=[pltpu.VMEM((nbp, LANE), jnp.int32), pltpu.VMEM((nbp, LANE), jnp.int32)],
        compiler_params=_params(2),
        name="nsa_cmp_select",
    )(q_t, kvc, zs_t, gb_col)


def _sel_attn_body(q_ref, nm_ref, ka_ref, v_ref, zs_ref, gb_ref, o_ref, *, tk):
    qi = pl.program_id(1)
    tq = q_ref.shape[2]
    nbp = nm_ref.shape[1] // NSA_KV_HEADS
    cols = NSA_GROUP * tq
    t0 = qi * tq
    n_before = t0 // tk
    gates = _gate_rows(zs_ref, gb_ref)
    qas = []
    for kh in range(NSA_KV_HEADS):
        nm = nm_ref[0, kh * nbp:(kh + 1) * nbp, :]
        qas.append(jnp.concatenate(
            [jnp.concatenate([nm, (q_ref[0, h * NSA_DH:(h + 1) * NSA_DH, :] * (NSA_DH ** -0.5)).astype(BF16)], axis=0)
             for h in range(kh * NSA_GROUP, (kh + 1) * NSA_GROUP)], axis=1))

    def tile(kh, kt, carry, causal, out):
        m_old, l_old, acc = carry
        k0 = pl.multiple_of(kt * tk, tk)
        s = _mm(ka_ref[0, kh, pl.ds(k0, tk), :], qas[kh])
        yield
        if causal:
            kpos = k0 + _iota((tk, cols), 0)
            tpos = t0 + jnp.bitwise_and(_iota((tk, cols), 1), tq - 1)
            s = jnp.where(kpos <= tpos, s, -jnp.inf)
        m_new = jnp.maximum(m_old, jnp.max(s, axis=0, keepdims=True))
        alpha = jnp.exp(m_old - m_new)
        p = jnp.exp(s - m_new)
        l_new = alpha * l_old + jnp.sum(p, axis=0, keepdims=True)
        yield
        out[kh] = (m_new, l_new, alpha * acc + _mm(v_ref[0, kh, :, pl.ds(k0, tk)], p.astype(BF16)))

    def tiles(kt, carries, causal):
        out = [None] * NSA_KV_HEADS
        _round_robin([tile(kh, kt, carries[kh], causal, out) for kh in range(NSA_KV_HEADS)])
        return tuple(out)

    init = (jnp.full((1, cols), -jnp.inf, F32), jnp.zeros((1, cols), F32), jnp.zeros((NSA_DH, cols), F32))
    carries = lax.fori_loop(0, n_before, lambda kt, c: tiles(kt, c, False), (init,) * NSA_KV_HEADS)
    carries = tiles(n_before, carries, True)
    for kh in range(NSA_KV_HEADS):
        _, l_fin, acc = carries[kh]
        o = acc / l_fin
        for gi in range(NSA_GROUP):
            h = kh * NSA_GROUP + gi
            o_ref[0, h * NSA_DH:(h + 1) * NSA_DH, :] = o[:, gi * tq:(gi + 1) * tq] * _gate_row(gates, h, 1)


def _sel_attn(q_t, nm_t, kaug, vs_t, zs_t, gb_col, tq, tk):
    bsz, hd, t = q_t.shape
    tq, tk = min(tq, t), min(tk, t)
    assert t % tq == 0 and t % tk == 0 and tk % tq == 0 and tq & (tq - 1) == 0
    nt = t // tq
    tok = lambda n: pl.BlockSpec((1, n, tq), lambda b, i: (b, 0, i))
    whole = lambda a: pl.BlockSpec((1,) + a.shape[1:], lambda b, i: (b, 0, 0, 0))
    return pl.pallas_call(
        functools.partial(_sel_attn_body, tk=tk),
        grid=(bsz, nt),
        in_specs=[tok(hd), tok(nm_t.shape[1]), whole(kaug), whole(vs_t), tok(LANE), _resident((LANE, 1))],
        out_specs=tok(hd),
        out_shape=jax.ShapeDtypeStruct((bsz, hd, t), F32),
        compiler_params=_params(2, VMEM_BIG),
        name="nsa_selected",
    )(q_t, nm_t, kaug, vs_t, zs_t, gb_col)


def _win_attn_body(q_ref, k_ref, v_ref, zs_ref, gb_ref, o_ref, *, span):
    qi = pl.program_id(1)
    tq = q_ref.shape[2]
    cols = NSA_GROUP * tq
    t0 = qi * tq
    start = pl.multiple_of(jnp.maximum(t0 + tq - span, 0), tq)
    gates = _gate_rows(zs_ref, gb_ref)
    kpos = start + _iota((span, cols), 0)
    tpos = t0 + jnp.bitwise_and(_iota((span, cols), 1), tq - 1)
    ok = (kpos <= tpos) & (tpos - kpos < NSA_WINDOW)
    for kh in range(NSA_KV_HEADS):
        qa = jnp.concatenate([(q_ref[0, h * NSA_DH:(h + 1) * NSA_DH, :] * (NSA_DH ** -0.5)).astype(BF16)
                              for h in range(kh * NSA_GROUP, (kh + 1) * NSA_GROUP)], axis=1)
        s = jnp.where(ok, _mm(k_ref[0, kh, pl.ds(start, span), :], qa), -jnp.inf)
        e = jnp.exp(s - jnp.max(s, axis=0, keepdims=True))
        o = _mm(v_ref[0, kh, :, pl.ds(start, span)], e.astype(BF16)) / jnp.sum(e, axis=0, keepdims=True)
        for gi in range(NSA_GROUP):
            h = kh * NSA_GROUP + gi
            o_ref[0, h * NSA_DH:(h + 1) * NSA_DH, :] = o[:, gi * tq:(gi + 1) * tq] * _gate_row(gates, h, 2)


def _win_attn(q_t, kw, vw_t, zs_t, gb_col, tq):
    bsz, hd, t = q_t.shape
    tq = min(tq, t)
    span = min(NSA_WINDOW + tq, t)
    assert t % tq == 0 and tq & (tq - 1) == 0
    nt = t // tq
    tok = lambda n: pl.BlockSpec((1, n, tq), lambda b, i: (b, 0, i))
    whole = lambda a: pl.BlockSpec((1,) + a.shape[1:], lambda b, i: (b, 0, 0, 0))
    return pl.pallas_call(
        functools.partial(_win_attn_body, span=span),
        grid=(bsz, nt),
        in_specs=[tok(hd), whole(kw), whole(vw_t), tok(LANE), _resident((LANE, 1))],
        out_specs=tok(hd),
        out_shape=jax.ShapeDtypeStruct((bsz, hd, t), F32),
        compiler_params=_params(2),
        name="nsa_window",
    )(q_t, kw, vw_t, zs_t, gb_col)


def _decode_forced(n_past_blk):
    cur = n_past_blk
    return sorted({0, cur - 1, cur} - {-1})


def _cmp_decode_body(q_ref, kvc_ref, gz_ref, gb_ref, o_ref, idx_ref, *, past, n_pick):
    n_pages = kvc_ref.shape[0] // (2 * NSA_KV_HEADS)
    nb = _BLOCKS_PER_PAGE * n_pages

    def block_id(shape):
        pos = _iota(shape, 1)
        page = jnp.where(pos >= n_pages, pos - n_pages, pos)
        return page * _BLOCKS_PER_PAGE + jnp.where(pos >= n_pages, 1, 0)

    lane = block_id((NSA_HEADS, nb))
    hrow = _iota((NSA_HEADS, nb), 0)
    vis = lane * NSA_BLOCK + (NSA_BLOCK - 1) <= past
    q = (q_ref[0] * (NSA_DH ** -0.5)).astype(BF16)
    gates = _sigmoid(gz_ref[0] + gb_ref[...])
    cur = past // NSA_BLOCK
    o_all = jnp.zeros((NSA_HEADS, NSA_DH), F32)
    idx_all = jnp.zeros((SUBLANE, LANE), F32)
    orow = _iota((NSA_HEADS, NSA_DH), 0)
    slot_r, slot_c = _iota((SUBLANE, LANE), 0), _iota((SUBLANE, LANE), 1)
    l1 = block_id((1, nb))
    l1f = l1.astype(F32)
    forced = (l1 == 0) | (l1 == cur) | (l1 == cur - 1)
    for kh in range(NSA_KV_HEADS):
        kc = kvc_ref[kh * n_pages:(kh + 1) * n_pages, :].astype(BF16)
        vc = kvc_ref[(NSA_KV_HEADS + kh) * n_pages:(NSA_KV_HEADS + kh + 1) * n_pages, :].astype(BF16)
        s = jnp.concatenate([_mm(q, kc[:, j * NSA_DH:(j + 1) * NSA_DH], _NT) for j in range(_BLOCKS_PER_PAGE)], axis=1)
        s = jnp.where(vis, s, -jnp.inf)
        m = jnp.max(s, axis=1, keepdims=True)
        m = jnp.where(m == -jnp.inf, 0.0, m)
        e = jnp.exp(s - m)
        p = e / jnp.maximum(jnp.sum(e, axis=1, keepdims=True), TINY)
        mine = (hrow >= kh * NSA_GROUP) & (hrow < (kh + 1) * NSA_GROUP)
        pb = p.astype(BF16)
        o_kh = _mm(pb[:, :n_pages], vc[:, :NSA_DH])
        for j in range(1, _BLOCKS_PER_PAGE):
            o_kh = o_kh + _mm(pb[:, j * n_pages:(j + 1) * n_pages], vc[:, j * NSA_DH:(j + 1) * NSA_DH])
        o_all = jnp.where((orow >= kh * NSA_GROUP) & (orow < (kh + 1) * NSA_GROUP), o_kh, o_all)
        imp = jnp.sum(jnp.where(mine, p, 0.0), axis=0, keepdims=True)
        score = jnp.where(forced, -jnp.inf, imp)
        for r in range(n_pick):
            mx = jnp.max(score, axis=1, keepdims=True)
            pick = jnp.min(jnp.where(score == mx, l1f, float(nb)), axis=1, keepdims=True)
            score = jnp.where(l1f == pick, -jnp.inf, score)
            idx_all = jnp.where((slot_r == kh) & (slot_c == r), pick, idx_all)
    o_ref[0] = o_all * gates[:, 0:1]
    idx_ref[0] = idx_all.astype(jnp.int32)


def _cmp_decode(qh, kvc, gz, gb3, past, n_pick):
    b = qh.shape[0]
    rows = kvc.shape[0] // b
    return pl.pallas_call(
        functools.partial(_cmp_decode_body, past=past, n_pick=n_pick),
        grid=(b,),
        in_specs=[pl.BlockSpec((1, NSA_HEADS, NSA_DH), lambda i: (i, 0, 0)), pl.BlockSpec((rows, kvc.shape[1]), lambda i: (i, 0)),
                  pl.BlockSpec((1, NSA_HEADS, NSA_BRANCHES), lambda i: (i, 0, 0)), _resident((NSA_HEADS, NSA_BRANCHES))],
        out_specs=[pl.BlockSpec((1, NSA_HEADS, NSA_DH), lambda i: (i, 0, 0)), pl.BlockSpec((1, SUBLANE, LANE), lambda i: (i, 0, 0))],
        out_shape=[jax.ShapeDtypeStruct((b, NSA_HEADS, NSA_DH), F32), jax.ShapeDtypeStruct((b, SUBLANE, LANE), jnp.int32)],
        compiler_params=_params(1),
        name="nsa_cmp_decode",
    )(qh, kvc, gz, gb3)


def _selwin_decode_body(info_ref, q_ref, ns_ref, nw_ref, win_ref, *refs, n_shared, n_own, first_win_row):
    n_blk = n_shared + NSA_KV_HEADS * n_own
    page_refs = refs[:n_blk]
    gz_ref, gb_ref, o_ref, wout_ref = refs[n_blk:]
    i = pl.program_id(0)
    q = q_ref[0] * (NSA_DH ** -0.5)
    qb = q.astype(BF16)
    gates = _sigmoid(gz_ref[0] + gb_ref[...])
    hrow = _iota((NSA_HEADS, NSA_DH), 0)
    n_win = win_ref.shape[2]
    wlane = _iota((NSA_HEADS, n_win), 1)
    pblk = jnp.right_shift(_iota((NSA_HEADS, PAGE_SIZE), 1), NSA_BLOCK.bit_length() - 1)
    o_all = jnp.zeros((NSA_HEADS, NSA_DH), F32)
    voff = NSA_KV_HEADS * NSA_DH
    for kh in range(NSA_KV_HEADS):
        ksl = slice(kh * NSA_DH, (kh + 1) * NSA_DH)
        vsl = slice(voff + kh * NSA_DH, voff + (kh + 1) * NSA_DH)
        slots = list(range(n_shared)) + list(range(n_shared + kh * n_own, n_shared + (kh + 1) * n_own))
        s_parts = []
        for sl in slots:
            blk_in_page = jnp.bitwise_and(info_ref[i, sl], _BLOCKS_PER_PAGE - 1)
            sp = _mm(qb, page_refs[sl][0, ksl, :].astype(BF16))
            s_parts.append(jnp.where(pblk == blk_in_page, sp, -jnp.inf))
        kn, vn = ns_ref[0, :, ksl], ns_ref[0, :, vsl]
        sn = jnp.sum(q * kn, axis=1, keepdims=True)
        m = sn
        for sp in s_parts:
            m = jnp.maximum(m, jnp.max(sp, axis=1, keepdims=True))
        en = jnp.exp(sn - m)
        l, acc = en, en * vn
        for sl, sp in zip(slots, s_parts):
            e = jnp.exp(sp - m)
            l = l + jnp.sum(e, axis=1, keepdims=True)
            acc = acc + _mm(e.astype(BF16), page_refs[sl][0, vsl, :].astype(BF16), _NT)
        o_sel = acc / l
        sw = jnp.where(wlane >= first_win_row, _mm(qb, win_ref[0, ksl, :].astype(BF16)), -jnp.inf)
        kwn, vwn = nw_ref[0, :, ksl], nw_ref[0, :, vsl]
        swn = jnp.sum(q * kwn, axis=1, keepdims=True)
        mw = jnp.maximum(jnp.max(sw, axis=1, keepdims=True), swn)
        ew, ewn = jnp.exp(sw - mw), jnp.exp(swn - mw)
        o_win = (_mm(ew.astype(BF16), win_ref[0, vsl, :].astype(BF16), _NT) + ewn * vwn) / (jnp.sum(ew, axis=1, keepdims=True) + ewn)
        keep = (hrow >= kh * NSA_GROUP) & (hrow < (kh + 1) * NSA_GROUP)
        o_all = jnp.where(keep, gates[:, 1:2] * o_sel + gates[:, 2:3] * o_win, o_all)
    o_ref[0] = o_all
    r2, c2 = _iota((NSA_KV_W, NSA_KV_W), 0), _iota((NSA_KV_W, NSA_KV_W), 1)
    new_col = jnp.sum(jnp.where(r2 == c2, jnp.broadcast_to(nw_ref[0], (NSA_KV_W, NSA_KV_W)), 0.0), axis=1, keepdims=True)
    shifted = pltpu.roll(win_ref[0], n_win - 1, 1)
    wout_ref[0] = jnp.where(_iota((NSA_KV_W, n_win), 1) == n_win - 1, new_col, shifted)


def _selwin_decode(info, qh, new_s, new_w, win_t, pool_t, gz, gb3, n_shared, n_own, first_win_row):
    b = qh.shape[0]
    n_blk = info.shape[1]
    row3 = lambda n: pl.BlockSpec((1, 1, n), lambda i, r: (i, 0, 0))

    def page_spec(s):
        return pl.BlockSpec((1, NSA_KV_W, PAGE_SIZE), lambda i, r: (jnp.right_shift(r[i, s], _PAGE_SHIFT), 0, 0))

    win_spec = pl.BlockSpec((1,) + win_t.shape[1:], lambda i, r: (i, 0, 0))
    grid_spec = pltpu.PrefetchScalarGridSpec(
        num_scalar_prefetch=1,
        grid=(b,),
        in_specs=[pl.BlockSpec((1, NSA_HEADS, NSA_DH), lambda i, r: (i, 0, 0)), row3(NSA_KV_W), row3(NSA_KV_W), win_spec]
        + [page_spec(s) for s in range(n_blk)]
        + [pl.BlockSpec((1, NSA_HEADS, NSA_BRANCHES), lambda i, r: (i, 0, 0)),
           pl.BlockSpec((NSA_HEADS, NSA_BRANCHES), lambda i, r: (0, 0))],
        out_specs=[pl.BlockSpec((1, NSA_HEADS, NSA_DH), lambda i, r: (i, 0, 0)), win_spec],
    )
    return pl.pallas_call(
        functools.partial(_selwin_decode_body, n_shared=n_shared, n_own=n_own, first_win_row=first_win_row),
        grid_spec=grid_spec,
        out_shape=[jax.ShapeDtypeStruct((b, NSA_HEADS, NSA_DH), F32), jax.ShapeDtypeStruct(win_t.shape, F32)],
        compiler_params=_params(1),
        name="nsa_selwin_decode",
    )(info, qh, new_s.reshape(b, 1, -1), new_w.reshape(b, 1, -1), win_t, *([pool_t] * n_blk), gz, gb3)


def _feature_major(cache):
    n, rows = cache.shape[:2]
    return jnp.transpose(cache, (0, 2, 3, 4, 1)).reshape(n, NSA_KV_W, rows)


def _pad_cols(w, n):
    return jnp.pad(w, ((0, 0), (0, n - w.shape[1])))


def _even_w_in(w):
    sizes = (GLA_HEADS * GLA_DK, GLA_HEADS * GLA_DK, GLA_HEADS * GLA_DV, GLA_HEADS * GLA_DV, GLA_RANK,
             NSA_HEADS * NSA_DH, NSA_BRANCHES * NSA_KV_W, NSA_HEADS * NSA_BRANCHES)
    cuts = [0]
    for s in sizes:
        cuts.append(cuts[-1] + s)
    gq, gk, gv, gg, ga, nq, nkv, ng = (w[:, cuts[i]:cuts[i + 1]] for i in range(len(sizes)))
    small = _pad_cols(jnp.concatenate([ga, ng], axis=1), LANE)
    sample = jnp.concatenate([gq, gk, gv, gg, nq, nkv, small], axis=1).astype(BF16)
    prompt = jnp.concatenate([gq, gk, gv, gg, nkv, small], axis=1).astype(BF16)
    prompt_t = jnp.concatenate([nq, nkv, small], axis=1).T.astype(BF16)
    return sample, prompt, prompt_t


_EVEN_WIDTHS = (GLA_Z_W, NSA_HEADS * NSA_DH, NSA_KV_W, NSA_KV_W, NSA_KV_W, LANE)
_EVEN_WIDTHS_P = (GLA_Z_W, NSA_KV_W, NSA_KV_W, NSA_KV_W, LANE)
_EVEN_WIDTHS_PT = (NSA_HEADS * NSA_DH, NSA_KV_W, NSA_KV_W, NSA_KV_W, LANE)


def _odd_w_in(w):
    main = M_Z_W
    return jnp.concatenate([w[:, :main], _pad_cols(w[:, main:], LANE)], axis=1).astype(BF16)


_ODD_WIDTHS = (M_Z_W, LANE)


def _compress_weights(pe, w1, w2):
    eye_k = jnp.eye(NSA_KV_HEADS, dtype=F32)
    eye_c = jnp.eye(2, dtype=F32)
    w1big = jnp.einsum("csde,cx,ky->sckdxye", w1, eye_c, eye_k).reshape(NSA_BLOCK * NSA_KV_W, NSA_KV_W)
    w2big = jnp.einsum("ced,cx,ky->ckexyd", w2, eye_c, eye_k).reshape(NSA_KV_W, NSA_KV_W)
    pe_flat = jnp.broadcast_to(pe.transpose(1, 0, 2)[:, :, None, :], (NSA_BLOCK, 2, NSA_KV_HEADS, NSA_DH)).reshape(1, -1)
    return pe_flat, w1big.astype(BF16), w2big.astype(BF16)


def _compress_weights_paged(pe, w1, w2):
    eye_b = jnp.eye(_BLOCKS_PER_PAGE, dtype=F32)
    pe_t = jnp.tile(pe.transpose(0, 2, 1), (1, 1, _BLOCKS_PER_PAGE)).reshape(2 * NSA_DH, PAGE_SIZE)
    w1_t = jnp.einsum("csde,hx->cdhsxe", w1, eye_b).reshape(2, NSA_DH // _D_PER_STEP, _D_PER_STEP * PAGE_SIZE,
                                                            _BLOCKS_PER_PAGE * NSA_CMP_HID)
    w2_t = jnp.einsum("ced,hx->chexd", w2, eye_b).reshape(2, _BLOCKS_PER_PAGE * NSA_CMP_HID, _BLOCKS_PER_PAGE * NSA_DH)
    return pe_t, w1_t.astype(BF16), w2_t.astype(BF16)


def _gate_bias_row(gb):
    return jnp.pad(gb, (GATE_OFF, LANE - GATE_OFF - gb.shape[0])).reshape(1, LANE)


def _split_outs(outs, n):
    return outs[:n], outs[n:]


def _values_t(kv_t):
    b, _, t = kv_t.shape
    return kv_t.reshape(b, 2, NSA_KV_HEADS, NSA_DH, t)[:, 1].astype(BF16)


def _kv_rows(kv_t):
    b, _, t = kv_t.shape
    return jnp.transpose(kv_t.reshape(b, 2, NSA_KV_HEADS, NSA_DH, t), (0, 4, 1, 2, 3))


def _heads_major(kv, which):
    b, t, _ = kv.shape
    return kv.reshape(b, t, 2, NSA_KV_HEADS, NSA_DH)[:, :, which].transpose(0, 2, 1, 3).astype(BF16)


def _even_layer_prompt(x, bsz, g, prm):
    m = x.shape[0]
    t = m // bsz
    (z, kvc, kvs, kvw, zs), (q_t, kvc_t, kvs_t, kvw_t, zs_t) = _split_outs(
        _norm_proj(x, g, prm["w_in_p"], _EVEN_WIDTHS_P, TM_PROJ, prm["w_in_pt"], _EVEN_WIDTHS_PT, bsz), len(_EVEN_WIDTHS_P))
    o_gla, s_fin = _gla_prompt(z, zs, prm["wa"], prm["ba"], prm["gn"],
                               jnp.zeros((bsz, GLA_HEADS, GLA_DK, GLA_DV), F32), bsz)
    nb = t // NSA_BLOCK
    nbp = -(-nb // LANE) * LANE
    kvcmp = _compress_dense(kvc.reshape(bsz * nb, NSA_BLOCK * NSA_KV_W), prm["pe"], prm["w1"], prm["w2"])
    kvcmp = jnp.pad(kvcmp.reshape(bsz, nb, NSA_KV_W), ((0, 0), (0, nbp - nb), (0, 0))).reshape(bsz * nbp, NSA_KV_W)
    o_cmp, nm = _cmp_sel(q_t, kvcmp, zs_t, prm["gb_col"], nbp, TQ_CMP)
    onehot = (jnp.arange(t)[:, None] // NSA_BLOCK == jnp.arange(nbp)[None, :]).astype(BF16)
    ks = _heads_major(kvs.reshape(bsz, t, -1), 0)
    kaug = jnp.concatenate([jnp.broadcast_to(onehot, (bsz, NSA_KV_HEADS, t, nbp)), ks], axis=-1)
    o_sel = _sel_attn(q_t, nm, kaug, _values_t(kvs_t), zs_t, prm["gb_col"], TQ_ATTN, TK_SEL)
    o_win = _win_attn(q_t, _heads_major(kvw.reshape(bsz, t, -1), 0), _values_t(kvw_t), zs_t, prm["gb_col"], TQ_ATTN)
    y = _out_proj(x, [o_gla], (1,), prm["w_out"], TM_OUT, [o_cmp, o_sel, o_win], bsz)
    n_keep = min(NSA_WINDOW, t)
    return y, s_fin, _kv_rows(kvc_t), _kv_rows(kvs_t), _kv_rows(kvw_t[:, :, t - n_keep:])


def _even_layer_sample(x, g, prm, gla_state, cmp_pool, sel_pool, win_buf, page_table):
    b = x.shape[0]
    n_pages = page_table.shape[1]
    past = n_pages * PAGE_SIZE
    n_past_blk = past // NSA_BLOCK
    z, nq, kvc, kvs, kvw, zs = _norm_proj(x, g, prm["w_in"], _EVEN_WIDTHS, b)
    o_gla, s_new = _gla_decode(z, zs, prm["wa"], prm["ba"], prm["gn"], gla_state)
    kvcmp = _compress_paged(_feature_major(cmp_pool), page_table, prm["pe_t"], prm["w1_t"], prm["w2_t"])
    qh = nq.reshape(b, NSA_HEADS, NSA_DH)
    gz = zs[:, GATE_OFF:GATE_OFF + NSA_HEADS * NSA_BRANCHES].reshape(b, NSA_HEADS, NSA_BRANCHES)
    forced = _decode_forced(n_past_blk)
    n_pick = NSA_TOP_N - len(forced)
    assert n_past_blk - len(forced) + 1 >= n_pick
    o_cmp, idx = _cmp_decode(qh, kvcmp, gz, prm["gb3"], past, n_pick)
    shared = [f for f in forced if f < n_past_blk]
    logical = jnp.concatenate([jnp.broadcast_to(jnp.asarray(shared, jnp.int32), (b, len(shared))),
                               idx[:, :NSA_KV_HEADS, :n_pick].reshape(b, NSA_KV_HEADS * n_pick)], axis=1)
    per_page = _BLOCKS_PER_PAGE
    phys = jnp.take_along_axis(page_table, logical // per_page, axis=1) * per_page + logical % per_page
    win_keep = win_buf.shape[1]
    assert win_keep >= 1
    first_win_row = max(win_keep - NSA_WINDOW + 1, 0)
    o_sw, win_new_t = _selwin_decode(phys.astype(jnp.int32), qh, kvs, kvw, _feature_major(win_buf), _feature_major(sel_pool),
                                     gz, prm["gb3"], len(shared), n_pick, first_win_row)
    hd = NSA_HEADS * NSA_DH
    y = _out_proj(x, [o_gla, o_cmp.reshape(b, hd), o_sw.reshape(b, hd)], (1, 2), prm["w_out"], b)
    kv_shape = (b, 1, 2, NSA_KV_HEADS, NSA_DH)
    win_new = jnp.transpose(win_new_t.reshape(b, 2, NSA_KV_HEADS, NSA_DH, win_keep), (0, 4, 1, 2, 3))
    return y, s_new, kvc.reshape(kv_shape), kvs.reshape(kv_shape), win_new


def _odd_layer_prompt(x, bsz, g, prm):
    z, zs = _norm_proj(x, g, prm["w_in"], _ODD_WIDTHS, TM_PROJ)
    h, c_aug, m_fin = _mlstm_prompt(z, zs, prm["bias"], prm["mn"], jnp.zeros((bsz, M_HEADS, M_DQK, M_AUG), F32),
                                    jnp.zeros((bsz, 1, LANE), F32), bsz)
    y = _out_proj(x, [h], (1,), prm["w_out"], TM_OUT)
    return y, c_aug[..., :M_DV], c_aug[..., M_DV], m_fin[:, 0, :M_HEADS]


def _odd_layer_sample(x, g, prm, c0, n0, m0):
    b = x.shape[0]
    z, zs = _norm_proj(x, g, prm["w_in"], _ODD_WIDTHS, b)
    h, cn, nn, mn = _mlstm_decode(z, zs, prm["bias"], prm["mn"], c0, n0, m0)
    return _out_proj(x, [h], (1,), prm["w_out"], b), cn, nn, mn


def kernel(x_prompt, x_sample, cache_cmp_kv, cache_sel_kv, cache_win_kv, state_gla, state_mlstm_c, state_mlstm_n, state_mlstm_m, state_ffn_conv, page_table, norm_mix, norm_ffn, norm_final, even_w_in, even_w_out, gla_w_a2, gla_b_a, gla_norm, nsa_cmp_pe, nsa_cmp_w1, nsa_cmp_w2, nsa_gate_b, odd_w_in, odd_w_out, mlstm_b_i, mlstm_b_f, mlstm_norm, ffn_w_up, ffn_conv_w, ffn_conv_b, ffn_w_down):
    bp, t, d = x_prompt.shape
    bs = x_sample.shape[0]
    assert x_sample.shape[1] == 1
    depth = norm_mix.shape[0]
    f = ffn_conv_w.shape[2]
    xp = x_prompt.reshape(bp * t, d)
    xs = x_sample.reshape(bs, d)
    outs = {k: [] for k in ("cmp_p", "cmp_s", "sel_p", "sel_s", "win_p", "win_s", "gla_p", "gla_s",
                            "mc_p", "mc_s", "mn_p", "mn_s", "mm_p", "mm_s", "cv_p", "cv_s")}
    for l in range(depth):
        if l % 2 == 0:
            e = l // 2
            pe, w1, w2 = _compress_weights(nsa_cmp_pe[e], nsa_cmp_w1[e], nsa_cmp_w2[e])
            pe_t, w1_t, w2_t = _compress_weights_paged(nsa_cmp_pe[e], nsa_cmp_w1[e], nsa_cmp_w2[e])
            w_in_s, w_in_p, w_in_pt = _even_w_in(even_w_in[e])
            prm = dict(pe_t=pe_t, w1_t=w1_t, w2_t=w2_t, w_in=w_in_s, w_in_p=w_in_p, w_in_pt=w_in_pt,
                       w_out=even_w_out[e].astype(BF16),
                       wa=jnp.pad(gla_w_a2[e], ((0, LANE - GLA_RANK), (0, 0))), ba=gla_b_a[e].reshape(1, -1),
                       gn=gla_norm[e].reshape(1, -1), pe=pe, w1=w1, w2=w2, gb_col=_gate_bias_row(nsa_gate_b[e]).reshape(LANE, 1),
                       gb3=nsa_gate_b[e].reshape(NSA_HEADS, NSA_BRANCHES))
            xp, s_, c_, k_, w_ = _even_layer_prompt(xp, bp, norm_mix[l], prm)
            outs["gla_p"].append(s_); outs["cmp_p"].append(c_); outs["sel_p"].append(k_); outs["win_p"].append(w_)
            xs, s_, c_, k_, w_ = _even_layer_sample(xs, norm_mix[l], prm, state_gla[e], cache_cmp_kv[e], cache_sel_kv[e],
                                                    cache_win_kv[e], page_table)
            outs["gla_s"].append(s_); outs["cmp_s"].append(c_); outs["sel_s"].append(k_); outs["win_s"].append(w_)
        else:
            o = l // 2
            bias = jnp.pad(jnp.concatenate([mlstm_b_i[o], mlstm_b_f[o]]), (0, LANE - 2 * M_HEADS)).reshape(1, LANE)
            prm = dict(w_in=_odd_w_in(odd_w_in[o]), w_out=odd_w_out[o].astype(BF16), bias=bias, mn=mlstm_norm[o].reshape(1, -1))
            xp, c_, n_, m_ = _odd_layer_prompt(xp, bp, norm_mix[l], prm)
            outs["mc_p"].append(c_); outs["mn_p"].append(n_); outs["mm_p"].append(m_)
            xs, c_, n_, m_ = _odd_layer_sample(xs, norm_mix[l], prm, state_mlstm_c[o], state_mlstm_n[o], state_mlstm_m[o])
            outs["mc_s"].append(c_); outs["mn_s"].append(n_); outs["mm_s"].append(m_)
        final = l == depth - 1
        wup, wd = ffn_w_up[l].astype(BF16), ffn_w_down[l].astype(BF16)
        xp, cv = _ffn_prompt(xp, bp, norm_ffn[l], wup, ffn_conv_w[l], ffn_conv_b[l], wd,
                             jnp.zeros((bp, CONV_W - 1, f), F32), norm_final, final, TM_FFN)
        outs["cv_p"].append(cv)
        xs, cv = _ffn_decode(xs, norm_ffn[l], wup, ffn_conv_w[l], ffn_conv_b[l], wd, state_ffn_conv[l], norm_final, final)
        outs["cv_s"].append(cv)
    st = lambda k: jnp.stack(outs[k])
    return (xp.reshape(bp, t, d), xs.reshape(bs, 1, d),
            st("cmp_p"), st("cmp_s"), st("sel_p"), st("sel_s"), st("win_p"), st("win_s"), st("gla_p"), st("gla_s"),
            st("mc_p"), st("mc_s"), st("mn_p"), st("mn_s"), st("mm_p"), st("mm_s"), st("cv_p"), st("cv_s"))
```

```python
import functools

import jax
import jax.numpy as jnp
from jax import lax
from jax.experimental import pallas as pl
from jax.experimental.pallas import tpu as pltpu

F32 = jnp.float32
BF16 = jnp.bfloat16

GLA_HEADS, GLA_DK, GLA_DV, GLA_RANK, GLA_TAU, GLA_CHUNK = 4, 64, 128, 16, 16.0, 64
NSA_HEADS, NSA_KV_HEADS, NSA_GROUP, NSA_DH = 8, 2, 4, 64
NSA_BRANCHES, NSA_BLOCK, NSA_TOP_N, NSA_WINDOW, NSA_CMP_HID = 3, 64, 16, 512, 64
M_HEADS, M_DQK, M_DV, M_CHUNK = 4, 128, 256, 64
CONV_W = 3
PAGE_SIZE = 128
EPS, TINY = 1e-6, 1e-30

LANE = 128
SUBLANE = 8
VMEM_BIG = 52 * 1024 * 1024
VMEM_MID = 40 * 1024 * 1024

MASK_BIG = 32768.0

NSA_KV_W = 2 * NSA_KV_HEADS * NSA_DH
GLA_Z_W = 2 * GLA_HEADS * GLA_DK + 2 * GLA_HEADS * GLA_DV
M_Z_W = 2 * M_HEADS * M_DQK + 2 * M_HEADS * M_DV
M_AUG = M_DV + LANE
GATE_OFF = GLA_RANK
TM_PROJ = 256
TM_OUT = 512
TM_FFN = 256
TQ_CMP = 512
TQ_ATTN = 128
TK_SEL = 512
V_AUG = NSA_DH + 16
_BLOCKS_PER_PAGE = PAGE_SIZE // NSA_BLOCK
_PAGE_SHIFT = _BLOCKS_PER_PAGE.bit_length() - 1

_NN = (((1,), (0,)), ((), ()))
_NT = (((1,), (1,)), ((), ()))
_TN = (((0,), (0,)), ((), ()))


def _mm(a, b, dims=_NN):
    return lax.dot_general(a, b, dims, preferred_element_type=F32)


def _mm_bf(a, b, dims=_NN):
    return _mm(a.astype(BF16), b.astype(BF16), dims)


def _split_bf16(x, n):
    parts, r = [], x
    for _ in range(n):
        p = r.astype(BF16)
        parts.append(p)
        r = r - p.astype(F32)
    return parts


def _mm_sel(sel, x, dims=_NN, x_is_rhs=True):
    out = None
    for p in _split_bf16(x, 3):
        t = _mm(sel, p, dims) if x_is_rhs else _mm(p, sel, dims)
        out = t if out is None else out + t
    return out


def _mm_hp(a, b, dims=_NN):
    a1, a2 = _split_bf16(a, 2)
    b1, b2 = _split_bf16(b, 2)
    return _mm(a1, b1, dims) + (_mm(a1, b2, dims) + _mm(a2, b1, dims))


def _gelu(x):
    return 0.5 * x * (1.0 + jnp.tanh(0.7978845608028654 * (x + 0.044715 * (x * x * x))))


def _sigmoid(x):
    return 1.0 / (1.0 + jnp.exp(-x))


def _logsigmoid(x):
    return jnp.minimum(x, 0.0) - jnp.log(1.0 + jnp.exp(-jnp.abs(x)))


def _rms(x, g):
    return x * lax.rsqrt(jnp.mean(x * x, axis=-1, keepdims=True) + EPS) * g


def _iota(shape, dim):
    return lax.broadcasted_iota(jnp.int32, shape, dim)


def _params(n_axes, vmem=VMEM_MID):
    return pltpu.CompilerParams(dimension_semantics=("arbitrary",) * n_axes, vmem_limit_bytes=vmem)


def _resident(shape):
    nd = len(shape)
    return pl.BlockSpec(shape, lambda *_: (0,) * nd, pipeline_mode=pl.Buffered(1))


def _norm_proj_body(x_ref, g_ref, w_ref, *refs, widths, t_widths):
    xb = _rms(x_ref[...], g_ref[...]).astype(BF16)
    o_refs = refs[1:] if t_widths else refs
    off = 0
    for o_ref, n in zip(o_refs[:len(widths)], widths):
        o_ref[...] = _mm(xb, w_ref[:, off:off + n])
        off += n
    off = 0
    for o_ref, n in zip(o_refs[len(widths):], t_widths):
        o_ref[0] = _mm(refs[0][off:off + n, :], xb, _NT)
        off += n


def _norm_proj(x, g, w, widths, tm, wt=None, t_widths=(), bsz=1):
    m, d = x.shape
    tm = min(tm, m)
    nt = m // bsz // tm
    assert m % tm == 0 and sum(widths) == w.shape[1] and (m // bsz) % tm == 0
    t_in = [_resident(wt.shape)] if t_widths else []
    t_args = [wt] if t_widths else []
    return pl.pallas_call(
        functools.partial(_norm_proj_body, widths=tuple(widths), t_widths=tuple(t_widths)),
        grid=(m // tm,),
        in_specs=[pl.BlockSpec((tm, d), lambda i: (i, 0)), _resident((1, d)), _resident(w.shape)] + t_in,
        out_specs=[pl.BlockSpec((tm, n), lambda i: (i, 0)) for n in widths]
        + [pl.BlockSpec((1, n, tm), lambda i: (i // nt, 0, i % nt)) for n in t_widths],
        out_shape=[jax.ShapeDtypeStruct((m, n), F32) for n in widths]
        + [jax.ShapeDtypeStruct((bsz, n, m // bsz), F32) for n in t_widths],
        compiler_params=_params(1),
        name="norm_proj",
    )(x, g.reshape(1, d), w, *t_args)


def _out_proj_body(x_ref, *refs, groups, n_t):
    h_refs, w_ref, o_ref = refs[:-2], refs[-2], refs[-1]
    acc = x_ref[...]
    i = off = 0
    for gsz in groups:
        h = h_refs[i][...]
        for j in range(1, gsz):
            h = h + h_refs[i + j][...]
        i += gsz
        n = h.shape[1]
        acc = acc + _mm(h.astype(BF16), w_ref[off:off + n, :])
        off += n
    if n_t:
        ht = h_refs[i][0]
        for j in range(1, n_t):
            ht = ht + h_refs[i + j][0]
        acc = acc + _mm(ht.astype(BF16), w_ref[off:off + ht.shape[0], :], _TN)
    o_ref[...] = acc


def _out_proj(x, hs, groups, w, tm, t_hs=(), bsz=1):
    m, d = x.shape
    tm = min(tm, m)
    nt = m // bsz // tm
    assert m % tm == 0 and (m // bsz) % tm == 0
    return pl.pallas_call(
        functools.partial(_out_proj_body, groups=tuple(groups), n_t=len(t_hs)),
        grid=(m // tm,),
        in_specs=[pl.BlockSpec((tm, d), lambda i: (i, 0))]
        + [pl.BlockSpec((tm, h.shape[1]), lambda i: (i, 0)) for h in hs]
        + [pl.BlockSpec((1, h.shape[1], tm), lambda i: (i // nt, 0, i % nt)) for h in t_hs]
        + [_resident(w.shape)],
        out_specs=pl.BlockSpec((tm, d), lambda i: (i, 0)),
        out_shape=jax.ShapeDtypeStruct((m, d), F32),
        compiler_params=_params(1),
        name="out_proj",
    )(x, *hs, *t_hs, w)


def _ffn_chunks(xn, resid, wup_ref, cw_ref, cb_ref, wd_ref, fc, prev_fn, keep_fn):
    f = cw_ref.shape[1]
    n = f // fc

    def up_proj(c):
        return _mm(xn, wup_ref[:, c * fc:(c + 1) * fc]), _mm(xn, wup_ref[:, f + c * fc:f + (c + 1) * fc])

    acc = resid
    nxt = up_proj(0)
    for c in range(n):
        sl = slice(c * fc, (c + 1) * fc)
        gp, up = nxt
        if c + 1 < n:
            nxt = up_proj(c + 1)
        g2, g1 = prev_fn(gp, sl)
        a = cb_ref[:, sl] + g2 * cw_ref[0:1, sl] + g1 * cw_ref[1:2, sl] + gp * cw_ref[2:3, sl]
        acc = acc + _mm((_gelu(a) * up).astype(BF16), wd_ref[sl, :])
        keep_fn(gp, sl)
    return acc


def _ffn_prompt_body(x_ref, g_ref, wup_ref, cw_ref, cb_ref, wd_ref, st_ref, gf_ref, y_ref, ns_ref, carry, *, final, fc):
    t = pl.program_id(1)
    tm = x_ref.shape[0]

    @pl.when(t == 0)
    def _():
        carry[...] = st_ref[0]

    x = x_ref[...]
    xn = _rms(x, g_ref[...]).astype(BF16)
    row = _iota((tm, fc), 0)

    def prev_fn(gp, sl):
        c0, c1 = carry[0:1, sl], carry[1:2, sl]
        g1 = jnp.where(row == 0, c1, pltpu.roll(gp, 1, 0))
        g2 = jnp.where(row == 0, c0, jnp.where(row == 1, c1, pltpu.roll(gp, 2, 0)))
        return g2, g1

    def keep_fn(gp, sl):
        carry[:, sl] = gp[tm - 2:tm, :]

    acc = _ffn_chunks(xn, x, wup_ref, cw_ref, cb_ref, wd_ref, fc, prev_fn, keep_fn)
    y_ref[...] = _rms(acc, gf_ref[...]) if final else acc

    @pl.when(t == pl.num_programs(1) - 1)
    def _():
        ns_ref[0] = carry[...]


def _ffn_prompt(x, bsz, g, wup, cw, cb, wd, st, gf, final, tm, fc=256):
    m, d = x.shape
    t = m // bsz
    tm = min(tm, t)
    f = cw.shape[1]
    assert t % tm == 0 and f % fc == 0
    nt = t // tm
    return pl.pallas_call(
        functools.partial(_ffn_prompt_body, final=final, fc=fc),
        grid=(bsz, nt),
        in_specs=[pl.BlockSpec((tm, d), lambda b, i: (b * nt + i, 0)), _resident((1, d)), _resident(wup.shape),
                  _resident(cw.shape), _resident((1, f)), _resident(wd.shape),
                  pl.BlockSpec((1, CONV_W - 1, f), lambda b, i: (b, 0, 0)), _resident((1, d))],
        out_specs=[pl.BlockSpec((tm, d), lambda b, i: (b * nt + i, 0)),
                   pl.BlockSpec((1, CONV_W - 1, f), lambda b, i: (b, 0, 0))],
        out_shape=[jax.ShapeDtypeStruct((m, d), F32), jax.ShapeDtypeStruct((bsz, CONV_W - 1, f), F32)],
        scratch_shapes=[pltpu.VMEM((CONV_W - 1, f), F32)],
        compiler_params=_params(2, VMEM_BIG),
        name="ffn_prompt",
    )(x, g.reshape(1, d), wup, cw, cb.reshape(1, f), wd, st, gf.reshape(1, d))


def _ffn_decode_body(x_ref, g_ref, wup_ref, cw_ref, cb_ref, wd_ref, s0_ref, s1_ref, gf_ref, y_ref, gp_ref, *, final, fc):
    x = x_ref[...]
    xn = _rms(x, g_ref[...]).astype(BF16)

    def prev_fn(gp, sl):
        return s0_ref[:, sl], s1_ref[:, sl]

    def keep_fn(gp, sl):
        gp_ref[:, sl] = gp

    acc = _ffn_chunks(xn, x, wup_ref, cw_ref, cb_ref, wd_ref, fc, prev_fn, keep_fn)
    y_ref[...] = _rms(acc, gf_ref[...]) if final else acc


def _ffn_decode(x, g, wup, cw, cb, wd, st, gf, final, fc=256):
    m, d = x.shape
    f = cw.shape[1]
    y, gp = pl.pallas_call(
        functools.partial(_ffn_decode_body, final=final, fc=fc),
        grid=(1,),
        in_specs=[_resident((m, d)), _resident((1, d)), _resident(wup.shape), _resident(cw.shape), _resident((1, f)),
                  _resident(wd.shape), _resident((m, f)), _resident((m, f)), _resident((1, d))],
        out_specs=[pl.BlockSpec((m, d), lambda i: (0, 0)), pl.BlockSpec((m, f), lambda i: (0, 0))],
        out_shape=[jax.ShapeDtypeStruct((m, d), F32), jax.ShapeDtypeStruct((m, f), F32)],
        compiler_params=_params(1, VMEM_BIG),
        name="ffn_decode",
    )(x, g.reshape(1, d), wup, cw, cb.reshape(1, f), wd, st[:, 0], st[:, 1], gf.reshape(1, d))
    return y, jnp.stack([st[:, 1], gp], axis=1)


def _gla_finish(o, gg, gn):
    return _rms(o, gn) * (gg * _sigmoid(gg))


def _gla_body(z_ref, zs_ref, wa_ref, ba_ref, gn_ref, s0_ref, o_ref, sfin_ref, s_scr):
    t = pl.program_id(1)

    @pl.when(t == 0)
    def _():
        s_scr[...] = s0_ref[...]

    _round_robin([_gla_chunk(z_ref.at[bi], zs_ref.at[bi], wa_ref, ba_ref, gn_ref, o_ref.at[bi], s_scr.at[bi])
                  for bi in range(z_ref.shape[0])])

    @pl.when(t == pl.num_programs(1) - 1)
    def _():
        sfin_ref[...] = s_scr[...]


def _round_robin(chains):
    chains = list(chains)
    while chains:
        chains = [c for c in chains if next(c, _DONE) is not _DONE]


_DONE = object()


def _gla_chunk(z_ref, zs_ref, wa_ref, ba_ref, gn_ref, o_ref, s_scr):
    c = z_ref.shape[0]
    hk = GLA_HEADS * GLA_DK
    la = _logsigmoid(_mm_hp(zs_ref[...], wa_ref[...]) + ba_ref[...]) * (1.0 / GLA_TAU)
    yield
    r, cidx = _iota((c, c), 0), _iota((c, c), 1)
    tri = r >= cidx
    cum = _mm_sel(jnp.where(tri, 1.0, 0.0).astype(BF16), la)
    yield
    last = cum[c - 1:c, :]
    eq, ek, ekl, el = jnp.exp(cum), jnp.exp(-cum), jnp.exp(last - cum), jnp.exp(last)
    gn = gn_ref[...]
    for h in range(GLA_HEADS):
        ks = slice(h * GLA_DK, (h + 1) * GLA_DK)
        vs = slice(2 * hk + h * GLA_DV, 2 * hk + (h + 1) * GLA_DV)
        gs = slice(2 * hk + GLA_HEADS * GLA_DV + h * GLA_DV, 2 * hk + GLA_HEADS * GLA_DV + (h + 1) * GLA_DV)
        q = z_ref[:, ks] * (GLA_DK ** -0.5)
        k = z_ref[:, hk + h * GLA_DK:hk + (h + 1) * GLA_DK]
        v = z_ref[:, vs]
        qt = (q * eq[:, ks]).astype(BF16)
        s_old = s_scr[h]
        att = jnp.where(tri, _mm(qt, (k * ek[:, ks]).astype(BF16), _NT), 0.0)
        vb = v.astype(BF16)
        o_inter = _mm(qt, s_old.astype(BF16))
        kv_new = _mm((k * ekl[:, ks]).astype(BF16), vb, _TN)
        yield
        o = o_inter + _mm(att.astype(BF16), vb)
        ecol = jnp.sum(jnp.where(r == cidx, jnp.broadcast_to(el[:, ks], (c, c)), 0.0), axis=1, keepdims=True)
        s_scr[h] = ecol * s_old + kv_new
        yield
        o_ref[:, h * GLA_DV:(h + 1) * GLA_DV] = _gla_finish(o, z_ref[:, gs], gn)


def _seq_group(bsz):
    return next(n for n in (4, 2, 1) if bsz % n == 0)


def _gla_prompt(z, zs, wa, ba, gn, s0, bsz):
    m = z.shape[0]
    t = m // bsz
    c = GLA_CHUNK
    assert t % c == 0 and GLA_DK == c
    nc = t // c
    hk = GLA_HEADS * GLA_DK
    nb = _seq_group(bsz)
    st = (nb, GLA_HEADS, GLA_DK, GLA_DV)
    tok = lambda n: pl.BlockSpec((nb, c, n), lambda b, i: (b, i, 0))
    o, s_fin = pl.pallas_call(
        _gla_body,
        grid=(bsz // nb, nc),
        in_specs=[tok(GLA_Z_W), tok(LANE), _resident((LANE, hk)), _resident((1, hk)), _resident((1, GLA_DV)),
                  pl.BlockSpec(st, lambda b, i: (b, 0, 0, 0))],
        out_specs=[tok(GLA_HEADS * GLA_DV), pl.BlockSpec(st, lambda b, i: (b, 0, 0, 0))],
        out_shape=[jax.ShapeDtypeStruct((bsz, t, GLA_HEADS * GLA_DV), F32),
                   jax.ShapeDtypeStruct((bsz, GLA_HEADS, GLA_DK, GLA_DV), F32)],
        scratch_shapes=[pltpu.VMEM(st, F32)],
        compiler_params=_params(2),
        name="gla_prompt",
    )(z.reshape(bsz, t, -1), zs.reshape(bsz, t, -1), wa, ba, gn, s0)
    return o.reshape(m, -1), s_fin


def _gla_decode_body(z_ref, zs_ref, wa_ref, ba_ref, gn_ref, s_ref, o_ref, sn_ref):
    hk = GLA_HEADS * GLA_DK
    rows = SUBLANE
    z = jnp.broadcast_to(z_ref[0], (rows, GLA_Z_W))
    ga = jnp.broadcast_to(zs_ref[0], (rows, LANE))
    la = _logsigmoid(_mm_hp(ga, wa_ref[...]) + ba_ref[...]) * (1.0 / GLA_TAU)
    ea = jnp.exp(la)
    r, cidx = _iota((GLA_DK, GLA_DK), 0), _iota((GLA_DK, GLA_DK), 1)
    row0 = _iota((rows, GLA_DK), 0) == 0
    gn = gn_ref[...]
    for h in range(GLA_HEADS):
        ks = slice(h * GLA_DK, (h + 1) * GLA_DK)
        q = z[:, ks] * (GLA_DK ** -0.5)
        k = z[:, hk + h * GLA_DK:hk + (h + 1) * GLA_DK]
        v = z[:, 2 * hk + h * GLA_DV:2 * hk + (h + 1) * GLA_DV]
        gg = z[:, 2 * hk + GLA_HEADS * GLA_DV + h * GLA_DV:2 * hk + GLA_HEADS * GLA_DV + (h + 1) * GLA_DV]
        s_old = s_ref[0, h]
        o = _mm_hp(q * ea[:, ks], s_old) + jnp.sum(q * k, axis=1, keepdims=True) * v
        ecol = jnp.sum(jnp.where(r == cidx, jnp.broadcast_to(ea[0:1, ks], (GLA_DK, GLA_DK)), 0.0), axis=1, keepdims=True)
        sn_ref[0, h] = ecol * s_old + _mm_hp(jnp.where(row0, k, 0.0), v, _TN)
        o_ref[0, :, h * GLA_DV:(h + 1) * GLA_DV] = _gla_finish(o, gg, gn)[0:1]


def _gla_decode(z, zs, wa, ba, gn, s0):
    b = z.shape[0]
    hk = GLA_HEADS * GLA_DK
    st = (1, GLA_HEADS, GLA_DK, GLA_DV)
    o, sn = pl.pallas_call(
        _gla_decode_body,
        grid=(b,),
        in_specs=[pl.BlockSpec((1, 1, GLA_Z_W), lambda i: (i, 0, 0)), pl.BlockSpec((1, 1, LANE), lambda i: (i, 0, 0)),
                  _resident((LANE, hk)), _resident((1, hk)), _resident((1, GLA_DV)),
                  pl.BlockSpec(st, lambda i: (i, 0, 0, 0))],
        out_specs=[pl.BlockSpec((1, 1, GLA_HEADS * GLA_DV), lambda i: (i, 0, 0)), pl.BlockSpec(st, lambda i: (i, 0, 0, 0))],
        out_shape=[jax.ShapeDtypeStruct((b, 1, GLA_HEADS * GLA_DV), F32), jax.ShapeDtypeStruct((b,) + st[1:], F32)],
        compiler_params=_params(1),
        name="gla_decode",
    )(z.reshape(b, 1, -1), zs.reshape(b, 1, -1), wa, ba, gn, s0)
    return o.reshape(b, -1), sn


def _mlstm_finish(hh, og, mn):
    return _rms(hh, mn) * _sigmoid(og)


def _mlstm_body(z_ref, zs_ref, bias_ref, mn_ref, c0_ref, m0_ref, h_ref, cfin_ref, mfin_ref, c_scr, m_scr):
    t = pl.program_id(1)

    @pl.when(t == 0)
    def _():
        c_scr[...] = c0_ref[...]
        m_scr[...] = m0_ref[...]

    _round_robin([_mlstm_chunk(z_ref.at[bi], zs_ref.at[bi], bias_ref, mn_ref, h_ref.at[bi], c_scr.at[bi], m_scr.at[bi])
                  for bi in range(z_ref.shape[0])])

    @pl.when(t == pl.num_programs(1) - 1)
    def _():
        cfin_ref[...] = c_scr[...]
        mfin_ref[...] = m_scr[...]


def _mlstm_chunk(z_ref, zs_ref, bias_ref, mn_ref, h_ref, c_scr, m_scr):
    L = z_ref.shape[0]
    hq = M_HEADS * M_DQK
    g = zs_ref[...] + bias_ref[...]
    lane = _iota((L, LANE), 1)
    gx = jnp.where((lane >= M_HEADS) & (lane < 2 * M_HEADS), _logsigmoid(g), g)
    r, cidx = _iota((L, L), 0), _iota((L, L), 1)
    tri = r >= cidx
    cum_c = _mm_sel(jnp.where(tri, 1.0, 0.0).astype(BF16), gx)
    rows = _mm_sel(jnp.where(r == cidx, 1.0, 0.0).astype(BF16), gx, _TN, x_is_rhs=False)
    cum_r = _mm_sel(jnp.where(r <= cidx, 1.0, 0.0).astype(BF16), gx, _TN, x_is_rhs=False)
    yield
    ones_col = jnp.where(_iota((L, LANE), 1) == 0, 1.0, 0.0).astype(BF16)
    m_all = m_scr[...]
    m_new_all = m_all
    for h in range(M_HEADS):
        q = z_ref[:, h * M_DQK:(h + 1) * M_DQK].astype(BF16)
        k = z_ref[:, hq + h * M_DQK:hq + (h + 1) * M_DQK] * (M_DQK ** -0.5)
        v = z_ref[:, 2 * hq + h * M_DV:2 * hq + (h + 1) * M_DV]
        og = z_ref[:, 2 * hq + M_HEADS * M_DV + h * M_DV:2 * hq + M_HEADS * M_DV + (h + 1) * M_DV]
        va = jnp.concatenate([v.astype(BF16), ones_col], axis=1)
        ic_r, ic_c = rows[h:h + 1, :], gx[:, h:h + 1]
        cr, cc = cum_r[M_HEADS + h:M_HEADS + h + 1, :], cum_c[:, M_HEADS + h:M_HEADS + h + 1]
        m_old = m_all[0:1, h:h + 1]
        dlog = jnp.where(tri, cc - cr + ic_r, -jnp.inf)
        inter = cc + m_old
        mi = jnp.maximum(inter, jnp.max(dlog, axis=1, keepdims=True))
        w = jnp.exp(dlog - mi)
        wi = jnp.exp(inter - mi)
        qk = _mm(q, k.astype(BF16), _NT)
        c_old = c_scr[h]
        qc = _mm(q, c_old.astype(BF16))
        yield
        s = qk * w
        num = wi * qc + _mm(s.astype(BF16), va)
        yield
        qn = num[:, M_DV:M_DV + 1]
        hh = num[:, :M_DV] / jnp.maximum(jnp.abs(qn), jnp.exp(-mi))
        last = cc[L - 1:L, :]
        gl = last - cc + ic_c
        m_new = jnp.maximum(last + m_old, jnp.max(gl, axis=0, keepdims=True))
        wj = jnp.exp(gl - m_new)
        keep = jnp.exp(last + m_old - m_new)
        c_scr[h] = keep * c_old + _mm((wj * k).astype(BF16), va, _TN)
        yield
        m_new_all = jnp.where(_iota((1, LANE), 1) == h, m_new, m_new_all)
        h_ref[:, h * M_DV:(h + 1) * M_DV] = _mlstm_finish(hh, og, mn_ref[:, h * M_DV:(h + 1) * M_DV])
    m_scr[...] = m_new_all


def _mlstm_prompt(z, zs, bias, mn, c0, m0, bsz):
    m = z.shape[0]
    t = m // bsz
    L = M_CHUNK
    assert t % L == 0
    nc = t // L
    nb = _seq_group(bsz)
    st, ms = (nb, M_HEADS, M_DQK, M_AUG), (nb, 1, LANE)
    tok = lambda n: pl.BlockSpec((nb, L, n), lambda b, i: (b, i, 0))
    h, c_fin, m_fin = pl.pallas_call(
        _mlstm_body,
        grid=(bsz // nb, nc),
        in_specs=[tok(M_Z_W), tok(LANE), _resident((1, LANE)), _resident((1, M_HEADS * M_DV)),
                  pl.BlockSpec(st, lambda b, i: (b, 0, 0, 0)), pl.BlockSpec(ms, lambda b, i: (b, 0, 0))],
        out_specs=[tok(M_HEADS * M_DV), pl.BlockSpec(st, lambda b, i: (b, 0, 0, 0)), pl.BlockSpec(ms, lambda b, i: (b, 0, 0))],
        out_shape=[jax.ShapeDtypeStruct((bsz, t, M_HEADS * M_DV), F32), jax.ShapeDtypeStruct((bsz,) + st[1:], F32),
                   jax.ShapeDtypeStruct((bsz, 1, LANE), F32)],
        scratch_shapes=[pltpu.VMEM(st, F32), pltpu.VMEM(ms, F32)],
        compiler_params=_params(2),
        name="mlstm_prompt",
    )(z.reshape(bsz, t, -1), zs.reshape(bsz, t, -1), bias, mn, c0, m0)
    return h.reshape(m, -1), c_fin, m_fin


def _mlstm_decode_body(z_ref, zs_ref, bias_ref, mn_ref, c_ref, n_ref, m_ref, h_ref, cn_ref, nn_ref, mo_ref):
    hq = M_HEADS * M_DQK
    rows = SUBLANE
    z = jnp.broadcast_to(z_ref[0], (rows, M_Z_W))
    g = zs_ref[0] + bias_ref[...]
    m_all = m_ref[0]
    m_new_all = m_all
    row0 = _iota((rows, M_DQK), 0) == 0
    for h in range(M_HEADS):
        q = z[:, h * M_DQK:(h + 1) * M_DQK]
        k = z[:, hq + h * M_DQK:hq + (h + 1) * M_DQK] * (M_DQK ** -0.5)
        v = z[:, 2 * hq + h * M_DV:2 * hq + (h + 1) * M_DV]
        og = z[:, 2 * hq + M_HEADS * M_DV + h * M_DV:2 * hq + M_HEADS * M_DV + (h + 1) * M_DV]
        ic = g[:, h:h + 1]
        fl = _logsigmoid(g[:, M_HEADS + h:M_HEADS + h + 1])
        m_old = m_all[:, h:h + 1]
        c_old, n_old = c_ref[0, h], n_ref[0, h:h + 1, :]
        mi = jnp.maximum(fl + m_old, ic)
        w = jnp.exp(ic - mi)
        wi = jnp.exp(fl + m_old - mi)
        s = jnp.sum(q * k, axis=1, keepdims=True) * w
        num = wi * _mm_hp(q, c_old) + s * v
        qn = wi * jnp.sum(q * n_old, axis=1, keepdims=True) + s
        hh = num / jnp.maximum(jnp.abs(qn), jnp.exp(-mi))
        cn_ref[0, h] = wi * c_old + w * _mm_hp(jnp.where(row0, k, 0.0), v, _TN)
        nn_ref[0, h:h + 1, :] = wi * n_old + w * k[0:1]
        m_new_all = jnp.where(_iota((1, LANE), 1) == h, mi, m_new_all)
        h_ref[0, :, h * M_DV:(h + 1) * M_DV] = _mlstm_finish(hh, og, mn_ref[:, h * M_DV:(h + 1) * M_DV])[0:1]
    mo_ref[0] = m_new_all


def _mlstm_decode(z, zs, bias, mn, c0, n0, m0):
    b = z.shape[0]
    cs, ns = (1, M_HEADS, M_DQK, M_DV), (1, M_HEADS, M_DQK)
    m0p = jnp.pad(m0, ((0, 0), (0, LANE - M_HEADS))).reshape(b, 1, LANE)
    row3 = lambda n: pl.BlockSpec((1, 1, n), lambda i: (i, 0, 0))
    h, cn, nn, mo = pl.pallas_call(
        _mlstm_decode_body,
        grid=(b,),
        in_specs=[row3(M_Z_W), row3(LANE), _resident((1, LANE)), _resident((1, M_HEADS * M_DV)),
                  pl.BlockSpec(cs, lambda i: (i, 0, 0, 0)), pl.BlockSpec(ns, lambda i: (i, 0, 0)), row3(LANE)],
        out_specs=[row3(M_HEADS * M_DV), pl.BlockSpec(cs, lambda i: (i, 0, 0, 0)), pl.BlockSpec(ns, lambda i: (i, 0, 0)), row3(LANE)],
        out_shape=[jax.ShapeDtypeStruct((b, 1, M_HEADS * M_DV), F32), jax.ShapeDtypeStruct((b,) + cs[1:], F32),
                   jax.ShapeDtypeStruct((b,) + ns[1:], F32), jax.ShapeDtypeStruct((b, 1, LANE), F32)],
        compiler_params=_params(1),
        name="mlstm_decode",
    )(z.reshape(b, 1, -1), zs.reshape(b, 1, -1), bias, mn, c0, n0, m0p)
    return h.reshape(b, -1), cn, nn, mo[:, 0, :M_HEADS]


def _compress_dense_body(x_ref, pe_ref, w1_ref, w2_ref, o_ref, acc):
    kk = pl.program_id(1)

    @pl.when(kk == 0)
    def _():
        acc[...] = jnp.zeros_like(acc)

    acc[...] += _mm((x_ref[...] + pe_ref[...]).astype(BF16), w1_ref[...])

    @pl.when(kk == pl.num_programs(1) - 1)
    def _():
        o_ref[...] = _mm(_gelu(acc[...]).astype(BF16), w2_ref[...])


def _compress_dense(x, pe, w1, w2, tk=2048):
    r, kdim = x.shape
    tr = r if r <= 512 else 512
    assert r % tr == 0 and kdim % tk == 0
    return pl.pallas_call(
        _compress_dense_body,
        grid=(r // tr, kdim // tk),
        in_specs=[pl.BlockSpec((tr, tk), lambda i, k: (i, k)), pl.BlockSpec((1, tk), lambda i, k: (0, k)),
                  pl.BlockSpec((tk, NSA_KV_W), lambda i, k: (k, 0)), _resident(w2.shape)],
        out_specs=pl.BlockSpec((tr, NSA_KV_W), lambda i, k: (i, 0)),
        out_shape=jax.ShapeDtypeStruct((r, NSA_KV_W), F32),
        scratch_shapes=[pltpu.VMEM((tr, NSA_KV_W), F32)],
        compiler_params=_params(2),
        name="nsa_compress",
    )(x, pe, w1, w2)


_MAX_PAGES_PER_STEP = 32
_D_PER_STEP = 8


def _compress_paged_body(pt_ref, *refs, n_pages, pps):
    page_refs = refs[:pps]
    pe_ref, w1_ref, w2_ref, o_ref, xs = refs[pps:]
    g = pl.program_id(1)
    for p, pr in enumerate(page_refs):
        r0 = pl.multiple_of((g * pps + p) * NSA_KV_W, NSA_KV_W)
        xs[pl.ds(r0, NSA_KV_W), :] = pr[0]

    @pl.when(g == pl.num_programs(1) - 1)
    def _():
        kvd = NSA_KV_HEADS * NSA_DH
        for c in range(2):
            def step(j, acc):
                parts = []
                for dd in range(_D_PER_STEP):
                    d = j * _D_PER_STEP + dd
                    pe_row = pe_ref[pl.ds(c * NSA_DH + d, 1), :]
                    rows = [xs[pl.ds(c * kvd + k * NSA_DH + d, n_pages, stride=NSA_KV_W), :] + pe_row
                            for k in range(NSA_KV_HEADS)]
                    parts.append(jnp.concatenate(rows, axis=0).astype(BF16))
                return acc + _mm(jnp.concatenate(parts, axis=1), w1_ref[c, j])
            hid = lax.fori_loop(0, NSA_DH // _D_PER_STEP, step, jnp.zeros((NSA_KV_HEADS * n_pages, PAGE_SIZE), F32))
            out = _mm(_gelu(hid).astype(BF16), w2_ref[c])
            o_ref[c * NSA_KV_HEADS * n_pages:(c + 1) * NSA_KV_HEADS * n_pages, :] = out


def _compress_paged(pool_t, page_table, pe_t, w1_t, w2_t):
    b, n_pages = page_table.shape
    pps = next(n for n in range(min(_MAX_PAGES_PER_STEP, n_pages), 0, -1) if n_pages % n == 0)
    assert _BLOCKS_PER_PAGE == 2
    rows_out = 2 * NSA_KV_HEADS * n_pages

    def page_spec(p):
        return pl.BlockSpec((1, NSA_KV_W, PAGE_SIZE), lambda i, g, pt: (pt[i, g * pps + p], 0, 0))

    def fixed(a):
        nd = a.ndim
        return pl.BlockSpec(a.shape, lambda i, g, pt: (0,) * nd, pipeline_mode=pl.Buffered(1))

    grid_spec = pltpu.PrefetchScalarGridSpec(
        num_scalar_prefetch=1,
        grid=(b, n_pages // pps),
        in_specs=[page_spec(p) for p in range(pps)] + [fixed(pe_t), fixed(w1_t), fixed(w2_t)],
        out_specs=pl.BlockSpec((rows_out, PAGE_SIZE), lambda i, g, pt: (i, 0)),
        scratch_shapes=[pltpu.VMEM((n_pages * NSA_KV_W, PAGE_SIZE), F32)],
    )
    return pl.pallas_call(
        functools.partial(_compress_paged_body, n_pages=n_pages, pps=pps),
        grid_spec=grid_spec,
        out_shape=jax.ShapeDtypeStruct((b * rows_out, PAGE_SIZE), F32),
        compiler_params=_params(2, VMEM_BIG),
        name="nsa_compress_paged",
    )(page_table, *([pool_t] * pps), pe_t, w1_t, w2_t)


def _gate_rows(zs_ref, gb_ref):
    return _sigmoid(zs_ref[0] + gb_ref[...])


def _gate_row(gates, head, branch):
    r = GATE_OFF + head * NSA_BRANCHES + branch
    return gates[r:r + 1, :]


def _cmp_sel_body(q_ref, kvc_ref, zs_ref, gb_ref, ocmp_ref, nm_ref, sc_scr, thr_scr):
    qi = pl.program_id(1)
    tq = q_ref.shape[2]
    nbp = kvc_ref.shape[0]
    t0 = qi * tq
    tpos = t0 + _iota((nbp, tq), 1)
    blk = _iota((nbp, tq), 0)
    cur = jnp.right_shift(tpos, NSA_BLOCK.bit_length() - 1)
    vis = blk * NSA_BLOCK + (NSA_BLOCK - 1) <= tpos
    forced = (blk == 0) | (blk == cur) | (blk == cur - 1)
    allowed = blk <= cur
    gates = _gate_rows(zs_ref, gb_ref)
    for kh in range(NSA_KV_HEADS):
        kc = kvc_ref[:, kh * NSA_DH:(kh + 1) * NSA_DH].astype(BF16)
        vc = kvc_ref[:, NSA_KV_HEADS * NSA_DH + kh * NSA_DH:NSA_KV_HEADS * NSA_DH + (kh + 1) * NSA_DH].astype(BF16)
        imp = jnp.zeros((nbp, tq), F32)
        for gi in range(NSA_GROUP):
            h = kh * NSA_GROUP + gi
            q = (q_ref[0, h * NSA_DH:(h + 1) * NSA_DH, :] * (NSA_DH ** -0.5)).astype(BF16)
            s = jnp.where(vis, _mm(kc, q), -jnp.inf)
            m = jnp.max(s, axis=0, keepdims=True)
            m = jnp.where(m == -jnp.inf, 0.0, m)
            e = jnp.exp(s - m)
            p = e / jnp.maximum(jnp.sum(e, axis=0, keepdims=True), TINY)
            imp = imp + p
            ocmp_ref[0, h * NSA_DH:(h + 1) * NSA_DH, :] = _mm(vc, p.astype(BF16), _TN) * _gate_row(gates, h, 0)
        score = jnp.where(forced, jnp.inf, jnp.where(allowed, imp, -jnp.inf))
        key = lax.bitcast_convert_type(score, jnp.int32)
        for lc in range(tq // LANE):
            ls = slice(lc * LANE, (lc + 1) * LANE)
            n_live = (t0 + (lc + 1) * LANE - 1) // NSA_BLOCK + 1
            sc_scr[...] = key[:, ls]
            thr_scr[...] = key[:, ls]

            def rank_step(i, cnt):
                for j in (2 * i, 2 * i + 1):
                    row = sc_scr[pl.ds(j, 1), :]
                    thr_scr[pl.ds(j, 1), :] = row + 1
                    cnt = cnt + jnp.where(row >= thr_scr[...], 1, 0)
                return cnt

            cnt = lax.fori_loop(0, n_live // 2, rank_step, jnp.zeros((nbp, LANE), jnp.int32))
            nm_ref[0, kh * nbp:(kh + 1) * nbp, ls] = jnp.where(
                allowed[:, ls], jnp.where(cnt < NSA_TOP_N, 0.0, -MASK_BIG), -MASK_BIG).astype(BF16)


def _cmp_sel(q_t, kvc, zs_t, gb_col, nbp, tq):
    bsz, hd, t = q_t.shape
    tq = min(tq, t)
    nt = t // tq
    tok = lambda n: pl.BlockSpec((1, n, tq), lambda b, i: (b, 0, i))
    return pl.pallas_call(
        _cmp_sel_body,
        grid=(bsz, nt),
        in_specs=[tok(hd), pl.BlockSpec((nbp, NSA_KV_W), lambda b, i: (b, 0)), tok(LANE), _resident((LANE, 1))],
        out_specs=[tok(hd), tok(NSA_KV_HEADS * nbp)],
        out_shape=[jax.ShapeDtypeStruct((bsz, hd, t), F32), jax.ShapeDtypeStruct((bsz, NSA_KV_HEADS * nbp, t), BF16)],
        scratch_shapes=[pltpu.VMEM((nbp, LANE), jnp.int32), pltpu.VMEM((nbp, LANE), jnp.int32)],
        compiler_params=_params(2),
        name="nsa_cmp_select",
    )(q_t, kvc, zs_t, gb_col)


def _sel_attn_body(q_ref, nm_ref, kv_ref, oh_ref, v_ref, zs_ref, gb_ref, o_ref, *, tk):
    qi = pl.program_id(1)
    tq = q_ref.shape[2]
    nbp = nm_ref.shape[1] // NSA_KV_HEADS
    cols = NSA_GROUP * tq
    t0 = qi * tq
    n_before = t0 // tk
    gates = _gate_rows(zs_ref, gb_ref)
    qas = []
    for kh in range(NSA_KV_HEADS):
        nm = nm_ref[0, kh * nbp:(kh + 1) * nbp, :]
        qas.append(jnp.concatenate(
            [jnp.concatenate([nm, (q_ref[0, h * NSA_DH:(h + 1) * NSA_DH, :] * (NSA_DH ** -0.5)).astype(BF16)], axis=0)
             for h in range(kh * NSA_GROUP, (kh + 1) * NSA_GROUP)], axis=1))

    def tile(kh, kt, carry, causal, out):
        m_old, acc = carry
        k0 = pl.multiple_of(kt * tk, tk)
        ka = jnp.concatenate([oh_ref[pl.ds(k0, tk), :],
                              kv_ref[pl.ds(k0, tk), kh * NSA_DH:(kh + 1) * NSA_DH].astype(BF16)], axis=1)
        s = _mm(ka, qas[kh])
        yield
        if causal:
            kpos = k0 + _iota((tk, cols), 0)
            tpos = t0 + jnp.bitwise_and(_iota((tk, cols), 1), tq - 1)
            s = jnp.where(kpos <= tpos, s, -jnp.inf)
        m_new = jnp.maximum(m_old, jnp.max(s, axis=0, keepdims=True))
        alpha = jnp.exp(m_old - m_new)
        p = jnp.exp(s - m_new)
        yield
        out[kh] = (m_new, alpha * acc + _mm(v_ref[0, kh, :, pl.ds(k0, tk)], p.astype(BF16)))

    def tiles(kt, carries, causal):
        out = [None] * NSA_KV_HEADS
        _round_robin([tile(kh, kt, carries[kh], causal, out) for kh in range(NSA_KV_HEADS)])
        return tuple(out)

    init = (jnp.full((1, cols), -jnp.inf, F32), jnp.zeros((V_AUG, cols), F32))
    carries = lax.fori_loop(0, n_before, lambda kt, c: tiles(kt, c, False), (init,) * NSA_KV_HEADS)
    carries = tiles(n_before, carries, True)
    for kh in range(NSA_KV_HEADS):
        acc = carries[kh][1]
        o = acc[:NSA_DH] / acc[NSA_DH:NSA_DH + 1]
        for gi in range(NSA_GROUP):
            h = kh * NSA_GROUP + gi
            o_ref[0, h * NSA_DH:(h + 1) * NSA_DH, :] = o[:, gi * tq:(gi + 1) * tq] * _gate_row(gates, h, 1)


def _sel_attn(q_t, nm_t, kv, onehot, vs_t, zs_t, gb_col, tq, tk):
    bsz, hd, t = q_t.shape
    tq, tk = min(tq, t), min(tk, t)
    assert t % tq == 0 and t % tk == 0 and tk % tq == 0 and tq & (tq - 1) == 0
    nt = t // tq
    tok = lambda n: pl.BlockSpec((1, n, tq), lambda b, i: (b, 0, i))
    return pl.pallas_call(
        functools.partial(_sel_attn_body, tk=tk),
        grid=(bsz, nt),
        in_specs=[tok(hd), tok(nm_t.shape[1]), pl.BlockSpec((t, NSA_KV_W), lambda b, i: (b, 0)), _resident(onehot.shape),
                  pl.BlockSpec((1,) + vs_t.shape[1:], lambda b, i: (b, 0, 0, 0)), tok(LANE), _resident((LANE, 1))],
        out_specs=tok(hd),
        out_shape=jax.ShapeDtypeStruct((bsz, hd, t), F32),
        compiler_params=_params(2, VMEM_BIG),
        name="nsa_selected",
    )(q_t, nm_t, kv, onehot, vs_t, zs_t, gb_col)


def _win_attn_body(q_ref, k_ref, v_ref, zs_ref, gb_ref, o_ref, *, span):
    qi = pl.program_id(1)
    tq = q_ref.shape[2]
    cols = NSA_GROUP * tq
    t0 = qi * tq
    start = pl.multiple_of(jnp.maximum(t0 + tq - span, 0), tq)
    gates = _gate_rows(zs_ref, gb_ref)
    kpos = start + _iota((span, cols), 0)
    tpos = t0 + jnp.bitwise_and(_iota((span, cols), 1), tq - 1)
    ok = (kpos <= tpos) & (tpos - kpos < NSA_WINDOW)
    for kh in range(NSA_KV_HEADS):
        qa = jnp.concatenate([(q_ref[0, h * NSA_DH:(h + 1) * NSA_DH, :] * (NSA_DH ** -0.5)).astype(BF16)
                              for h in range(kh * NSA_GROUP, (kh + 1) * NSA_GROUP)], axis=1)
        kw = k_ref[pl.ds(start, span), kh * NSA_DH:(kh + 1) * NSA_DH].astype(BF16)
        s = jnp.where(ok, _mm(kw, qa), -jnp.inf)
        e = jnp.exp(s - jnp.max(s, axis=0, keepdims=True))
        o = _mm(v_ref[0, kh, :, pl.ds(start, span)], e.astype(BF16)) / jnp.sum(e, axis=0, keepdims=True)
        for gi in range(NSA_GROUP):
            h = kh * NSA_GROUP + gi
            o_ref[0, h * NSA_DH:(h + 1) * NSA_DH, :] = o[:, gi * tq:(gi + 1) * tq] * _gate_row(gates, h, 2)


def _win_attn(q_t, kv, vw_t, zs_t, gb_col, tq):
    bsz, hd, t = q_t.shape
    tq = min(tq, t)
    span = min(NSA_WINDOW + tq, t)
    assert t % tq == 0 and tq & (tq - 1) == 0
    nt = t // tq
    tok = lambda n: pl.BlockSpec((1, n, tq), lambda b, i: (b, 0, i))
    whole = lambda a: pl.BlockSpec((1,) + a.shape[1:], lambda b, i: (b, 0, 0, 0))
    return pl.pallas_call(
        functools.partial(_win_attn_body, span=span),
        grid=(bsz, nt),
        in_specs=[tok(hd), pl.BlockSpec((t, NSA_KV_W), lambda b, i: (b, 0)), whole(vw_t), tok(LANE), _resident((LANE, 1))],
        out_specs=tok(hd),
        out_shape=jax.ShapeDtypeStruct((bsz, hd, t), F32),
        compiler_params=_params(2),
        name="nsa_window",
    )(q_t, kv, vw_t, zs_t, gb_col)


def _decode_forced(n_past_blk):
    cur = n_past_blk
    return sorted({0, cur - 1, cur} - {-1})


def _cmp_decode_body(q_ref, kvc_ref, gz_ref, gb_ref, o_ref, idx_ref, *, past, n_pick):
    n_pages = kvc_ref.shape[0] // (2 * NSA_KV_HEADS)
    nb = _BLOCKS_PER_PAGE * n_pages

    def block_id(shape):
        pos = _iota(shape, 1)
        page = jnp.where(pos >= n_pages, pos - n_pages, pos)
        return page * _BLOCKS_PER_PAGE + jnp.where(pos >= n_pages, 1, 0)

    lane = block_id((NSA_HEADS, nb))
    hrow = _iota((NSA_HEADS, nb), 0)
    vis = lane * NSA_BLOCK + (NSA_BLOCK - 1) <= past
    q = (q_ref[0] * (NSA_DH ** -0.5)).astype(BF16)
    gates = _sigmoid(gz_ref[0] + gb_ref[...])
    cur = past // NSA_BLOCK
    o_all = jnp.zeros((NSA_HEADS, NSA_DH), F32)
    idx_all = jnp.zeros((SUBLANE, LANE), F32)
    orow = _iota((NSA_HEADS, NSA_DH), 0)
    slot_r, slot_c = _iota((SUBLANE, LANE), 0), _iota((SUBLANE, LANE), 1)
    l1 = block_id((1, nb))
    l1f = l1.astype(F32)
    forced = (l1 == 0) | (l1 == cur) | (l1 == cur - 1)
    for kh in range(NSA_KV_HEADS):
        kc = kvc_ref[kh * n_pages:(kh + 1) * n_pages, :].astype(BF16)
        vc = kvc_ref[(NSA_KV_HEADS + kh) * n_pages:(NSA_KV_HEADS + kh + 1) * n_pages, :].astype(BF16)
        s = jnp.concatenate([_mm(q, kc[:, j * NSA_DH:(j + 1) * NSA_DH], _NT) for j in range(_BLOCKS_PER_PAGE)], axis=1)
        s = jnp.where(vis, s, -jnp.inf)
        m = jnp.max(s, axis=1, keepdims=True)
        m = jnp.where(m == -jnp.inf, 0.0, m)
        e = jnp.exp(s - m)
        p = e / jnp.maximum(jnp.sum(e, axis=1, keepdims=True), TINY)
        mine = (hrow >= kh * NSA_GROUP) & (hrow < (kh + 1) * NSA_GROUP)
        pb = p.astype(BF16)
        o_kh = _mm(pb[:, :n_pages], vc[:, :NSA_DH])
        for j in range(1, _BLOCKS_PER_PAGE):
            o_kh = o_kh + _mm(pb[:, j * n_pages:(j + 1) * n_pages], vc[:, j * NSA_DH:(j + 1) * NSA_DH])
        o_all = jnp.where((orow >= kh * NSA_GROUP) & (orow < (kh + 1) * NSA_GROUP), o_kh, o_all)
        imp = jnp.sum(jnp.where(mine, p, 0.0), axis=0, keepdims=True)
        score = jnp.where(forced, -jnp.inf, imp)
        for r in range(n_pick):
            mx = jnp.max(score, axis=1, keepdims=True)
            pick = jnp.min(jnp.where(score == mx, l1f, float(nb)), axis=1, keepdims=True)
            score = jnp.where(l1f == pick, -jnp.inf, score)
            idx_all = jnp.where((slot_r == kh) & (slot_c == r), pick, idx_all)
    o_ref[0] = o_all * gates[:, 0:1]
    idx_ref[0] = idx_all.astype(jnp.int32)


def _cmp_decode(qh, kvc, gz, gb3, past, n_pick):
    b = qh.shape[0]
    rows = kvc.shape[0] // b
    return pl.pallas_call(
        functools.partial(_cmp_decode_body, past=past, n_pick=n_pick),
        grid=(b,),
        in_specs=[pl.BlockSpec((1, NSA_HEADS, NSA_DH), lambda i: (i, 0, 0)), pl.BlockSpec((rows, kvc.shape[1]), lambda i: (i, 0)),
                  pl.BlockSpec((1, NSA_HEADS, NSA_BRANCHES), lambda i: (i, 0, 0)), _resident((NSA_HEADS, NSA_BRANCHES))],
        out_specs=[pl.BlockSpec((1, NSA_HEADS, NSA_DH), lambda i: (i, 0, 0)), pl.BlockSpec((1, SUBLANE, LANE), lambda i: (i, 0, 0))],
        out_shape=[jax.ShapeDtypeStruct((b, NSA_HEADS, NSA_DH), F32), jax.ShapeDtypeStruct((b, SUBLANE, LANE), jnp.int32)],
        compiler_params=_params(1),
        name="nsa_cmp_decode",
    )(qh, kvc, gz, gb3)


def _selwin_decode_body(info_ref, q_ref, ns_ref, nw_ref, win_ref, *refs, n_shared, n_own, first_win_row):
    n_blk = n_shared + NSA_KV_HEADS * n_own
    page_refs = refs[:n_blk]
    gz_ref, gb_ref, o_ref, wout_ref = refs[n_blk:]
    i = pl.program_id(0)
    q = q_ref[0] * (NSA_DH ** -0.5)
    qb = q.astype(BF16)
    gates = _sigmoid(gz_ref[0] + gb_ref[...])
    hrow = _iota((NSA_HEADS, NSA_DH), 0)
    n_win = win_ref.shape[2]
    wlane = _iota((NSA_HEADS, n_win), 1)
    pblk = jnp.right_shift(_iota((NSA_HEADS, PAGE_SIZE), 1), NSA_BLOCK.bit_length() - 1)
    o_all = jnp.zeros((NSA_HEADS, NSA_DH), F32)
    voff = NSA_KV_HEADS * NSA_DH
    for kh in range(NSA_KV_HEADS):
        ksl = slice(kh * NSA_DH, (kh + 1) * NSA_DH)
        vsl = slice(voff + kh * NSA_DH, voff + (kh + 1) * NSA_DH)
        slots = list(range(n_shared)) + list(range(n_shared + kh * n_own, n_shared + (kh + 1) * n_own))
        s_parts = []
        for sl in slots:
            blk_in_page = jnp.bitwise_and(info_ref[i, sl], _BLOCKS_PER_PAGE - 1)
            sp = _mm(qb, page_refs[sl][0, ksl, :].astype(BF16))
            s_parts.append(jnp.where(pblk == blk_in_page, sp, -jnp.inf))
        kn, vn = ns_ref[0, :, ksl], ns_ref[0, :, vsl]
        sn = jnp.sum(q * kn, axis=1, keepdims=True)
        m = sn
        for sp in s_parts:
            m = jnp.maximum(m, jnp.max(sp, axis=1, keepdims=True))
        en = jnp.exp(sn - m)
        l, acc = en, en * vn
        for sl, sp in zip(slots, s_parts):
            e = jnp.exp(sp - m)
            l = l + jnp.sum(e, axis=1, keepdims=True)
            acc = acc + _mm(e.astype(BF16), page_refs[sl][0, vsl, :].astype(BF16), _NT)
        o_sel = acc / l
        sw = jnp.where(wlane >= first_win_row, _mm(qb, win_ref[0, ksl, :].astype(BF16)), -jnp.inf)
        kwn, vwn = nw_ref[0, :, ksl], nw_ref[0, :, vsl]
        swn = jnp.sum(q * kwn, axis=1, keepdims=True)
        mw = jnp.maximum(jnp.max(sw, axis=1, keepdims=True), swn)
        ew, ewn = jnp.exp(sw - mw), jnp.exp(swn - mw)
        o_win = (_mm(ew.astype(BF16), win_ref[0, vsl, :].astype(BF16), _NT) + ewn * vwn) / (jnp.sum(ew, axis=1, keepdims=True) + ewn)
        keep = (hrow >= kh * NSA_GROUP) & (hrow < (kh + 1) * NSA_GROUP)
        o_all = jnp.where(keep, gates[:, 1:2] * o_sel + gates[:, 2:3] * o_win, o_all)
    o_ref[0] = o_all
    r2, c2 = _iota((NSA_KV_W, NSA_KV_W), 0), _iota((NSA_KV_W, NSA_KV_W), 1)
    new_col = jnp.sum(jnp.where(r2 == c2, jnp.broadcast_to(nw_ref[0], (NSA_KV_W, NSA_KV_W)), 0.0), axis=1, keepdims=True)
    shifted = pltpu.roll(win_ref[0], n_win - 1, 1)
    wout_ref[0] = jnp.where(_iota((NSA_KV_W, n_win), 1) == n_win - 1, new_col, shifted)


def _selwin_decode(info, qh, new_s, new_w, win_t, pool_t, gz, gb3, n_shared, n_own, first_win_row):
    b = qh.shape[0]
    n_blk = info.shape[1]
    row3 = lambda n: pl.BlockSpec((1, 1, n), lambda i, r: (i, 0, 0))

    def page_spec(s):
        return pl.BlockSpec((1, NSA_KV_W, PAGE_SIZE), lambda i, r: (jnp.right_shift(r[i, s], _PAGE_SHIFT), 0, 0))

    win_spec = pl.BlockSpec((1,) + win_t.shape[1:], lambda i, r: (i, 0, 0))
    grid_spec = pltpu.PrefetchScalarGridSpec(
        num_scalar_prefetch=1,
        grid=(b,),
        in_specs=[pl.BlockSpec((1, NSA_HEADS, NSA_DH), lambda i, r: (i, 0, 0)), row3(NSA_KV_W), row3(NSA_KV_W), win_spec]
        + [page_spec(s) for s in range(n_blk)]
        + [pl.BlockSpec((1, NSA_HEADS, NSA_BRANCHES), lambda i, r: (i, 0, 0)),
           pl.BlockSpec((NSA_HEADS, NSA_BRANCHES), lambda i, r: (0, 0))],
        out_specs=[pl.BlockSpec((1, NSA_HEADS, NSA_DH), lambda i, r: (i, 0, 0)), win_spec],
    )
    return pl.pallas_call(
        functools.partial(_selwin_decode_body, n_shared=n_shared, n_own=n_own, first_win_row=first_win_row),
        grid_spec=grid_spec,
        out_shape=[jax.ShapeDtypeStruct((b, NSA_HEADS, NSA_DH), F32), jax.ShapeDtypeStruct(win_t.shape, F32)],
        compiler_params=_params(1),
        name="nsa_selwin_decode",
    )(info, qh, new_s.reshape(b, 1, -1), new_w.reshape(b, 1, -1), win_t, *([pool_t] * n_blk), gz, gb3)


def _feature_major(cache):
    n, rows = cache.shape[:2]
    return jnp.transpose(cache, (0, 2, 3, 4, 1)).reshape(n, NSA_KV_W, rows)


def _pad_cols(w, n):
    return jnp.pad(w, ((0, 0), (0, n - w.shape[1])))


def _even_w_in(w):
    sizes = (GLA_HEADS * GLA_DK, GLA_HEADS * GLA_DK, GLA_HEADS * GLA_DV, GLA_HEADS * GLA_DV, GLA_RANK,
             NSA_HEADS * NSA_DH, NSA_BRANCHES * NSA_KV_W, NSA_HEADS * NSA_BRANCHES)
    cuts = [0]
    for s in sizes:
        cuts.append(cuts[-1] + s)
    gq, gk, gv, gg, ga, nq, nkv, ng = (w[:, cuts[i]:cuts[i + 1]] for i in range(len(sizes)))
    small = _pad_cols(jnp.concatenate([ga, ng], axis=1), LANE)
    sample = jnp.concatenate([gq, gk, gv, gg, nq, nkv, small], axis=1).astype(BF16)
    prompt = jnp.concatenate([gq, gk, gv, gg, nkv, small], axis=1).astype(BF16)
    prompt_t = jnp.concatenate([nq, nkv, small], axis=1).T.astype(BF16)
    return sample, prompt, prompt_t


_EVEN_WIDTHS = (GLA_Z_W, NSA_HEADS * NSA_DH, NSA_KV_W, NSA_KV_W, NSA_KV_W, LANE)
_EVEN_WIDTHS_P = (GLA_Z_W, NSA_KV_W, NSA_KV_W, NSA_KV_W, LANE)
_EVEN_WIDTHS_PT = (NSA_HEADS * NSA_DH, NSA_KV_W, NSA_KV_W, NSA_KV_W, LANE)


def _odd_w_in(w):
    main = M_Z_W
    return jnp.concatenate([w[:, :main], _pad_cols(w[:, main:], LANE)], axis=1).astype(BF16)


_ODD_WIDTHS = (M_Z_W, LANE)


def _compress_weights(pe, w1, w2):
    eye_k = jnp.eye(NSA_KV_HEADS, dtype=F32)
    eye_c = jnp.eye(2, dtype=F32)
    w1big = jnp.einsum("csde,cx,ky->sckdxye", w1, eye_c, eye_k).reshape(NSA_BLOCK * NSA_KV_W, NSA_KV_W)
    w2big = jnp.einsum("ced,cx,ky->ckexyd", w2, eye_c, eye_k).reshape(NSA_KV_W, NSA_KV_W)
    pe_flat = jnp.broadcast_to(pe.transpose(1, 0, 2)[:, :, None, :], (NSA_BLOCK, 2, NSA_KV_HEADS, NSA_DH)).reshape(1, -1)
    return pe_flat, w1big.astype(BF16), w2big.astype(BF16)


def _compress_weights_paged(pe, w1, w2):
    eye_b = jnp.eye(_BLOCKS_PER_PAGE, dtype=F32)
    pe_t = jnp.tile(pe.transpose(0, 2, 1), (1, 1, _BLOCKS_PER_PAGE)).reshape(2 * NSA_DH, PAGE_SIZE)
    w1_t = jnp.einsum("csde,hx->cdhsxe", w1, eye_b).reshape(2, NSA_DH // _D_PER_STEP, _D_PER_STEP * PAGE_SIZE,
                                                            _BLOCKS_PER_PAGE * NSA_CMP_HID)
    w2_t = jnp.einsum("ced,hx->chexd", w2, eye_b).reshape(2, _BLOCKS_PER_PAGE * NSA_CMP_HID, _BLOCKS_PER_PAGE * NSA_DH)
    return pe_t, w1_t.astype(BF16), w2_t.astype(BF16)


def _gate_bias_row(gb):
    return jnp.pad(gb, (GATE_OFF, LANE - GATE_OFF - gb.shape[0])).reshape(1, LANE)


def _split_outs(outs, n):
    return outs[:n], outs[n:]


def _values_t(kv_t, rows=NSA_DH):
    b, _, t = kv_t.shape
    v = kv_t.reshape(b, 2, NSA_KV_HEADS, NSA_DH, t)[:, 1].astype(BF16)
    if rows > NSA_DH:
        extra = jnp.zeros((b, NSA_KV_HEADS, rows - NSA_DH, t), BF16).at[:, :, 0].set(1.0)
        v = jnp.concatenate([v, extra], axis=2)
    return v


def _kv_rows(kv_t):
    b, _, t = kv_t.shape
    return jnp.transpose(kv_t.reshape(b, 2, NSA_KV_HEADS, NSA_DH, t), (0, 4, 1, 2, 3))


def _heads_major(kv, which):
    b, t, _ = kv.shape
    return kv.reshape(b, t, 2, NSA_KV_HEADS, NSA_DH)[:, :, which].transpose(0, 2, 1, 3).astype(BF16)


def _even_layer_prompt(x, bsz, g, prm):
    m = x.shape[0]
    t = m // bsz
    (z, kvc, kvs, kvw, zs), (q_t, kvc_t, kvs_t, kvw_t, zs_t) = _split_outs(
        _norm_proj(x, g, prm["w_in_p"], _EVEN_WIDTHS_P, TM_PROJ, prm["w_in_pt"], _EVEN_WIDTHS_PT, bsz), len(_EVEN_WIDTHS_P))
    o_gla, s_fin = _gla_prompt(z, zs, prm["wa"], prm["ba"], prm["gn"],
                               jnp.zeros((bsz, GLA_HEADS, GLA_DK, GLA_DV), F32), bsz)
    nb = t // NSA_BLOCK
    nbp = -(-nb // LANE) * LANE
    kvcmp = _compress_dense(kvc.reshape(bsz * nb, NSA_BLOCK * NSA_KV_W), prm["pe"], prm["w1"], prm["w2"])
    kvcmp = jnp.pad(kvcmp.reshape(bsz, nb, NSA_KV_W), ((0, 0), (0, nbp - nb), (0, 0))).reshape(bsz * nbp, NSA_KV_W)
    o_cmp, nm = _cmp_sel(q_t, kvcmp, zs_t, prm["gb_col"], nbp, TQ_CMP)
    onehot = (jnp.arange(t)[:, None] // NSA_BLOCK == jnp.arange(nbp)[None, :]).astype(BF16)
    o_sel = _sel_attn(q_t, nm, kvs, onehot, _values_t(kvs_t, V_AUG), zs_t, prm["gb_col"], TQ_ATTN, TK_SEL)
    o_win = _win_attn(q_t, kvw, _values_t(kvw_t), zs_t, prm["gb_col"], TQ_ATTN)
    y = _out_proj(x, [o_gla], (1,), prm["w_out"], TM_OUT, [o_cmp, o_sel, o_win], bsz)
    n_keep = min(NSA_WINDOW, t)
    return y, s_fin, _kv_rows(kvc_t), _kv_rows(kvs_t), _kv_rows(kvw_t[:, :, t - n_keep:])


def _even_layer_sample(x, g, prm, gla_state, cmp_pool, sel_pool, win_buf, page_table):
    b = x.shape[0]
    n_pages = page_table.shape[1]
    past = n_pages * PAGE_SIZE
    n_past_blk = past // NSA_BLOCK
    z, nq, kvc, kvs, kvw, zs = _norm_proj(x, g, prm["w_in"], _EVEN_WIDTHS, b)
    o_gla, s_new = _gla_decode(z, zs, prm["wa"], prm["ba"], prm["gn"], gla_state)
    kvcmp = _compress_paged(_feature_major(cmp_pool), page_table, prm["pe_t"], prm["w1_t"], prm["w2_t"])
    qh = nq.reshape(b, NSA_HEADS, NSA_DH)
    gz = zs[:, GATE_OFF:GATE_OFF + NSA_HEADS * NSA_BRANCHES].reshape(b, NSA_HEADS, NSA_BRANCHES)
    forced = _decode_forced(n_past_blk)
    n_pick = NSA_TOP_N - len(forced)
    assert n_past_blk - len(forced) + 1 >= n_pick
    o_cmp, idx = _cmp_decode(qh, kvcmp, gz, prm["gb3"], past, n_pick)
    shared = [f for f in forced if f < n_past_blk]
    logical = jnp.concatenate([jnp.broadcast_to(jnp.asarray(shared, jnp.int32), (b, len(shared))),
                               idx[:, :NSA_KV_HEADS, :n_pick].reshape(b, NSA_KV_HEADS * n_pick)], axis=1)
    per_page = _BLOCKS_PER_PAGE
    phys = jnp.take_along_axis(page_table, logical // per_page, axis=1) * per_page + logical % per_page
    win_keep = win_buf.shape[1]
    assert win_keep >= 1
    first_win_row = max(win_keep - NSA_WINDOW + 1, 0)
    o_sw, win_new_t = _selwin_decode(phys.astype(jnp.int32), qh, kvs, kvw, _feature_major(win_buf), _feature_major(sel_pool),
                                     gz, prm["gb3"], len(shared), n_pick, first_win_row)
    hd = NSA_HEADS * NSA_DH
    y = _out_proj(x, [o_gla, o_cmp.reshape(b, hd), o_sw.reshape(b, hd)], (1, 2), prm["w_out"], b)
    kv_shape = (b, 1, 2, NSA_KV_HEADS, NSA_DH)
    win_new = jnp.transpose(win_new_t.reshape(b, 2, NSA_KV_HEADS, NSA_DH, win_keep), (0, 4, 1, 2, 3))
    return y, s_new, kvc.reshape(kv_shape), kvs.reshape(kv_shape), win_new


def _odd_layer_prompt(x, bsz, g, prm):
    z, zs = _norm_proj(x, g, prm["w_in"], _ODD_WIDTHS, TM_PROJ)
    h, c_aug, m_fin = _mlstm_prompt(z, zs, prm["bias"], prm["mn"], jnp.zeros((bsz, M_HEADS, M_DQK, M_AUG), F32),
                                    jnp.zeros((bsz, 1, LANE), F32), bsz)
    y = _out_proj(x, [h], (1,), prm["w_out"], TM_OUT)
    return y, c_aug[..., :M_DV], c_aug[..., M_DV], m_fin[:, 0, :M_HEADS]


def _odd_layer_sample(x, g, prm, c0, n0, m0):
    b = x.shape[0]
    z, zs = _norm_proj(x, g, prm["w_in"], _ODD_WIDTHS, b)
    h, cn, nn, mn = _mlstm_decode(z, zs, prm["bias"], prm["mn"], c0, n0, m0)
    return _out_proj(x, [h], (1,), prm["w_out"], b), cn, nn, mn


def kernel(x_prompt, x_sample, cache_cmp_kv, cache_sel_kv, cache_win_kv, state_gla, state_mlstm_c, state_mlstm_n, state_mlstm_m, state_ffn_conv, page_table, norm_mix, norm_ffn, norm_final, even_w_in, even_w_out, gla_w_a2, gla_b_a, gla_norm, nsa_cmp_pe, nsa_cmp_w1, nsa_cmp_w2, nsa_gate_b, odd_w_in, odd_w_out, mlstm_b_i, mlstm_b_f, mlstm_norm, ffn_w_up, ffn_conv_w, ffn_conv_b, ffn_w_down):
    bp, t, d = x_prompt.shape
    bs = x_sample.shape[0]
    assert x_sample.shape[1] == 1
    depth = norm_mix.shape[0]
    f = ffn_conv_w.shape[2]
    xp = x_prompt.reshape(bp * t, d)
    xs = x_sample.reshape(bs, d)
    outs = {k: [] for k in ("cmp_p", "cmp_s", "sel_p", "sel_s", "win_p", "win_s", "gla_p", "gla_s",
                            "mc_p", "mc_s", "mn_p", "mn_s", "mm_p", "mm_s", "cv_p", "cv_s")}
    for l in range(depth):
        if l % 2 == 0:
            e = l // 2
            pe, w1, w2 = _compress_weights(nsa_cmp_pe[e], nsa_cmp_w1[e], nsa_cmp_w2[e])
            pe_t, w1_t, w2_t = _compress_weights_paged(nsa_cmp_pe[e], nsa_cmp_w1[e], nsa_cmp_w2[e])
            w_in_s, w_in_p, w_in_pt = _even_w_in(even_w_in[e])
            prm = dict(pe_t=pe_t, w1_t=w1_t, w2_t=w2_t, w_in=w_in_s, w_in_p=w_in_p, w_in_pt=w_in_pt,
                       w_out=even_w_out[e].astype(BF16),
                       wa=jnp.pad(gla_w_a2[e], ((0, LANE - GLA_RANK), (0, 0))), ba=gla_b_a[e].reshape(1, -1),
                       gn=gla_norm[e].reshape(1, -1), pe=pe, w1=w1, w2=w2, gb_col=_gate_bias_row(nsa_gate_b[e]).reshape(LANE, 1),
                       gb3=nsa_gate_b[e].reshape(NSA_HEADS, NSA_BRANCHES))
            xp, s_, c_, k_, w_ = _even_layer_prompt(xp, bp, norm_mix[l], prm)
            outs["gla_p"].append(s_); outs["cmp_p"].append(c_); outs["sel_p"].append(k_); outs["win_p"].append(w_)
            xs, s_, c_, k_, w_ = _even_layer_sample(xs, norm_mix[l], prm, state_gla[e], cache_cmp_kv[e], cache_sel_kv[e],
                                                    cache_win_kv[e], page_table)
            outs["gla_s"].append(s_); outs["cmp_s"].append(c_); outs["sel_s"].append(k_); outs["win_s"].append(w_)
        else:
            o = l // 2
            bias = jnp.pad(jnp.concatenate([mlstm_b_i[o], mlstm_b_f[o]]), (0, LANE - 2 * M_HEADS)).reshape(1, LANE)
            prm = dict(w_in=_odd_w_in(odd_w_in[o]), w_out=odd_w_out[o].astype(BF16), bias=bias, mn=mlstm_norm[o].reshape(1, -1))
            xp, c_, n_, m_ = _odd_layer_prompt(xp, bp, norm_mix[l], prm)
            outs["mc_p"].append(c_); outs["mn_p"].append(n_); outs["mm_p"].append(m_)
            xs, c_, n_, m_ = _odd_layer_sample(xs, norm_mix[l], prm, state_mlstm_c[o], state_mlstm_n[o], state_mlstm_m[o])
            outs["mc_s"].append(c_); outs["mn_s"].append(n_); outs["mm_s"].append(m_)
        final = l == depth - 1
        wup, wd = ffn_w_up[l].astype(BF16), ffn_w_down[l].astype(BF16)
        xp, cv = _ffn_prompt(xp, bp, norm_ffn[l], wup, ffn_conv_w[l], ffn_conv_b[l], wd,
                             jnp.zeros((bp, CONV_W - 1, f), F32), norm_final, final, TM_FFN)
        outs["cv_p"].append(cv)
        xs, cv = _ffn_decode(xs, norm_ffn[l], wup, ffn_conv_w[l], ffn_conv_b[l], wd, state_ffn_conv[l], norm_final, final)
        outs["cv_s"].append(cv)
    st = lambda k: jnp.stack(outs[k])
    return (xp.reshape(bp, t, d), xs.reshape(bs, 1, d),
            st("cmp_p"), st("cmp_s"), st("sel_p"), st("sel_s"), st("win_p"), st("win_s"), st("gla_p"), st("gla_s"),
            st("mc_p"), st("mc_s"), st("mn_p"), st("mn_s"), st("mm_p"), st("mm_s"), st("cv_p"), st("cv_s"))
```

```python
import functools

import jax
import jax.numpy as jnp
from jax import lax
from jax.experimental import pallas as pl
from jax.experimental.pallas import tpu as pltpu

F32 = jnp.float32
BF16 = jnp.bfloat16

GLA_HEADS, GLA_DK, GLA_DV, GLA_RANK, GLA_TAU, GLA_CHUNK = 4, 64, 128, 16, 16.0, 64
NSA_HEADS, NSA_KV_HEADS, NSA_GROUP, NSA_DH = 8, 2, 4, 64
NSA_BRANCHES, NSA_BLOCK, NSA_TOP_N, NSA_WINDOW, NSA_CMP_HID = 3, 64, 16, 512, 64
M_HEADS, M_DQK, M_DV, M_CHUNK = 4, 128, 256, 64
M_CHUNK_PALLAS = 128
CONV_W = 3
PAGE_SIZE = 128
EPS, TINY = 1e-6, 1e-30
LOG2E = 1.4426950408889634

LANE = 128
SUBLANE = 8
VMEM_BIG = 52 * 1024 * 1024
VMEM_MID = 40 * 1024 * 1024

MASK_BIG = 32768.0

NSA_KV_W = 2 * NSA_KV_HEADS * NSA_DH
GLA_Z_W = 2 * GLA_HEADS * GLA_DK + 2 * GLA_HEADS * GLA_DV
M_Z_W = 2 * M_HEADS * M_DQK + 2 * M_HEADS * M_DV
M_AUG = M_DV + LANE
GATE_OFF = GLA_RANK
TM_PROJ = 256
TM_OUT = 512
TM_FFN = 256
TQ_CMP = 512
TQ_ATTN = 128
TK_SEL = 512
V_AUG = NSA_DH + 16
_BLOCKS_PER_PAGE = PAGE_SIZE // NSA_BLOCK
_PAGE_SHIFT = _BLOCKS_PER_PAGE.bit_length() - 1

_NN = (((1,), (0,)), ((), ()))
_NT = (((1,), (1,)), ((), ()))
_TN = (((0,), (0,)), ((), ()))


def _mm(a, b, dims=_NN):
    return lax.dot_general(a, b, dims, preferred_element_type=F32)


def _mm_bf(a, b, dims=_NN):
    return _mm(a.astype(BF16), b.astype(BF16), dims)


def _split_bf16(x, n):
    parts, r = [], x
    for _ in range(n):
        p = r.astype(BF16)
        parts.append(p)
        r = r - p.astype(F32)
    return parts


def _mm_sel(sel, x, dims=_NN, x_is_rhs=True):
    out = None
    for p in _split_bf16(x, 3):
        t = _mm(sel, p, dims) if x_is_rhs else _mm(p, sel, dims)
        out = t if out is None else out + t
    return out


def _mm_hp(a, b, dims=_NN):
    a1, a2 = _split_bf16(a, 2)
    b1, b2 = _split_bf16(b, 2)
    return _mm(a1, b1, dims) + (_mm(a1, b2, dims) + _mm(a2, b1, dims))


def _gelu(x):
    return 0.5 * x * (1.0 + jnp.tanh(0.7978845608028654 * (x + 0.044715 * (x * x * x))))


def _sigmoid(x):
    return 1.0 / (1.0 + jnp.exp(-x))


def _logsigmoid(x):
    return jnp.minimum(x, 0.0) - jnp.log(1.0 + jnp.exp(-jnp.abs(x)))


def _rms(x, g):
    return x * lax.rsqrt(jnp.mean(x * x, axis=-1, keepdims=True) + EPS) * g


def _iota(shape, dim):
    return lax.broadcasted_iota(jnp.int32, shape, dim)


def _params(n_axes, vmem=VMEM_MID):
    return pltpu.CompilerParams(dimension_semantics=("arbitrary",) * n_axes, vmem_limit_bytes=vmem)


def _resident(shape):
    nd = len(shape)
    return pl.BlockSpec(shape, lambda *_: (0,) * nd, pipeline_mode=pl.Buffered(1))


def _norm_proj_body(x_ref, g_ref, w_ref, *refs, widths, t_widths):
    xb = _rms(x_ref[...], g_ref[...]).astype(BF16)
    o_refs = refs[1:] if t_widths else refs
    off = 0
    for o_ref, n in zip(o_refs[:len(widths)], widths):
        o_ref[...] = _mm(xb, w_ref[:, off:off + n])
        off += n
    off = 0
    for o_ref, n in zip(o_refs[len(widths):], t_widths):
        o_ref[0] = _mm(refs[0][off:off + n, :], xb, _NT)
        off += n


def _norm_proj(x, g, w, widths, tm, wt=None, t_widths=(), bsz=1):
    m, d = x.shape
    tm = min(tm, m)
    nt = m // bsz // tm
    assert m % tm == 0 and sum(widths) == w.shape[1] and (m // bsz) % tm == 0
    t_in = [_resident(wt.shape)] if t_widths else []
    t_args = [wt] if t_widths else []
    return pl.pallas_call(
        functools.partial(_norm_proj_body, widths=tuple(widths), t_widths=tuple(t_widths)),
        grid=(m // tm,),
        in_specs=[pl.BlockSpec((tm, d), lambda i: (i, 0)), _resident((1, d)), _resident(w.shape)] + t_in,
        out_specs=[pl.BlockSpec((tm, n), lambda i: (i, 0)) for n in widths]
        + [pl.BlockSpec((1, n, tm), lambda i: (i // nt, 0, i % nt)) for n in t_widths],
        out_shape=[jax.ShapeDtypeStruct((m, n), F32) for n in widths]
        + [jax.ShapeDtypeStruct((bsz, n, m // bsz), F32) for n in t_widths],
        compiler_params=_params(1),
        name="norm_proj",
    )(x, g.reshape(1, d), w, *t_args)


def _out_proj_body(x_ref, *refs, groups, n_t):
    h_refs, w_ref, o_ref = refs[:-2], refs[-2], refs[-1]
    acc = x_ref[...]
    i = off = 0
    for gsz in groups:
        h = h_refs[i][...]
        for j in range(1, gsz):
            h = h + h_refs[i + j][...]
        i += gsz
        n = h.shape[1]
        acc = acc + _mm(h.astype(BF16), w_ref[off:off + n, :])
        off += n
    if n_t:
        ht = h_refs[i][0]
        for j in range(1, n_t):
            ht = ht + h_refs[i + j][0]
        acc = acc + _mm(ht.astype(BF16), w_ref[off:off + ht.shape[0], :], _TN)
    o_ref[...] = acc


def _out_proj(x, hs, groups, w, tm, t_hs=(), bsz=1):
    m, d = x.shape
    tm = min(tm, m)
    nt = m // bsz // tm
    assert m % tm == 0 and (m // bsz) % tm == 0
    return pl.pallas_call(
        functools.partial(_out_proj_body, groups=tuple(groups), n_t=len(t_hs)),
        grid=(m // tm,),
        in_specs=[pl.BlockSpec((tm, d), lambda i: (i, 0))]
        + [pl.BlockSpec((tm, h.shape[1]), lambda i: (i, 0)) for h in hs]
        + [pl.BlockSpec((1, h.shape[1], tm), lambda i: (i // nt, 0, i % nt)) for h in t_hs]
        + [_resident(w.shape)],
        out_specs=pl.BlockSpec((tm, d), lambda i: (i, 0)),
        out_shape=jax.ShapeDtypeStruct((m, d), F32),
        compiler_params=_params(1),
        name="out_proj",
    )(x, *hs, *t_hs, w)


def _ffn_chunks(xn, resid, wup_ref, cw_ref, cb_ref, wd_ref, fc, prev_fn, keep_fn):
    f = cw_ref.shape[1]
    n = f // fc

    def up_proj(c):
        return _mm(xn, wup_ref[:, c * fc:(c + 1) * fc]), _mm(xn, wup_ref[:, f + c * fc:f + (c + 1) * fc])

    acc = resid
    nxt = up_proj(0)
    for c in range(n):
        sl = slice(c * fc, (c + 1) * fc)
        gp, up = nxt
        if c + 1 < n:
            nxt = up_proj(c + 1)
        g2, g1 = prev_fn(gp, sl)
        a = cb_ref[:, sl] + g2 * cw_ref[0:1, sl] + g1 * cw_ref[1:2, sl] + gp * cw_ref[2:3, sl]
        acc = acc + _mm((_gelu(a) * up).astype(BF16), wd_ref[sl, :])
        keep_fn(gp, sl)
    return acc


def _ffn_prompt_body(x_ref, g_ref, wup_ref, cw_ref, cb_ref, wd_ref, st_ref, gf_ref, y_ref, ns_ref, carry, *, final, fc):
    t = pl.program_id(1)
    tm = x_ref.shape[0]

    @pl.when(t == 0)
    def _():
        carry[...] = st_ref[0]

    x = x_ref[...]
    xn = _rms(x, g_ref[...]).astype(BF16)
    row = _iota((tm, fc), 0)

    def prev_fn(gp, sl):
        c0, c1 = carry[0:1, sl], carry[1:2, sl]
        g1 = jnp.where(row == 0, c1, pltpu.roll(gp, 1, 0))
        g2 = jnp.where(row == 0, c0, jnp.where(row == 1, c1, pltpu.roll(gp, 2, 0)))
        return g2, g1

    def keep_fn(gp, sl):
        carry[:, sl] = gp[tm - 2:tm, :]

    acc = _ffn_chunks(xn, x, wup_ref, cw_ref, cb_ref, wd_ref, fc, prev_fn, keep_fn)
    y_ref[...] = _rms(acc, gf_ref[...]) if final else acc

    @pl.when(t == pl.num_programs(1) - 1)
    def _():
        ns_ref[0] = carry[...]


def _ffn_prompt(x, bsz, g, wup, cw, cb, wd, st, gf, final, tm, fc=256):
    m, d = x.shape
    t = m // bsz
    tm = min(tm, t)
    f = cw.shape[1]
    assert t % tm == 0 and f % fc == 0
    nt = t // tm
    return pl.pallas_call(
        functools.partial(_ffn_prompt_body, final=final, fc=fc),
        grid=(bsz, nt),
        in_specs=[pl.BlockSpec((tm, d), lambda b, i: (b * nt + i, 0)), _resident((1, d)), _resident(wup.shape),
                  _resident(cw.shape), _resident((1, f)), _resident(wd.shape),
                  pl.BlockSpec((1, CONV_W - 1, f), lambda b, i: (b, 0, 0)), _resident((1, d))],
        out_specs=[pl.BlockSpec((tm, d), lambda b, i: (b * nt + i, 0)),
                   pl.BlockSpec((1, CONV_W - 1, f), lambda b, i: (b, 0, 0))],
        out_shape=[jax.ShapeDtypeStruct((m, d), F32), jax.ShapeDtypeStruct((bsz, CONV_W - 1, f), F32)],
        scratch_shapes=[pltpu.VMEM((CONV_W - 1, f), F32)],
        compiler_params=_params(2, VMEM_BIG),
        name="ffn_prompt",
    )(x, g.reshape(1, d), wup, cw, cb.reshape(1, f), wd, st, gf.reshape(1, d))


def _ffn_decode_body(x_ref, g_ref, wup_ref, cw_ref, cb_ref, wd_ref, s0_ref, s1_ref, gf_ref, y_ref, gp_ref, *, final, fc):
    x = x_ref[...]
    xn = _rms(x, g_ref[...]).astype(BF16)

    def prev_fn(gp, sl):
        return s0_ref[:, sl], s1_ref[:, sl]

    def keep_fn(gp, sl):
        gp_ref[:, sl] = gp

    acc = _ffn_chunks(xn, x, wup_ref, cw_ref, cb_ref, wd_ref, fc, prev_fn, keep_fn)
    y_ref[...] = _rms(acc, gf_ref[...]) if final else acc


def _ffn_decode(x, g, wup, cw, cb, wd, st, gf, final, fc=256):
    m, d = x.shape
    f = cw.shape[1]
    y, gp = pl.pallas_call(
        functools.partial(_ffn_decode_body, final=final, fc=fc),
        grid=(1,),
        in_specs=[_resident((m, d)), _resident((1, d)), _resident(wup.shape), _resident(cw.shape), _resident((1, f)),
                  _resident(wd.shape), _resident((m, f)), _resident((m, f)), _resident((1, d))],
        out_specs=[pl.BlockSpec((m, d), lambda i: (0, 0)), pl.BlockSpec((m, f), lambda i: (0, 0))],
        out_shape=[jax.ShapeDtypeStruct((m, d), F32), jax.ShapeDtypeStruct((m, f), F32)],
        compiler_params=_params(1, VMEM_BIG),
        name="ffn_decode",
    )(x, g.reshape(1, d), wup, cw, cb.reshape(1, f), wd, st[:, 0], st[:, 1], gf.reshape(1, d))
    return y, jnp.stack([st[:, 1], gp], axis=1)


def _gla_finish(o, gg, gn):
    return _rms(o, gn) * (gg * _sigmoid(gg))


def _gla_body(z_ref, zs_ref, wa_ref, ba_ref, gn_ref, s0_ref, o_ref, sfin_ref, s_scr):
    t = pl.program_id(1)

    @pl.when(t == 0)
    def _():
        s_scr[...] = s0_ref[...]

    _round_robin([_gla_chunk(z_ref.at[bi], zs_ref.at[bi], wa_ref, ba_ref, gn_ref, o_ref.at[bi], s_scr.at[bi])
                  for bi in range(z_ref.shape[0])])

    @pl.when(t == pl.num_programs(1) - 1)
    def _():
        sfin_ref[...] = s_scr[...]


def _round_robin(chains):
    chains = list(chains)
    while chains:
        chains = [c for c in chains if next(c, _DONE) is not _DONE]


_DONE = object()


def _gla_chunk(z_ref, zs_ref, wa_ref, ba_ref, gn_ref, o_ref, s_scr):
    c = z_ref.shape[0]
    hk = GLA_HEADS * GLA_DK
    la = _logsigmoid(_mm_hp(zs_ref[...], wa_ref[...]) + ba_ref[...]) * (1.0 / GLA_TAU)
    yield
    r, cidx = _iota((c, c), 0), _iota((c, c), 1)
    tri = r >= cidx
    cum = _mm_sel(jnp.where(tri, 1.0, 0.0).astype(BF16), la)
    yield
    last = cum[c - 1:c, :]
    eq, ek, ekl, el = jnp.exp(cum), jnp.exp(-cum), jnp.exp(last - cum), jnp.exp(last)
    gn = gn_ref[...]
    for h in range(GLA_HEADS):
        ks = slice(h * GLA_DK, (h + 1) * GLA_DK)
        vs = slice(2 * hk + h * GLA_DV, 2 * hk + (h + 1) * GLA_DV)
        gs = slice(2 * hk + GLA_HEADS * GLA_DV + h * GLA_DV, 2 * hk + GLA_HEADS * GLA_DV + (h + 1) * GLA_DV)
        q = z_ref[:, ks] * (GLA_DK ** -0.5)
        k = z_ref[:, hk + h * GLA_DK:hk + (h + 1) * GLA_DK]
        v = z_ref[:, vs]
        qt = (q * eq[:, ks]).astype(BF16)
        s_old = s_scr[h]
        att = jnp.where(tri, _mm(qt, (k * ek[:, ks]).astype(BF16), _NT), 0.0)
        vb = v.astype(BF16)
        o_inter = _mm(qt, s_old.astype(BF16))
        kv_new = _mm((k * ekl[:, ks]).astype(BF16), vb, _TN)
        yield
        o = o_inter + _mm(att.astype(BF16), vb)
        ecol = jnp.sum(jnp.where(r == cidx, jnp.broadcast_to(el[:, ks], (c, c)), 0.0), axis=1, keepdims=True)
        s_scr[h] = ecol * s_old + kv_new
        yield
        o_ref[:, h * GLA_DV:(h + 1) * GLA_DV] = _gla_finish(o, z_ref[:, gs], gn)


def _seq_group(bsz):
    return next(n for n in (4, 2, 1) if bsz % n == 0)


def _gla_prompt(z, zs, wa, ba, gn, s0, bsz):
    m = z.shape[0]
    t = m // bsz
    c = GLA_CHUNK
    assert t % c == 0 and GLA_DK == c
    nc = t // c
    hk = GLA_HEADS * GLA_DK
    nb = _seq_group(bsz)
    st = (nb, GLA_HEADS, GLA_DK, GLA_DV)
    tok = lambda n: pl.BlockSpec((nb, c, n), lambda b, i: (b, i, 0))
    o, s_fin = pl.pallas_call(
        _gla_body,
        grid=(bsz // nb, nc),
        in_specs=[tok(GLA_Z_W), tok(LANE), _resident((LANE, hk)), _resident((1, hk)), _resident((1, GLA_DV)),
                  pl.BlockSpec(st, lambda b, i: (b, 0, 0, 0))],
        out_specs=[tok(GLA_HEADS * GLA_DV), pl.BlockSpec(st, lambda b, i: (b, 0, 0, 0))],
        out_shape=[jax.ShapeDtypeStruct((bsz, t, GLA_HEADS * GLA_DV), F32),
                   jax.ShapeDtypeStruct((bsz, GLA_HEADS, GLA_DK, GLA_DV), F32)],
        scratch_shapes=[pltpu.VMEM(st, F32)],
        compiler_params=_params(2),
        name="gla_prompt",
    )(z.reshape(bsz, t, -1), zs.reshape(bsz, t, -1), wa, ba, gn, s0)
    return o.reshape(m, -1), s_fin


def _gla_decode_body(z_ref, zs_ref, wa_ref, ba_ref, gn_ref, s_ref, o_ref, sn_ref):
    hk = GLA_HEADS * GLA_DK
    rows = SUBLANE
    z = jnp.broadcast_to(z_ref[0], (rows, GLA_Z_W))
    ga = jnp.broadcast_to(zs_ref[0], (rows, LANE))
    la = _logsigmoid(_mm_hp(ga, wa_ref[...]) + ba_ref[...]) * (1.0 / GLA_TAU)
    ea = jnp.exp(la)
    r, cidx = _iota((GLA_DK, GLA_DK), 0), _iota((GLA_DK, GLA_DK), 1)
    row0 = _iota((rows, GLA_DK), 0) == 0
    gn = gn_ref[...]
    for h in range(GLA_HEADS):
        ks = slice(h * GLA_DK, (h + 1) * GLA_DK)
        q = z[:, ks] * (GLA_DK ** -0.5)
        k = z[:, hk + h * GLA_DK:hk + (h + 1) * GLA_DK]
        v = z[:, 2 * hk + h * GLA_DV:2 * hk + (h + 1) * GLA_DV]
        gg = z[:, 2 * hk + GLA_HEADS * GLA_DV + h * GLA_DV:2 * hk + GLA_HEADS * GLA_DV + (h + 1) * GLA_DV]
        s_old = s_ref[0, h]
        o = _mm_hp(q * ea[:, ks], s_old) + jnp.sum(q * k, axis=1, keepdims=True) * v
        ecol = jnp.sum(jnp.where(r == cidx, jnp.broadcast_to(ea[0:1, ks], (GLA_DK, GLA_DK)), 0.0), axis=1, keepdims=True)
        sn_ref[0, h] = ecol * s_old + _mm_hp(jnp.where(row0, k, 0.0), v, _TN)
        o_ref[0, :, h * GLA_DV:(h + 1) * GLA_DV] = _gla_finish(o, gg, gn)[0:1]


def _gla_decode(z, zs, wa, ba, gn, s0):
    b = z.shape[0]
    hk = GLA_HEADS * GLA_DK
    st = (1, GLA_HEADS, GLA_DK, GLA_DV)
    o, sn = pl.pallas_call(
        _gla_decode_body,
        grid=(b,),
        in_specs=[pl.BlockSpec((1, 1, GLA_Z_W), lambda i: (i, 0, 0)), pl.BlockSpec((1, 1, LANE), lambda i: (i, 0, 0)),
                  _resident((LANE, hk)), _resident((1, hk)), _resident((1, GLA_DV)),
                  pl.BlockSpec(st, lambda i: (i, 0, 0, 0))],
        out_specs=[pl.BlockSpec((1, 1, GLA_HEADS * GLA_DV), lambda i: (i, 0, 0)), pl.BlockSpec(st, lambda i: (i, 0, 0, 0))],
        out_shape=[jax.ShapeDtypeStruct((b, 1, GLA_HEADS * GLA_DV), F32), jax.ShapeDtypeStruct((b,) + st[1:], F32)],
        compiler_params=_params(1),
        name="gla_decode",
    )(z.reshape(b, 1, -1), zs.reshape(b, 1, -1), wa, ba, gn, s0)
    return o.reshape(b, -1), sn


def _mlstm_finish(hh, og, mn):
    return _rms(hh, mn) * _sigmoid(og)


def _mlstm_body(z_ref, zs_ref, bias_ref, mn_ref, c0_ref, m0_ref, h_ref, cfin_ref, mfin_ref, c_scr, m_scr):
    t = pl.program_id(1)

    @pl.when(t == 0)
    def _():
        c_scr[...] = c0_ref[...]
        m_scr[...] = m0_ref[...]

    _round_robin([_mlstm_chunk(z_ref.at[bi], zs_ref.at[bi], bias_ref, mn_ref, h_ref.at[bi], c_scr.at[bi], m_scr.at[bi])
                  for bi in range(z_ref.shape[0])])

    @pl.when(t == pl.num_programs(1) - 1)
    def _():
        cfin_ref[...] = c_scr[...]
        mfin_ref[...] = m_scr[...]


def _mlstm_chunk(z_ref, zs_ref, bias_ref, mn_ref, h_ref, c_scr, m_scr):
    L = z_ref.shape[0]
    hq = M_HEADS * M_DQK
    g = zs_ref[...] + bias_ref[...]
    lane = _iota((L, LANE), 1)
    gx = jnp.where((lane >= M_HEADS) & (lane < 2 * M_HEADS), _logsigmoid(g), g)
    r, cidx = _iota((L, L), 0), _iota((L, L), 1)
    tri = r >= cidx
    cum_c = _mm_sel(jnp.where(tri, 1.0, 0.0).astype(BF16), gx)
    rows = _mm_sel(jnp.where(r == cidx, 1.0, 0.0).astype(BF16), gx, _TN, x_is_rhs=False)
    cum_r = _mm_sel(jnp.where(r <= cidx, 1.0, 0.0).astype(BF16), gx, _TN, x_is_rhs=False)
    yield
    ones_col = jnp.where(_iota((L, LANE), 1) == 0, 1.0, 0.0).astype(BF16)
    m_all = m_scr[...]
    m_new_all = m_all
    for h in range(M_HEADS):
        q = z_ref[:, h * M_DQK:(h + 1) * M_DQK].astype(BF16)
        k = z_ref[:, hq + h * M_DQK:hq + (h + 1) * M_DQK] * (M_DQK ** -0.5)
        v = z_ref[:, 2 * hq + h * M_DV:2 * hq + (h + 1) * M_DV]
        og = z_ref[:, 2 * hq + M_HEADS * M_DV + h * M_DV:2 * hq + M_HEADS * M_DV + (h + 1) * M_DV]
        va = jnp.concatenate([v.astype(BF16), ones_col], axis=1)
        ic_r, ic_c = rows[h:h + 1, :], gx[:, h:h + 1]
        cr, cc = cum_r[M_HEADS + h:M_HEADS + h + 1, :], cum_c[:, M_HEADS + h:M_HEADS + h + 1]
        m_old = m_all[0:1, h:h + 1]
        dlog = jnp.where(tri, cc - cr + ic_r, -jnp.inf)
        inter = cc + m_old
        mi = jnp.maximum(inter, jnp.max(dlog, axis=1, keepdims=True))
        w = jnp.exp(dlog - mi)
        wi = jnp.exp(inter - mi)
        qk = _mm(q, k.astype(BF16), _NT)
        c_old = c_scr[h]
        qc = _mm(q, c_old.astype(BF16))
        yield
        s = qk * w
        num = wi * qc + _mm(s.astype(BF16), va)
        yield
        qn = num[:, M_DV:M_DV + 1]
        hh = num[:, :M_DV] / jnp.maximum(jnp.abs(qn), jnp.exp(-mi))
        last = cc[L - 1:L, :]
        gl = last - cc + ic_c
        m_new = jnp.maximum(last + m_old, jnp.max(gl, axis=0, keepdims=True))
        wj = jnp.exp(gl - m_new)
        keep = jnp.exp(last + m_old - m_new)
        c_scr[h] = keep * c_old + _mm((wj * k).astype(BF16), va, _TN)
        yield
        m_new_all = jnp.where(_iota((1, LANE), 1) == h, m_new, m_new_all)
        h_ref[:, h * M_DV:(h + 1) * M_DV] = _mlstm_finish(hh, og, mn_ref[:, h * M_DV:(h + 1) * M_DV])
    m_scr[...] = m_new_all


def _mlstm_prompt(z, zs, bias, mn, c0, m0, bsz):
    m = z.shape[0]
    t = m // bsz
    L = next(n for n in (M_CHUNK_PALLAS, M_CHUNK) if t % n == 0)
    nc = t // L
    nb = _seq_group(bsz)
    st, ms = (nb, M_HEADS, M_DQK, M_AUG), (nb, 1, LANE)
    tok = lambda n: pl.BlockSpec((nb, L, n), lambda b, i: (b, i, 0))
    h, c_fin, m_fin = pl.pallas_call(
        _mlstm_body,
        grid=(bsz // nb, nc),
        in_specs=[tok(M_Z_W), tok(LANE), _resident((1, LANE)), _resident((1, M_HEADS * M_DV)),
                  pl.BlockSpec(st, lambda b, i: (b, 0, 0, 0)), pl.BlockSpec(ms, lambda b, i: (b, 0, 0))],
        out_specs=[tok(M_HEADS * M_DV), pl.BlockSpec(st, lambda b, i: (b, 0, 0, 0)), pl.BlockSpec(ms, lambda b, i: (b, 0, 0))],
        out_shape=[jax.ShapeDtypeStruct((bsz, t, M_HEADS * M_DV), F32), jax.ShapeDtypeStruct((bsz,) + st[1:], F32),
                   jax.ShapeDtypeStruct((bsz, 1, LANE), F32)],
        scratch_shapes=[pltpu.VMEM(st, F32), pltpu.VMEM(ms, F32)],
        compiler_params=_params(2),
        name="mlstm_prompt",
    )(z.reshape(bsz, t, -1), zs.reshape(bsz, t, -1), bias, mn, c0, m0)
    return h.reshape(m, -1), c_fin, m_fin


def _mlstm_decode_body(z_ref, zs_ref, bias_ref, mn_ref, c_ref, n_ref, m_ref, h_ref, cn_ref, nn_ref, mo_ref):
    hq = M_HEADS * M_DQK
    rows = SUBLANE
    z = jnp.broadcast_to(z_ref[0], (rows, M_Z_W))
    g = zs_ref[0] + bias_ref[...]
    m_all = m_ref[0]
    m_new_all = m_all
    row0 = _iota((rows, M_DQK), 0) == 0
    for h in range(M_HEADS):
        q = z[:, h * M_DQK:(h + 1) * M_DQK]
        k = z[:, hq + h * M_DQK:hq + (h + 1) * M_DQK] * (M_DQK ** -0.5)
        v = z[:, 2 * hq + h * M_DV:2 * hq + (h + 1) * M_DV]
        og = z[:, 2 * hq + M_HEADS * M_DV + h * M_DV:2 * hq + M_HEADS * M_DV + (h + 1) * M_DV]
        ic = g[:, h:h + 1]
        fl = _logsigmoid(g[:, M_HEADS + h:M_HEADS + h + 1])
        m_old = m_all[:, h:h + 1]
        c_old, n_old = c_ref[0, h], n_ref[0, h:h + 1, :]
        mi = jnp.maximum(fl + m_old, ic)
        w = jnp.exp(ic - mi)
        wi = jnp.exp(fl + m_old - mi)
        s = jnp.sum(q * k, axis=1, keepdims=True) * w
        num = wi * _mm_hp(q, c_old) + s * v
        qn = wi * jnp.sum(q * n_old, axis=1, keepdims=True) + s
        hh = num / jnp.maximum(jnp.abs(qn), jnp.exp(-mi))
        cn_ref[0, h] = wi * c_old + w * _mm_hp(jnp.where(row0, k, 0.0), v, _TN)
        nn_ref[0, h:h + 1, :] = wi * n_old + w * k[0:1]
        m_new_all = jnp.where(_iota((1, LANE), 1) == h, mi, m_new_all)
        h_ref[0, :, h * M_DV:(h + 1) * M_DV] = _mlstm_finish(hh, og, mn_ref[:, h * M_DV:(h + 1) * M_DV])[0:1]
    mo_ref[0] = m_new_all


def _mlstm_decode(z, zs, bias, mn, c0, n0, m0):
    b = z.shape[0]
    cs, ns = (1, M_HEADS, M_DQK, M_DV), (1, M_HEADS, M_DQK)
    m0p = jnp.pad(m0, ((0, 0), (0, LANE - M_HEADS))).reshape(b, 1, LANE)
    row3 = lambda n: pl.BlockSpec((1, 1, n), lambda i: (i, 0, 0))
    h, cn, nn, mo = pl.pallas_call(
        _mlstm_decode_body,
        grid=(b,),
        in_specs=[row3(M_Z_W), row3(LANE), _resident((1, LANE)), _resident((1, M_HEADS * M_DV)),
                  pl.BlockSpec(cs, lambda i: (i, 0, 0, 0)), pl.BlockSpec(ns, lambda i: (i, 0, 0)), row3(LANE)],
        out_specs=[row3(M_HEADS * M_DV), pl.BlockSpec(cs, lambda i: (i, 0, 0, 0)), pl.BlockSpec(ns, lambda i: (i, 0, 0)), row3(LANE)],
        out_shape=[jax.ShapeDtypeStruct((b, 1, M_HEADS * M_DV), F32), jax.ShapeDtypeStruct((b,) + cs[1:], F32),
                   jax.ShapeDtypeStruct((b,) + ns[1:], F32), jax.ShapeDtypeStruct((b, 1, LANE), F32)],
        compiler_params=_params(1),
        name="mlstm_decode",
    )(z.reshape(b, 1, -1), zs.reshape(b, 1, -1), bias, mn, c0, n0, m0p)
    return h.reshape(b, -1), cn, nn, mo[:, 0, :M_HEADS]


def _compress_dense_body(x_ref, pe_ref, w1_ref, w2_ref, o_ref, acc):
    kk = pl.program_id(1)

    @pl.when(kk == 0)
    def _():
        acc[...] = jnp.zeros_like(acc)

    acc[...] += _mm((x_ref[...] + pe_ref[...]).astype(BF16), w1_ref[...])

    @pl.when(kk == pl.num_programs(1) - 1)
    def _():
        o_ref[...] = _mm(_gelu(acc[...]).astype(BF16), w2_ref[...])


def _compress_dense(x, pe, w1, w2, tk=2048):
    r, kdim = x.shape
    tr = r if r <= 512 else 512
    assert r % tr == 0 and kdim % tk == 0
    return pl.pallas_call(
        _compress_dense_body,
        grid=(r // tr, kdim // tk),
        in_specs=[pl.BlockSpec((tr, tk), lambda i, k: (i, k)), pl.BlockSpec((1, tk), lambda i, k: (0, k)),
                  pl.BlockSpec((tk, NSA_KV_W), lambda i, k: (k, 0)), _resident(w2.shape)],
        out_specs=pl.BlockSpec((tr, NSA_KV_W), lambda i, k: (i, 0)),
        out_shape=jax.ShapeDtypeStruct((r, NSA_KV_W), F32),
        scratch_shapes=[pltpu.VMEM((tr, NSA_KV_W), F32)],
        compiler_params=_params(2),
        name="nsa_compress",
    )(x, pe, w1, w2)


_MAX_PAGES_PER_STEP = 32
_D_PER_STEP = 8


def _compress_paged_body(pt_ref, *refs, n_pages, pps):
    page_refs = refs[:pps]
    pe_ref, w1_ref, w2_ref, o_ref, xs = refs[pps:]
    g = pl.program_id(1)
    for p, pr in enumerate(page_refs):
        r0 = pl.multiple_of((g * pps + p) * NSA_KV_W, NSA_KV_W)
        xs[pl.ds(r0, NSA_KV_W), :] = pr[0]

    @pl.when(g == pl.num_programs(1) - 1)
    def _():
        kvd = NSA_KV_HEADS * NSA_DH
        for c in range(2):
            def step(j, acc):
                parts = []
                for dd in range(_D_PER_STEP):
                    d = j * _D_PER_STEP + dd
                    pe_row = pe_ref[pl.ds(c * NSA_DH + d, 1), :]
                    rows = [xs[pl.ds(c * kvd + k * NSA_DH + d, n_pages, stride=NSA_KV_W), :] + pe_row
                            for k in range(NSA_KV_HEADS)]
                    parts.append(jnp.concatenate(rows, axis=0).astype(BF16))
                return acc + _mm(jnp.concatenate(parts, axis=1), w1_ref[c, j])
            hid = lax.fori_loop(0, NSA_DH // _D_PER_STEP, step, jnp.zeros((NSA_KV_HEADS * n_pages, PAGE_SIZE), F32))
            out = _mm(_gelu(hid).astype(BF16), w2_ref[c])
            o_ref[c * NSA_KV_HEADS * n_pages:(c + 1) * NSA_KV_HEADS * n_pages, :] = out


def _compress_paged(pool_t, page_table, pe_t, w1_t, w2_t):
    b, n_pages = page_table.shape
    pps = next(n for n in range(min(_MAX_PAGES_PER_STEP, n_pages), 0, -1) if n_pages % n == 0)
    assert _BLOCKS_PER_PAGE == 2
    rows_out = 2 * NSA_KV_HEADS * n_pages

    def page_spec(p):
        return pl.BlockSpec((1, NSA_KV_W, PAGE_SIZE), lambda i, g, pt: (pt[i, g * pps + p], 0, 0))

    def fixed(a):
        nd = a.ndim
        return pl.BlockSpec(a.shape, lambda i, g, pt: (0,) * nd, pipeline_mode=pl.Buffered(1))

    grid_spec = pltpu.PrefetchScalarGridSpec(
        num_scalar_prefetch=1,
        grid=(b, n_pages // pps),
        in_specs=[page_spec(p) for p in range(pps)] + [fixed(pe_t), fixed(w1_t), fixed(w2_t)],
        out_specs=pl.BlockSpec((rows_out, PAGE_SIZE), lambda i, g, pt: (i, 0)),
        scratch_shapes=[pltpu.VMEM((n_pages * NSA_KV_W, PAGE_SIZE), F32)],
    )
    return pl.pallas_call(
        functools.partial(_compress_paged_body, n_pages=n_pages, pps=pps),
        grid_spec=grid_spec,
        out_shape=jax.ShapeDtypeStruct((b * rows_out, PAGE_SIZE), F32),
        compiler_params=_params(2, VMEM_BIG),
        name="nsa_compress_paged",
    )(page_table, *([pool_t] * pps), pe_t, w1_t, w2_t)


def _gate_rows(zs_ref, gb_ref):
    return _sigmoid(zs_ref[0] + gb_ref[...])


def _gate_row(gates, head, branch):
    r = GATE_OFF + head * NSA_BRANCHES + branch
    return gates[r:r + 1, :]


def _cmp_sel_body(q_ref, kvc_ref, zs_ref, gb_ref, ocmp_ref, nm_ref, sc_scr, thr_scr):
    qi = pl.program_id(1)
    tq = q_ref.shape[2]
    nbp = kvc_ref.shape[0]
    t0 = qi * tq
    tpos = t0 + _iota((nbp, tq), 1)
    blk = _iota((nbp, tq), 0)
    cur = jnp.right_shift(tpos, NSA_BLOCK.bit_length() - 1)
    vis = blk * NSA_BLOCK + (NSA_BLOCK - 1) <= tpos
    forced = (blk == 0) | (blk == cur) | (blk == cur - 1)
    allowed = blk <= cur
    gates = _gate_rows(zs_ref, gb_ref)
    for kh in range(NSA_KV_HEADS):
        kc = kvc_ref[:, kh * NSA_DH:(kh + 1) * NSA_DH].astype(BF16)
        vc = kvc_ref[:, NSA_KV_HEADS * NSA_DH + kh * NSA_DH:NSA_KV_HEADS * NSA_DH + (kh + 1) * NSA_DH].astype(BF16)
        probs = [None] * NSA_GROUP

        def head(gi):
            h = kh * NSA_GROUP + gi
            q = (q_ref[0, h * NSA_DH:(h + 1) * NSA_DH, :] * (NSA_DH ** -0.5)).astype(BF16)
            s = _mm(kc, q)
            yield
            s = jnp.where(vis, s, -jnp.inf)
            m = jnp.max(s, axis=0, keepdims=True)
            m = jnp.where(m == -jnp.inf, 0.0, m)
            e = jnp.exp(s - m)
            p = e / jnp.maximum(jnp.sum(e, axis=0, keepdims=True), TINY)
            probs[gi] = p
            o = _mm(vc, p.astype(BF16), _TN)
            yield
            ocmp_ref[0, h * NSA_DH:(h + 1) * NSA_DH, :] = o * _gate_row(gates, h, 0)

        _round_robin([head(gi) for gi in range(NSA_GROUP)])
        imp = probs[0]
        for p in probs[1:]:
            imp = imp + p
        score = jnp.where(forced, jnp.inf, jnp.where(allowed, imp, -jnp.inf))
        key = lax.bitcast_convert_type(score, jnp.int32)
        for lc in range(tq // LANE):
            ls = slice(lc * LANE, (lc + 1) * LANE)
            n_live = (t0 + (lc + 1) * LANE - 1) // NSA_BLOCK + 1

            def rank(rows):
                sc_scr[0:rows, :] = key[0:rows, ls]
                thr_scr[0:rows, :] = key[0:rows, ls]

                def rank_step(i, cnt):
                    for j in (2 * i, 2 * i + 1):
                        row = sc_scr[pl.ds(j, 1), :]
                        thr_scr[pl.ds(j, 1), :] = row + 1
                        cnt = cnt + jnp.where(row >= thr_scr[0:rows, :], 1, 0)
                    return cnt

                cnt = lax.fori_loop(0, n_live // 2, rank_step, jnp.zeros((rows, LANE), jnp.int32))
                nm_ref[0, kh * nbp:kh * nbp + rows, ls] = jnp.where(
                    allowed[0:rows, ls], jnp.where(cnt < NSA_TOP_N, 0.0, -MASK_BIG), -MASK_BIG).astype(BF16)

            half = nbp // 2

            @pl.when(n_live <= half)
            def _():
                rank(half)
                nm_ref[0, kh * nbp + half:(kh + 1) * nbp, ls] = jnp.full((nbp - half, LANE), -MASK_BIG, BF16)

            @pl.when(n_live > half)
            def _():
                rank(nbp)


def _cmp_sel(q_t, kvc, zs_t, gb_col, nbp, tq):
    bsz, hd, t = q_t.shape
    tq = min(tq, t)
    nt = t // tq
    tok = lambda n: pl.BlockSpec((1, n, tq), lambda b, i: (b, 0, i))
    return pl.pallas_call(
        _cmp_sel_body,
        grid=(bsz, nt),
        in_specs=[tok(hd), pl.BlockSpec((nbp, NSA_KV_W), lambda b, i: (b, 0)), tok(LANE), _resident((LANE, 1))],
        out_specs=[tok(hd), tok(NSA_KV_HEADS * nbp)],
        out_shape=[jax.ShapeDtypeStruct((bsz, hd, t), F32), jax.ShapeDtypeStruct((bsz, NSA_KV_HEADS * nbp, t), BF16)],
        scratch_shapes=[pltpu.VMEM((nbp, LANE), jnp.int32), pltpu.VMEM((nbp, LANE), jnp.int32)],
        compiler_params=_params(2),
        name="nsa_cmp_select",
    )(q_t, kvc, zs_t, gb_col)


def _sel_attn_body(q_ref, nm_ref, kv_ref, oh_ref, v_ref, zs_ref, gb_ref, o_ref, *, tk):
    qi = pl.program_id(1)
    tq = q_ref.shape[2]
    nbp = nm_ref.shape[1] // NSA_KV_HEADS
    cols = NSA_GROUP * tq
    t0 = qi * tq
    n_before = t0 // tk
    gates = _gate_rows(zs_ref, gb_ref)
    qas = []
    for kh in range(NSA_KV_HEADS):
        nm = nm_ref[0, kh * nbp:(kh + 1) * nbp, :]
        qas.append(jnp.concatenate(
            [jnp.concatenate([nm, (q_ref[0, h * NSA_DH:(h + 1) * NSA_DH, :] * (NSA_DH ** -0.5 * LOG2E)).astype(BF16)], axis=0)
             for h in range(kh * NSA_GROUP, (kh + 1) * NSA_GROUP)], axis=1))

    def tile(kh, kt, carry, causal, out):
        m_old, acc = carry
        k0 = pl.multiple_of(kt * tk, tk)
        ka = jnp.concatenate([oh_ref[pl.ds(k0, tk), :],
                              kv_ref[pl.ds(k0, tk), kh * NSA_DH:(kh + 1) * NSA_DH].astype(BF16)], axis=1)
        s = _mm(ka, qas[kh])
        yield
        if causal:
            kpos = k0 + _iota((tk, cols), 0)
            tpos = t0 + jnp.bitwise_and(_iota((tk, cols), 1), tq - 1)
            s = jnp.where(kpos <= tpos, s, -jnp.inf)
        m_new = jnp.maximum(m_old, jnp.max(s, axis=0, keepdims=True))
        alpha = jnp.exp2(m_old - m_new)
        p = jnp.exp2(s - m_new)
        yield
        out[kh] = (m_new, alpha * acc + _mm(v_ref[0, kh, :, pl.ds(k0, tk)], p.astype(BF16)))

    def tiles(kt, carries, causal):
        out = [None] * NSA_KV_HEADS
        _round_robin([tile(kh, kt, carries[kh], causal, out) for kh in range(NSA_KV_HEADS)])
        return tuple(out)

    init = (jnp.full((1, cols), -jnp.inf, F32), jnp.zeros((V_AUG, cols), F32))
    carries = lax.fori_loop(0, n_before, lambda kt, c: tiles(kt, c, False), (init,) * NSA_KV_HEADS)
    carries = tiles(n_before, carries, True)
    for kh in range(NSA_KV_HEADS):
        acc = carries[kh][1]
        o = acc[:NSA_DH] / acc[NSA_DH:NSA_DH + 1]
        for gi in range(NSA_GROUP):
            h = kh * NSA_GROUP + gi
            o_ref[0, h * NSA_DH:(h + 1) * NSA_DH, :] = o[:, gi * tq:(gi + 1) * tq] * _gate_row(gates, h, 1)


def _sel_attn(q_t, nm_t, kv, onehot, vs_t, zs_t, gb_col, tq, tk):
    bsz, hd, t = q_t.shape
    tq, tk = min(tq, t), min(tk, t)
    assert t % tq == 0 and t % tk == 0 and tk % tq == 0 and tq & (tq - 1) == 0
    nt = t // tq
    tok = lambda n: pl.BlockSpec((1, n, tq), lambda b, i: (b, 0, i))
    return pl.pallas_call(
        functools.partial(_sel_attn_body, tk=tk),
        grid=(bsz, nt),
        in_specs=[tok(hd), tok(nm_t.shape[1]), pl.BlockSpec((t, NSA_KV_W), lambda b, i: (b, 0)), _resident(onehot.shape),
                  pl.BlockSpec((1,) + vs_t.shape[1:], lambda b, i: (b, 0, 0, 0)), tok(LANE), _resident((LANE, 1))],
        out_specs=tok(hd),
        out_shape=jax.ShapeDtypeStruct((bsz, hd, t), F32),
        compiler_params=_params(2, VMEM_BIG),
        name="nsa_selected",
    )(q_t, nm_t, kv, onehot, vs_t, zs_t, gb_col)


def _win_attn_body(q_ref, k_ref, v_ref, zs_ref, gb_ref, o_ref, *, span):
    qi = pl.program_id(1)
    tq = q_ref.shape[2]
    cols = NSA_GROUP * tq
    t0 = qi * tq
    start = pl.multiple_of(jnp.maximum(t0 + tq - span, 0), tq)
    gates = _gate_rows(zs_ref, gb_ref)
    kpos = start + _iota((span, cols), 0)
    tpos = t0 + jnp.bitwise_and(_iota((span, cols), 1), tq - 1)
    ok = (kpos <= tpos) & (tpos - kpos < NSA_WINDOW)
    for kh in range(NSA_KV_HEADS):
        qa = jnp.concatenate([(q_ref[0, h * NSA_DH:(h + 1) * NSA_DH, :] * (NSA_DH ** -0.5 * LOG2E)).astype(BF16)
                              for h in range(kh * NSA_GROUP, (kh + 1) * NSA_GROUP)], axis=1)
        kw = k_ref[pl.ds(start, span), kh * NSA_DH:(kh + 1) * NSA_DH].astype(BF16)
        s = jnp.where(ok, _mm(kw, qa), -jnp.inf)
        e = jnp.exp2(s - jnp.max(s, axis=0, keepdims=True))
        o = _mm(v_ref[0, kh, :, pl.ds(start, span)], e.astype(BF16)) / jnp.sum(e, axis=0, keepdims=True)
        for gi in range(NSA_GROUP):
            h = kh * NSA_GROUP + gi
            o_ref[0, h * NSA_DH:(h + 1) * NSA_DH, :] = o[:, gi * tq:(gi + 1) * tq] * _gate_row(gates, h, 2)


def _win_attn(q_t, kv, vw_t, zs_t, gb_col, tq):
    bsz, hd, t = q_t.shape
    tq = min(tq, t)
    span = min(NSA_WINDOW + tq, t)
    assert t % tq == 0 and tq & (tq - 1) == 0
    nt = t // tq
    tok = lambda n: pl.BlockSpec((1, n, tq), lambda b, i: (b, 0, i))
    whole = lambda a: pl.BlockSpec((1,) + a.shape[1:], lambda b, i: (b, 0, 0, 0))
    return pl.pallas_call(
        functools.partial(_win_attn_body, span=span),
        grid=(bsz, nt),
        in_specs=[tok(hd), pl.BlockSpec((t, NSA_KV_W), lambda b, i: (b, 0)), whole(vw_t), tok(LANE), _resident((LANE, 1))],
        out_specs=tok(hd),
        out_shape=jax.ShapeDtypeStruct((bsz, hd, t), F32),
        compiler_params=_params(2),
        name="nsa_window",
    )(q_t, kv, vw_t, zs_t, gb_col)


def _decode_forced(n_past_blk):
    cur = n_past_blk
    return sorted({0, cur - 1, cur} - {-1})


def _cmp_decode_body(q_ref, kvc_ref, gz_ref, gb_ref, o_ref, idx_ref, *, past, n_pick):
    n_pages = kvc_ref.shape[0] // (2 * NSA_KV_HEADS)
    nb = _BLOCKS_PER_PAGE * n_pages

    def block_id(shape):
        pos = _iota(shape, 1)
        page = jnp.where(pos >= n_pages, pos - n_pages, pos)
        return page * _BLOCKS_PER_PAGE + jnp.where(pos >= n_pages, 1, 0)

    lane = block_id((NSA_HEADS, nb))
    hrow = _iota((NSA_HEADS, nb), 0)
    vis = lane * NSA_BLOCK + (NSA_BLOCK - 1) <= past
    q = (q_ref[0] * (NSA_DH ** -0.5)).astype(BF16)
    gates = _sigmoid(gz_ref[0] + gb_ref[...])
    cur = past // NSA_BLOCK
    o_all = jnp.zeros((NSA_HEADS, NSA_DH), F32)
    idx_all = jnp.zeros((SUBLANE, LANE), F32)
    orow = _iota((NSA_HEADS, NSA_DH), 0)
    slot_r, slot_c = _iota((SUBLANE, LANE), 0), _iota((SUBLANE, LANE), 1)
    l1 = block_id((1, nb))
    l1f = l1.astype(F32)
    forced = (l1 == 0) | (l1 == cur) | (l1 == cur - 1)
    for kh in range(NSA_KV_HEADS):
        kc = kvc_ref[kh * n_pages:(kh + 1) * n_pages, :].astype(BF16)
        vc = kvc_ref[(NSA_KV_HEADS + kh) * n_pages:(NSA_KV_HEADS + kh + 1) * n_pages, :].astype(BF16)
        s = jnp.concatenate([_mm(q, kc[:, j * NSA_DH:(j + 1) * NSA_DH], _NT) for j in range(_BLOCKS_PER_PAGE)], axis=1)
        s = jnp.where(vis, s, -jnp.inf)
        m = jnp.max(s, axis=1, keepdims=True)
        m = jnp.where(m == -jnp.inf, 0.0, m)
        e = jnp.exp(s - m)
        p = e / jnp.maximum(jnp.sum(e, axis=1, keepdims=True), TINY)
        mine = (hrow >= kh * NSA_GROUP) & (hrow < (kh + 1) * NSA_GROUP)
        pb = p.astype(BF16)
        o_kh = _mm(pb[:, :n_pages], vc[:, :NSA_DH])
        for j in range(1, _BLOCKS_PER_PAGE):
            o_kh = o_kh + _mm(pb[:, j * n_pages:(j + 1) * n_pages], vc[:, j * NSA_DH:(j + 1) * NSA_DH])
        o_all = jnp.where((orow >= kh * NSA_GROUP) & (orow < (kh + 1) * NSA_GROUP), o_kh, o_all)
        imp = jnp.sum(jnp.where(mine, p, 0.0), axis=0, keepdims=True)
        score = jnp.where(forced, -jnp.inf, imp)
        for r in range(n_pick):
            mx = jnp.max(score, axis=1, keepdims=True)
            pick = jnp.min(jnp.where(score == mx, l1f, float(nb)), axis=1, keepdims=True)
            score = jnp.where(l1f == pick, -jnp.inf, score)
            idx_all = jnp.where((slot_r == kh) & (slot_c == r), pick, idx_all)
    o_ref[0] = o_all * gates[:, 0:1]
    idx_ref[0] = idx_all.astype(jnp.int32)


def _cmp_decode(qh, kvc, gz, gb3, past, n_pick):
    b = qh.shape[0]
    rows = kvc.shape[0] // b
    return pl.pallas_call(
        functools.partial(_cmp_decode_body, past=past, n_pick=n_pick),
        grid=(b,),
        in_specs=[pl.BlockSpec((1, NSA_HEADS, NSA_DH), lambda i: (i, 0, 0)), pl.BlockSpec((rows, kvc.shape[1]), lambda i: (i, 0)),
                  pl.BlockSpec((1, NSA_HEADS, NSA_BRANCHES), lambda i: (i, 0, 0)), _resident((NSA_HEADS, NSA_BRANCHES))],
        out_specs=[pl.BlockSpec((1, NSA_HEADS, NSA_DH), lambda i: (i, 0, 0)), pl.BlockSpec((1, SUBLANE, LANE), lambda i: (i, 0, 0))],
        out_shape=[jax.ShapeDtypeStruct((b, NSA_HEADS, NSA_DH), F32), jax.ShapeDtypeStruct((b, SUBLANE, LANE), jnp.int32)],
        compiler_params=_params(1),
        name="nsa_cmp_decode",
    )(qh, kvc, gz, gb3)


def _selwin_decode_body(info_ref, q_ref, ns_ref, nw_ref, win_ref, *refs, n_shared, n_own, first_win_row):
    n_blk = n_shared + NSA_KV_HEADS * n_own
    page_refs = refs[:n_blk]
    gz_ref, gb_ref, o_ref, wout_ref = refs[n_blk:]
    i = pl.program_id(0)
    q = q_ref[0] * (NSA_DH ** -0.5)
    qb = q.astype(BF16)
    gates = _sigmoid(gz_ref[0] + gb_ref[...])
    hrow = _iota((NSA_HEADS, NSA_DH), 0)
    n_win = win_ref.shape[2]
    wlane = _iota((NSA_HEADS, n_win), 1)
    pblk = jnp.right_shift(_iota((NSA_HEADS, PAGE_SIZE), 1), NSA_BLOCK.bit_length() - 1)
    o_all = jnp.zeros((NSA_HEADS, NSA_DH), F32)
    voff = NSA_KV_HEADS * NSA_DH
    for kh in range(NSA_KV_HEADS):
        ksl = slice(kh * NSA_DH, (kh + 1) * NSA_DH)
        vsl = slice(voff + kh * NSA_DH, voff + (kh + 1) * NSA_DH)
        slots = list(range(n_shared)) + list(range(n_shared + kh * n_own, n_shared + (kh + 1) * n_own))
        s_parts = []
        for sl in slots:
            blk_in_page = jnp.bitwise_and(info_ref[i, sl], _BLOCKS_PER_PAGE - 1)
            sp = _mm(qb, page_refs[sl][0, ksl, :].astype(BF16))
            s_parts.append(jnp.where(pblk == blk_in_page, sp, -jnp.inf))
        kn, vn = ns_ref[0, :, ksl], ns_ref[0, :, vsl]
        sn = jnp.sum(q * kn, axis=1, keepdims=True)
        m = sn
        for sp in s_parts:
            m = jnp.maximum(m, jnp.max(sp, axis=1, keepdims=True))
        en = jnp.exp(sn - m)
        l, acc = en, en * vn
        for sl, sp in zip(slots, s_parts):
            e = jnp.exp(sp - m)
            l = l + jnp.sum(e, axis=1, keepdims=True)
            acc = acc + _mm(e.astype(BF16), page_refs[sl][0, vsl, :].astype(BF16), _NT)
        o_sel = acc / l
        sw = jnp.where(wlane >= first_win_row, _mm(qb, win_ref[0, ksl, :].astype(BF16)), -jnp.inf)
        kwn, vwn = nw_ref[0, :, ksl], nw_ref[0, :, vsl]
        swn = jnp.sum(q * kwn, axis=1, keepdims=True)
        mw = jnp.maximum(jnp.max(sw, axis=1, keepdims=True), swn)
        ew, ewn = jnp.exp(sw - mw), jnp.exp(swn - mw)
        o_win = (_mm(ew.astype(BF16), win_ref[0, vsl, :].astype(BF16), _NT) + ewn * vwn) / (jnp.sum(ew, axis=1, keepdims=True) + ewn)
        keep = (hrow >= kh * NSA_GROUP) & (hrow < (kh + 1) * NSA_GROUP)
        o_all = jnp.where(keep, gates[:, 1:2] * o_sel + gates[:, 2:3] * o_win, o_all)
    o_ref[0] = o_all
    r2, c2 = _iota((NSA_KV_W, NSA_KV_W), 0), _iota((NSA_KV_W, NSA_KV_W), 1)
    new_col = jnp.sum(jnp.where(r2 == c2, jnp.broadcast_to(nw_ref[0], (NSA_KV_W, NSA_KV_W)), 0.0), axis=1, keepdims=True)
    shifted = pltpu.roll(win_ref[0], n_win - 1, 1)
    wout_ref[0] = jnp.where(_iota((NSA_KV_W, n_win), 1) == n_win - 1, new_col, shifted)


def _selwin_decode(info, qh, new_s, new_w, win_t, pool_t, gz, gb3, n_shared, n_own, first_win_row):
    b = qh.shape[0]
    n_blk = info.shape[1]
    row3 = lambda n: pl.BlockSpec((1, 1, n), lambda i, r: (i, 0, 0))

    def page_spec(s):
        return pl.BlockSpec((1, NSA_KV_W, PAGE_SIZE), lambda i, r: (jnp.right_shift(r[i, s], _PAGE_SHIFT), 0, 0))

    win_spec = pl.BlockSpec((1,) + win_t.shape[1:], lambda i, r: (i, 0, 0))
    grid_spec = pltpu.PrefetchScalarGridSpec(
        num_scalar_prefetch=1,
        grid=(b,),
        in_specs=[pl.BlockSpec((1, NSA_HEADS, NSA_DH), lambda i, r: (i, 0, 0)), row3(NSA_KV_W), row3(NSA_KV_W), win_spec]
        + [page_spec(s) for s in range(n_blk)]
        + [pl.BlockSpec((1, NSA_HEADS, NSA_BRANCHES), lambda i, r: (i, 0, 0)),
           pl.BlockSpec((NSA_HEADS, NSA_BRANCHES), lambda i, r: (0, 0))],
        out_specs=[pl.BlockSpec((1, NSA_HEADS, NSA_DH), lambda i, r: (i, 0, 0)), win_spec],
    )
    return pl.pallas_call(
        functools.partial(_selwin_decode_body, n_shared=n_shared, n_own=n_own, first_win_row=first_win_row),
        grid_spec=grid_spec,
        out_shape=[jax.ShapeDtypeStruct((b, NSA_HEADS, NSA_DH), F32), jax.ShapeDtypeStruct(win_t.shape, F32)],
        compiler_params=_params(1),
        name="nsa_selwin_decode",
    )(info, qh, new_s.reshape(b, 1, -1), new_w.reshape(b, 1, -1), win_t, *([pool_t] * n_blk), gz, gb3)


def _feature_major(cache):
    n, rows = cache.shape[:2]
    return jnp.transpose(cache, (0, 2, 3, 4, 1)).reshape(n, NSA_KV_W, rows)


def _pad_cols(w, n):
    return jnp.pad(w, ((0, 0), (0, n - w.shape[1])))


def _even_w_in(w):
    sizes = (GLA_HEADS * GLA_DK, GLA_HEADS * GLA_DK, GLA_HEADS * GLA_DV, GLA_HEADS * GLA_DV, GLA_RANK,
             NSA_HEADS * NSA_DH, NSA_BRANCHES * NSA_KV_W, NSA_HEADS * NSA_BRANCHES)
    cuts = [0]
    for s in sizes:
        cuts.append(cuts[-1] + s)
    gq, gk, gv, gg, ga, nq, nkv, ng = (w[:, cuts[i]:cuts[i + 1]] for i in range(len(sizes)))
    small = _pad_cols(jnp.concatenate([ga, ng], axis=1), LANE)
    sample = jnp.concatenate([gq, gk, gv, gg, nq, nkv, small], axis=1).astype(BF16)
    prompt = jnp.concatenate([gq, gk, gv, gg, nkv, small], axis=1).astype(BF16)
    prompt_t = jnp.concatenate([nq, nkv, small], axis=1).T.astype(BF16)
    return sample, prompt, prompt_t


_EVEN_WIDTHS = (GLA_Z_W, NSA_HEADS * NSA_DH, NSA_KV_W, NSA_KV_W, NSA_KV_W, LANE)
_EVEN_WIDTHS_P = (GLA_Z_W, NSA_KV_W, NSA_KV_W, NSA_KV_W, LANE)
_EVEN_WIDTHS_PT = (NSA_HEADS * NSA_DH, NSA_KV_W, NSA_KV_W, NSA_KV_W, LANE)


def _odd_w_in(w):
    main = M_Z_W
    return jnp.concatenate([w[:, :main], _pad_cols(w[:, main:], LANE)], axis=1).astype(BF16)


_ODD_WIDTHS = (M_Z_W, LANE)


def _compress_weights(pe, w1, w2):
    eye_k = jnp.eye(NSA_KV_HEADS, dtype=F32)
    eye_c = jnp.eye(2, dtype=F32)
    w1big = jnp.einsum("csde,cx,ky->sckdxye", w1, eye_c, eye_k).reshape(NSA_BLOCK * NSA_KV_W, NSA_KV_W)
    w2big = jnp.einsum("ced,cx,ky->ckexyd", w2, eye_c, eye_k).reshape(NSA_KV_W, NSA_KV_W)
    pe_flat = jnp.broadcast_to(pe.transpose(1, 0, 2)[:, :, None, :], (NSA_BLOCK, 2, NSA_KV_HEADS, NSA_DH)).reshape(1, -1)
    return pe_flat, w1big.astype(BF16), w2big.astype(BF16)


def _compress_weights_paged(pe, w1, w2):
    eye_b = jnp.eye(_BLOCKS_PER_PAGE, dtype=F32)
    pe_t = jnp.tile(pe.transpose(0, 2, 1), (1, 1, _BLOCKS_PER_PAGE)).reshape(2 * NSA_DH, PAGE_SIZE)
    w1_t = jnp.einsum("csde,hx->cdhsxe", w1, eye_b).reshape(2, NSA_DH // _D_PER_STEP, _D_PER_STEP * PAGE_SIZE,
                                                            _BLOCKS_PER_PAGE * NSA_CMP_HID)
    w2_t = jnp.einsum("ced,hx->chexd", w2, eye_b).reshape(2, _BLOCKS_PER_PAGE * NSA_CMP_HID, _BLOCKS_PER_PAGE * NSA_DH)
    return pe_t, w1_t.astype(BF16), w2_t.astype(BF16)


def _gate_bias_row(gb):
    return jnp.pad(gb, (GATE_OFF, LANE - GATE_OFF - gb.shape[0])).reshape(1, LANE)


def _split_outs(outs, n):
    return outs[:n], outs[n:]


def _values_t(kv_t, rows=NSA_DH):
    b, _, t = kv_t.shape
    v = kv_t.reshape(b, 2, NSA_KV_HEADS, NSA_DH, t)[:, 1].astype(BF16)
    if rows > NSA_DH:
        extra = jnp.zeros((b, NSA_KV_HEADS, rows - NSA_DH, t), BF16).at[:, :, 0].set(1.0)
        v = jnp.concatenate([v, extra], axis=2)
    return v


def _kv_rows(kv_t):
    b, _, t = kv_t.shape
    return jnp.transpose(kv_t.reshape(b, 2, NSA_KV_HEADS, NSA_DH, t), (0, 4, 1, 2, 3))


def _heads_major(kv, which):
    b, t, _ = kv.shape
    return kv.reshape(b, t, 2, NSA_KV_HEADS, NSA_DH)[:, :, which].transpose(0, 2, 1, 3).astype(BF16)


def _even_layer_prompt(x, bsz, g, prm):
    m = x.shape[0]
    t = m // bsz
    (z, kvc, kvs, kvw, zs), (q_t, kvc_t, kvs_t, kvw_t, zs_t) = _split_outs(
        _norm_proj(x, g, prm["w_in_p"], _EVEN_WIDTHS_P, TM_PROJ, prm["w_in_pt"], _EVEN_WIDTHS_PT, bsz), len(_EVEN_WIDTHS_P))
    o_gla, s_fin = _gla_prompt(z, zs, prm["wa"], prm["ba"], prm["gn"],
                               jnp.zeros((bsz, GLA_HEADS, GLA_DK, GLA_DV), F32), bsz)
    nb = t // NSA_BLOCK
    nbp = -(-nb // LANE) * LANE
    kvcmp = _compress_dense(kvc.reshape(bsz * nb, NSA_BLOCK * NSA_KV_W), prm["pe"], prm["w1"], prm["w2"])
    kvcmp = jnp.pad(kvcmp.reshape(bsz, nb, NSA_KV_W), ((0, 0), (0, nbp - nb), (0, 0))).reshape(bsz * nbp, NSA_KV_W)
    o_cmp, nm = _cmp_sel(q_t, kvcmp, zs_t, prm["gb_col"], nbp, TQ_CMP)
    onehot = (jnp.arange(t)[:, None] // NSA_BLOCK == jnp.arange(nbp)[None, :]).astype(BF16)
    o_sel = _sel_attn(q_t, nm, kvs, onehot, _values_t(kvs_t, V_AUG), zs_t, prm["gb_col"], TQ_ATTN, TK_SEL)
    o_win = _win_attn(q_t, kvw, _values_t(kvw_t), zs_t, prm["gb_col"], TQ_ATTN)
    y = _out_proj(x, [o_gla], (1,), prm["w_out"], TM_OUT, [o_cmp, o_sel, o_win], bsz)
    n_keep = min(NSA_WINDOW, t)
    return y, s_fin, _kv_rows(kvc_t), _kv_rows(kvs_t), _kv_rows(kvw_t[:, :, t - n_keep:])


def _even_layer_sample(x, g, prm, gla_state, cmp_pool, sel_pool, win_buf, page_table):
    b = x.shape[0]
    n_pages = page_table.shape[1]
    past = n_pages * PAGE_SIZE
    n_past_blk = past // NSA_BLOCK
    z, nq, kvc, kvs, kvw, zs = _norm_proj(x, g, prm["w_in"], _EVEN_WIDTHS, b)
    o_gla, s_new = _gla_decode(z, zs, prm["wa"], prm["ba"], prm["gn"], gla_state)
    kvcmp = _compress_paged(_feature_major(cmp_pool), page_table, prm["pe_t"], prm["w1_t"], prm["w2_t"])
    qh = nq.reshape(b, NSA_HEADS, NSA_DH)
    gz = zs[:, GATE_OFF:GATE_OFF + NSA_HEADS * NSA_BRANCHES].reshape(b, NSA_HEADS, NSA_BRANCHES)
    forced = _decode_forced(n_past_blk)
    n_pick = NSA_TOP_N - len(forced)
    assert n_past_blk - len(forced) + 1 >= n_pick
    o_cmp, idx = _cmp_decode(qh, kvcmp, gz, prm["gb3"], past, n_pick)
    shared = [f for f in forced if f < n_past_blk]
    logical = jnp.concatenate([jnp.broadcast_to(jnp.asarray(shared, jnp.int32), (b, len(shared))),
                               idx[:, :NSA_KV_HEADS, :n_pick].reshape(b, NSA_KV_HEADS * n_pick)], axis=1)
    per_page = _BLOCKS_PER_PAGE
    phys = jnp.take_along_axis(page_table, logical // per_page, axis=1) * per_page + logical % per_page
    win_keep = win_buf.shape[1]
    assert win_keep >= 1
    first_win_row = max(win_keep - NSA_WINDOW + 1, 0)
    o_sw, win_new_t = _selwin_decode(phys.astype(jnp.int32), qh, kvs, kvw, _feature_major(win_buf), _feature_major(sel_pool),
                                     gz, prm["gb3"], len(shared), n_pick, first_win_row)
    hd = NSA_HEADS * NSA_DH
    y = _out_proj(x, [o_gla, o_cmp.reshape(b, hd), o_sw.reshape(b, hd)], (1, 2), prm["w_out"], b)
    kv_shape = (b, 1, 2, NSA_KV_HEADS, NSA_DH)
    win_new = jnp.transpose(win_new_t.reshape(b, 2, NSA_KV_HEADS, NSA_DH, win_keep), (0, 4, 1, 2, 3))
    return y, s_new, kvc.reshape(kv_shape), kvs.reshape(kv_shape), win_new


def _odd_layer_prompt(x, bsz, g, prm):
    z, zs = _norm_proj(x, g, prm["w_in"], _ODD_WIDTHS, TM_PROJ)
    h, c_aug, m_fin = _mlstm_prompt(z, zs, prm["bias"], prm["mn"], jnp.zeros((bsz, M_HEADS, M_DQK, M_AUG), F32),
                                    jnp.zeros((bsz, 1, LANE), F32), bsz)
    y = _out_proj(x, [h], (1,), prm["w_out"], TM_OUT)
    return y, c_aug[..., :M_DV], c_aug[..., M_DV], m_fin[:, 0, :M_HEADS]


def _odd_layer_sample(x, g, prm, c0, n0, m0):
    b = x.shape[0]
    z, zs = _norm_proj(x, g, prm["w_in"], _ODD_WIDTHS, b)
    h, cn, nn, mn = _mlstm_decode(z, zs, prm["bias"], prm["mn"], c0, n0, m0)
    return _out_proj(x, [h], (1,), prm["w_out"], b), cn, nn, mn


def kernel(x_prompt, x_sample, cache_cmp_kv, cache_sel_kv, cache_win_kv, state_gla, state_mlstm_c, state_mlstm_n, state_mlstm_m, state_ffn_conv, page_table, norm_mix, norm_ffn, norm_final, even_w_in, even_w_out, gla_w_a2, gla_b_a, gla_norm, nsa_cmp_pe, nsa_cmp_w1, nsa_cmp_w2, nsa_gate_b, odd_w_in, odd_w_out, mlstm_b_i, mlstm_b_f, mlstm_norm, ffn_w_up, ffn_conv_w, ffn_conv_b, ffn_w_down):
    bp, t, d = x_prompt.shape
    bs = x_sample.shape[0]
    assert x_sample.shape[1] == 1
    depth = norm_mix.shape[0]
    f = ffn_conv_w.shape[2]
    xp = x_prompt.reshape(bp * t, d)
    xs = x_sample.reshape(bs, d)
    outs = {k: [] for k in ("cmp_p", "cmp_s", "sel_p", "sel_s", "win_p", "win_s", "gla_p", "gla_s",
                            "mc_p", "mc_s", "mn_p", "mn_s", "mm_p", "mm_s", "cv_p", "cv_s")}
    for l in range(depth):
        if l % 2 == 0:
            e = l // 2
            pe, w1, w2 = _compress_weights(nsa_cmp_pe[e], nsa_cmp_w1[e], nsa_cmp_w2[e])
            pe_t, w1_t, w2_t = _compress_weights_paged(nsa_cmp_pe[e], nsa_cmp_w1[e], nsa_cmp_w2[e])
            w_in_s, w_in_p, w_in_pt = _even_w_in(even_w_in[e])
            prm = dict(pe_t=pe_t, w1_t=w1_t, w2_t=w2_t, w_in=w_in_s, w_in_p=w_in_p, w_in_pt=w_in_pt,
                       w_out=even_w_out[e].astype(BF16),
                       wa=jnp.pad(gla_w_a2[e], ((0, LANE - GLA_RANK), (0, 0))), ba=gla_b_a[e].reshape(1, -1),
                       gn=gla_norm[e].reshape(1, -1), pe=pe, w1=w1, w2=w2, gb_col=_gate_bias_row(nsa_gate_b[e]).reshape(LANE, 1),
                       gb3=nsa_gate_b[e].reshape(NSA_HEADS, NSA_BRANCHES))
            xp, s_, c_, k_, w_ = _even_layer_prompt(xp, bp, norm_mix[l], prm)
            outs["gla_p"].append(s_); outs["cmp_p"].append(c_); outs["sel_p"].append(k_); outs["win_p"].append(w_)
            xs, s_, c_, k_, w_ = _even_layer_sample(xs, norm_mix[l], prm, state_gla[e], cache_cmp_kv[e], cache_sel_kv[e],
                                                    cache_win_kv[e], page_table)
            outs["gla_s"].append(s_); outs["cmp_s"].append(c_); outs["sel_s"].append(k_); outs["win_s"].append(w_)
        else:
            o = l // 2
            bias = jnp.pad(jnp.concatenate([mlstm_b_i[o], mlstm_b_f[o]]), (0, LANE - 2 * M_HEADS)).reshape(1, LANE)
            prm = dict(w_in=_odd_w_in(odd_w_in[o]), w_out=odd_w_out[o].astype(BF16), bias=bias, mn=mlstm_norm[o].reshape(1, -1))
            xp, c_, n_, m_ = _odd_layer_prompt(xp, bp, norm_mix[l], prm)
            outs["mc_p"].append(c_); outs["mn_p"].append(n_); outs["mm_p"].append(m_)
            xs, c_, n_, m_ = _odd_layer_sample(xs, norm_mix[l], prm, state_mlstm_c[o], state_mlstm_n[o], state_mlstm_m[o])
            outs["mc_s"].append(c_); outs["mn_s"].append(n_); outs["mm_s"].append(m_)
        final = l == depth - 1
        wup, wd = ffn_w_up[l].astype(BF16), ffn_w_down[l].astype(BF16)
        xp, cv = _ffn_prompt(xp, bp, norm_ffn[l], wup, ffn_conv_w[l], ffn_conv_b[l], wd,
                             jnp.zeros((bp, CONV_W - 1, f), F32), norm_final, final, TM_FFN)
        outs["cv_p"].append(cv)
        xs, cv = _ffn_decode(xs, norm_ffn[l], wup, ffn_conv_w[l], ffn_conv_b[l], wd, state_ffn_conv[l], norm_final, final)
        outs["cv_s"].append(cv)
    st = lambda k: jnp.stack(outs[k])
    return (xp.reshape(bp, t, d), xs.reshape(bs, 1, d),
            st("cmp_p"), st("cmp_s"), st("sel_p"), st("sel_s"), st("win_p"), st("win_s"), st("gla_p"), st("gla_s"),
            st("mc_p"), st("mc_s"), st("mn_p"), st("mn_s"), st("mm_p"), st("mm_s"), st("cv_p"), st("cv_s"))
```

```python
import functools

import jax
import jax.numpy as jnp
from jax import lax
from jax.experimental import pallas as pl
from jax.experimental.pallas import tpu as pltpu

F32 = jnp.float32
BF16 = jnp.bfloat16

GLA_HEADS, GLA_DK, GLA_DV, GLA_RANK, GLA_TAU, GLA_CHUNK = 4, 64, 128, 16, 16.0, 64
NSA_HEADS, NSA_KV_HEADS, NSA_GROUP, NSA_DH = 8, 2, 4, 64
NSA_BRANCHES, NSA_BLOCK, NSA_TOP_N, NSA_WINDOW, NSA_CMP_HID = 3, 64, 16, 512, 64
M_HEADS, M_DQK, M_DV, M_CHUNK = 4, 128, 256, 64
M_CHUNK_PALLAS = 128
CONV_W = 3
PAGE_SIZE = 128
EPS, TINY = 1e-6, 1e-30
LOG2E = 1.4426950408889634

LANE = 128
SUBLANE = 8
VMEM_BIG = 52 * 1024 * 1024
VMEM_MID = 40 * 1024 * 1024

MASK_BIG = 32768.0

NSA_KV_W = 2 * NSA_KV_HEADS * NSA_DH
GLA_Z_W = 2 * GLA_HEADS * GLA_DK + 2 * GLA_HEADS * GLA_DV
M_Z_W = 2 * M_HEADS * M_DQK + 2 * M_HEADS * M_DV
M_AUG = M_DV + LANE
GATE_OFF = GLA_RANK
TM_PROJ = 256
TM_OUT = 512
TM_FFN = 256
TQ_CMP = 512
TQ_ATTN = 128
TK_SEL = 512
V_AUG = NSA_DH + 16
_BLOCKS_PER_PAGE = PAGE_SIZE // NSA_BLOCK
_PAGE_SHIFT = _BLOCKS_PER_PAGE.bit_length() - 1

_NN = (((1,), (0,)), ((), ()))
_NT = (((1,), (1,)), ((), ()))
_TN = (((0,), (0,)), ((), ()))


def _mm(a, b, dims=_NN):
    return lax.dot_general(a, b, dims, preferred_element_type=F32)


def _mm_bf(a, b, dims=_NN):
    return _mm(a.astype(BF16), b.astype(BF16), dims)


def _split_bf16(x, n):
    parts, r = [], x
    for _ in range(n):
        p = r.astype(BF16)
        parts.append(p)
        r = r - p.astype(F32)
    return parts


def _mm_sel(sel, x, dims=_NN, x_is_rhs=True):
    out = None
    for p in _split_bf16(x, 3):
        t = _mm(sel, p, dims) if x_is_rhs else _mm(p, sel, dims)
        out = t if out is None else out + t
    return out


def _mm_hp(a, b, dims=_NN):
    a1, a2 = _split_bf16(a, 2)
    b1, b2 = _split_bf16(b, 2)
    return _mm(a1, b1, dims) + (_mm(a1, b2, dims) + _mm(a2, b1, dims))


def _gelu(x):
    return 0.5 * x * (1.0 + jnp.tanh(0.7978845608028654 * (x + 0.044715 * (x * x * x))))


def _sigmoid(x):
    return 1.0 / (1.0 + jnp.exp(-x))


def _logsigmoid(x):
    return jnp.minimum(x, 0.0) - jnp.log(1.0 + jnp.exp(-jnp.abs(x)))


def _rms(x, g):
    return x * lax.rsqrt(jnp.mean(x * x, axis=-1, keepdims=True) + EPS) * g


def _iota(shape, dim):
    return lax.broadcasted_iota(jnp.int32, shape, dim)


def _params(n_axes, vmem=VMEM_MID):
    return pltpu.CompilerParams(dimension_semantics=("arbitrary",) * n_axes, vmem_limit_bytes=vmem)


def _resident(shape):
    nd = len(shape)
    return pl.BlockSpec(shape, lambda *_: (0,) * nd, pipeline_mode=pl.Buffered(1))


def _norm_proj_body(x_ref, g_ref, w_ref, *refs, widths, t_widths):
    xb = _rms(x_ref[...], g_ref[...]).astype(BF16)
    o_refs = refs[1:] if t_widths else refs
    off = 0
    for o_ref, n in zip(o_refs[:len(widths)], widths):
        o_ref[...] = _mm(xb, w_ref[:, off:off + n])
        off += n
    off = 0
    for o_ref, n in zip(o_refs[len(widths):], t_widths):
        o_ref[0] = _mm(refs[0][off:off + n, :], xb, _NT)
        off += n


def _norm_proj(x, g, w, widths, tm, wt=None, t_widths=(), bsz=1):
    m, d = x.shape
    tm = min(tm, m)
    nt = m // bsz // tm
    assert m % tm == 0 and sum(widths) == w.shape[1] and (m // bsz) % tm == 0
    t_in = [_resident(wt.shape)] if t_widths else []
    t_args = [wt] if t_widths else []
    return pl.pallas_call(
        functools.partial(_norm_proj_body, widths=tuple(widths), t_widths=tuple(t_widths)),
        grid=(m // tm,),
        in_specs=[pl.BlockSpec((tm, d), lambda i: (i, 0)), _resident((1, d)), _resident(w.shape)] + t_in,
        out_specs=[pl.BlockSpec((tm, n), lambda i: (i, 0)) for n in widths]
        + [pl.BlockSpec((1, n, tm), lambda i: (i // nt, 0, i % nt)) for n in t_widths],
        out_shape=[jax.ShapeDtypeStruct((m, n), F32) for n in widths]
        + [jax.ShapeDtypeStruct((bsz, n, m // bsz), F32) for n in t_widths],
        compiler_params=_params(1),
        name="norm_proj",
    )(x, g.reshape(1, d), w, *t_args)


def _mix_residual(x, h_refs, w_ref, groups, n_t):
    acc = x
    i = off = 0
    for gsz in groups:
        h = h_refs[i][...]
        for j in range(1, gsz):
            h = h + h_refs[i + j][...]
        i += gsz
        n = h.shape[1]
        acc = acc + _mm(h.astype(BF16), w_ref[off:off + n, :])
        off += n
    if n_t:
        ht = h_refs[i][0]
        for j in range(1, n_t):
            ht = ht + h_refs[i + j][0]
        acc = acc + _mm(ht.astype(BF16), w_ref[off:off + ht.shape[0], :], _TN)
    return acc


def _out_proj_body(x_ref, *refs, groups, n_t):
    h_refs, w_ref, o_ref = refs[:-2], refs[-2], refs[-1]
    o_ref[...] = _mix_residual(x_ref[...], h_refs, w_ref, groups, n_t)


def _out_proj(x, hs, groups, w, tm, t_hs=(), bsz=1):
    m, d = x.shape
    tm = min(tm, m)
    nt = m // bsz // tm
    assert m % tm == 0 and (m // bsz) % tm == 0
    return pl.pallas_call(
        functools.partial(_out_proj_body, groups=tuple(groups), n_t=len(t_hs)),
        grid=(m // tm,),
        in_specs=[pl.BlockSpec((tm, d), lambda i: (i, 0))]
        + [pl.BlockSpec((tm, h.shape[1]), lambda i: (i, 0)) for h in hs]
        + [pl.BlockSpec((1, h.shape[1], tm), lambda i: (i // nt, 0, i % nt)) for h in t_hs]
        + [_resident(w.shape)],
        out_specs=pl.BlockSpec((tm, d), lambda i: (i, 0)),
        out_shape=jax.ShapeDtypeStruct((m, d), F32),
        compiler_params=_params(1),
        name="out_proj",
    )(x, *hs, *t_hs, w)


def _ffn_chunks(xn, resid, wup_ref, cw_ref, cb_ref, wd_ref, fc, prev_fn, keep_fn):
    f = cw_ref.shape[1]
    n = f // fc

    def up_proj(c):
        return _mm(xn, wup_ref[:, c * fc:(c + 1) * fc]), _mm(xn, wup_ref[:, f + c * fc:f + (c + 1) * fc])

    acc = resid
    nxt = up_proj(0)
    for c in range(n):
        sl = slice(c * fc, (c + 1) * fc)
        gp, up = nxt
        if c + 1 < n:
            nxt = up_proj(c + 1)
        g2, g1 = prev_fn(gp, sl)
        a = cb_ref[:, sl] + g2 * cw_ref[0:1, sl] + g1 * cw_ref[1:2, sl] + gp * cw_ref[2:3, sl]
        acc = acc + _mm((_gelu(a) * up).astype(BF16), wd_ref[sl, :])
        keep_fn(gp, sl)
    return acc


def _ffn_prompt_body(x_ref, g_ref, wup_ref, cw_ref, cb_ref, wd_ref, st_ref, gf_ref, y_ref, ns_ref, carry, *, final, fc):
    t = pl.program_id(1)
    tm = x_ref.shape[0]

    @pl.when(t == 0)
    def _():
        carry[...] = st_ref[0]

    x = x_ref[...]
    xn = _rms(x, g_ref[...]).astype(BF16)
    row = _iota((tm, fc), 0)

    def prev_fn(gp, sl):
        c0, c1 = carry[0:1, sl], carry[1:2, sl]
        g1 = jnp.where(row == 0, c1, pltpu.roll(gp, 1, 0))
        g2 = jnp.where(row == 0, c0, jnp.where(row == 1, c1, pltpu.roll(gp, 2, 0)))
        return g2, g1

    def keep_fn(gp, sl):
        carry[:, sl] = gp[tm - 2:tm, :]

    acc = _ffn_chunks(xn, x, wup_ref, cw_ref, cb_ref, wd_ref, fc, prev_fn, keep_fn)
    y_ref[...] = _rms(acc, gf_ref[...]) if final else acc

    @pl.when(t == pl.num_programs(1) - 1)
    def _():
        ns_ref[0] = carry[...]


def _ffn_prompt(x, bsz, g, wup, cw, cb, wd, st, gf, final, tm, fc=256):
    m, d = x.shape
    t = m // bsz
    tm = min(tm, t)
    f = cw.shape[1]
    assert t % tm == 0 and f % fc == 0
    nt = t // tm
    rows = lambda n: pl.BlockSpec((tm, n), lambda b, i: (b * nt + i, 0))
    return pl.pallas_call(
        functools.partial(_ffn_prompt_body, final=final, fc=fc),
        grid=(bsz, nt),
        in_specs=[rows(d), _resident((1, d)), _resident(wup.shape), _resident(cw.shape), _resident((1, f)),
                  _resident(wd.shape), pl.BlockSpec((1, CONV_W - 1, f), lambda b, i: (b, 0, 0)), _resident((1, d))],
        out_specs=[rows(d), pl.BlockSpec((1, CONV_W - 1, f), lambda b, i: (b, 0, 0))],
        out_shape=[jax.ShapeDtypeStruct((m, d), F32), jax.ShapeDtypeStruct((bsz, CONV_W - 1, f), F32)],
        scratch_shapes=[pltpu.VMEM((CONV_W - 1, f), F32)],
        compiler_params=_params(2, VMEM_BIG),
        name="ffn_prompt",
    )(x, g.reshape(1, d), wup, cw, cb.reshape(1, f), wd, st, gf.reshape(1, d))


def _ffn_decode_body(x_ref, g_ref, wup_ref, cw_ref, cb_ref, wd_ref, s0_ref, s1_ref, gf_ref, y_ref, gp_ref, *, final, fc):
    x = x_ref[...]
    xn = _rms(x, g_ref[...]).astype(BF16)

    def prev_fn(gp, sl):
        return s0_ref[:, sl], s1_ref[:, sl]

    def keep_fn(gp, sl):
        gp_ref[:, sl] = gp

    acc = _ffn_chunks(xn, x, wup_ref, cw_ref, cb_ref, wd_ref, fc, prev_fn, keep_fn)
    y_ref[...] = _rms(acc, gf_ref[...]) if final else acc


def _ffn_decode(x, g, wup, cw, cb, wd, st, gf, final, fc=256):
    m, d = x.shape
    f = cw.shape[1]
    y, gp = pl.pallas_call(
        functools.partial(_ffn_decode_body, final=final, fc=fc),
        grid=(1,),
        in_specs=[_resident((m, d)), _resident((1, d)), _resident(wup.shape), _resident(cw.shape), _resident((1, f)),
                  _resident(wd.shape), _resident((m, f)), _resident((m, f)), _resident((1, d))],
        out_specs=[pl.BlockSpec((m, d), lambda i: (0, 0)), pl.BlockSpec((m, f), lambda i: (0, 0))],
        out_shape=[jax.ShapeDtypeStruct((m, d), F32), jax.ShapeDtypeStruct((m, f), F32)],
        compiler_params=_params(1, VMEM_BIG),
        name="ffn_decode",
    )(x, g.reshape(1, d), wup, cw, cb.reshape(1, f), wd, st[:, 0], st[:, 1], gf.reshape(1, d))
    return y, jnp.stack([st[:, 1], gp], axis=1)


def _gla_finish(o, gg, gn):
    return _rms(o, gn) * (gg * _sigmoid(gg))


def _gla_body(z_ref, zs_ref, wa_ref, ba_ref, gn_ref, s0_ref, o_ref, sfin_ref, s_scr):
    t = pl.program_id(1)

    @pl.when(t == 0)
    def _():
        s_scr[...] = s0_ref[...]

    _round_robin([_gla_chunk(z_ref.at[bi], zs_ref.at[bi], wa_ref, ba_ref, gn_ref, o_ref.at[bi], s_scr.at[bi])
                  for bi in range(z_ref.shape[0])])

    @pl.when(t == pl.num_programs(1) - 1)
    def _():
        sfin_ref[...] = s_scr[...]


def _round_robin(chains):
    chains = list(chains)
    while chains:
        chains = [c for c in chains if next(c, _DONE) is not _DONE]


_DONE = object()


def _gla_chunk(z_ref, zs_ref, wa_ref, ba_ref, gn_ref, o_ref, s_scr):
    c = z_ref.shape[0]
    hk = GLA_HEADS * GLA_DK
    la = _logsigmoid(_mm_hp(zs_ref[...], wa_ref[...]) + ba_ref[...]) * (1.0 / GLA_TAU)
    yield
    r, cidx = _iota((c, c), 0), _iota((c, c), 1)
    tri = r >= cidx
    cum = _mm_sel(jnp.where(tri, 1.0, 0.0).astype(BF16), la)
    yield
    last = cum[c - 1:c, :]
    eq, ek, ekl, el = jnp.exp(cum), jnp.exp(-cum), jnp.exp(last - cum), jnp.exp(last)
    gn = gn_ref[...]
    for h in range(GLA_HEADS):
        ks = slice(h * GLA_DK, (h + 1) * GLA_DK)
        vs = slice(2 * hk + h * GLA_DV, 2 * hk + (h + 1) * GLA_DV)
        gs = slice(2 * hk + GLA_HEADS * GLA_DV + h * GLA_DV, 2 * hk + GLA_HEADS * GLA_DV + (h + 1) * GLA_DV)
        q = z_ref[:, ks] * (GLA_DK ** -0.5)
        k = z_ref[:, hk + h * GLA_DK:hk + (h + 1) * GLA_DK]
        v = z_ref[:, vs]
        qt = (q * eq[:, ks]).astype(BF16)
        s_old = s_scr[h]
        att = jnp.where(tri, _mm(qt, (k * ek[:, ks]).astype(BF16), _NT), 0.0)
        vb = v.astype(BF16)
        o_inter = _mm(qt, s_old.astype(BF16))
        kv_new = _mm((k * ekl[:, ks]).astype(BF16), vb, _TN)
        yield
        o = o_inter + _mm(att.astype(BF16), vb)
        ecol = jnp.sum(jnp.where(r == cidx, jnp.broadcast_to(el[:, ks], (c, c)), 0.0), axis=1, keepdims=True)
        s_scr[h] = ecol * s_old + kv_new
        yield
        o_ref[:, h * GLA_DV:(h + 1) * GLA_DV] = _gla_finish(o, z_ref[:, gs], gn)


def _seq_group(bsz):
    return next(n for n in (4, 2, 1) if bsz % n == 0)


def _gla_prompt(z, zs, wa, ba, gn, s0, bsz):
    m = z.shape[0]
    t = m // bsz
    c = GLA_CHUNK
    assert t % c == 0 and GLA_DK == c
    nc = t // c
    hk = GLA_HEADS * GLA_DK
    nb = _seq_group(bsz)
    st = (nb, GLA_HEADS, GLA_DK, GLA_DV)
    tok = lambda n: pl.BlockSpec((nb, c, n), lambda b, i: (b, i, 0))
    o, s_fin = pl.pallas_call(
        _gla_body,
        grid=(bsz // nb, nc),
        in_specs=[tok(GLA_Z_W), tok(LANE), _resident((LANE, hk)), _resident((1, hk)), _resident((1, GLA_DV)),
                  pl.BlockSpec(st, lambda b, i: (b, 0, 0, 0))],
        out_specs=[tok(GLA_HEADS * GLA_DV), pl.BlockSpec(st, lambda b, i: (b, 0, 0, 0))],
        out_shape=[jax.ShapeDtypeStruct((bsz, t, GLA_HEADS * GLA_DV), F32),
                   jax.ShapeDtypeStruct((bsz, GLA_HEADS, GLA_DK, GLA_DV), F32)],
        scratch_shapes=[pltpu.VMEM(st, F32)],
        compiler_params=_params(2),
        name="gla_prompt",
    )(z.reshape(bsz, t, -1), zs.reshape(bsz, t, -1), wa, ba, gn, s0)
    return o.reshape(m, -1), s_fin


def _gla_decode_body(z_ref, zs_ref, wa_ref, ba_ref, gn_ref, s_ref, o_ref, sn_ref):
    hk = GLA_HEADS * GLA_DK
    rows = SUBLANE
    z = jnp.broadcast_to(z_ref[0], (rows, GLA_Z_W))
    ga = jnp.broadcast_to(zs_ref[0], (rows, LANE))
    la = _logsigmoid(_mm_hp(ga, wa_ref[...]) + ba_ref[...]) * (1.0 / GLA_TAU)
    ea = jnp.exp(la)
    r, cidx = _iota((GLA_DK, GLA_DK), 0), _iota((GLA_DK, GLA_DK), 1)
    row0 = _iota((rows, GLA_DK), 0) == 0
    gn = gn_ref[...]
    for h in range(GLA_HEADS):
        ks = slice(h * GLA_DK, (h + 1) * GLA_DK)
        q = z[:, ks] * (GLA_DK ** -0.5)
        k = z[:, hk + h * GLA_DK:hk + (h + 1) * GLA_DK]
        v = z[:, 2 * hk + h * GLA_DV:2 * hk + (h + 1) * GLA_DV]
        gg = z[:, 2 * hk + GLA_HEADS * GLA_DV + h * GLA_DV:2 * hk + GLA_HEADS * GLA_DV + (h + 1) * GLA_DV]
        s_old = s_ref[0, h]
        o = _mm_hp(q * ea[:, ks], s_old) + jnp.sum(q * k, axis=1, keepdims=True) * v
        ecol = jnp.sum(jnp.where(r == cidx, jnp.broadcast_to(ea[0:1, ks], (GLA_DK, GLA_DK)), 0.0), axis=1, keepdims=True)
        sn_ref[0, h] = ecol * s_old + _mm_hp(jnp.where(row0, k, 0.0), v, _TN)
        o_ref[0, :, h * GLA_DV:(h + 1) * GLA_DV] = _gla_finish(o, gg, gn)[0:1]


def _gla_decode(z, zs, wa, ba, gn, s0):
    b = z.shape[0]
    hk = GLA_HEADS * GLA_DK
    st = (1, GLA_HEADS, GLA_DK, GLA_DV)
    o, sn = pl.pallas_call(
        _gla_decode_body,
        grid=(b,),
        in_specs=[pl.BlockSpec((1, 1, GLA_Z_W), lambda i: (i, 0, 0)), pl.BlockSpec((1, 1, LANE), lambda i: (i, 0, 0)),
                  _resident((LANE, hk)), _resident((1, hk)), _resident((1, GLA_DV)),
                  pl.BlockSpec(st, lambda i: (i, 0, 0, 0))],
        out_specs=[pl.BlockSpec((1, 1, GLA_HEADS * GLA_DV), lambda i: (i, 0, 0)), pl.BlockSpec(st, lambda i: (i, 0, 0, 0))],
        out_shape=[jax.ShapeDtypeStruct((b, 1, GLA_HEADS * GLA_DV), F32), jax.ShapeDtypeStruct((b,) + st[1:], F32)],
        compiler_params=_params(1),
        name="gla_decode",
    )(z.reshape(b, 1, -1), zs.reshape(b, 1, -1), wa, ba, gn, s0)
    return o.reshape(b, -1), sn


def _mlstm_finish(hh, og, mn):
    return _rms(hh, mn) * _sigmoid(og)


def _mlstm_body(z_ref, zs_ref, bias_ref, mn_ref, c0_ref, m0_ref, h_ref, cfin_ref, mfin_ref, c_scr, m_scr):
    t = pl.program_id(1)

    @pl.when(t == 0)
    def _():
        c_scr[...] = c0_ref[...]
        m_scr[...] = m0_ref[...]

    _round_robin([_mlstm_chunk(z_ref.at[bi], zs_ref.at[bi], bias_ref, mn_ref, h_ref.at[bi], c_scr.at[bi], m_scr.at[bi])
                  for bi in range(z_ref.shape[0])])

    @pl.when(t == pl.num_programs(1) - 1)
    def _():
        cfin_ref[...] = c_scr[...]
        mfin_ref[...] = m_scr[...]


def _mlstm_chunk(z_ref, zs_ref, bias_ref, mn_ref, h_ref, c_scr, m_scr):
    L = z_ref.shape[0]
    hq = M_HEADS * M_DQK
    g = zs_ref[...] + bias_ref[...]
    lane = _iota((L, LANE), 1)
    gx = jnp.where((lane >= M_HEADS) & (lane < 2 * M_HEADS), _logsigmoid(g), g)
    r, cidx = _iota((L, L), 0), _iota((L, L), 1)
    tri = r >= cidx
    cum_c = _mm_sel(jnp.where(tri, 1.0, 0.0).astype(BF16), gx)
    rows = _mm_sel(jnp.where(r == cidx, 1.0, 0.0).astype(BF16), gx, _TN, x_is_rhs=False)
    cum_r = _mm_sel(jnp.where(r <= cidx, 1.0, 0.0).astype(BF16), gx, _TN, x_is_rhs=False)
    yield
    ones_col = jnp.where(_iota((L, LANE), 1) == 0, 1.0, 0.0).astype(BF16)
    m_all = m_scr[...]
    m_new_all = m_all
    for h in range(M_HEADS):
        q = z_ref[:, h * M_DQK:(h + 1) * M_DQK].astype(BF16)
        k = z_ref[:, hq + h * M_DQK:hq + (h + 1) * M_DQK] * (M_DQK ** -0.5)
        v = z_ref[:, 2 * hq + h * M_DV:2 * hq + (h + 1) * M_DV]
        og = z_ref[:, 2 * hq + M_HEADS * M_DV + h * M_DV:2 * hq + M_HEADS * M_DV + (h + 1) * M_DV]
        va = jnp.concatenate([v.astype(BF16), ones_col], axis=1)
        ic_r, ic_c = rows[h:h + 1, :], gx[:, h:h + 1]
        cr, cc = cum_r[M_HEADS + h:M_HEADS + h + 1, :], cum_c[:, M_HEADS + h:M_HEADS + h + 1]
        m_old = m_all[0:1, h:h + 1]
        dlog = jnp.where(tri, cc - cr + ic_r, -jnp.inf)
        inter = cc + m_old
        mi = jnp.maximum(inter, jnp.max(dlog, axis=1, keepdims=True))
        w = jnp.exp(dlog - mi)
        wi = jnp.exp(inter - mi)
        qk = _mm(q, k.astype(BF16), _NT)
        c_old = c_scr[h]
        qc = _mm(q, c_old.astype(BF16))
        yield
        s = qk * w
        num = wi * qc + _mm(s.astype(BF16), va)
        yield
        qn = num[:, M_DV:M_DV + 1]
        hh = num[:, :M_DV] / jnp.maximum(jnp.abs(qn), jnp.exp(-mi))
        last = cc[L - 1:L, :]
        gl = last - cc + ic_c
        m_new = jnp.maximum(last + m_old, jnp.max(gl, axis=0, keepdims=True))
        wj = jnp.exp(gl - m_new)
        keep = jnp.exp(last + m_old - m_new)
        c_scr[h] = keep * c_old + _mm((wj * k).astype(BF16), va, _TN)
        yield
        m_new_all = jnp.where(_iota((1, LANE), 1) == h, m_new, m_new_all)
        h_ref[:, h * M_DV:(h + 1) * M_DV] = _mlstm_finish(hh, og, mn_ref[:, h * M_DV:(h + 1) * M_DV])
    m_scr[...] = m_new_all


def _mlstm_prompt(z, zs, bias, mn, c0, m0, bsz):
    m = z.shape[0]
    t = m // bsz
    L = next(n for n in (M_CHUNK_PALLAS, M_CHUNK) if t % n == 0)
    nc = t // L
    nb = _seq_group(bsz)
    st, ms = (nb, M_HEADS, M_DQK, M_AUG), (nb, 1, LANE)
    tok = lambda n: pl.BlockSpec((nb, L, n), lambda b, i: (b, i, 0))
    h, c_fin, m_fin = pl.pallas_call(
        _mlstm_body,
        grid=(bsz // nb, nc),
        in_specs=[tok(M_Z_W), tok(LANE), _resident((1, LANE)), _resident((1, M_HEADS * M_DV)),
                  pl.BlockSpec(st, lambda b, i: (b, 0, 0, 0)), pl.BlockSpec(ms, lambda b, i: (b, 0, 0))],
        out_specs=[tok(M_HEADS * M_DV), pl.BlockSpec(st, lambda b, i: (b, 0, 0, 0)), pl.BlockSpec(ms, lambda b, i: (b, 0, 0))],
        out_shape=[jax.ShapeDtypeStruct((bsz, t, M_HEADS * M_DV), F32), jax.ShapeDtypeStruct((bsz,) + st[1:], F32),
                   jax.ShapeDtypeStruct((bsz, 1, LANE), F32)],
        scratch_shapes=[pltpu.VMEM(st, F32), pltpu.VMEM(ms, F32)],
        compiler_params=_params(2),
        name="mlstm_prompt",
    )(z.reshape(bsz, t, -1), zs.reshape(bsz, t, -1), bias, mn, c0, m0)
    return h.reshape(m, -1), c_fin, m_fin


def _mlstm_decode_body(z_ref, zs_ref, bias_ref, mn_ref, c_ref, n_ref, m_ref, h_ref, cn_ref, nn_ref, mo_ref):
    hq = M_HEADS * M_DQK
    rows = SUBLANE
    z = jnp.broadcast_to(z_ref[0], (rows, M_Z_W))
    g = zs_ref[0] + bias_ref[...]
    m_all = m_ref[0]
    m_new_all = m_all
    row0 = _iota((rows, M_DQK), 0) == 0
    for h in range(M_HEADS):
        q = z[:, h * M_DQK:(h + 1) * M_DQK]
        k = z[:, hq + h * M_DQK:hq + (h + 1) * M_DQK] * (M_DQK ** -0.5)
        v = z[:, 2 * hq + h * M_DV:2 * hq + (h + 1) * M_DV]
        og = z[:, 2 * hq + M_HEADS * M_DV + h * M_DV:2 * hq + M_HEADS * M_DV + (h + 1) * M_DV]
        ic = g[:, h:h + 1]
        fl = _logsigmoid(g[:, M_HEADS + h:M_HEADS + h + 1])
        m_old = m_all[:, h:h + 1]
        c_old, n_old = c_ref[0, h], n_ref[0, h:h + 1, :]
        mi = jnp.maximum(fl + m_old, ic)
        w = jnp.exp(ic - mi)
        wi = jnp.exp(fl + m_old - mi)
        s = jnp.sum(q * k, axis=1, keepdims=True) * w
        num = wi * _mm_hp(q, c_old) + s * v
        qn = wi * jnp.sum(q * n_old, axis=1, keepdims=True) + s
        hh = num / jnp.maximum(jnp.abs(qn), jnp.exp(-mi))
        cn_ref[0, h] = wi * c_old + w * _mm_hp(jnp.where(row0, k, 0.0), v, _TN)
        nn_ref[0, h:h + 1, :] = wi * n_old + w * k[0:1]
        m_new_all = jnp.where(_iota((1, LANE), 1) == h, mi, m_new_all)
        h_ref[0, :, h * M_DV:(h + 1) * M_DV] = _mlstm_finish(hh, og, mn_ref[:, h * M_DV:(h + 1) * M_DV])[0:1]
    mo_ref[0] = m_new_all


def _mlstm_decode(z, zs, bias, mn, c0, n0, m0):
    b = z.shape[0]
    cs, ns = (1, M_HEADS, M_DQK, M_DV), (1, M_HEADS, M_DQK)
    m0p = jnp.pad(m0, ((0, 0), (0, LANE - M_HEADS))).reshape(b, 1, LANE)
    row3 = lambda n: pl.BlockSpec((1, 1, n), lambda i: (i, 0, 0))
    h, cn, nn, mo = pl.pallas_call(
        _mlstm_decode_body,
        grid=(b,),
        in_specs=[row3(M_Z_W), row3(LANE), _resident((1, LANE)), _resident((1, M_HEADS * M_DV)),
                  pl.BlockSpec(cs, lambda i: (i, 0, 0, 0)), pl.BlockSpec(ns, lambda i: (i, 0, 0)), row3(LANE)],
        out_specs=[row3(M_HEADS * M_DV), pl.BlockSpec(cs, lambda i: (i, 0, 0, 0)), pl.BlockSpec(ns, lambda i: (i, 0, 0)), row3(LANE)],
        out_shape=[jax.ShapeDtypeStruct((b, 1, M_HEADS * M_DV), F32), jax.ShapeDtypeStruct((b,) + cs[1:], F32),
                   jax.ShapeDtypeStruct((b,) + ns[1:], F32), jax.ShapeDtypeStruct((b, 1, LANE), F32)],
        compiler_params=_params(1),
        name="mlstm_decode",
    )(z.reshape(b, 1, -1), zs.reshape(b, 1, -1), bias, mn, c0, n0, m0p)
    return h.reshape(b, -1), cn, nn, mo[:, 0, :M_HEADS]


def _compress_dense_body(x_ref, pe_ref, w1_ref, w2_ref, o_ref, acc):
    kk = pl.program_id(1)

    @pl.when(kk == 0)
    def _():
        acc[...] = jnp.zeros_like(acc)

    acc[...] += _mm((x_ref[...] + pe_ref[...]).astype(BF16), w1_ref[...])

    @pl.when(kk == pl.num_programs(1) - 1)
    def _():
        o_ref[...] = _mm(_gelu(acc[...]).astype(BF16), w2_ref[...])


def _compress_dense(x, pe, w1, w2, tk=2048):
    r, kdim = x.shape
    tr = r if r <= 512 else 512
    assert r % tr == 0 and kdim % tk == 0
    return pl.pallas_call(
        _compress_dense_body,
        grid=(r // tr, kdim // tk),
        in_specs=[pl.BlockSpec((tr, tk), lambda i, k: (i, k)), pl.BlockSpec((1, tk), lambda i, k: (0, k)),
                  pl.BlockSpec((tk, NSA_KV_W), lambda i, k: (k, 0)), _resident(w2.shape)],
        out_specs=pl.BlockSpec((tr, NSA_KV_W), lambda i, k: (i, 0)),
        out_shape=jax.ShapeDtypeStruct((r, NSA_KV_W), F32),
        scratch_shapes=[pltpu.VMEM((tr, NSA_KV_W), F32)],
        compiler_params=_params(2),
        name="nsa_compress",
    )(x, pe, w1, w2)


_MAX_PAGES_PER_STEP = 32
_D_PER_STEP = 8


def _compress_paged_body(pt_ref, *refs, n_pages, pps):
    page_refs = refs[:pps]
    pe_ref, w1_ref, w2_ref, o_ref, xs = refs[pps:]
    g = pl.program_id(1)
    for p, pr in enumerate(page_refs):
        r0 = pl.multiple_of((g * pps + p) * NSA_KV_W, NSA_KV_W)
        xs[pl.ds(r0, NSA_KV_W), :] = pr[0]

    @pl.when(g == pl.num_programs(1) - 1)
    def _():
        kvd = NSA_KV_HEADS * NSA_DH
        for c in range(2):
            def step(j, acc):
                parts = []
                for dd in range(_D_PER_STEP):
                    d = j * _D_PER_STEP + dd
                    pe_row = pe_ref[pl.ds(c * NSA_DH + d, 1), :]
                    rows = [xs[pl.ds(c * kvd + k * NSA_DH + d, n_pages, stride=NSA_KV_W), :] + pe_row
                            for k in range(NSA_KV_HEADS)]
                    parts.append(jnp.concatenate(rows, axis=0).astype(BF16))
                return acc + _mm(jnp.concatenate(parts, axis=1), w1_ref[c, j])
            hid = lax.fori_loop(0, NSA_DH // _D_PER_STEP, step, jnp.zeros((NSA_KV_HEADS * n_pages, PAGE_SIZE), F32))
            out = _mm(_gelu(hid).astype(BF16), w2_ref[c])
            o_ref[c * NSA_KV_HEADS * n_pages:(c + 1) * NSA_KV_HEADS * n_pages, :] = out


def _compress_paged(pool_t, page_table, pe_t, w1_t, w2_t):
    b, n_pages = page_table.shape
    pps = next(n for n in range(min(_MAX_PAGES_PER_STEP, n_pages), 0, -1) if n_pages % n == 0)
    assert _BLOCKS_PER_PAGE == 2
    rows_out = 2 * NSA_KV_HEADS * n_pages

    def page_spec(p):
        return pl.BlockSpec((1, NSA_KV_W, PAGE_SIZE), lambda i, g, pt: (pt[i, g * pps + p], 0, 0))

    def fixed(a):
        nd = a.ndim
        return pl.BlockSpec(a.shape, lambda i, g, pt: (0,) * nd, pipeline_mode=pl.Buffered(1))

    grid_spec = pltpu.PrefetchScalarGridSpec(
        num_scalar_prefetch=1,
        grid=(b, n_pages // pps),
        in_specs=[page_spec(p) for p in range(pps)] + [fixed(pe_t), fixed(w1_t), fixed(w2_t)],
        out_specs=pl.BlockSpec((rows_out, PAGE_SIZE), lambda i, g, pt: (i, 0)),
        scratch_shapes=[pltpu.VMEM((n_pages * NSA_KV_W, PAGE_SIZE), F32)],
    )
    return pl.pallas_call(
        functools.partial(_compress_paged_body, n_pages=n_pages, pps=pps),
        grid_spec=grid_spec,
        out_shape=jax.ShapeDtypeStruct((b * rows_out, PAGE_SIZE), F32),
        compiler_params=_params(2, VMEM_BIG),
        name="nsa_compress_paged",
    )(page_table, *([pool_t] * pps), pe_t, w1_t, w2_t)


def _gate_rows(zs_ref, gb_ref):
    return _sigmoid(zs_ref[0] + gb_ref[...])


def _gate_row(gates, head, branch):
    r = GATE_OFF + head * NSA_BRANCHES + branch
    return gates[r:r + 1, :]


def _cmp_sel_body(q_ref, kvc_ref, zs_ref, gb_ref, ocmp_ref, nm_ref, sc_scr, thr_scr):
    qi = pl.program_id(1)
    tq = q_ref.shape[2]
    nbp = kvc_ref.shape[0]
    t0 = qi * tq
    tpos = t0 + _iota((nbp, tq), 1)
    blk = _iota((nbp, tq), 0)
    cur = jnp.right_shift(tpos, NSA_BLOCK.bit_length() - 1)
    vis = blk * NSA_BLOCK + (NSA_BLOCK - 1) <= tpos
    forced = (blk == 0) | (blk == cur) | (blk == cur - 1)
    allowed = blk <= cur
    gates = _gate_rows(zs_ref, gb_ref)
    for kh in range(NSA_KV_HEADS):
        kc = kvc_ref[:, kh * NSA_DH:(kh + 1) * NSA_DH].astype(BF16)
        vc = kvc_ref[:, NSA_KV_HEADS * NSA_DH + kh * NSA_DH:NSA_KV_HEADS * NSA_DH + (kh + 1) * NSA_DH].astype(BF16)
        probs = [None] * NSA_GROUP

        def head(gi):
            h = kh * NSA_GROUP + gi
            q = (q_ref[0, h * NSA_DH:(h + 1) * NSA_DH, :] * (NSA_DH ** -0.5)).astype(BF16)
            s = _mm(kc, q)
            yield
            s = jnp.where(vis, s, -jnp.inf)
            m = jnp.max(s, axis=0, keepdims=True)
            m = jnp.where(m == -jnp.inf, 0.0, m)
            e = jnp.exp(s - m)
            p = e / jnp.maximum(jnp.sum(e, axis=0, keepdims=True), TINY)
            probs[gi] = p
            o = _mm(vc, p.astype(BF16), _TN)
            yield
            ocmp_ref[0, h * NSA_DH:(h + 1) * NSA_DH, :] = o * _gate_row(gates, h, 0)

        _round_robin([head(gi) for gi in range(NSA_GROUP)])
        imp = probs[0]
        for p in probs[1:]:
            imp = imp + p
        score = jnp.where(forced, jnp.inf, jnp.where(allowed, imp, -jnp.inf))
        key = lax.bitcast_convert_type(score, jnp.int32)
        for lc in range(tq // LANE):
            ls = slice(lc * LANE, (lc + 1) * LANE)
            n_live = (t0 + (lc + 1) * LANE - 1) // NSA_BLOCK + 1

            def rank(rows):
                sc_scr[0:rows, :] = key[0:rows, ls]
                thr_scr[0:rows, :] = key[0:rows, ls]

                def rank_step(i, cnt):
                    for j in (2 * i, 2 * i + 1):
                        row = sc_scr[pl.ds(j, 1), :]
                        thr_scr[pl.ds(j, 1), :] = row + 1
                        cnt = cnt + jnp.where(row >= thr_scr[0:rows, :], 1, 0)
                    return cnt

                cnt = lax.fori_loop(0, n_live // 2, rank_step, jnp.zeros((rows, LANE), jnp.int32))
                nm_ref[0, kh * nbp:kh * nbp + rows, ls] = jnp.where(
                    allowed[0:rows, ls], jnp.where(cnt < NSA_TOP_N, 0.0, -MASK_BIG), -MASK_BIG).astype(BF16)

            half = nbp // 2

            @pl.when(n_live <= half)
            def _():
                rank(half)
                nm_ref[0, kh * nbp + half:(kh + 1) * nbp, ls] = jnp.full((nbp - half, LANE), -MASK_BIG, BF16)

            @pl.when(n_live > half)
            def _():
                rank(nbp)


def _cmp_sel(q_t, kvc, zs_t, gb_col, nbp, tq):
    bsz, hd, t = q_t.shape
    tq = min(tq, t)
    nt = t // tq
    tok = lambda n: pl.BlockSpec((1, n, tq), lambda b, i: (b, 0, i))
    return pl.pallas_call(
        _cmp_sel_body,
        grid=(bsz, nt),
        in_specs=[tok(hd), pl.BlockSpec((nbp, NSA_KV_W), lambda b, i: (b, 0)), tok(LANE), _resident((LANE, 1))],
        out_specs=[tok(hd), tok(NSA_KV_HEADS * nbp)],
        out_shape=[jax.ShapeDtypeStruct((bsz, hd, t), F32), jax.ShapeDtypeStruct((bsz, NSA_KV_HEADS * nbp, t), BF16)],
        scratch_shapes=[pltpu.VMEM((nbp, LANE), jnp.int32), pltpu.VMEM((nbp, LANE), jnp.int32)],
        compiler_params=_params(2),
        name="nsa_cmp_select",
    )(q_t, kvc, zs_t, gb_col)


def _sel_attn_body(q_ref, nm_ref, kv_ref, oh_ref, v_ref, zs_ref, gb_ref, o_ref, *, tk):
    qi = pl.program_id(1)
    tq = q_ref.shape[2]
    nbp = nm_ref.shape[1] // NSA_KV_HEADS
    cols = NSA_GROUP * tq
    t0 = qi * tq
    n_before = t0 // tk
    gates = _gate_rows(zs_ref, gb_ref)
    qas = []
    for kh in range(NSA_KV_HEADS):
        nm = nm_ref[0, kh * nbp:(kh + 1) * nbp, :]
        qas.append(jnp.concatenate(
            [jnp.concatenate([nm, (q_ref[0, h * NSA_DH:(h + 1) * NSA_DH, :] * (NSA_DH ** -0.5 * LOG2E)).astype(BF16)], axis=0)
             for h in range(kh * NSA_GROUP, (kh + 1) * NSA_GROUP)], axis=1))

    def tile(kh, kt, carry, causal, out):
        m_old, acc = carry
        k0 = pl.multiple_of(kt * tk, tk)
        ka = jnp.concatenate([oh_ref[pl.ds(k0, tk), :],
                              kv_ref[pl.ds(k0, tk), kh * NSA_DH:(kh + 1) * NSA_DH].astype(BF16)], axis=1)
        s = _mm(ka, qas[kh])
        yield
        if causal:
            kpos = k0 + _iota((tk, cols), 0)
            tpos = t0 + jnp.bitwise_and(_iota((tk, cols), 1), tq - 1)
            s = jnp.where(kpos <= tpos, s, -jnp.inf)
        m_new = jnp.maximum(m_old, jnp.max(s, axis=0, keepdims=True))
        alpha = jnp.exp2(m_old - m_new)
        p = jnp.exp2(s - m_new)
        yield
        out[kh] = (m_new, alpha * acc + _mm(v_ref[0, kh, :, pl.ds(k0, tk)], p.astype(BF16)))

    def tiles(kt, carries, causal):
        out = [None] * NSA_KV_HEADS
        _round_robin([tile(kh, kt, carries[kh], causal, out) for kh in range(NSA_KV_HEADS)])
        return tuple(out)

    init = (jnp.full((1, cols), -jnp.inf, F32), jnp.zeros((V_AUG, cols), F32))
    carries = lax.fori_loop(0, n_before, lambda kt, c: tiles(kt, c, False), (init,) * NSA_KV_HEADS)
    carries = tiles(n_before, carries, True)
    for kh in range(NSA_KV_HEADS):
        acc = carries[kh][1]
        o = acc[:NSA_DH] / acc[NSA_DH:NSA_DH + 1]
        for gi in range(NSA_GROUP):
            h = kh * NSA_GROUP + gi
            o_ref[0, h * NSA_DH:(h + 1) * NSA_DH, :] = o[:, gi * tq:(gi + 1) * tq] * _gate_row(gates, h, 1)


def _sel_attn(q_t, nm_t, kv, onehot, vs_t, zs_t, gb_col, tq, tk):
    bsz, hd, t = q_t.shape
    tq, tk = min(tq, t), min(tk, t)
    assert t % tq == 0 and t % tk == 0 and tk % tq == 0 and tq & (tq - 1) == 0
    nt = t // tq
    tok = lambda n: pl.BlockSpec((1, n, tq), lambda b, i: (b, 0, i))
    return pl.pallas_call(
        functools.partial(_sel_attn_body, tk=tk),
        grid=(bsz, nt),
        in_specs=[tok(hd), tok(nm_t.shape[1]), pl.BlockSpec((t, NSA_KV_W), lambda b, i: (b, 0)), _resident(onehot.shape),
                  pl.BlockSpec((1,) + vs_t.shape[1:], lambda b, i: (b, 0, 0, 0)), tok(LANE), _resident((LANE, 1))],
        out_specs=tok(hd),
        out_shape=jax.ShapeDtypeStruct((bsz, hd, t), F32),
        compiler_params=_params(2, VMEM_BIG),
        name="nsa_selected",
    )(q_t, nm_t, kv, onehot, vs_t, zs_t, gb_col)


def _win_attn_body(q_ref, k_ref, v_ref, zs_ref, gb_ref, o_ref, bias_scr, *, span):
    qi = pl.program_id(1)
    tq = q_ref.shape[2]
    cols = NSA_GROUP * tq
    t0 = qi * tq
    start = pl.multiple_of(jnp.maximum(t0 + tq - span, 0), tq)
    gates = _gate_rows(zs_ref, gb_ref)

    @pl.when(t0 + tq - span <= 0)
    def _():
        kpos = start + _iota((span, cols), 0)
        tpos = t0 + jnp.bitwise_and(_iota((span, cols), 1), tq - 1)
        bias_scr[...] = jnp.where((kpos <= tpos) & (tpos - kpos < NSA_WINDOW), 0.0, -jnp.inf)

    for kh in range(NSA_KV_HEADS):
        qa = jnp.concatenate([(q_ref[0, h * NSA_DH:(h + 1) * NSA_DH, :] * (NSA_DH ** -0.5 * LOG2E)).astype(BF16)
                              for h in range(kh * NSA_GROUP, (kh + 1) * NSA_GROUP)], axis=1)
        kw = k_ref[pl.ds(start, span), kh * NSA_DH:(kh + 1) * NSA_DH].astype(BF16)
        s = jnp.where(bias_scr[...] == 0.0, _mm(kw, qa), -jnp.inf)
        e = jnp.exp2(s - jnp.max(s, axis=0, keepdims=True))
        oa = _mm(v_ref[0, kh, :, pl.ds(start, span)], e.astype(BF16))
        o = oa[:NSA_DH] / oa[NSA_DH:NSA_DH + 1]
        for gi in range(NSA_GROUP):
            h = kh * NSA_GROUP + gi
            o_ref[0, h * NSA_DH:(h + 1) * NSA_DH, :] = o[:, gi * tq:(gi + 1) * tq] * _gate_row(gates, h, 2)


def _win_attn(q_t, kv, vw_t, zs_t, gb_col, tq):
    bsz, hd, t = q_t.shape
    tq = min(tq, t)
    span = min(NSA_WINDOW + tq, t)
    assert t % tq == 0 and tq & (tq - 1) == 0
    nt = t // tq
    tok = lambda n: pl.BlockSpec((1, n, tq), lambda b, i: (b, 0, i))
    whole = lambda a: pl.BlockSpec((1,) + a.shape[1:], lambda b, i: (b, 0, 0, 0))
    return pl.pallas_call(
        functools.partial(_win_attn_body, span=span),
        grid=(bsz, nt),
        in_specs=[tok(hd), pl.BlockSpec((t, NSA_KV_W), lambda b, i: (b, 0)), whole(vw_t), tok(LANE), _resident((LANE, 1))],
        out_specs=tok(hd),
        out_shape=jax.ShapeDtypeStruct((bsz, hd, t), F32),
        scratch_shapes=[pltpu.VMEM((span, NSA_GROUP * tq), F32)],
        compiler_params=_params(2),
        name="nsa_window",
    )(q_t, kv, vw_t, zs_t, gb_col)


def _decode_forced(n_past_blk):
    cur = n_past_blk
    return sorted({0, cur - 1, cur} - {-1})


def _cmp_decode_body(q_ref, kvc_ref, gz_ref, gb_ref, o_ref, idx_ref, *, past, n_pick):
    n_pages = kvc_ref.shape[0] // (2 * NSA_KV_HEADS)
    nb = _BLOCKS_PER_PAGE * n_pages

    def block_id(shape):
        pos = _iota(shape, 1)
        page = jnp.where(pos >= n_pages, pos - n_pages, pos)
        return page * _BLOCKS_PER_PAGE + jnp.where(pos >= n_pages, 1, 0)

    lane = block_id((NSA_HEADS, nb))
    hrow = _iota((NSA_HEADS, nb), 0)
    vis = lane * NSA_BLOCK + (NSA_BLOCK - 1) <= past
    q = (q_ref[0] * (NSA_DH ** -0.5)).astype(BF16)
    gates = _sigmoid(gz_ref[0] + gb_ref[...])
    cur = past // NSA_BLOCK
    o_all = jnp.zeros((NSA_HEADS, NSA_DH), F32)
    idx_all = jnp.zeros((SUBLANE, LANE), F32)
    orow = _iota((NSA_HEADS, NSA_DH), 0)
    slot_r, slot_c = _iota((SUBLANE, LANE), 0), _iota((SUBLANE, LANE), 1)
    l1 = block_id((1, nb))
    l1f = l1.astype(F32)
    forced = (l1 == 0) | (l1 == cur) | (l1 == cur - 1)
    for kh in range(NSA_KV_HEADS):
        kc = kvc_ref[kh * n_pages:(kh + 1) * n_pages, :].astype(BF16)
        vc = kvc_ref[(NSA_KV_HEADS + kh) * n_pages:(NSA_KV_HEADS + kh + 1) * n_pages, :].astype(BF16)
        s = jnp.concatenate([_mm(q, kc[:, j * NSA_DH:(j + 1) * NSA_DH], _NT) for j in range(_BLOCKS_PER_PAGE)], axis=1)
        s = jnp.where(vis, s, -jnp.inf)
        m = jnp.max(s, axis=1, keepdims=True)
        m = jnp.where(m == -jnp.inf, 0.0, m)
        e = jnp.exp(s - m)
        p = e / jnp.maximum(jnp.sum(e, axis=1, keepdims=True), TINY)
        mine = (hrow >= kh * NSA_GROUP) & (hrow < (kh + 1) * NSA_GROUP)
        pb = p.astype(BF16)
        o_kh = _mm(pb[:, :n_pages], vc[:, :NSA_DH])
        for j in range(1, _BLOCKS_PER_PAGE):
            o_kh = o_kh + _mm(pb[:, j * n_pages:(j + 1) * n_pages], vc[:, j * NSA_DH:(j + 1) * NSA_DH])
        o_all = jnp.where((orow >= kh * NSA_GROUP) & (orow < (kh + 1) * NSA_GROUP), o_kh, o_all)
        imp = jnp.sum(jnp.where(mine, p, 0.0), axis=0, keepdims=True)
        score = jnp.where(forced, -jnp.inf, imp)
        for r in range(n_pick):
            mx = jnp.max(score, axis=1, keepdims=True)
            pick = jnp.min(jnp.where(score == mx, l1f, float(nb)), axis=1, keepdims=True)
            score = jnp.where(l1f == pick, -jnp.inf, score)
            idx_all = jnp.where((slot_r == kh) & (slot_c == r), pick, idx_all)
    o_ref[0] = o_all * gates[:, 0:1]
    idx_ref[0] = idx_all.astype(jnp.int32)


def _cmp_decode(qh, kvc, gz, gb3, past, n_pick):
    b = qh.shape[0]
    rows = kvc.shape[0] // b
    return pl.pallas_call(
        functools.partial(_cmp_decode_body, past=past, n_pick=n_pick),
        grid=(b,),
        in_specs=[pl.BlockSpec((1, NSA_HEADS, NSA_DH), lambda i: (i, 0, 0)), pl.BlockSpec((rows, kvc.shape[1]), lambda i: (i, 0)),
                  pl.BlockSpec((1, NSA_HEADS, NSA_BRANCHES), lambda i: (i, 0, 0)), _resident((NSA_HEADS, NSA_BRANCHES))],
        out_specs=[pl.BlockSpec((1, NSA_HEADS, NSA_DH), lambda i: (i, 0, 0)), pl.BlockSpec((1, SUBLANE, LANE), lambda i: (i, 0, 0))],
        out_shape=[jax.ShapeDtypeStruct((b, NSA_HEADS, NSA_DH), F32), jax.ShapeDtypeStruct((b, SUBLANE, LANE), jnp.int32)],
        compiler_params=_params(1),
        name="nsa_cmp_decode",
    )(qh, kvc, gz, gb3)


def _selwin_decode_body(info_ref, q_ref, ns_ref, nw_ref, win_ref, *refs, n_shared, n_own, first_win_row):
    n_blk = n_shared + NSA_KV_HEADS * n_own
    page_refs = refs[:n_blk]
    gz_ref, gb_ref, o_ref, wout_ref = refs[n_blk:]
    i = pl.program_id(0)
    q = q_ref[0] * (NSA_DH ** -0.5)
    qb = q.astype(BF16)
    gates = _sigmoid(gz_ref[0] + gb_ref[...])
    hrow = _iota((NSA_HEADS, NSA_DH), 0)
    n_win = win_ref.shape[2]
    wlane = _iota((NSA_HEADS, n_win), 1)
    pblk = jnp.right_shift(_iota((NSA_HEADS, PAGE_SIZE), 1), NSA_BLOCK.bit_length() - 1)
    o_all = jnp.zeros((NSA_HEADS, NSA_DH), F32)
    voff = NSA_KV_HEADS * NSA_DH
    for kh in range(NSA_KV_HEADS):
        ksl = slice(kh * NSA_DH, (kh + 1) * NSA_DH)
        vsl = slice(voff + kh * NSA_DH, voff + (kh + 1) * NSA_DH)
        slots = list(range(n_shared)) + list(range(n_shared + kh * n_own, n_shared + (kh + 1) * n_own))
        s_parts = []
        for sl in slots:
            blk_in_page = jnp.bitwise_and(info_ref[i, sl], _BLOCKS_PER_PAGE - 1)
            sp = _mm(qb, page_refs[sl][0, ksl, :].astype(BF16))
            s_parts.append(jnp.where(pblk == blk_in_page, sp, -jnp.inf))
        kn, vn = ns_ref[0, :, ksl], ns_ref[0, :, vsl]
        sn = jnp.sum(q * kn, axis=1, keepdims=True)
        m = sn
        for sp in s_parts:
            m = jnp.maximum(m, jnp.max(sp, axis=1, keepdims=True))
        en = jnp.exp(sn - m)
        l, acc = en, en * vn
        for sl, sp in zip(slots, s_parts):
            e = jnp.exp(sp - m)
            l = l + jnp.sum(e, axis=1, keepdims=True)
            acc = acc + _mm(e.astype(BF16), page_refs[sl][0, vsl, :].astype(BF16), _NT)
        o_sel = acc / l
        sw = jnp.where(wlane >= first_win_row, _mm(qb, win_ref[0, ksl, :].astype(BF16)), -jnp.inf)
        kwn, vwn = nw_ref[0, :, ksl], nw_ref[0, :, vsl]
        swn = jnp.sum(q * kwn, axis=1, keepdims=True)
        mw = jnp.maximum(jnp.max(sw, axis=1, keepdims=True), swn)
        ew, ewn = jnp.exp(sw - mw), jnp.exp(swn - mw)
        o_win = (_mm(ew.astype(BF16), win_ref[0, vsl, :].astype(BF16), _NT) + ewn * vwn) / (jnp.sum(ew, axis=1, keepdims=True) + ewn)
        keep = (hrow >= kh * NSA_GROUP) & (hrow < (kh + 1) * NSA_GROUP)
        o_all = jnp.where(keep, gates[:, 1:2] * o_sel + gates[:, 2:3] * o_win, o_all)
    o_ref[0] = o_all
    r2, c2 = _iota((NSA_KV_W, NSA_KV_W), 0), _iota((NSA_KV_W, NSA_KV_W), 1)
    new_col = jnp.sum(jnp.where(r2 == c2, jnp.broadcast_to(nw_ref[0], (NSA_KV_W, NSA_KV_W)), 0.0), axis=1, keepdims=True)
    shifted = pltpu.roll(win_ref[0], n_win - 1, 1)
    wout_ref[0] = jnp.where(_iota((NSA_KV_W, n_win), 1) == n_win - 1, new_col, shifted)


def _selwin_decode(info, qh, new_s, new_w, win_t, pool_t, gz, gb3, n_shared, n_own, first_win_row):
    b = qh.shape[0]
    n_blk = info.shape[1]
    row3 = lambda n: pl.BlockSpec((1, 1, n), lambda i, r: (i, 0, 0))

    def page_spec(s):
        return pl.BlockSpec((1, NSA_KV_W, PAGE_SIZE), lambda i, r: (jnp.right_shift(r[i, s], _PAGE_SHIFT), 0, 0))

    win_spec = pl.BlockSpec((1,) + win_t.shape[1:], lambda i, r: (i, 0, 0))
    grid_spec = pltpu.PrefetchScalarGridSpec(
        num_scalar_prefetch=1,
        grid=(b,),
        in_specs=[pl.BlockSpec((1, NSA_HEADS, NSA_DH), lambda i, r: (i, 0, 0)), row3(NSA_KV_W), row3(NSA_KV_W), win_spec]
        + [page_spec(s) for s in range(n_blk)]
        + [pl.BlockSpec((1, NSA_HEADS, NSA_BRANCHES), lambda i, r: (i, 0, 0)),
           pl.BlockSpec((NSA_HEADS, NSA_BRANCHES), lambda i, r: (0, 0))],
        out_specs=[pl.BlockSpec((1, NSA_HEADS, NSA_DH), lambda i, r: (i, 0, 0)), win_spec],
    )
    return pl.pallas_call(
        functools.partial(_selwin_decode_body, n_shared=n_shared, n_own=n_own, first_win_row=first_win_row),
        grid_spec=grid_spec,
        out_shape=[jax.ShapeDtypeStruct((b, NSA_HEADS, NSA_DH), F32), jax.ShapeDtypeStruct(win_t.shape, F32)],
        compiler_params=_params(1),
        name="nsa_selwin_decode",
    )(info, qh, new_s.reshape(b, 1, -1), new_w.reshape(b, 1, -1), win_t, *([pool_t] * n_blk), gz, gb3)


def _feature_major(cache):
    n, rows = cache.shape[:2]
    return jnp.transpose(cache, (0, 2, 3, 4, 1)).reshape(n, NSA_KV_W, rows)


def _pad_cols(w, n):
    return jnp.pad(w, ((0, 0), (0, n - w.shape[1])))


def _even_w_in(w):
    sizes = (GLA_HEADS * GLA_DK, GLA_HEADS * GLA_DK, GLA_HEADS * GLA_DV, GLA_HEADS * GLA_DV, GLA_RANK,
             NSA_HEADS * NSA_DH, NSA_BRANCHES * NSA_KV_W, NSA_HEADS * NSA_BRANCHES)
    cuts = [0]
    for s in sizes:
        cuts.append(cuts[-1] + s)
    gq, gk, gv, gg, ga, nq, nkv, ng = (w[:, cuts[i]:cuts[i + 1]] for i in range(len(sizes)))
    small = _pad_cols(jnp.concatenate([ga, ng], axis=1), LANE)
    sample = jnp.concatenate([gq, gk, gv, gg, nq, nkv, small], axis=1).astype(BF16)
    prompt = jnp.concatenate([gq, gk, gv, gg, nkv, small], axis=1).astype(BF16)
    prompt_t = jnp.concatenate([nq, nkv, small], axis=1).T.astype(BF16)
    return sample, prompt, prompt_t


_EVEN_WIDTHS = (GLA_Z_W, NSA_HEADS * NSA_DH, NSA_KV_W, NSA_KV_W, NSA_KV_W, LANE)
_EVEN_WIDTHS_P = (GLA_Z_W, NSA_KV_W, NSA_KV_W, NSA_KV_W, LANE)
_EVEN_WIDTHS_PT = (NSA_HEADS * NSA_DH, NSA_KV_W, NSA_KV_W, NSA_KV_W, LANE)


def _odd_w_in(w):
    main = M_Z_W
    return jnp.concatenate([w[:, :main], _pad_cols(w[:, main:], LANE)], axis=1).astype(BF16)


_ODD_WIDTHS = (M_Z_W, LANE)


def _compress_weights(pe, w1, w2):
    eye_k = jnp.eye(NSA_KV_HEADS, dtype=F32)
    eye_c = jnp.eye(2, dtype=F32)
    n_ck = 2 * NSA_KV_HEADS
    rows = jnp.broadcast_to(w1.transpose(1, 0, 2, 3)[:, :, None], (NSA_BLOCK, 2, NSA_KV_HEADS, NSA_DH, NSA_CMP_HID))
    rows = jnp.tile(rows.reshape(NSA_BLOCK * NSA_KV_W, NSA_CMP_HID), (1, n_ck))
    row_ck = (jnp.arange(NSA_BLOCK * NSA_KV_W) // NSA_DH) % n_ck
    col_ck = jnp.arange(NSA_KV_W) // NSA_CMP_HID
    w1big = jnp.where(row_ck[:, None] == col_ck[None, :], rows, 0.0)
    w2big = jnp.einsum("ced,cx,ky->ckexyd", w2, eye_c, eye_k).reshape(NSA_KV_W, NSA_KV_W)
    pe_flat = jnp.broadcast_to(pe.transpose(1, 0, 2)[:, :, None, :], (NSA_BLOCK, 2, NSA_KV_HEADS, NSA_DH)).reshape(1, -1)
    return pe_flat, w1big.astype(BF16), w2big.astype(BF16)


def _compress_weights_paged(pe, w1, w2):
    eye_b = jnp.eye(_BLOCKS_PER_PAGE, dtype=F32)
    pe_t = jnp.tile(pe.transpose(0, 2, 1), (1, 1, _BLOCKS_PER_PAGE)).reshape(2 * NSA_DH, PAGE_SIZE)
    rows = jnp.broadcast_to(w1.transpose(0, 2, 1, 3)[:, :, None], (2, NSA_DH, _BLOCKS_PER_PAGE, NSA_BLOCK, NSA_CMP_HID))
    rows = jnp.tile(rows.reshape(2, NSA_DH // _D_PER_STEP, _D_PER_STEP * PAGE_SIZE, NSA_CMP_HID), (1, 1, 1, _BLOCKS_PER_PAGE))
    row_blk = (jnp.arange(_D_PER_STEP * PAGE_SIZE) // NSA_BLOCK) % _BLOCKS_PER_PAGE
    col_blk = jnp.arange(_BLOCKS_PER_PAGE * NSA_CMP_HID) // NSA_CMP_HID
    w1_t = jnp.where(row_blk[:, None] == col_blk[None, :], rows, 0.0)
    w2_t = jnp.einsum("ced,hx->chexd", w2, eye_b).reshape(2, _BLOCKS_PER_PAGE * NSA_CMP_HID, _BLOCKS_PER_PAGE * NSA_DH)
    return pe_t, w1_t.astype(BF16), w2_t.astype(BF16)


def _gate_bias_row(gb):
    return jnp.pad(gb, (GATE_OFF, LANE - GATE_OFF - gb.shape[0])).reshape(1, LANE)


def _split_outs(outs, n):
    return outs[:n], outs[n:]


def _values_t(kv_t, rows=NSA_DH):
    b, _, t = kv_t.shape
    v = kv_t.reshape(b, 2, NSA_KV_HEADS, NSA_DH, t)[:, 1].astype(BF16)
    if rows > NSA_DH:
        extra = jnp.zeros((b, NSA_KV_HEADS, rows - NSA_DH, t), BF16).at[:, :, 0].set(1.0)
        v = jnp.concatenate([v, extra], axis=2)
    return v


def _kv_rows(kv_t):
    b, _, t = kv_t.shape
    return jnp.transpose(kv_t.reshape(b, 2, NSA_KV_HEADS, NSA_DH, t), (0, 4, 1, 2, 3))


def _heads_major(kv, which):
    b, t, _ = kv.shape
    return kv.reshape(b, t, 2, NSA_KV_HEADS, NSA_DH)[:, :, which].transpose(0, 2, 1, 3).astype(BF16)


def _even_layer_prompt(x, bsz, g, prm):
    m = x.shape[0]
    t = m // bsz
    (z, kvc, kvs, kvw, zs), (q_t, kvc_t, kvs_t, kvw_t, zs_t) = _split_outs(
        _norm_proj(x, g, prm["w_in_p"], _EVEN_WIDTHS_P, TM_PROJ, prm["w_in_pt"], _EVEN_WIDTHS_PT, bsz), len(_EVEN_WIDTHS_P))
    o_gla, s_fin = _gla_prompt(z, zs, prm["wa"], prm["ba"], prm["gn"],
                               jnp.zeros((bsz, GLA_HEADS, GLA_DK, GLA_DV), F32), bsz)
    nb = t // NSA_BLOCK
    nbp = -(-nb // LANE) * LANE
    kvcmp = _compress_dense(kvc.reshape(bsz * nb, NSA_BLOCK * NSA_KV_W), prm["pe"], prm["w1"], prm["w2"])
    kvcmp = jnp.pad(kvcmp.reshape(bsz, nb, NSA_KV_W), ((0, 0), (0, nbp - nb), (0, 0))).reshape(bsz * nbp, NSA_KV_W)
    o_cmp, nm = _cmp_sel(q_t, kvcmp, zs_t, prm["gb_col"], nbp, TQ_CMP)
    onehot = (jnp.arange(t)[:, None] // NSA_BLOCK == jnp.arange(nbp)[None, :]).astype(BF16)
    o_sel = _sel_attn(q_t, nm, kvs, onehot, _values_t(kvs_t, V_AUG), zs_t, prm["gb_col"], TQ_ATTN, TK_SEL)
    o_win = _win_attn(q_t, kvw, _values_t(kvw_t, V_AUG), zs_t, prm["gb_col"], TQ_ATTN)
    y = _out_proj(x, [o_gla], (1,), prm["w_out"], TM_OUT, [o_cmp, o_sel, o_win], bsz)
    n_keep = min(NSA_WINDOW, t)
    return y, s_fin, _kv_rows(kvc_t), _kv_rows(kvs_t), _kv_rows(kvw_t[:, :, t - n_keep:])


def _even_layer_sample(x, g, prm, gla_state, cmp_pool, sel_pool, win_buf, page_table):
    b = x.shape[0]
    n_pages = page_table.shape[1]
    past = n_pages * PAGE_SIZE
    n_past_blk = past // NSA_BLOCK
    z, nq, kvc, kvs, kvw, zs = _norm_proj(x, g, prm["w_in"], _EVEN_WIDTHS, b)
    o_gla, s_new = _gla_decode(z, zs, prm["wa"], prm["ba"], prm["gn"], gla_state)
    kvcmp = _compress_paged(_feature_major(cmp_pool), page_table, prm["pe_t"], prm["w1_t"], prm["w2_t"])
    qh = nq.reshape(b, NSA_HEADS, NSA_DH)
    gz = zs[:, GATE_OFF:GATE_OFF + NSA_HEADS * NSA_BRANCHES].reshape(b, NSA_HEADS, NSA_BRANCHES)
    forced = _decode_forced(n_past_blk)
    n_pick = NSA_TOP_N - len(forced)
    assert n_past_blk - len(forced) + 1 >= n_pick
    o_cmp, idx = _cmp_decode(qh, kvcmp, gz, prm["gb3"], past, n_pick)
    shared = [f for f in forced if f < n_past_blk]
    logical = jnp.concatenate([jnp.broadcast_to(jnp.asarray(shared, jnp.int32), (b, len(shared))),
                               idx[:, :NSA_KV_HEADS, :n_pick].reshape(b, NSA_KV_HEADS * n_pick)], axis=1)
    per_page = _BLOCKS_PER_PAGE
    phys = jnp.take_along_axis(page_table, logical // per_page, axis=1) * per_page + logical % per_page
    win_keep = win_buf.shape[1]
    assert win_keep >= 1
    first_win_row = max(win_keep - NSA_WINDOW + 1, 0)
    o_sw, win_new_t = _selwin_decode(phys.astype(jnp.int32), qh, kvs, kvw, _feature_major(win_buf), _feature_major(sel_pool),
                                     gz, prm["gb3"], len(shared), n_pick, first_win_row)
    hd = NSA_HEADS * NSA_DH
    y = _out_proj(x, [o_gla, o_cmp.reshape(b, hd), o_sw.reshape(b, hd)], (1, 2), prm["w_out"], b)
    kv_shape = (b, 1, 2, NSA_KV_HEADS, NSA_DH)
    win_new = jnp.transpose(win_new_t.reshape(b, 2, NSA_KV_HEADS, NSA_DH, win_keep), (0, 4, 1, 2, 3))
    return y, s_new, kvc.reshape(kv_shape), kvs.reshape(kv_shape), win_new


def _odd_layer_prompt(x, bsz, g, prm):
    z, zs = _norm_proj(x, g, prm["w_in"], _ODD_WIDTHS, TM_PROJ)
    h, c_aug, m_fin = _mlstm_prompt(z, zs, prm["bias"], prm["mn"], jnp.zeros((bsz, M_HEADS, M_DQK, M_AUG), F32),
                                    jnp.zeros((bsz, 1, LANE), F32), bsz)
    y = _out_proj(x, [h], (1,), prm["w_out"], TM_OUT)
    return y, c_aug[..., :M_DV], c_aug[..., M_DV], m_fin[:, 0, :M_HEADS]


def _odd_layer_sample(x, g, prm, c0, n0, m0):
    b = x.shape[0]
    z, zs = _norm_proj(x, g, prm["w_in"], _ODD_WIDTHS, b)
    h, cn, nn, mn = _mlstm_decode(z, zs, prm["bias"], prm["mn"], c0, n0, m0)
    return _out_proj(x, [h], (1,), prm["w_out"], b), cn, nn, mn


def kernel(x_prompt, x_sample, cache_cmp_kv, cache_sel_kv, cache_win_kv, state_gla, state_mlstm_c, state_mlstm_n, state_mlstm_m, state_ffn_conv, page_table, norm_mix, norm_ffn, norm_final, even_w_in, even_w_out, gla_w_a2, gla_b_a, gla_norm, nsa_cmp_pe, nsa_cmp_w1, nsa_cmp_w2, nsa_gate_b, odd_w_in, odd_w_out, mlstm_b_i, mlstm_b_f, mlstm_norm, ffn_w_up, ffn_conv_w, ffn_conv_b, ffn_w_down):
    bp, t, d = x_prompt.shape
    bs = x_sample.shape[0]
    assert x_sample.shape[1] == 1
    depth = norm_mix.shape[0]
    f = ffn_conv_w.shape[2]
    xp = x_prompt.reshape(bp * t, d)
    xs = x_sample.reshape(bs, d)
    outs = {k: [] for k in ("cmp_p", "cmp_s", "sel_p", "sel_s", "win_p", "win_s", "gla_p", "gla_s",
                            "mc_p", "mc_s", "mn_p", "mn_s", "mm_p", "mm_s", "cv_p", "cv_s")}
    for l in range(depth):
        if l % 2 == 0:
            e = l // 2
            pe, w1, w2 = _compress_weights(nsa_cmp_pe[e], nsa_cmp_w1[e], nsa_cmp_w2[e])
            pe_t, w1_t, w2_t = _compress_weights_paged(nsa_cmp_pe[e], nsa_cmp_w1[e], nsa_cmp_w2[e])
            w_in_s, w_in_p, w_in_pt = _even_w_in(even_w_in[e])
            prm = dict(pe_t=pe_t, w1_t=w1_t, w2_t=w2_t, w_in=w_in_s, w_in_p=w_in_p, w_in_pt=w_in_pt,
                       w_out=even_w_out[e].astype(BF16),
                       wa=jnp.pad(gla_w_a2[e], ((0, LANE - GLA_RANK), (0, 0))), ba=gla_b_a[e].reshape(1, -1),
                       gn=gla_norm[e].reshape(1, -1), pe=pe, w1=w1, w2=w2, gb_col=_gate_bias_row(nsa_gate_b[e]).reshape(LANE, 1),
                       gb3=nsa_gate_b[e].reshape(NSA_HEADS, NSA_BRANCHES))
            xp, s_, c_, k_, w_ = _even_layer_prompt(xp, bp, norm_mix[l], prm)
            outs["gla_p"].append(s_); outs["cmp_p"].append(c_); outs["sel_p"].append(k_); outs["win_p"].append(w_)
            xs, s_, c_, k_, w_ = _even_layer_sample(xs, norm_mix[l], prm, state_gla[e], cache_cmp_kv[e], cache_sel_kv[e],
                                                    cache_win_kv[e], page_table)
            outs["gla_s"].append(s_); outs["cmp_s"].append(c_); outs["sel_s"].append(k_); outs["win_s"].append(w_)
        else:
            o = l // 2
            bias = jnp.pad(jnp.concatenate([mlstm_b_i[o], mlstm_b_f[o]]), (0, LANE - 2 * M_HEADS)).reshape(1, LANE)
            prm = dict(w_in=_odd_w_in(odd_w_in[o]), w_out=odd_w_out[o].astype(BF16), bias=bias, mn=mlstm_norm[o].reshape(1, -1))
            xp, c_, n_, m_ = _odd_layer_prompt(xp, bp, norm_mix[l], prm)
            outs["mc_p"].append(c_); outs["mn_p"].append(n_); outs["mm_p"].append(m_)
            xs, c_, n_, m_ = _odd_layer_sample(xs, norm_mix[l], prm, state_mlstm_c[o], state_mlstm_n[o], state_mlstm_m[o])
            outs["mc_s"].append(c_); outs["mn_s"].append(n_); outs["mm_s"].append(m_)
        final = l == depth - 1
        wup, wd = ffn_w_up[l].astype(BF16), ffn_w_down[l].astype(BF16)
        xp, cv = _ffn_prompt(xp, bp, norm_ffn[l], wup, ffn_conv_w[l], ffn_conv_b[l], wd,
                             jnp.zeros((bp, CONV_W - 1, f), F32), norm_final, final, TM_FFN)
        outs["cv_p"].append(cv)
        xs, cv = _ffn_decode(xs, norm_ffn[l], wup, ffn_conv_w[l], ffn_conv_b[l], wd, state_ffn_conv[l], norm_final, final)
        outs["cv_s"].append(cv)
    st = lambda k: jnp.stack(outs[k])
    return (xp.reshape(bp, t, d), xs.reshape(bs, 1, d),
            st("cmp_p"), st("cmp_s"), st("sel_p"), st("sel_s"), st("win_p"), st("win_s"), st("gla_p"), st("gla_s"),
            st("mc_p"), st("mc_s"), st("mn_p"), st("mn_s"), st("mm_p"), st("mm_s"), st("cv_p"), st("cv_s"))
```

```python
import functools

import jax
import jax.numpy as jnp
from jax import lax
from jax.experimental import pallas as pl
from jax.experimental.pallas import tpu as pltpu

F32 = jnp.float32
BF16 = jnp.bfloat16

GLA_HEADS, GLA_DK, GLA_DV, GLA_RANK, GLA_TAU, GLA_CHUNK = 4, 64, 128, 16, 16.0, 64
NSA_HEADS, NSA_KV_HEADS, NSA_GROUP, NSA_DH = 8, 2, 4, 64
NSA_BRANCHES, NSA_BLOCK, NSA_TOP_N, NSA_WINDOW, NSA_CMP_HID = 3, 64, 16, 512, 64
M_HEADS, M_DQK, M_DV, M_CHUNK = 4, 128, 256, 64
M_CHUNK_PALLAS = 128
CONV_W = 3
PAGE_SIZE = 128
EPS, TINY = 1e-6, 1e-30
LOG2E = 1.4426950408889634

LANE = 128
SUBLANE = 8
VMEM_BIG = 52 * 1024 * 1024
VMEM_MID = 40 * 1024 * 1024

MASK_BIG = 32768.0

NSA_KV_W = 2 * NSA_KV_HEADS * NSA_DH
GLA_Z_W = 2 * GLA_HEADS * GLA_DK + 2 * GLA_HEADS * GLA_DV
M_Z_W = 2 * M_HEADS * M_DQK + 2 * M_HEADS * M_DV
M_AUG = M_DV + LANE
GATE_OFF = GLA_RANK
TM_PROJ = 256
TM_OUT = 512
TM_FFN = 256
TQ_CMP = 512
TQ_ATTN = 128
TK_SEL = 512
V_AUG = NSA_DH + 16
_BLOCKS_PER_PAGE = PAGE_SIZE // NSA_BLOCK
_PAGE_SHIFT = _BLOCKS_PER_PAGE.bit_length() - 1

_NN = (((1,), (0,)), ((), ()))
_NT = (((1,), (1,)), ((), ()))
_TN = (((0,), (0,)), ((), ()))


def _mm(a, b, dims=_NN):
    return lax.dot_general(a, b, dims, preferred_element_type=F32)


def _mm_bf(a, b, dims=_NN):
    return _mm(a.astype(BF16), b.astype(BF16), dims)


def _split_bf16(x, n):
    parts, r = [], x
    for _ in range(n):
        p = r.astype(BF16)
        parts.append(p)
        r = r - p.astype(F32)
    return parts


def _mm_sel(sel, x, dims=_NN, x_is_rhs=True):
    out = None
    for p in _split_bf16(x, 3):
        t = _mm(sel, p, dims) if x_is_rhs else _mm(p, sel, dims)
        out = t if out is None else out + t
    return out


def _mm_hp(a, b, dims=_NN):
    a1, a2 = _split_bf16(a, 2)
    b1, b2 = _split_bf16(b, 2)
    return _mm(a1, b1, dims) + (_mm(a1, b2, dims) + _mm(a2, b1, dims))


def _gelu(x):
    return 0.5 * x * (1.0 + jnp.tanh(0.7978845608028654 * (x + 0.044715 * (x * x * x))))


def _sigmoid(x):
    return 1.0 / (1.0 + jnp.exp(-x))


def _logsigmoid(x):
    return jnp.minimum(x, 0.0) - jnp.log(1.0 + jnp.exp(-jnp.abs(x)))


def _rms(x, g):
    return x * lax.rsqrt(jnp.mean(x * x, axis=-1, keepdims=True) + EPS) * g


def _iota(shape, dim):
    return lax.broadcasted_iota(jnp.int32, shape, dim)


def _params(n_axes, vmem=VMEM_MID):
    return pltpu.CompilerParams(dimension_semantics=("arbitrary",) * n_axes, vmem_limit_bytes=vmem)


def _resident(shape):
    nd = len(shape)
    return pl.BlockSpec(shape, lambda *_: (0,) * nd, pipeline_mode=pl.Buffered(1))


def _norm_proj_body(x_ref, g_ref, w_ref, *refs, widths, t_widths):
    xb = _rms(x_ref[...], g_ref[...]).astype(BF16)
    o_refs = refs[1:] if t_widths else refs
    off = 0
    for o_ref, n in zip(o_refs[:len(widths)], widths):
        o_ref[...] = _mm(xb, w_ref[:, off:off + n])
        off += n
    off = 0
    for o_ref, n in zip(o_refs[len(widths):], t_widths):
        o_ref[0] = _mm(refs[0][off:off + n, :], xb, _NT)
        off += n


def _norm_proj(x, g, w, widths, tm, wt=None, t_widths=(), bsz=1):
    m, d = x.shape
    tm = min(tm, m)
    nt = m // bsz // tm
    assert m % tm == 0 and sum(widths) == w.shape[1] and (m // bsz) % tm == 0
    t_in = [_resident(wt.shape)] if t_widths else []
    t_args = [wt] if t_widths else []
    return pl.pallas_call(
        functools.partial(_norm_proj_body, widths=tuple(widths), t_widths=tuple(t_widths)),
        grid=(m // tm,),
        in_specs=[pl.BlockSpec((tm, d), lambda i: (i, 0)), _resident((1, d)), _resident(w.shape)] + t_in,
        out_specs=[pl.BlockSpec((tm, n), lambda i: (i, 0)) for n in widths]
        + [pl.BlockSpec((1, n, tm), lambda i: (i // nt, 0, i % nt)) for n in t_widths],
        out_shape=[jax.ShapeDtypeStruct((m, n), F32) for n in widths]
        + [jax.ShapeDtypeStruct((bsz, n, m // bsz), F32) for n in t_widths],
        compiler_params=_params(1),
        name="norm_proj",
    )(x, g.reshape(1, d), w, *t_args)


def _mix_residual(x, h_refs, w_ref, groups, n_t):
    acc = x
    i = off = 0
    for gsz in groups:
        h = h_refs[i][...]
        for j in range(1, gsz):
            h = h + h_refs[i + j][...]
        i += gsz
        n = h.shape[1]
        acc = acc + _mm(h.astype(BF16), w_ref[off:off + n, :])
        off += n
    if n_t:
        ht = h_refs[i][0]
        for j in range(1, n_t):
            ht = ht + h_refs[i + j][0]
        acc = acc + _mm(ht.astype(BF16), w_ref[off:off + ht.shape[0], :], _TN)
    return acc


def _out_proj_body(x_ref, *refs, groups, n_t):
    h_refs, w_ref, o_ref = refs[:-2], refs[-2], refs[-1]
    o_ref[...] = _mix_residual(x_ref[...], h_refs, w_ref, groups, n_t)


def _out_proj(x, hs, groups, w, tm, t_hs=(), bsz=1):
    m, d = x.shape
    tm = min(tm, m)
    nt = m // bsz // tm
    assert m % tm == 0 and (m // bsz) % tm == 0
    return pl.pallas_call(
        functools.partial(_out_proj_body, groups=tuple(groups), n_t=len(t_hs)),
        grid=(m // tm,),
        in_specs=[pl.BlockSpec((tm, d), lambda i: (i, 0))]
        + [pl.BlockSpec((tm, h.shape[1]), lambda i: (i, 0)) for h in hs]
        + [pl.BlockSpec((1, h.shape[1], tm), lambda i: (i // nt, 0, i % nt)) for h in t_hs]
        + [_resident(w.shape)],
        out_specs=pl.BlockSpec((tm, d), lambda i: (i, 0)),
        out_shape=jax.ShapeDtypeStruct((m, d), F32),
        compiler_params=_params(1),
        name="out_proj",
    )(x, *hs, *t_hs, w)


def _ffn_chunks(xn, resid, wup_ref, cw_ref, cb_ref, wd_ref, fc, prev_fn, keep_fn):
    f = cw_ref.shape[1]
    n = f // fc

    def up_proj(c):
        return _mm(xn, wup_ref[:, c * fc:(c + 1) * fc]), _mm(xn, wup_ref[:, f + c * fc:f + (c + 1) * fc])

    acc = resid
    nxt = up_proj(0)
    for c in range(n):
        sl = slice(c * fc, (c + 1) * fc)
        gp, up = nxt
        if c + 1 < n:
            nxt = up_proj(c + 1)
        g2, g1 = prev_fn(gp, sl)
        a = cb_ref[:, sl] + g2 * cw_ref[0:1, sl] + g1 * cw_ref[1:2, sl] + gp * cw_ref[2:3, sl]
        acc = acc + _mm((_gelu(a) * up).astype(BF16), wd_ref[sl, :])
        keep_fn(gp, sl)
    return acc


def _ffn_prompt_body(x_ref, g_ref, wup_ref, cw_ref, cb_ref, wd_ref, st_ref, gf_ref, y_ref, ns_ref, carry, *, final, fc):
    t = pl.program_id(1)
    tm = x_ref.shape[0]

    @pl.when(t == 0)
    def _():
        carry[...] = st_ref[0]

    x = x_ref[...]
    xn = _rms(x, g_ref[...]).astype(BF16)
    row = _iota((tm, fc), 0)

    def prev_fn(gp, sl):
        c0, c1 = carry[0:1, sl], carry[1:2, sl]
        g1 = jnp.where(row == 0, c1, pltpu.roll(gp, 1, 0))
        g2 = jnp.where(row == 0, c0, jnp.where(row == 1, c1, pltpu.roll(gp, 2, 0)))
        return g2, g1

    def keep_fn(gp, sl):
        carry[:, sl] = gp[tm - 2:tm, :]

    acc = _ffn_chunks(xn, x, wup_ref, cw_ref, cb_ref, wd_ref, fc, prev_fn, keep_fn)
    y_ref[...] = _rms(acc, gf_ref[...]) if final else acc

    @pl.when(t == pl.num_programs(1) - 1)
    def _():
        ns_ref[0] = carry[...]


def _ffn_prompt(x, bsz, g, wup, cw, cb, wd, st, gf, final, tm, fc=256):
    m, d = x.shape
    t = m // bsz
    tm = min(tm, t)
    f = cw.shape[1]
    assert t % tm == 0 and f % fc == 0
    nt = t // tm
    rows = lambda n: pl.BlockSpec((tm, n), lambda b, i: (b * nt + i, 0))
    return pl.pallas_call(
        functools.partial(_ffn_prompt_body, final=final, fc=fc),
        grid=(bsz, nt),
        in_specs=[rows(d), _resident((1, d)), _resident(wup.shape), _resident(cw.shape), _resident((1, f)),
                  _resident(wd.shape), pl.BlockSpec((1, CONV_W - 1, f), lambda b, i: (b, 0, 0)), _resident((1, d))],
        out_specs=[rows(d), pl.BlockSpec((1, CONV_W - 1, f), lambda b, i: (b, 0, 0))],
        out_shape=[jax.ShapeDtypeStruct((m, d), F32), jax.ShapeDtypeStruct((bsz, CONV_W - 1, f), F32)],
        scratch_shapes=[pltpu.VMEM((CONV_W - 1, f), F32)],
        compiler_params=_params(2, VMEM_BIG),
        name="ffn_prompt",
    )(x, g.reshape(1, d), wup, cw, cb.reshape(1, f), wd, st, gf.reshape(1, d))


def _ffn_decode_body(x_ref, g_ref, wup_ref, cw_ref, cb_ref, wd_ref, s0_ref, s1_ref, gf_ref, y_ref, gp_ref, *, final, fc):
    x = x_ref[...]
    xn = _rms(x, g_ref[...]).astype(BF16)

    def prev_fn(gp, sl):
        return s0_ref[:, sl], s1_ref[:, sl]

    def keep_fn(gp, sl):
        gp_ref[:, sl] = gp

    acc = _ffn_chunks(xn, x, wup_ref, cw_ref, cb_ref, wd_ref, fc, prev_fn, keep_fn)
    y_ref[...] = _rms(acc, gf_ref[...]) if final else acc


def _ffn_decode(x, g, wup, cw, cb, wd, st, gf, final, fc=256):
    m, d = x.shape
    f = cw.shape[1]
    y, gp = pl.pallas_call(
        functools.partial(_ffn_decode_body, final=final, fc=fc),
        grid=(1,),
        in_specs=[_resident((m, d)), _resident((1, d)), _resident(wup.shape), _resident(cw.shape), _resident((1, f)),
                  _resident(wd.shape), _resident((m, f)), _resident((m, f)), _resident((1, d))],
        out_specs=[pl.BlockSpec((m, d), lambda i: (0, 0)), pl.BlockSpec((m, f), lambda i: (0, 0))],
        out_shape=[jax.ShapeDtypeStruct((m, d), F32), jax.ShapeDtypeStruct((m, f), F32)],
        compiler_params=_params(1, VMEM_BIG),
        name="ffn_decode",
    )(x, g.reshape(1, d), wup, cw, cb.reshape(1, f), wd, st[:, 0], st[:, 1], gf.reshape(1, d))
    return y, jnp.stack([st[:, 1], gp], axis=1)


def _gla_finish(o, gg, gn):
    return _rms(o, gn) * (gg * _sigmoid(gg))


def _gla_body(z_ref, zs_ref, wa_ref, ba_ref, gn_ref, s0_ref, o_ref, sfin_ref, s_scr, s_prev):
    t = pl.program_id(1)
    nb = z_ref.shape[0]

    @pl.when(t == 0)
    def _():
        s_scr[...] = s0_ref[...]

    s_prev[...] = s_scr[...]
    lasts = []
    _round_robin([_gla_chunk(z_ref.at[bi], zs_ref.at[bi], wa_ref, ba_ref, gn_ref, o_ref.at[bi], s_scr.at[bi], lasts)
                  for bi in range(nb)])

    @pl.when(jnp.min(jnp.concatenate(lasts, axis=0)) < -_GLA_FACTORED_RANGE)
    def _():
        for bi in range(nb):
            for h in range(GLA_HEADS):
                _gla_head_exact(h, z_ref.at[bi], zs_ref.at[bi], wa_ref, ba_ref, gn_ref, o_ref.at[bi], s_prev.at[bi])

    @pl.when(t == pl.num_programs(1) - 1)
    def _():
        sfin_ref[...] = s_scr[...]


_GLA_FACTORED_RANGE = 80.0


def _gla_gates(zs_ref, wa_ref, ba_ref):
    c = zs_ref.shape[0]
    la = _logsigmoid(_mm_hp(zs_ref[...], wa_ref[...]) + ba_ref[...]) * (1.0 / GLA_TAU)
    tri = _iota((c, c), 0) >= _iota((c, c), 1)
    return la, _mm_sel(jnp.where(tri, 1.0, 0.0).astype(BF16), la)


def _gla_head_exact(h, z_ref, zs_ref, wa_ref, ba_ref, gn_ref, o_ref, s_prev):
    c = z_ref.shape[0]
    hk = GLA_HEADS * GLA_DK
    ks = slice(h * GLA_DK, (h + 1) * GLA_DK)
    ch = _gla_gates(zs_ref, wa_ref, ba_ref)[1][:, ks]
    q = z_ref[:, ks] * (GLA_DK ** -0.5)
    k = z_ref[:, hk + h * GLA_DK:hk + (h + 1) * GLA_DK]
    v = z_ref[:, 2 * hk + h * GLA_DV:2 * hk + (h + 1) * GLA_DV]
    gg = z_ref[:, 2 * hk + GLA_HEADS * GLA_DV + h * GLA_DV:2 * hk + GLA_HEADS * GLA_DV + (h + 1) * GLA_DV]
    o_inter = _mm((q * jnp.exp(ch)).astype(BF16), s_prev[h].astype(BF16))
    row, rowv = _iota((c, GLA_DK), 0), _iota((c, GLA_DV), 0)

    def token(i, o_acc):
        sel = row == i
        qi = jnp.sum(jnp.where(sel, q, 0.0), axis=0, keepdims=True)
        ci = jnp.sum(jnp.where(sel, ch, 0.0), axis=0, keepdims=True)
        w = jnp.exp(jnp.minimum(ci - ch, 0.0))
        att = jnp.sum(jnp.where(row <= i, qi * k * w, 0.0), axis=1, keepdims=True)
        return jnp.where(rowv == i, jnp.sum(att * v, axis=0, keepdims=True), o_acc)

    o_intra = lax.fori_loop(0, c, token, jnp.zeros((c, GLA_DV), F32))
    o_ref[:, h * GLA_DV:(h + 1) * GLA_DV] = _gla_finish(o_inter + o_intra, gg, gn_ref[...])


def _round_robin(chains):
    chains = list(chains)
    while chains:
        chains = [c for c in chains if next(c, _DONE) is not _DONE]


_DONE = object()


def _gla_chunk(z_ref, zs_ref, wa_ref, ba_ref, gn_ref, o_ref, s_scr, lasts):
    c = z_ref.shape[0]
    hk = GLA_HEADS * GLA_DK
    la = _logsigmoid(_mm_hp(zs_ref[...], wa_ref[...]) + ba_ref[...]) * (1.0 / GLA_TAU)
    yield
    r, cidx = _iota((c, c), 0), _iota((c, c), 1)
    tri = r >= cidx
    cum = _mm_sel(jnp.where(tri, 1.0, 0.0).astype(BF16), la)
    yield
    last = cum[c - 1:c, :]
    lasts.append(last)
    eq, ek, ekl, el = jnp.exp(cum), jnp.exp(-cum), jnp.exp(last - cum), jnp.exp(last)
    gn = gn_ref[...]
    for h in range(GLA_HEADS):
        ks = slice(h * GLA_DK, (h + 1) * GLA_DK)
        vs = slice(2 * hk + h * GLA_DV, 2 * hk + (h + 1) * GLA_DV)
        gs = slice(2 * hk + GLA_HEADS * GLA_DV + h * GLA_DV, 2 * hk + GLA_HEADS * GLA_DV + (h + 1) * GLA_DV)
        q = z_ref[:, ks] * (GLA_DK ** -0.5)
        k = z_ref[:, hk + h * GLA_DK:hk + (h + 1) * GLA_DK]
        v = z_ref[:, vs]
        qt = (q * eq[:, ks]).astype(BF16)
        s_old = s_scr[h]
        att = jnp.where(tri, _mm(qt, (k * ek[:, ks]).astype(BF16), _NT), 0.0)
        vb = v.astype(BF16)
        o_inter = _mm(qt, s_old.astype(BF16))
        kv_new = _mm((k * ekl[:, ks]).astype(BF16), vb, _TN)
        yield
        o = o_inter + _mm(att.astype(BF16), vb)
        ecol = jnp.sum(jnp.where(r == cidx, jnp.broadcast_to(el[:, ks], (c, c)), 0.0), axis=1, keepdims=True)
        s_scr[h] = ecol * s_old + kv_new
        yield
        o_ref[:, h * GLA_DV:(h + 1) * GLA_DV] = _gla_finish(o, z_ref[:, gs], gn)


def _seq_group(bsz):
    return next(n for n in (4, 2, 1) if bsz % n == 0)


def _gla_prompt(z, zs, wa, ba, gn, s0, bsz):
    m = z.shape[0]
    t = m // bsz
    c = GLA_CHUNK
    assert t % c == 0 and GLA_DK == c
    nc = t // c
    hk = GLA_HEADS * GLA_DK
    nb = _seq_group(bsz)
    st = (nb, GLA_HEADS, GLA_DK, GLA_DV)
    tok = lambda n: pl.BlockSpec((nb, c, n), lambda b, i: (b, i, 0))
    o, s_fin = pl.pallas_call(
        _gla_body,
        grid=(bsz // nb, nc),
        in_specs=[tok(GLA_Z_W), tok(LANE), _resident((LANE, hk)), _resident((1, hk)), _resident((1, GLA_DV)),
                  pl.BlockSpec(st, lambda b, i: (b, 0, 0, 0))],
        out_specs=[tok(GLA_HEADS * GLA_DV), pl.BlockSpec(st, lambda b, i: (b, 0, 0, 0))],
        out_shape=[jax.ShapeDtypeStruct((bsz, t, GLA_HEADS * GLA_DV), F32),
                   jax.ShapeDtypeStruct((bsz, GLA_HEADS, GLA_DK, GLA_DV), F32)],
        scratch_shapes=[pltpu.VMEM(st, F32), pltpu.VMEM(st, F32)],
        compiler_params=_params(2),
        name="gla_prompt",
    )(z.reshape(bsz, t, -1), zs.reshape(bsz, t, -1), wa, ba, gn, s0)
    return o.reshape(m, -1), s_fin


def _gla_decode_body(z_ref, zs_ref, wa_ref, ba_ref, gn_ref, s_ref, o_ref, sn_ref):
    hk = GLA_HEADS * GLA_DK
    rows = SUBLANE
    z = jnp.broadcast_to(z_ref[0], (rows, GLA_Z_W))
    ga = jnp.broadcast_to(zs_ref[0], (rows, LANE))
    la = _logsigmoid(_mm_hp(ga, wa_ref[...]) + ba_ref[...]) * (1.0 / GLA_TAU)
    ea = jnp.exp(la)
    r, cidx = _iota((GLA_DK, GLA_DK), 0), _iota((GLA_DK, GLA_DK), 1)
    row0 = _iota((rows, GLA_DK), 0) == 0
    gn = gn_ref[...]
    for h in range(GLA_HEADS):
        ks = slice(h * GLA_DK, (h + 1) * GLA_DK)
        q = z[:, ks] * (GLA_DK ** -0.5)
        k = z[:, hk + h * GLA_DK:hk + (h + 1) * GLA_DK]
        v = z[:, 2 * hk + h * GLA_DV:2 * hk + (h + 1) * GLA_DV]
        gg = z[:, 2 * hk + GLA_HEADS * GLA_DV + h * GLA_DV:2 * hk + GLA_HEADS * GLA_DV + (h + 1) * GLA_DV]
        s_old = s_ref[0, h]
        o = _mm_hp(q * ea[:, ks], s_old) + jnp.sum(q * k, axis=1, keepdims=True) * v
        ecol = jnp.sum(jnp.where(r == cidx, jnp.broadcast_to(ea[0:1, ks], (GLA_DK, GLA_DK)), 0.0), axis=1, keepdims=True)
        sn_ref[0, h] = ecol * s_old + _mm_hp(jnp.where(row0, k, 0.0), v, _TN)
        o_ref[0, :, h * GLA_DV:(h + 1) * GLA_DV] = _gla_finish(o, gg, gn)[0:1]


def _gla_decode(z, zs, wa, ba, gn, s0):
    b = z.shape[0]
    hk = GLA_HEADS * GLA_DK
    st = (1, GLA_HEADS, GLA_DK, GLA_DV)
    o, sn = pl.pallas_call(
        _gla_decode_body,
        grid=(b,),
        in_specs=[pl.BlockSpec((1, 1, GLA_Z_W), lambda i: (i, 0, 0)), pl.BlockSpec((1, 1, LANE), lambda i: (i, 0, 0)),
                  _resident((LANE, hk)), _resident((1, hk)), _resident((1, GLA_DV)),
                  pl.BlockSpec(st, lambda i: (i, 0, 0, 0))],
        out_specs=[pl.BlockSpec((1, 1, GLA_HEADS * GLA_DV), lambda i: (i, 0, 0)), pl.BlockSpec(st, lambda i: (i, 0, 0, 0))],
        out_shape=[jax.ShapeDtypeStruct((b, 1, GLA_HEADS * GLA_DV), F32), jax.ShapeDtypeStruct((b,) + st[1:], F32)],
        compiler_params=_params(1),
        name="gla_decode",
    )(z.reshape(b, 1, -1), zs.reshape(b, 1, -1), wa, ba, gn, s0)
    return o.reshape(b, -1), sn


def _mlstm_finish(hh, og, mn):
    return _rms(hh, mn) * _sigmoid(og)


def _mlstm_body(z_ref, zs_ref, bias_ref, mn_ref, c0_ref, m0_ref, h_ref, cfin_ref, mfin_ref, c_scr, m_scr):
    t = pl.program_id(1)

    @pl.when(t == 0)
    def _():
        c_scr[...] = c0_ref[...]
        m_scr[...] = m0_ref[...]

    _round_robin([_mlstm_chunk(z_ref.at[bi], zs_ref.at[bi], bias_ref, mn_ref, h_ref.at[bi], c_scr.at[bi], m_scr.at[bi])
                  for bi in range(z_ref.shape[0])])

    @pl.when(t == pl.num_programs(1) - 1)
    def _():
        cfin_ref[...] = c_scr[...]
        mfin_ref[...] = m_scr[...]


def _mlstm_chunk(z_ref, zs_ref, bias_ref, mn_ref, h_ref, c_scr, m_scr):
    L = z_ref.shape[0]
    hq = M_HEADS * M_DQK
    g = zs_ref[...] + bias_ref[...]
    lane = _iota((L, LANE), 1)
    gx = jnp.where((lane >= M_HEADS) & (lane < 2 * M_HEADS), _logsigmoid(g), g)
    r, cidx = _iota((L, L), 0), _iota((L, L), 1)
    tri = r >= cidx
    cum_c = _mm_sel(jnp.where(tri, 1.0, 0.0).astype(BF16), gx)
    rows = _mm_sel(jnp.where(r == cidx, 1.0, 0.0).astype(BF16), gx, _TN, x_is_rhs=False)
    cum_r = _mm_sel(jnp.where(r <= cidx, 1.0, 0.0).astype(BF16), gx, _TN, x_is_rhs=False)
    yield
    ones_col = jnp.where(_iota((L, LANE), 1) == 0, 1.0, 0.0).astype(BF16)
    m_all = m_scr[...]
    m_new_all = m_all
    for h in range(M_HEADS):
        q = z_ref[:, h * M_DQK:(h + 1) * M_DQK].astype(BF16)
        k = z_ref[:, hq + h * M_DQK:hq + (h + 1) * M_DQK] * (M_DQK ** -0.5)
        v = z_ref[:, 2 * hq + h * M_DV:2 * hq + (h + 1) * M_DV]
        og = z_ref[:, 2 * hq + M_HEADS * M_DV + h * M_DV:2 * hq + M_HEADS * M_DV + (h + 1) * M_DV]
        va = jnp.concatenate([v.astype(BF16), ones_col], axis=1)
        ic_r, ic_c = rows[h:h + 1, :], gx[:, h:h + 1]
        cr, cc = cum_r[M_HEADS + h:M_HEADS + h + 1, :], cum_c[:, M_HEADS + h:M_HEADS + h + 1]
        m_old = m_all[0:1, h:h + 1]
        dlog = jnp.where(tri, cc - cr + ic_r, -jnp.inf)
        inter = cc + m_old
        mi = jnp.maximum(inter, jnp.max(dlog, axis=1, keepdims=True))
        w = jnp.exp(dlog - mi)
        wi = jnp.exp(inter - mi)
        qk = _mm(q, k.astype(BF16), _NT)
        c_old = c_scr[h]
        qc = _mm(q, c_old.astype(BF16))
        yield
        s = qk * w
        num = wi * qc + _mm(s.astype(BF16), va)
        yield
        qn = num[:, M_DV:M_DV + 1]
        hh = num[:, :M_DV] / jnp.maximum(jnp.abs(qn), jnp.exp(-mi))
        last = cc[L - 1:L, :]
        gl = last - cc + ic_c
        m_new = jnp.maximum(last + m_old, jnp.max(gl, axis=0, keepdims=True))
        wj = jnp.exp(gl - m_new)
        keep = jnp.exp(last + m_old - m_new)
        c_scr[h] = keep * c_old + _mm((wj * k).astype(BF16), va, _TN)
        yield
        m_new_all = jnp.where(_iota((1, LANE), 1) == h, m_new, m_new_all)
        h_ref[:, h * M_DV:(h + 1) * M_DV] = _mlstm_finish(hh, og, mn_ref[:, h * M_DV:(h + 1) * M_DV])
    m_scr[...] = m_new_all


def _mlstm_prompt(z, zs, bias, mn, c0, m0, bsz):
    m = z.shape[0]
    t = m // bsz
    L = next(n for n in (M_CHUNK_PALLAS, M_CHUNK) if t % n == 0)
    nc = t // L
    nb = _seq_group(bsz)
    st, ms = (nb, M_HEADS, M_DQK, M_AUG), (nb, 1, LANE)
    tok = lambda n: pl.BlockSpec((nb, L, n), lambda b, i: (b, i, 0))
    h, c_fin, m_fin = pl.pallas_call(
        _mlstm_body,
        grid=(bsz // nb, nc),
        in_specs=[tok(M_Z_W), tok(LANE), _resident((1, LANE)), _resident((1, M_HEADS * M_DV)),
                  pl.BlockSpec(st, lambda b, i: (b, 0, 0, 0)), pl.BlockSpec(ms, lambda b, i: (b, 0, 0))],
        out_specs=[tok(M_HEADS * M_DV), pl.BlockSpec(st, lambda b, i: (b, 0, 0, 0)), pl.BlockSpec(ms, lambda b, i: (b, 0, 0))],
        out_shape=[jax.ShapeDtypeStruct((bsz, t, M_HEADS * M_DV), F32), jax.ShapeDtypeStruct((bsz,) + st[1:], F32),
                   jax.ShapeDtypeStruct((bsz, 1, LANE), F32)],
        scratch_shapes=[pltpu.VMEM(st, F32), pltpu.VMEM(ms, F32)],
        compiler_params=_params(2),
        name="mlstm_prompt",
    )(z.reshape(bsz, t, -1), zs.reshape(bsz, t, -1), bias, mn, c0, m0)
    return h.reshape(m, -1), c_fin, m_fin


def _mlstm_decode_body(z_ref, zs_ref, bias_ref, mn_ref, c_ref, n_ref, m_ref, h_ref, cn_ref, nn_ref, mo_ref):
    hq = M_HEADS * M_DQK
    rows = SUBLANE
    z = jnp.broadcast_to(z_ref[0], (rows, M_Z_W))
    g = zs_ref[0] + bias_ref[...]
    m_all = m_ref[0]
    m_new_all = m_all
    row0 = _iota((rows, M_DQK), 0) == 0
    for h in range(M_HEADS):
        q = z[:, h * M_DQK:(h + 1) * M_DQK]
        k = z[:, hq + h * M_DQK:hq + (h + 1) * M_DQK] * (M_DQK ** -0.5)
        v = z[:, 2 * hq + h * M_DV:2 * hq + (h + 1) * M_DV]
        og = z[:, 2 * hq + M_HEADS * M_DV + h * M_DV:2 * hq + M_HEADS * M_DV + (h + 1) * M_DV]
        ic = g[:, h:h + 1]
        fl = _logsigmoid(g[:, M_HEADS + h:M_HEADS + h + 1])
        m_old = m_all[:, h:h + 1]
        c_old, n_old = c_ref[0, h], n_ref[0, h:h + 1, :]
        mi = jnp.maximum(fl + m_old, ic)
        w = jnp.exp(ic - mi)
        wi = jnp.exp(fl + m_old - mi)
        s = jnp.sum(q * k, axis=1, keepdims=True) * w
        num = wi * _mm_hp(q, c_old) + s * v
        qn = wi * jnp.sum(q * n_old, axis=1, keepdims=True) + s
        hh = num / jnp.maximum(jnp.abs(qn), jnp.exp(-mi))
        cn_ref[0, h] = wi * c_old + w * _mm_hp(jnp.where(row0, k, 0.0), v, _TN)
        nn_ref[0, h:h + 1, :] = wi * n_old + w * k[0:1]
        m_new_all = jnp.where(_iota((1, LANE), 1) == h, mi, m_new_all)
        h_ref[0, :, h * M_DV:(h + 1) * M_DV] = _mlstm_finish(hh, og, mn_ref[:, h * M_DV:(h + 1) * M_DV])[0:1]
    mo_ref[0] = m_new_all


def _mlstm_decode(z, zs, bias, mn, c0, n0, m0):
    b = z.shape[0]
    cs, ns = (1, M_HEADS, M_DQK, M_DV), (1, M_HEADS, M_DQK)
    m0p = jnp.pad(m0, ((0, 0), (0, LANE - M_HEADS))).reshape(b, 1, LANE)
    row3 = lambda n: pl.BlockSpec((1, 1, n), lambda i: (i, 0, 0))
    h, cn, nn, mo = pl.pallas_call(
        _mlstm_decode_body,
        grid=(b,),
        in_specs=[row3(M_Z_W), row3(LANE), _resident((1, LANE)), _resident((1, M_HEADS * M_DV)),
                  pl.BlockSpec(cs, lambda i: (i, 0, 0, 0)), pl.BlockSpec(ns, lambda i: (i, 0, 0)), row3(LANE)],
        out_specs=[row3(M_HEADS * M_DV), pl.BlockSpec(cs, lambda i: (i, 0, 0, 0)), pl.BlockSpec(ns, lambda i: (i, 0, 0)), row3(LANE)],
        out_shape=[jax.ShapeDtypeStruct((b, 1, M_HEADS * M_DV), F32), jax.ShapeDtypeStruct((b,) + cs[1:], F32),
                   jax.ShapeDtypeStruct((b,) + ns[1:], F32), jax.ShapeDtypeStruct((b, 1, LANE), F32)],
        compiler_params=_params(1),
        name="mlstm_decode",
    )(z.reshape(b, 1, -1), zs.reshape(b, 1, -1), bias, mn, c0, n0, m0p)
    return h.reshape(b, -1), cn, nn, mo[:, 0, :M_HEADS]


def _compress_dense_body(x_ref, pe_ref, w1_ref, w2_ref, o_ref, acc):
    kk = pl.program_id(1)

    @pl.when(kk == 0)
    def _():
        acc[...] = jnp.zeros_like(acc)

    acc[...] += _mm((x_ref[...] + pe_ref[...]).astype(BF16), w1_ref[...])

    @pl.when(kk == pl.num_programs(1) - 1)
    def _():
        o_ref[...] = _mm(_gelu(acc[...]).astype(BF16), w2_ref[...])


def _compress_dense(x, pe, w1, w2, tk=2048):
    r, kdim = x.shape
    tr = r if r <= 512 else 512
    assert r % tr == 0 and kdim % tk == 0
    return pl.pallas_call(
        _compress_dense_body,
        grid=(r // tr, kdim // tk),
        in_specs=[pl.BlockSpec((tr, tk), lambda i, k: (i, k)), pl.BlockSpec((1, tk), lambda i, k: (0, k)),
                  pl.BlockSpec((tk, NSA_KV_W), lambda i, k: (k, 0)), _resident(w2.shape)],
        out_specs=pl.BlockSpec((tr, NSA_KV_W), lambda i, k: (i, 0)),
        out_shape=jax.ShapeDtypeStruct((r, NSA_KV_W), F32),
        scratch_shapes=[pltpu.VMEM((tr, NSA_KV_W), F32)],
        compiler_params=_params(2),
        name="nsa_compress",
    )(x, pe, w1, w2)


_MAX_PAGES_PER_STEP = 32
_D_PER_STEP = 8


def _compress_paged_body(pt_ref, *refs, n_pages, pps):
    page_refs = refs[:pps]
    pe_ref, w1_ref, w2_ref, o_ref, xs = refs[pps:]
    g = pl.program_id(1)
    for p, pr in enumerate(page_refs):
        r0 = pl.multiple_of((g * pps + p) * NSA_KV_W, NSA_KV_W)
        xs[pl.ds(r0, NSA_KV_W), :] = pr[0]

    @pl.when(g == pl.num_programs(1) - 1)
    def _():
        kvd = NSA_KV_HEADS * NSA_DH
        for c in range(2):
            def step(j, acc):
                parts = []
                for dd in range(_D_PER_STEP):
                    d = j * _D_PER_STEP + dd
                    pe_row = pe_ref[pl.ds(c * NSA_DH + d, 1), :]
                    rows = [xs[pl.ds(c * kvd + k * NSA_DH + d, n_pages, stride=NSA_KV_W), :] + pe_row
                            for k in range(NSA_KV_HEADS)]
                    parts.append(jnp.concatenate(rows, axis=0).astype(BF16))
                return acc + _mm(jnp.concatenate(parts, axis=1), w1_ref[c, j])
            hid = lax.fori_loop(0, NSA_DH // _D_PER_STEP, step, jnp.zeros((NSA_KV_HEADS * n_pages, PAGE_SIZE), F32))
            out = _mm(_gelu(hid).astype(BF16), w2_ref[c])
            o_ref[c * NSA_KV_HEADS * n_pages:(c + 1) * NSA_KV_HEADS * n_pages, :] = out


def _compress_paged(pool_t, page_table, pe_t, w1_t, w2_t):
    b, n_pages = page_table.shape
    pps = next(n for n in range(min(_MAX_PAGES_PER_STEP, n_pages), 0, -1) if n_pages % n == 0)
    assert _BLOCKS_PER_PAGE == 2
    rows_out = 2 * NSA_KV_HEADS * n_pages

    def page_spec(p):
        return pl.BlockSpec((1, NSA_KV_W, PAGE_SIZE), lambda i, g, pt: (pt[i, g * pps + p], 0, 0))

    def fixed(a):
        nd = a.ndim
        return pl.BlockSpec(a.shape, lambda i, g, pt: (0,) * nd, pipeline_mode=pl.Buffered(1))

    grid_spec = pltpu.PrefetchScalarGridSpec(
        num_scalar_prefetch=1,
        grid=(b, n_pages // pps),
        in_specs=[page_spec(p) for p in range(pps)] + [fixed(pe_t), fixed(w1_t), fixed(w2_t)],
        out_specs=pl.BlockSpec((rows_out, PAGE_SIZE), lambda i, g, pt: (i, 0)),
        scratch_shapes=[pltpu.VMEM((n_pages * NSA_KV_W, PAGE_SIZE), F32)],
    )
    return pl.pallas_call(
        functools.partial(_compress_paged_body, n_pages=n_pages, pps=pps),
        grid_spec=grid_spec,
        out_shape=jax.ShapeDtypeStruct((b * rows_out, PAGE_SIZE), F32),
        compiler_params=_params(2, VMEM_BIG),
        name="nsa_compress_paged",
    )(page_table, *([pool_t] * pps), pe_t, w1_t, w2_t)


def _gate_rows(zs_ref, gb_ref):
    return _sigmoid(zs_ref[0] + gb_ref[...])


def _gate_row(gates, head, branch):
    r = GATE_OFF + head * NSA_BRANCHES + branch
    return gates[r:r + 1, :]


def _cmp_sel_body(q_ref, kvc_ref, zs_ref, gb_ref, ocmp_ref, nm_ref, sc_scr):
    qi = pl.program_id(1)
    tq = q_ref.shape[2]
    nbp = kvc_ref.shape[0]
    t0 = qi * tq
    tpos = t0 + _iota((nbp, tq), 1)
    blk = _iota((nbp, tq), 0)
    cur = jnp.right_shift(tpos, NSA_BLOCK.bit_length() - 1)
    vis = blk * NSA_BLOCK + (NSA_BLOCK - 1) <= tpos
    forced = (blk == 0) | (blk == cur) | (blk == cur - 1)
    allowed = blk <= cur
    gates = _gate_rows(zs_ref, gb_ref)
    for kh in range(NSA_KV_HEADS):
        kc = kvc_ref[:, kh * NSA_DH:(kh + 1) * NSA_DH].astype(BF16)
        vc = kvc_ref[:, NSA_KV_HEADS * NSA_DH + kh * NSA_DH:NSA_KV_HEADS * NSA_DH + (kh + 1) * NSA_DH].astype(BF16)
        probs = [None] * NSA_GROUP

        def head(gi):
            h = kh * NSA_GROUP + gi
            q = (q_ref[0, h * NSA_DH:(h + 1) * NSA_DH, :] * (NSA_DH ** -0.5)).astype(BF16)
            s = _mm(kc, q)
            yield
            s = jnp.where(vis, s, -jnp.inf)
            m = jnp.max(s, axis=0, keepdims=True)
            m = jnp.where(m == -jnp.inf, 0.0, m)
            e = jnp.exp(s - m)
            p = e / jnp.maximum(jnp.sum(e, axis=0, keepdims=True), TINY)
            probs[gi] = p
            o = _mm(vc, p.astype(BF16), _TN)
            yield
            ocmp_ref[0, h * NSA_DH:(h + 1) * NSA_DH, :] = o * _gate_row(gates, h, 0)

        _round_robin([head(gi) for gi in range(NSA_GROUP)])
        imp = probs[0]
        for p in probs[1:]:
            imp = imp + p
        score = jnp.where(forced, jnp.inf, jnp.where(allowed, imp, -jnp.inf))
        for lc in range(tq // LANE):
            ls = slice(lc * LANE, (lc + 1) * LANE)
            n_live = (t0 + (lc + 1) * LANE - 1) // NSA_BLOCK + 1

            def rank(rows):
                sc_scr[0:rows, :] = score[0:rows, ls]
                blk_rows = _iota((rows, LANE), 0)

                def rank_step(i, cnt):
                    for j in (2 * i, 2 * i + 1):
                        row = sc_scr[pl.ds(j, 1), :]
                        sc = sc_scr[0:rows, :]
                        tie = jnp.where(j < blk_rows, 1.0, 0.0)
                        cnt = cnt + jnp.where(row > sc, 1.0, jnp.where(row == sc, tie, 0.0))
                    return cnt

                cnt = lax.fori_loop(0, n_live // 2, rank_step, jnp.zeros((rows, LANE), F32))
                nm_ref[0, kh * nbp:kh * nbp + rows, ls] = jnp.where(
                    allowed[0:rows, ls], jnp.where(cnt < NSA_TOP_N, 0.0, -MASK_BIG), -MASK_BIG).astype(BF16)

            half = nbp // 2

            @pl.when(n_live <= half)
            def _():
                rank(half)
                nm_ref[0, kh * nbp + half:(kh + 1) * nbp, ls] = jnp.full((nbp - half, LANE), -MASK_BIG, BF16)

            @pl.when(n_live > half)
            def _():
                rank(nbp)


def _cmp_sel(q_t, kvc, zs_t, gb_col, nbp, tq):
    bsz, hd, t = q_t.shape
    tq = min(tq, t)
    nt = t // tq
    tok = lambda n: pl.BlockSpec((1, n, tq), lambda b, i: (b, 0, i))
    return pl.pallas_call(
        _cmp_sel_body,
        grid=(bsz, nt),
        in_specs=[tok(hd), pl.BlockSpec((nbp, NSA_KV_W), lambda b, i: (b, 0)), tok(LANE), _resident((LANE, 1))],
        out_specs=[tok(hd), tok(NSA_KV_HEADS * nbp)],
        out_shape=[jax.ShapeDtypeStruct((bsz, hd, t), F32), jax.ShapeDtypeStruct((bsz, NSA_KV_HEADS * nbp, t), BF16)],
        scratch_shapes=[pltpu.VMEM((nbp, LANE), F32)],
        compiler_params=_params(2),
        name="nsa_cmp_select",
    )(q_t, kvc, zs_t, gb_col)


def _sel_attn_body(q_ref, nm_ref, kv_ref, oh_ref, v_ref, zs_ref, gb_ref, o_ref, *, tk):
    qi = pl.program_id(1)
    tq = q_ref.shape[2]
    nbp = nm_ref.shape[1] // NSA_KV_HEADS
    cols = NSA_GROUP * tq
    t0 = qi * tq
    n_before = t0 // tk
    gates = _gate_rows(zs_ref, gb_ref)
    qas = []
    for kh in range(NSA_KV_HEADS):
        nm = nm_ref[0, kh * nbp:(kh + 1) * nbp, :]
        qas.append(jnp.concatenate(
            [jnp.concatenate([nm, (q_ref[0, h * NSA_DH:(h + 1) * NSA_DH, :] * (NSA_DH ** -0.5 * LOG2E)).astype(BF16)], axis=0)
             for h in range(kh * NSA_GROUP, (kh + 1) * NSA_GROUP)], axis=1))

    def tile(kh, kt, carry, causal, out):
        m_old, acc = carry
        k0 = pl.multiple_of(kt * tk, tk)
        ka = jnp.concatenate([oh_ref[pl.ds(k0, tk), :],
                              kv_ref[pl.ds(k0, tk), kh * NSA_DH:(kh + 1) * NSA_DH].astype(BF16)], axis=1)
        s = _mm(ka, qas[kh])
        yield
        if causal:
            kpos = k0 + _iota((tk, cols), 0)
            tpos = t0 + jnp.bitwise_and(_iota((tk, cols), 1), tq - 1)
            s = jnp.where(kpos <= tpos, s, -jnp.inf)
        m_new = jnp.maximum(m_old, jnp.max(s, axis=0, keepdims=True))
        alpha = jnp.exp2(m_old - m_new)
        p = jnp.exp2(s - m_new)
        yield
        out[kh] = (m_new, alpha * acc + _mm(v_ref[0, kh, :, pl.ds(k0, tk)], p.astype(BF16)))

    def tiles(kt, carries, causal):
        out = [None] * NSA_KV_HEADS
        _round_robin([tile(kh, kt, carries[kh], causal, out) for kh in range(NSA_KV_HEADS)])
        return tuple(out)

    init = (jnp.full((1, cols), -jnp.inf, F32), jnp.zeros((V_AUG, cols), F32))
    carries = lax.fori_loop(0, n_before, lambda kt, c: tiles(kt, c, False), (init,) * NSA_KV_HEADS)
    carries = tiles(n_before, carries, True)
    for kh in range(NSA_KV_HEADS):
        acc = carries[kh][1]
        o = acc[:NSA_DH] / acc[NSA_DH:NSA_DH + 1]
        for gi in range(NSA_GROUP):
            h = kh * NSA_GROUP + gi
            o_ref[0, h * NSA_DH:(h + 1) * NSA_DH, :] = o[:, gi * tq:(gi + 1) * tq] * _gate_row(gates, h, 1)


def _sel_attn(q_t, nm_t, kv, onehot, vs_t, zs_t, gb_col, tq, tk):
    bsz, hd, t = q_t.shape
    tq, tk = min(tq, t), min(tk, t)
    assert t % tq == 0 and t % tk == 0 and tk % tq == 0 and tq & (tq - 1) == 0
    nt = t // tq
    tok = lambda n: pl.BlockSpec((1, n, tq), lambda b, i: (b, 0, i))
    return pl.pallas_call(
        functools.partial(_sel_attn_body, tk=tk),
        grid=(bsz, nt),
        in_specs=[tok(hd), tok(nm_t.shape[1]), pl.BlockSpec((t, NSA_KV_W), lambda b, i: (b, 0)), _resident(onehot.shape),
                  pl.BlockSpec((1,) + vs_t.shape[1:], lambda b, i: (b, 0, 0, 0)), tok(LANE), _resident((LANE, 1))],
        out_specs=tok(hd),
        out_shape=jax.ShapeDtypeStruct((bsz, hd, t), F32),
        compiler_params=_params(2, VMEM_BIG),
        name="nsa_selected",
    )(q_t, nm_t, kv, onehot, vs_t, zs_t, gb_col)


def _win_attn_body(q_ref, k_ref, v_ref, zs_ref, gb_ref, o_ref, bias_scr, *, span):
    qi = pl.program_id(1)
    tq = q_ref.shape[2]
    cols = NSA_GROUP * tq
    t0 = qi * tq
    start = pl.multiple_of(jnp.maximum(t0 + tq - span, 0), tq)
    gates = _gate_rows(zs_ref, gb_ref)

    @pl.when(t0 + tq - span <= 0)
    def _():
        kpos = start + _iota((span, cols), 0)
        tpos = t0 + jnp.bitwise_and(_iota((span, cols), 1), tq - 1)
        bias_scr[...] = jnp.where((kpos <= tpos) & (tpos - kpos < NSA_WINDOW), 0.0, -jnp.inf)

    for kh in range(NSA_KV_HEADS):
        qa = jnp.concatenate([(q_ref[0, h * NSA_DH:(h + 1) * NSA_DH, :] * (NSA_DH ** -0.5 * LOG2E)).astype(BF16)
                              for h in range(kh * NSA_GROUP, (kh + 1) * NSA_GROUP)], axis=1)
        kw = k_ref[pl.ds(start, span), kh * NSA_DH:(kh + 1) * NSA_DH].astype(BF16)
        s = jnp.where(bias_scr[...] == 0.0, _mm(kw, qa), -jnp.inf)
        e = jnp.exp2(s - jnp.max(s, axis=0, keepdims=True))
        oa = _mm(v_ref[0, kh, :, pl.ds(start, span)], e.astype(BF16))
        o = oa[:NSA_DH] / oa[NSA_DH:NSA_DH + 1]
        for gi in range(NSA_GROUP):
            h = kh * NSA_GROUP + gi
            o_ref[0, h * NSA_DH:(h + 1) * NSA_DH, :] = o[:, gi * tq:(gi + 1) * tq] * _gate_row(gates, h, 2)


def _win_attn(q_t, kv, vw_t, zs_t, gb_col, tq):
    bsz, hd, t = q_t.shape
    tq = min(tq, t)
    span = min(NSA_WINDOW + tq, t)
    assert t % tq == 0 and tq & (tq - 1) == 0
    nt = t // tq
    tok = lambda n: pl.BlockSpec((1, n, tq), lambda b, i: (b, 0, i))
    whole = lambda a: pl.BlockSpec((1,) + a.shape[1:], lambda b, i: (b, 0, 0, 0))
    return pl.pallas_call(
        functools.partial(_win_attn_body, span=span),
        grid=(bsz, nt),
        in_specs=[tok(hd), pl.BlockSpec((t, NSA_KV_W), lambda b, i: (b, 0)), whole(vw_t), tok(LANE), _resident((LANE, 1))],
        out_specs=tok(hd),
        out_shape=jax.ShapeDtypeStruct((bsz, hd, t), F32),
        scratch_shapes=[pltpu.VMEM((span, NSA_GROUP * tq), F32)],
        compiler_params=_params(2),
        name="nsa_window",
    )(q_t, kv, vw_t, zs_t, gb_col)


def _decode_forced(n_past_blk):
    cur = n_past_blk
    return sorted({0, cur - 1, cur} - {-1})


def _cmp_decode_body(q_ref, kvc_ref, gz_ref, gb_ref, o_ref, idx_ref, *, past, n_pick):
    n_pages = kvc_ref.shape[0] // (2 * NSA_KV_HEADS)
    nb = _BLOCKS_PER_PAGE * n_pages

    def block_id(shape):
        pos = _iota(shape, 1)
        page = jnp.where(pos >= n_pages, pos - n_pages, pos)
        return page * _BLOCKS_PER_PAGE + jnp.where(pos >= n_pages, 1, 0)

    lane = block_id((NSA_HEADS, nb))
    hrow = _iota((NSA_HEADS, nb), 0)
    vis = lane * NSA_BLOCK + (NSA_BLOCK - 1) <= past
    q = (q_ref[0] * (NSA_DH ** -0.5)).astype(BF16)
    gates = _sigmoid(gz_ref[0] + gb_ref[...])
    cur = past // NSA_BLOCK
    o_all = jnp.zeros((NSA_HEADS, NSA_DH), F32)
    idx_all = jnp.zeros((SUBLANE, LANE), F32)
    orow = _iota((NSA_HEADS, NSA_DH), 0)
    slot_r, slot_c = _iota((SUBLANE, LANE), 0), _iota((SUBLANE, LANE), 1)
    l1 = block_id((1, nb))
    l1f = l1.astype(F32)
    forced = (l1 == 0) | (l1 == cur) | (l1 == cur - 1)
    for kh in range(NSA_KV_HEADS):
        kc = kvc_ref[kh * n_pages:(kh + 1) * n_pages, :].astype(BF16)
        vc = kvc_ref[(NSA_KV_HEADS + kh) * n_pages:(NSA_KV_HEADS + kh + 1) * n_pages, :].astype(BF16)
        s = jnp.concatenate([_mm(q, kc[:, j * NSA_DH:(j + 1) * NSA_DH], _NT) for j in range(_BLOCKS_PER_PAGE)], axis=1)
        s = jnp.where(vis, s, -jnp.inf)
        m = jnp.max(s, axis=1, keepdims=True)
        m = jnp.where(m == -jnp.inf, 0.0, m)
        e = jnp.exp(s - m)
        p = e / jnp.maximum(jnp.sum(e, axis=1, keepdims=True), TINY)
        mine = (hrow >= kh * NSA_GROUP) & (hrow < (kh + 1) * NSA_GROUP)
        pb = p.astype(BF16)
        o_kh = _mm(pb[:, :n_pages], vc[:, :NSA_DH])
        for j in range(1, _BLOCKS_PER_PAGE):
            o_kh = o_kh + _mm(pb[:, j * n_pages:(j + 1) * n_pages], vc[:, j * NSA_DH:(j + 1) * NSA_DH])
        o_all = jnp.where((orow >= kh * NSA_GROUP) & (orow < (kh + 1) * NSA_GROUP), o_kh, o_all)
        imp = jnp.sum(jnp.where(mine, p, 0.0), axis=0, keepdims=True)
        score = jnp.where(forced, -jnp.inf, imp)
        for r in range(n_pick):
            mx = jnp.max(score, axis=1, keepdims=True)
            pick = jnp.min(jnp.where(score == mx, l1f, float(nb)), axis=1, keepdims=True)
            score = jnp.where(l1f == pick, -jnp.inf, score)
            idx_all = jnp.where((slot_r == kh) & (slot_c == r), pick, idx_all)
    o_ref[0] = o_all * gates[:, 0:1]
    idx_ref[0] = idx_all.astype(jnp.int32)


def _cmp_decode(qh, kvc, gz, gb3, past, n_pick):
    b = qh.shape[0]
    rows = kvc.shape[0] // b
    return pl.pallas_call(
        functools.partial(_cmp_decode_body, past=past, n_pick=n_pick),
        grid=(b,),
        in_specs=[pl.BlockSpec((1, NSA_HEADS, NSA_DH), lambda i: (i, 0, 0)), pl.BlockSpec((rows, kvc.shape[1]), lambda i: (i, 0)),
                  pl.BlockSpec((1, NSA_HEADS, NSA_BRANCHES), lambda i: (i, 0, 0)), _resident((NSA_HEADS, NSA_BRANCHES))],
        out_specs=[pl.BlockSpec((1, NSA_HEADS, NSA_DH), lambda i: (i, 0, 0)), pl.BlockSpec((1, SUBLANE, LANE), lambda i: (i, 0, 0))],
        out_shape=[jax.ShapeDtypeStruct((b, NSA_HEADS, NSA_DH), F32), jax.ShapeDtypeStruct((b, SUBLANE, LANE), jnp.int32)],
        compiler_params=_params(1),
        name="nsa_cmp_decode",
    )(qh, kvc, gz, gb3)


def _selwin_decode_body(info_ref, q_ref, ns_ref, nw_ref, win_ref, *refs, n_shared, n_own, first_win_row):
    n_blk = n_shared + NSA_KV_HEADS * n_own
    page_refs = refs[:n_blk]
    gz_ref, gb_ref, o_ref, wout_ref = refs[n_blk:]
    i = pl.program_id(0)
    q = q_ref[0] * (NSA_DH ** -0.5)
    qb = q.astype(BF16)
    gates = _sigmoid(gz_ref[0] + gb_ref[...])
    hrow = _iota((NSA_HEADS, NSA_DH), 0)
    n_win = win_ref.shape[2]
    wlane = _iota((NSA_HEADS, n_win), 1)
    pblk = jnp.right_shift(_iota((NSA_HEADS, PAGE_SIZE), 1), NSA_BLOCK.bit_length() - 1)
    o_all = jnp.zeros((NSA_HEADS, NSA_DH), F32)
    voff = NSA_KV_HEADS * NSA_DH
    for kh in range(NSA_KV_HEADS):
        ksl = slice(kh * NSA_DH, (kh + 1) * NSA_DH)
        vsl = slice(voff + kh * NSA_DH, voff + (kh + 1) * NSA_DH)
        slots = list(range(n_shared)) + list(range(n_shared + kh * n_own, n_shared + (kh + 1) * n_own))
        s_parts = []
        for sl in slots:
            blk_in_page = jnp.bitwise_and(info_ref[i, sl], _BLOCKS_PER_PAGE - 1)
            sp = _mm(qb, page_refs[sl][0, ksl, :].astype(BF16))
            s_parts.append(jnp.where(pblk == blk_in_page, sp, -jnp.inf))
        kn, vn = ns_ref[0, :, ksl], ns_ref[0, :, vsl]
        sn = jnp.sum(q * kn, axis=1, keepdims=True)
        m = sn
        for sp in s_parts:
            m = jnp.maximum(m, jnp.max(sp, axis=1, keepdims=True))
        en = jnp.exp(sn - m)
        l, acc = en, en * vn
        for sl, sp in zip(slots, s_parts):
            e = jnp.exp(sp - m)
            l = l + jnp.sum(e, axis=1, keepdims=True)
            acc = acc + _mm(e.astype(BF16), page_refs[sl][0, vsl, :].astype(BF16), _NT)
        o_sel = acc / l
        sw = jnp.where(wlane >= first_win_row, _mm(qb, win_ref[0, ksl, :].astype(BF16)), -jnp.inf)
        kwn, vwn = nw_ref[0, :, ksl], nw_ref[0, :, vsl]
        swn = jnp.sum(q * kwn, axis=1, keepdims=True)
        mw = jnp.maximum(jnp.max(sw, axis=1, keepdims=True), swn)
        ew, ewn = jnp.exp(sw - mw), jnp.exp(swn - mw)
        o_win = (_mm(ew.astype(BF16), win_ref[0, vsl, :].astype(BF16), _NT) + ewn * vwn) / (jnp.sum(ew, axis=1, keepdims=True) + ewn)
        keep = (hrow >= kh * NSA_GROUP) & (hrow < (kh + 1) * NSA_GROUP)
        o_all = jnp.where(keep, gates[:, 1:2] * o_sel + gates[:, 2:3] * o_win, o_all)
    o_ref[0] = o_all
    r2, c2 = _iota((NSA_KV_W, NSA_KV_W), 0), _iota((NSA_KV_W, NSA_KV_W), 1)
    new_col = jnp.sum(jnp.where(r2 == c2, jnp.broadcast_to(nw_ref[0], (NSA_KV_W, NSA_KV_W)), 0.0), axis=1, keepdims=True)
    shifted = pltpu.roll(win_ref[0], n_win - 1, 1)
    wout_ref[0] = jnp.where(_iota((NSA_KV_W, n_win), 1) == n_win - 1, new_col, shifted)


def _selwin_decode(info, qh, new_s, new_w, win_t, pool_t, gz, gb3, n_shared, n_own, first_win_row):
    b = qh.shape[0]
    n_blk = info.shape[1]
    row3 = lambda n: pl.BlockSpec((1, 1, n), lambda i, r: (i, 0, 0))

    def page_spec(s):
        return pl.BlockSpec((1, NSA_KV_W, PAGE_SIZE), lambda i, r: (jnp.right_shift(r[i, s], _PAGE_SHIFT), 0, 0))

    win_spec = pl.BlockSpec((1,) + win_t.shape[1:], lambda i, r: (i, 0, 0))
    grid_spec = pltpu.PrefetchScalarGridSpec(
        num_scalar_prefetch=1,
        grid=(b,),
        in_specs=[pl.BlockSpec((1, NSA_HEADS, NSA_DH), lambda i, r: (i, 0, 0)), row3(NSA_KV_W), row3(NSA_KV_W), win_spec]
        + [page_spec(s) for s in range(n_blk)]
        + [pl.BlockSpec((1, NSA_HEADS, NSA_BRANCHES), lambda i, r: (i, 0, 0)),
           pl.BlockSpec((NSA_HEADS, NSA_BRANCHES), lambda i, r: (0, 0))],
        out_specs=[pl.BlockSpec((1, NSA_HEADS, NSA_DH), lambda i, r: (i, 0, 0)), win_spec],
    )
    return pl.pallas_call(
        functools.partial(_selwin_decode_body, n_shared=n_shared, n_own=n_own, first_win_row=first_win_row),
        grid_spec=grid_spec,
        out_shape=[jax.ShapeDtypeStruct((b, NSA_HEADS, NSA_DH), F32), jax.ShapeDtypeStruct(win_t.shape, F32)],
        compiler_params=_params(1),
        name="nsa_selwin_decode",
    )(info, qh, new_s.reshape(b, 1, -1), new_w.reshape(b, 1, -1), win_t, *([pool_t] * n_blk), gz, gb3)


def _feature_major(cache):
    n, rows = cache.shape[:2]
    return jnp.transpose(cache, (0, 2, 3, 4, 1)).reshape(n, NSA_KV_W, rows)


def _pad_cols(w, n):
    return jnp.pad(w, ((0, 0), (0, n - w.shape[1])))


def _even_w_in(w):
    sizes = (GLA_HEADS * GLA_DK, GLA_HEADS * GLA_DK, GLA_HEADS * GLA_DV, GLA_HEADS * GLA_DV, GLA_RANK,
             NSA_HEADS * NSA_DH, NSA_BRANCHES * NSA_KV_W, NSA_HEADS * NSA_BRANCHES)
    cuts = [0]
    for s in sizes:
        cuts.append(cuts[-1] + s)
    gq, gk, gv, gg, ga, nq, nkv, ng = (w[:, cuts[i]:cuts[i + 1]] for i in range(len(sizes)))
    small = _pad_cols(jnp.concatenate([ga, ng], axis=1), LANE)
    sample = jnp.concatenate([gq, gk, gv, gg, nq, nkv, small], axis=1).astype(BF16)
    prompt = jnp.concatenate([gq, gk, gv, gg, nkv, small], axis=1).astype(BF16)
    prompt_t = jnp.concatenate([nq, nkv, small], axis=1).T.astype(BF16)
    return sample, prompt, prompt_t


_EVEN_WIDTHS = (GLA_Z_W, NSA_HEADS * NSA_DH, NSA_KV_W, NSA_KV_W, NSA_KV_W, LANE)
_EVEN_WIDTHS_P = (GLA_Z_W, NSA_KV_W, NSA_KV_W, NSA_KV_W, LANE)
_EVEN_WIDTHS_PT = (NSA_HEADS * NSA_DH, NSA_KV_W, NSA_KV_W, NSA_KV_W, LANE)


def _odd_w_in(w):
    main = M_Z_W
    return jnp.concatenate([w[:, :main], _pad_cols(w[:, main:], LANE)], axis=1).astype(BF16)


_ODD_WIDTHS = (M_Z_W, LANE)


def _compress_weights(pe, w1, w2):
    eye_k = jnp.eye(NSA_KV_HEADS, dtype=F32)
    eye_c = jnp.eye(2, dtype=F32)
    n_ck = 2 * NSA_KV_HEADS
    rows = jnp.broadcast_to(w1.transpose(1, 0, 2, 3)[:, :, None], (NSA_BLOCK, 2, NSA_KV_HEADS, NSA_DH, NSA_CMP_HID))
    rows = jnp.tile(rows.reshape(NSA_BLOCK * NSA_KV_W, NSA_CMP_HID), (1, n_ck))
    row_ck = (jnp.arange(NSA_BLOCK * NSA_KV_W) // NSA_DH) % n_ck
    col_ck = jnp.arange(NSA_KV_W) // NSA_CMP_HID
    w1big = jnp.where(row_ck[:, None] == col_ck[None, :], rows, 0.0)
    w2big = jnp.einsum("ced,cx,ky->ckexyd", w2, eye_c, eye_k).reshape(NSA_KV_W, NSA_KV_W)
    pe_flat = jnp.broadcast_to(pe.transpose(1, 0, 2)[:, :, None, :], (NSA_BLOCK, 2, NSA_KV_HEADS, NSA_DH)).reshape(1, -1)
    return pe_flat, w1big.astype(BF16), w2big.astype(BF16)


def _compress_weights_paged(pe, w1, w2):
    eye_b = jnp.eye(_BLOCKS_PER_PAGE, dtype=F32)
    pe_t = jnp.tile(pe.transpose(0, 2, 1), (1, 1, _BLOCKS_PER_PAGE)).reshape(2 * NSA_DH, PAGE_SIZE)
    rows = jnp.broadcast_to(w1.transpose(0, 2, 1, 3)[:, :, None], (2, NSA_DH, _BLOCKS_PER_PAGE, NSA_BLOCK, NSA_CMP_HID))
    rows = jnp.tile(rows.reshape(2, NSA_DH // _D_PER_STEP, _D_PER_STEP * PAGE_SIZE, NSA_CMP_HID), (1, 1, 1, _BLOCKS_PER_PAGE))
    row_blk = (jnp.arange(_D_PER_STEP * PAGE_SIZE) // NSA_BLOCK) % _BLOCKS_PER_PAGE
    col_blk = jnp.arange(_BLOCKS_PER_PAGE * NSA_CMP_HID) // NSA_CMP_HID
    w1_t = jnp.where(row_blk[:, None] == col_blk[None, :], rows, 0.0)
    w2_t = jnp.einsum("ced,hx->chexd", w2, eye_b).reshape(2, _BLOCKS_PER_PAGE * NSA_CMP_HID, _BLOCKS_PER_PAGE * NSA_DH)
    return pe_t, w1_t.astype(BF16), w2_t.astype(BF16)


def _gate_bias_row(gb):
    return jnp.pad(gb, (GATE_OFF, LANE - GATE_OFF - gb.shape[0])).reshape(1, LANE)


def _split_outs(outs, n):
    return outs[:n], outs[n:]


def _values_t(kv_t, rows=NSA_DH):
    b, _, t = kv_t.shape
    v = kv_t.reshape(b, 2, NSA_KV_HEADS, NSA_DH, t)[:, 1].astype(BF16)
    if rows > NSA_DH:
        extra = jnp.zeros((b, NSA_KV_HEADS, rows - NSA_DH, t), BF16).at[:, :, 0].set(1.0)
        v = jnp.concatenate([v, extra], axis=2)
    return v


def _kv_rows(kv_t):
    b, _, t = kv_t.shape
    return jnp.transpose(kv_t.reshape(b, 2, NSA_KV_HEADS, NSA_DH, t), (0, 4, 1, 2, 3))


def _heads_major(kv, which):
    b, t, _ = kv.shape
    return kv.reshape(b, t, 2, NSA_KV_HEADS, NSA_DH)[:, :, which].transpose(0, 2, 1, 3).astype(BF16)


def _even_layer_prompt(x, bsz, g, prm):
    m = x.shape[0]
    t = m // bsz
    (z, kvc, kvs, kvw, zs), (q_t, kvc_t, kvs_t, kvw_t, zs_t) = _split_outs(
        _norm_proj(x, g, prm["w_in_p"], _EVEN_WIDTHS_P, TM_PROJ, prm["w_in_pt"], _EVEN_WIDTHS_PT, bsz), len(_EVEN_WIDTHS_P))
    o_gla, s_fin = _gla_prompt(z, zs, prm["wa"], prm["ba"], prm["gn"],
                               jnp.zeros((bsz, GLA_HEADS, GLA_DK, GLA_DV), F32), bsz)
    nb = t // NSA_BLOCK
    nbp = -(-nb // LANE) * LANE
    kvcmp = _compress_dense(kvc.reshape(bsz * nb, NSA_BLOCK * NSA_KV_W), prm["pe"], prm["w1"], prm["w2"])
    kvcmp = jnp.pad(kvcmp.reshape(bsz, nb, NSA_KV_W), ((0, 0), (0, nbp - nb), (0, 0))).reshape(bsz * nbp, NSA_KV_W)
    o_cmp, nm = _cmp_sel(q_t, kvcmp, zs_t, prm["gb_col"], nbp, TQ_CMP)
    onehot = (jnp.arange(t)[:, None] // NSA_BLOCK == jnp.arange(nbp)[None, :]).astype(BF16)
    o_sel = _sel_attn(q_t, nm, kvs, onehot, _values_t(kvs_t, V_AUG), zs_t, prm["gb_col"], TQ_ATTN, TK_SEL)
    o_win = _win_attn(q_t, kvw, _values_t(kvw_t, V_AUG), zs_t, prm["gb_col"], TQ_ATTN)
    y = _out_proj(x, [o_gla], (1,), prm["w_out"], TM_OUT, [o_cmp, o_sel, o_win], bsz)
    n_keep = min(NSA_WINDOW, t)
    return y, s_fin, _kv_rows(kvc_t), _kv_rows(kvs_t), _kv_rows(kvw_t[:, :, t - n_keep:])


def _even_layer_sample(x, g, prm, gla_state, cmp_pool, sel_pool, win_buf, page_table):
    b = x.shape[0]
    n_pages = page_table.shape[1]
    past = n_pages * PAGE_SIZE
    n_past_blk = past // NSA_BLOCK
    z, nq, kvc, kvs, kvw, zs = _norm_proj(x, g, prm["w_in"], _EVEN_WIDTHS, b)
    o_gla, s_new = _gla_decode(z, zs, prm["wa"], prm["ba"], prm["gn"], gla_state)
    kvcmp = _compress_paged(_feature_major(cmp_pool), page_table, prm["pe_t"], prm["w1_t"], prm["w2_t"])
    qh = nq.reshape(b, NSA_HEADS, NSA_DH)
    gz = zs[:, GATE_OFF:GATE_OFF + NSA_HEADS * NSA_BRANCHES].reshape(b, NSA_HEADS, NSA_BRANCHES)
    forced = _decode_forced(n_past_blk)
    n_pick = NSA_TOP_N - len(forced)
    assert n_past_blk - len(forced) + 1 >= n_pick
    o_cmp, idx = _cmp_decode(qh, kvcmp, gz, prm["gb3"], past, n_pick)
    shared = [f for f in forced if f < n_past_blk]
    logical = jnp.concatenate([jnp.broadcast_to(jnp.asarray(shared, jnp.int32), (b, len(shared))),
                               idx[:, :NSA_KV_HEADS, :n_pick].reshape(b, NSA_KV_HEADS * n_pick)], axis=1)
    per_page = _BLOCKS_PER_PAGE
    phys = jnp.take_along_axis(page_table, logical // per_page, axis=1) * per_page + logical % per_page
    win_keep = win_buf.shape[1]
    assert win_keep >= 1
    first_win_row = max(win_keep - NSA_WINDOW + 1, 0)
    o_sw, win_new_t = _selwin_decode(phys.astype(jnp.int32), qh, kvs, kvw, _feature_major(win_buf), _feature_major(sel_pool),
                                     gz, prm["gb3"], len(shared), n_pick, first_win_row)
    hd = NSA_HEADS * NSA_DH
    y = _out_proj(x, [o_gla, o_cmp.reshape(b, hd), o_sw.reshape(b, hd)], (1, 2), prm["w_out"], b)
    kv_shape = (b, 1, 2, NSA_KV_HEADS, NSA_DH)
    win_new = jnp.transpose(win_new_t.reshape(b, 2, NSA_KV_HEADS, NSA_DH, win_keep), (0, 4, 1, 2, 3))
    return y, s_new, kvc.reshape(kv_shape), kvs.reshape(kv_shape), win_new


def _odd_layer_prompt(x, bsz, g, prm):
    z, zs = _norm_proj(x, g, prm["w_in"], _ODD_WIDTHS, TM_PROJ)
    h, c_aug, m_fin = _mlstm_prompt(z, zs, prm["bias"], prm["mn"], jnp.zeros((bsz, M_HEADS, M_DQK, M_AUG), F32),
                                    jnp.zeros((bsz, 1, LANE), F32), bsz)
    y = _out_proj(x, [h], (1,), prm["w_out"], TM_OUT)
    return y, c_aug[..., :M_DV], c_aug[..., M_DV], m_fin[:, 0, :M_HEADS]


def _odd_layer_sample(x, g, prm, c0, n0, m0):
    b = x.shape[0]
    z, zs = _norm_proj(x, g, prm["w_in"], _ODD_WIDTHS, b)
    h, cn, nn, mn = _mlstm_decode(z, zs, prm["bias"], prm["mn"], c0, n0, m0)
    return _out_proj(x, [h], (1,), prm["w_out"], b), cn, nn, mn


def kernel(x_prompt, x_sample, cache_cmp_kv, cache_sel_kv, cache_win_kv, state_gla, state_mlstm_c, state_mlstm_n, state_mlstm_m, state_ffn_conv, page_table, norm_mix, norm_ffn, norm_final, even_w_in, even_w_out, gla_w_a2, gla_b_a, gla_norm, nsa_cmp_pe, nsa_cmp_w1, nsa_cmp_w2, nsa_gate_b, odd_w_in, odd_w_out, mlstm_b_i, mlstm_b_f, mlstm_norm, ffn_w_up, ffn_conv_w, ffn_conv_b, ffn_w_down):
    bp, t, d = x_prompt.shape
    bs = x_sample.shape[0]
    assert x_sample.shape[1] == 1
    depth = norm_mix.shape[0]
    f = ffn_conv_w.shape[2]
    xp = x_prompt.reshape(bp * t, d)
    xs = x_sample.reshape(bs, d)
    outs = {k: [] for k in ("cmp_p", "cmp_s", "sel_p", "sel_s", "win_p", "win_s", "gla_p", "gla_s",
                            "mc_p", "mc_s", "mn_p", "mn_s", "mm_p", "mm_s", "cv_p", "cv_s")}
    for l in range(depth):
        if l % 2 == 0:
            e = l // 2
            pe, w1, w2 = _compress_weights(nsa_cmp_pe[e], nsa_cmp_w1[e], nsa_cmp_w2[e])
            pe_t, w1_t, w2_t = _compress_weights_paged(nsa_cmp_pe[e], nsa_cmp_w1[e], nsa_cmp_w2[e])
            w_in_s, w_in_p, w_in_pt = _even_w_in(even_w_in[e])
            prm = dict(pe_t=pe_t, w1_t=w1_t, w2_t=w2_t, w_in=w_in_s, w_in_p=w_in_p, w_in_pt=w_in_pt,
                       w_out=even_w_out[e].astype(BF16),
                       wa=jnp.pad(gla_w_a2[e], ((0, LANE - GLA_RANK), (0, 0))), ba=gla_b_a[e].reshape(1, -1),
                       gn=gla_norm[e].reshape(1, -1), pe=pe, w1=w1, w2=w2, gb_col=_gate_bias_row(nsa_gate_b[e]).reshape(LANE, 1),
                       gb3=nsa_gate_b[e].reshape(NSA_HEADS, NSA_BRANCHES))
            xp, s_, c_, k_, w_ = _even_layer_prompt(xp, bp, norm_mix[l], prm)
            outs["gla_p"].append(s_); outs["cmp_p"].append(c_); outs["sel_p"].append(k_); outs["win_p"].append(w_)
            xs, s_, c_, k_, w_ = _even_layer_sample(xs, norm_mix[l], prm, state_gla[e], cache_cmp_kv[e], cache_sel_kv[e],
                                                    cache_win_kv[e], page_table)
            outs["gla_s"].append(s_); outs["cmp_s"].append(c_); outs["sel_s"].append(k_); outs["win_s"].append(w_)
        else:
            o = l // 2
            bias = jnp.pad(jnp.concatenate([mlstm_b_i[o], mlstm_b_f[o]]), (0, LANE - 2 * M_HEADS)).reshape(1, LANE)
            prm = dict(w_in=_odd_w_in(odd_w_in[o]), w_out=odd_w_out[o].astype(BF16), bias=bias, mn=mlstm_norm[o].reshape(1, -1))
            xp, c_, n_, m_ = _odd_layer_prompt(xp, bp, norm_mix[l], prm)
            outs["mc_p"].append(c_); outs["mn_p"].append(n_); outs["mm_p"].append(m_)
            xs, c_, n_, m_ = _odd_layer_sample(xs, norm_mix[l], prm, state_mlstm_c[o], state_mlstm_n[o], state_mlstm_m[o])
            outs["mc_s"].append(c_); outs["mn_s"].append(n_); outs["mm_s"].append(m_)
        final = l == depth - 1
        wup, wd = ffn_w_up[l].astype(BF16), ffn_w_down[l].astype(BF16)
        xp, cv = _ffn_prompt(xp, bp, norm_ffn[l], wup, ffn_conv_w[l], ffn_conv_b[l], wd,
                             jnp.zeros((bp, CONV_W - 1, f), F32), norm_final, final, TM_FFN)
        outs["cv_p"].append(cv)
        xs, cv = _ffn_decode(xs, norm_ffn[l], wup, ffn_conv_w[l], ffn_conv_b[l], wd, state_ffn_conv[l], norm_final, final)
        outs["cv_s"].append(cv)
    st = lambda k: jnp.stack(outs[k])
    return (xp.reshape(bp, t, d), xs.reshape(bs, 1, d),
            st("cmp_p"), st("cmp_s"), st("sel_p"), st("sel_s"), st("win_p"), st("win_s"), st("gla_p"), st("gla_s"),
            st("mc_p"), st("mc_s"), st("mn_p"), st("mn_s"), st("mm_p"), st("mm_s"), st("cv_p"), st("cv_s"))
```

```python
import functools

import jax
import jax.numpy as jnp
from jax import lax
from jax.experimental import pallas as pl
from jax.experimental.pallas import tpu as pltpu

F32 = jnp.float32
BF16 = jnp.bfloat16

GLA_HEADS, GLA_DK, GLA_DV, GLA_RANK, GLA_TAU, GLA_CHUNK = 4, 64, 128, 16, 16.0, 64
NSA_HEADS, NSA_KV_HEADS, NSA_GROUP, NSA_DH = 8, 2, 4, 64
NSA_BRANCHES, NSA_BLOCK, NSA_TOP_N, NSA_WINDOW, NSA_CMP_HID = 3, 64, 16, 512, 64
M_HEADS, M_DQK, M_DV, M_CHUNK = 4, 128, 256, 64
M_CHUNK_PALLAS = 128
CONV_W = 3
PAGE_SIZE = 128
EPS, TINY = 1e-6, 1e-30
LOG2E = 1.4426950408889634

LANE = 128
SUBLANE = 8
VMEM_BIG = 52 * 1024 * 1024
VMEM_MID = 40 * 1024 * 1024

MASK_BIG = 32768.0

NSA_KV_W = 2 * NSA_KV_HEADS * NSA_DH
GLA_Z_W = 2 * GLA_HEADS * GLA_DK + 2 * GLA_HEADS * GLA_DV
M_Z_W = 2 * M_HEADS * M_DQK + 2 * M_HEADS * M_DV
M_AUG = M_DV + LANE
GATE_OFF = GLA_RANK
TM_PROJ = 256
TM_OUT = 512
TM_FFN = 256
TQ_CMP = 512
TQ_ATTN = 128
TK_SEL = 512
V_AUG = NSA_DH + 16
_BLOCKS_PER_PAGE = PAGE_SIZE // NSA_BLOCK
_PAGE_SHIFT = _BLOCKS_PER_PAGE.bit_length() - 1

_NN = (((1,), (0,)), ((), ()))
_NT = (((1,), (1,)), ((), ()))
_TN = (((0,), (0,)), ((), ()))


def _mm(a, b, dims=_NN):
    return lax.dot_general(a, b, dims, preferred_element_type=F32)


def _split_bf16(x, n):
    parts, r = [], x
    for _ in range(n):
        p = r.astype(BF16)
        parts.append(p)
        r = r - p.astype(F32)
    return parts


def _mm_sel(sel, x, dims=_NN, x_is_rhs=True):
    out = None
    for p in _split_bf16(x, 3):
        t = _mm(sel, p, dims) if x_is_rhs else _mm(p, sel, dims)
        out = t if out is None else out + t
    return out


def _mm_hp(a, b, dims=_NN):
    a1, a2 = _split_bf16(a, 2)
    b1, b2 = _split_bf16(b, 2)
    return _mm(a1, b1, dims) + (_mm(a1, b2, dims) + _mm(a2, b1, dims))


def _gelu(x):
    return 0.5 * x * (1.0 + jnp.tanh(0.7978845608028654 * (x + 0.044715 * (x * x * x))))


def _sigmoid(x):
    return 1.0 / (1.0 + jnp.exp(-x))


def _logsigmoid(x):
    return jnp.minimum(x, 0.0) - jnp.log(1.0 + jnp.exp(-jnp.abs(x)))


def _rms(x, g):
    return x * lax.rsqrt(jnp.mean(x * x, axis=-1, keepdims=True) + EPS) * g


def _iota(shape, dim):
    return lax.broadcasted_iota(jnp.int32, shape, dim)


def _params(n_axes, vmem=VMEM_MID):
    return pltpu.CompilerParams(dimension_semantics=("arbitrary",) * n_axes, vmem_limit_bytes=vmem)


def _resident(shape):
    nd = len(shape)
    return pl.BlockSpec(shape, lambda *_: (0,) * nd, pipeline_mode=pl.Buffered(1))


def _norm_proj_body(x_ref, g_ref, w_ref, *refs, widths, t_widths):
    xb = _rms(x_ref[...], g_ref[...]).astype(BF16)
    o_refs = refs[1:] if t_widths else refs
    off = 0
    for o_ref, n in zip(o_refs[:len(widths)], widths):
        o_ref[...] = _mm(xb, w_ref[:, off:off + n])
        off += n
    off = 0
    for o_ref, n in zip(o_refs[len(widths):], t_widths):
        o_ref[0] = _mm(refs[0][off:off + n, :], xb, _NT)
        off += n


def _norm_proj(x, g, w, widths, tm, wt=None, t_widths=(), bsz=1):
    m, d = x.shape
    tm = min(tm, m)
    nt = m // bsz // tm
    assert m % tm == 0 and sum(widths) == w.shape[1] and (m // bsz) % tm == 0
    t_in = [_resident(wt.shape)] if t_widths else []
    t_args = [wt] if t_widths else []
    return pl.pallas_call(
        functools.partial(_norm_proj_body, widths=tuple(widths), t_widths=tuple(t_widths)),
        grid=(m // tm,),
        in_specs=[pl.BlockSpec((tm, d), lambda i: (i, 0)), _resident((1, d)), _resident(w.shape)] + t_in,
        out_specs=[pl.BlockSpec((tm, n), lambda i: (i, 0)) for n in widths]
        + [pl.BlockSpec((1, n, tm), lambda i: (i // nt, 0, i % nt)) for n in t_widths],
        out_shape=[jax.ShapeDtypeStruct((m, n), F32) for n in widths]
        + [jax.ShapeDtypeStruct((bsz, n, m // bsz), F32) for n in t_widths],
        compiler_params=_params(1),
        name="norm_proj",
    )(x, g.reshape(1, d), w, *t_args)


def _mix_residual(x, h_refs, w_ref, groups, n_t):
    acc = x
    i = off = 0
    for gsz in groups:
        h = h_refs[i][...]
        for j in range(1, gsz):
            h = h + h_refs[i + j][...]
        i += gsz
        n = h.shape[1]
        acc = acc + _mm(h.astype(BF16), w_ref[off:off + n, :])
        off += n
    if n_t:
        ht = h_refs[i][0]
        for j in range(1, n_t):
            ht = ht + h_refs[i + j][0]
        acc = acc + _mm(ht.astype(BF16), w_ref[off:off + ht.shape[0], :], _TN)
    return acc


def _out_proj_body(x_ref, *refs, groups, n_t):
    h_refs, w_ref, o_ref = refs[:-2], refs[-2], refs[-1]
    o_ref[...] = _mix_residual(x_ref[...], h_refs, w_ref, groups, n_t)


def _out_proj(x, hs, groups, w, tm, t_hs=(), bsz=1):
    m, d = x.shape
    tm = min(tm, m)
    nt = m // bsz // tm
    assert m % tm == 0 and (m // bsz) % tm == 0
    return pl.pallas_call(
        functools.partial(_out_proj_body, groups=tuple(groups), n_t=len(t_hs)),
        grid=(m // tm,),
        in_specs=[pl.BlockSpec((tm, d), lambda i: (i, 0))]
        + [pl.BlockSpec((tm, h.shape[1]), lambda i: (i, 0)) for h in hs]
        + [pl.BlockSpec((1, h.shape[1], tm), lambda i: (i // nt, 0, i % nt)) for h in t_hs]
        + [_resident(w.shape)],
        out_specs=pl.BlockSpec((tm, d), lambda i: (i, 0)),
        out_shape=jax.ShapeDtypeStruct((m, d), F32),
        compiler_params=_params(1),
        name="out_proj",
    )(x, *hs, *t_hs, w)


def _ffn_chunks(xn, resid, wup_ref, cw_ref, cb_ref, wd_ref, fc, prev_fn, keep_fn):
    f = cw_ref.shape[1]
    n = f // fc

    def up_proj(c):
        return _mm(xn, wup_ref[:, c * fc:(c + 1) * fc]), _mm(xn, wup_ref[:, f + c * fc:f + (c + 1) * fc])

    acc = resid
    nxt = up_proj(0)
    for c in range(n):
        sl = slice(c * fc, (c + 1) * fc)
        gp, up = nxt
        if c + 1 < n:
            nxt = up_proj(c + 1)
        g2, g1 = prev_fn(gp, sl)
        a = cb_ref[:, sl] + g2 * cw_ref[0:1, sl] + g1 * cw_ref[1:2, sl] + gp * cw_ref[2:3, sl]
        acc = acc + _mm((_gelu(a) * up).astype(BF16), wd_ref[sl, :])
        keep_fn(gp, sl)
    return acc


def _ffn_prompt_body(x_ref, g_ref, wup_ref, cw_ref, cb_ref, wd_ref, st_ref, gf_ref, y_ref, ns_ref, carry, *, final, fc):
    t = pl.program_id(1)
    tm = x_ref.shape[0]

    @pl.when(t == 0)
    def _():
        carry[...] = st_ref[0]

    x = x_ref[...]
    xn = _rms(x, g_ref[...]).astype(BF16)
    row = _iota((tm, fc), 0)

    def prev_fn(gp, sl):
        c0, c1 = carry[0:1, sl], carry[1:2, sl]
        g1 = jnp.where(row == 0, c1, pltpu.roll(gp, 1, 0))
        g2 = jnp.where(row == 0, c0, jnp.where(row == 1, c1, pltpu.roll(gp, 2, 0)))
        return g2, g1

    def keep_fn(gp, sl):
        carry[:, sl] = gp[tm - 2:tm, :]

    acc = _ffn_chunks(xn, x, wup_ref, cw_ref, cb_ref, wd_ref, fc, prev_fn, keep_fn)
    y_ref[...] = _rms(acc, gf_ref[...]) if final else acc

    @pl.when(t == pl.num_programs(1) - 1)
    def _():
        ns_ref[0] = carry[...]


def _ffn_prompt(x, bsz, g, wup, cw, cb, wd, st, gf, final, tm, fc=256):
    m, d = x.shape
    t = m // bsz
    tm = min(tm, t)
    f = cw.shape[1]
    assert t % tm == 0 and f % fc == 0
    nt = t // tm
    rows = lambda n: pl.BlockSpec((tm, n), lambda b, i: (b * nt + i, 0))
    return pl.pallas_call(
        functools.partial(_ffn_prompt_body, final=final, fc=fc),
        grid=(bsz, nt),
        in_specs=[rows(d), _resident((1, d)), _resident(wup.shape), _resident(cw.shape), _resident((1, f)),
                  _resident(wd.shape), pl.BlockSpec((1, CONV_W - 1, f), lambda b, i: (b, 0, 0)), _resident((1, d))],
        out_specs=[rows(d), pl.BlockSpec((1, CONV_W - 1, f), lambda b, i: (b, 0, 0))],
        out_shape=[jax.ShapeDtypeStruct((m, d), F32), jax.ShapeDtypeStruct((bsz, CONV_W - 1, f), F32)],
        scratch_shapes=[pltpu.VMEM((CONV_W - 1, f), F32)],
        compiler_params=_params(2, VMEM_BIG),
        name="ffn_prompt",
    )(x, g.reshape(1, d), wup, cw, cb.reshape(1, f), wd, st, gf.reshape(1, d))


def _ffn_decode_body(x_ref, g_ref, wup_ref, cw_ref, cb_ref, wd_ref, s0_ref, s1_ref, gf_ref, y_ref, gp_ref, *, final, fc):
    x = x_ref[...]
    xn = _rms(x, g_ref[...]).astype(BF16)

    def prev_fn(gp, sl):
        return s0_ref[:, sl], s1_ref[:, sl]

    def keep_fn(gp, sl):
        gp_ref[:, sl] = gp

    acc = _ffn_chunks(xn, x, wup_ref, cw_ref, cb_ref, wd_ref, fc, prev_fn, keep_fn)
    y_ref[...] = _rms(acc, gf_ref[...]) if final else acc


def _ffn_decode(x, g, wup, cw, cb, wd, st, gf, final, fc=256):
    m, d = x.shape
    f = cw.shape[1]
    y, gp = pl.pallas_call(
        functools.partial(_ffn_decode_body, final=final, fc=fc),
        grid=(1,),
        in_specs=[_resident((m, d)), _resident((1, d)), _resident(wup.shape), _resident(cw.shape), _resident((1, f)),
                  _resident(wd.shape), _resident((m, f)), _resident((m, f)), _resident((1, d))],
        out_specs=[pl.BlockSpec((m, d), lambda i: (0, 0)), pl.BlockSpec((m, f), lambda i: (0, 0))],
        out_shape=[jax.ShapeDtypeStruct((m, d), F32), jax.ShapeDtypeStruct((m, f), F32)],
        compiler_params=_params(1, VMEM_BIG),
        name="ffn_decode",
    )(x, g.reshape(1, d), wup, cw, cb.reshape(1, f), wd, st[:, 0], st[:, 1], gf.reshape(1, d))
    return y, jnp.stack([st[:, 1], gp], axis=1)


def _gla_finish(o, gg, gn):
    return _rms(o, gn) * (gg * _sigmoid(gg))


def _gla_body(z_ref, zs_ref, wa_ref, ba_ref, gn_ref, s0_ref, o_ref, sfin_ref, s_scr, s_prev):
    t = pl.program_id(1)
    nb = z_ref.shape[0]

    @pl.when(t == 0)
    def _():
        s_scr[...] = s0_ref[...]

    s_prev[...] = s_scr[...]
    lasts = []
    _round_robin([_gla_chunk(z_ref.at[bi], zs_ref.at[bi], wa_ref, ba_ref, gn_ref, o_ref.at[bi], s_scr.at[bi], lasts)
                  for bi in range(nb)])

    @pl.when(jnp.min(jnp.concatenate(lasts, axis=0)) < -_GLA_FACTORED_RANGE)
    def _():
        for bi in range(nb):
            for h in range(GLA_HEADS):
                _gla_head_exact(h, z_ref.at[bi], zs_ref.at[bi], wa_ref, ba_ref, gn_ref, o_ref.at[bi], s_prev.at[bi])

    @pl.when(t == pl.num_programs(1) - 1)
    def _():
        sfin_ref[...] = s_scr[...]


_GLA_FACTORED_RANGE = 80.0


def _gla_gates(zs_ref, wa_ref, ba_ref):
    c = zs_ref.shape[0]
    la = _logsigmoid(_mm_hp(zs_ref[...], wa_ref[...]) + ba_ref[...]) * (1.0 / GLA_TAU)
    tri = _iota((c, c), 0) >= _iota((c, c), 1)
    return la, _mm_sel(jnp.where(tri, 1.0, 0.0).astype(BF16), la)


def _gla_head_exact(h, z_ref, zs_ref, wa_ref, ba_ref, gn_ref, o_ref, s_prev):
    c = z_ref.shape[0]
    hk = GLA_HEADS * GLA_DK
    ks = slice(h * GLA_DK, (h + 1) * GLA_DK)
    ch = _gla_gates(zs_ref, wa_ref, ba_ref)[1][:, ks]
    q = z_ref[:, ks] * (GLA_DK ** -0.5)
    k = z_ref[:, hk + h * GLA_DK:hk + (h + 1) * GLA_DK]
    v = z_ref[:, 2 * hk + h * GLA_DV:2 * hk + (h + 1) * GLA_DV]
    gg = z_ref[:, 2 * hk + GLA_HEADS * GLA_DV + h * GLA_DV:2 * hk + GLA_HEADS * GLA_DV + (h + 1) * GLA_DV]
    o_inter = _mm((q * jnp.exp(ch)).astype(BF16), s_prev[h].astype(BF16))
    row, rowv = _iota((c, GLA_DK), 0), _iota((c, GLA_DV), 0)

    def token(i, o_acc):
        sel = row == i
        qi = jnp.sum(jnp.where(sel, q, 0.0), axis=0, keepdims=True)
        ci = jnp.sum(jnp.where(sel, ch, 0.0), axis=0, keepdims=True)
        w = jnp.exp(jnp.minimum(ci - ch, 0.0))
        att = jnp.sum(jnp.where(row <= i, qi * k * w, 0.0), axis=1, keepdims=True)
        return jnp.where(rowv == i, jnp.sum(att * v, axis=0, keepdims=True), o_acc)

    o_intra = lax.fori_loop(0, c, token, jnp.zeros((c, GLA_DV), F32))
    o_ref[:, h * GLA_DV:(h + 1) * GLA_DV] = _gla_finish(o_inter + o_intra, gg, gn_ref[...])


def _round_robin(chains):
    chains = list(chains)
    while chains:
        chains = [c for c in chains if next(c, _DONE) is not _DONE]


_DONE = object()


def _gla_chunk(z_ref, zs_ref, wa_ref, ba_ref, gn_ref, o_ref, s_scr, lasts):
    c = z_ref.shape[0]
    hk = GLA_HEADS * GLA_DK
    la = _logsigmoid(_mm_hp(zs_ref[...], wa_ref[...]) + ba_ref[...]) * (1.0 / GLA_TAU)
    yield
    r, cidx = _iota((c, c), 0), _iota((c, c), 1)
    tri = r >= cidx
    cum = _mm_sel(jnp.where(tri, 1.0, 0.0).astype(BF16), la)
    yield
    last = cum[c - 1:c, :]
    lasts.append(last)
    eq, ek, ekl, el = jnp.exp(cum), jnp.exp(-cum), jnp.exp(last - cum), jnp.exp(last)
    gn = gn_ref[...]
    for h in range(GLA_HEADS):
        ks = slice(h * GLA_DK, (h + 1) * GLA_DK)
        vs = slice(2 * hk + h * GLA_DV, 2 * hk + (h + 1) * GLA_DV)
        gs = slice(2 * hk + GLA_HEADS * GLA_DV + h * GLA_DV, 2 * hk + GLA_HEADS * GLA_DV + (h + 1) * GLA_DV)
        q = z_ref[:, ks] * (GLA_DK ** -0.5)
        k = z_ref[:, hk + h * GLA_DK:hk + (h + 1) * GLA_DK]
        v = z_ref[:, vs]
        qt = (q * eq[:, ks]).astype(BF16)
        s_old = s_scr[h]
        att = jnp.where(tri, _mm(qt, (k * ek[:, ks]).astype(BF16), _NT), 0.0)
        vb = v.astype(BF16)
        o_inter = _mm(qt, s_old.astype(BF16))
        kv_new = _mm((k * ekl[:, ks]).astype(BF16), vb, _TN)
        yield
        o = o_inter + _mm(att.astype(BF16), vb)
        ecol = jnp.sum(jnp.where(r == cidx, jnp.broadcast_to(el[:, ks], (c, c)), 0.0), axis=1, keepdims=True)
        s_scr[h] = ecol * s_old + kv_new
        yield
        o_ref[:, h * GLA_DV:(h + 1) * GLA_DV] = _gla_finish(o, z_ref[:, gs], gn)


def _seq_group(bsz):
    return next(n for n in (4, 2, 1) if bsz % n == 0)


def _gla_prompt(z, zs, wa, ba, gn, s0, bsz):
    m = z.shape[0]
    t = m // bsz
    c = GLA_CHUNK
    assert t % c == 0 and GLA_DK == c
    nc = t // c
    hk = GLA_HEADS * GLA_DK
    nb = _seq_group(bsz)
    st = (nb, GLA_HEADS, GLA_DK, GLA_DV)
    tok = lambda n: pl.BlockSpec((nb, c, n), lambda b, i: (b, i, 0))
    o, s_fin = pl.pallas_call(
        _gla_body,
        grid=(bsz // nb, nc),
        in_specs=[tok(GLA_Z_W), tok(LANE), _resident((LANE, hk)), _resident((1, hk)), _resident((1, GLA_DV)),
                  pl.BlockSpec(st, lambda b, i: (b, 0, 0, 0))],
        out_specs=[tok(GLA_HEADS * GLA_DV), pl.BlockSpec(st, lambda b, i: (b, 0, 0, 0))],
        out_shape=[jax.ShapeDtypeStruct((bsz, t, GLA_HEADS * GLA_DV), F32),
                   jax.ShapeDtypeStruct((bsz, GLA_HEADS, GLA_DK, GLA_DV), F32)],
        scratch_shapes=[pltpu.VMEM(st, F32), pltpu.VMEM(st, F32)],
        compiler_params=_params(2),
        name="gla_prompt",
    )(z.reshape(bsz, t, -1), zs.reshape(bsz, t, -1), wa, ba, gn, s0)
    return o.reshape(m, -1), s_fin


def _gla_decode_body(z_ref, zs_ref, wa_ref, ba_ref, gn_ref, s_ref, o_ref, sn_ref):
    hk = GLA_HEADS * GLA_DK
    rows = SUBLANE
    z = jnp.broadcast_to(z_ref[0], (rows, GLA_Z_W))
    ga = jnp.broadcast_to(zs_ref[0], (rows, LANE))
    la = _logsigmoid(_mm_hp(ga, wa_ref[...]) + ba_ref[...]) * (1.0 / GLA_TAU)
    ea = jnp.exp(la)
    r, cidx = _iota((GLA_DK, GLA_DK), 0), _iota((GLA_DK, GLA_DK), 1)
    row0 = _iota((rows, GLA_DK), 0) == 0
    gn = gn_ref[...]
    for h in range(GLA_HEADS):
        ks = slice(h * GLA_DK, (h + 1) * GLA_DK)
        q = z[:, ks] * (GLA_DK ** -0.5)
        k = z[:, hk + h * GLA_DK:hk + (h + 1) * GLA_DK]
        v = z[:, 2 * hk + h * GLA_DV:2 * hk + (h + 1) * GLA_DV]
        gg = z[:, 2 * hk + GLA_HEADS * GLA_DV + h * GLA_DV:2 * hk + GLA_HEADS * GLA_DV + (h + 1) * GLA_DV]
        s_old = s_ref[0, h]
        o = _mm_hp(q * ea[:, ks], s_old) + jnp.sum(q * k, axis=1, keepdims=True) * v
        ecol = jnp.sum(jnp.where(r == cidx, jnp.broadcast_to(ea[0:1, ks], (GLA_DK, GLA_DK)), 0.0), axis=1, keepdims=True)
        sn_ref[0, h] = ecol * s_old + _mm_hp(jnp.where(row0, k, 0.0), v, _TN)
        o_ref[0, :, h * GLA_DV:(h + 1) * GLA_DV] = _gla_finish(o, gg, gn)[0:1]


def _gla_decode(z, zs, wa, ba, gn, s0):
    b = z.shape[0]
    hk = GLA_HEADS * GLA_DK
    st = (1, GLA_HEADS, GLA_DK, GLA_DV)
    o, sn = pl.pallas_call(
        _gla_decode_body,
        grid=(b,),
        in_specs=[pl.BlockSpec((1, 1, GLA_Z_W), lambda i: (i, 0, 0)), pl.BlockSpec((1, 1, LANE), lambda i: (i, 0, 0)),
                  _resident((LANE, hk)), _resident((1, hk)), _resident((1, GLA_DV)),
                  pl.BlockSpec(st, lambda i: (i, 0, 0, 0))],
        out_specs=[pl.BlockSpec((1, 1, GLA_HEADS * GLA_DV), lambda i: (i, 0, 0)), pl.BlockSpec(st, lambda i: (i, 0, 0, 0))],
        out_shape=[jax.ShapeDtypeStruct((b, 1, GLA_HEADS * GLA_DV), F32), jax.ShapeDtypeStruct((b,) + st[1:], F32)],
        compiler_params=_params(1),
        name="gla_decode",
    )(z.reshape(b, 1, -1), zs.reshape(b, 1, -1), wa, ba, gn, s0)
    return o.reshape(b, -1), sn


def _mlstm_finish(hh, og, mn):
    return _rms(hh, mn) * _sigmoid(og)


def _mlstm_body(z_ref, zs_ref, bias_ref, mn_ref, c0_ref, m0_ref, h_ref, cfin_ref, mfin_ref, c_scr, m_scr):
    t = pl.program_id(1)

    @pl.when(t == 0)
    def _():
        c_scr[...] = c0_ref[...]
        m_scr[...] = m0_ref[...]

    _round_robin([_mlstm_chunk(z_ref.at[bi], zs_ref.at[bi], bias_ref, mn_ref, h_ref.at[bi], c_scr.at[bi], m_scr.at[bi])
                  for bi in range(z_ref.shape[0])])

    @pl.when(t == pl.num_programs(1) - 1)
    def _():
        cfin_ref[...] = c_scr[...]
        mfin_ref[...] = m_scr[...]


def _mlstm_chunk(z_ref, zs_ref, bias_ref, mn_ref, h_ref, c_scr, m_scr):
    L = z_ref.shape[0]
    hq = M_HEADS * M_DQK
    g = zs_ref[...] + bias_ref[...]
    lane = _iota((L, LANE), 1)
    gx = jnp.where((lane >= M_HEADS) & (lane < 2 * M_HEADS), _logsigmoid(g), g)
    r, cidx = _iota((L, L), 0), _iota((L, L), 1)
    tri = r >= cidx
    cum_c = _mm_sel(jnp.where(tri, 1.0, 0.0).astype(BF16), gx)
    rows = _mm_sel(jnp.where(r == cidx, 1.0, 0.0).astype(BF16), gx, _TN, x_is_rhs=False)
    cum_r = _mm_sel(jnp.where(r <= cidx, 1.0, 0.0).astype(BF16), gx, _TN, x_is_rhs=False)
    yield
    ones_col = jnp.where(_iota((L, LANE), 1) == 0, 1.0, 0.0).astype(BF16)
    m_all = m_scr[...]
    m_new_all = m_all
    for h in range(M_HEADS):
        q = z_ref[:, h * M_DQK:(h + 1) * M_DQK].astype(BF16)
        k = z_ref[:, hq + h * M_DQK:hq + (h + 1) * M_DQK] * (M_DQK ** -0.5)
        v = z_ref[:, 2 * hq + h * M_DV:2 * hq + (h + 1) * M_DV]
        og = z_ref[:, 2 * hq + M_HEADS * M_DV + h * M_DV:2 * hq + M_HEADS * M_DV + (h + 1) * M_DV]
        va = jnp.concatenate([v.astype(BF16), ones_col], axis=1)
        ic_r, ic_c = rows[h:h + 1, :], gx[:, h:h + 1]
        cr, cc = cum_r[M_HEADS + h:M_HEADS + h + 1, :], cum_c[:, M_HEADS + h:M_HEADS + h + 1]
        m_old = m_all[0:1, h:h + 1]
        dlog = jnp.where(tri, cc - cr + ic_r, -jnp.inf)
        inter = cc + m_old
        mi = jnp.maximum(inter, jnp.max(dlog, axis=1, keepdims=True))
        w = jnp.exp(dlog - mi)
        wi = jnp.exp(inter - mi)
        qk = _mm(q, k.astype(BF16), _NT)
        c_old = c_scr[h]
        qc = _mm(q, c_old.astype(BF16))
        yield
        s = qk * w
        num = wi * qc + _mm(s.astype(BF16), va)
        yield
        qn = num[:, M_DV:M_DV + 1]
        hh = num[:, :M_DV] / jnp.maximum(jnp.abs(qn), jnp.exp(-mi))
        last = cc[L - 1:L, :]
        gl = last - cc + ic_c
        m_new = jnp.maximum(last + m_old, jnp.max(gl, axis=0, keepdims=True))
        wj = jnp.exp(gl - m_new)
        keep = jnp.exp(last + m_old - m_new)
        c_scr[h] = keep * c_old + _mm((wj * k).astype(BF16), va, _TN)
        yield
        m_new_all = jnp.where(_iota((1, LANE), 1) == h, m_new, m_new_all)
        h_ref[:, h * M_DV:(h + 1) * M_DV] = _mlstm_finish(hh, og, mn_ref[:, h * M_DV:(h + 1) * M_DV])
    m_scr[...] = m_new_all


def _mlstm_prompt(z, zs, bias, mn, c0, m0, bsz):
    m = z.shape[0]
    t = m // bsz
    L = next(n for n in (M_CHUNK_PALLAS, M_CHUNK) if t % n == 0)
    nc = t // L
    nb = _seq_group(bsz)
    st, ms = (nb, M_HEADS, M_DQK, M_AUG), (nb, 1, LANE)
    tok = lambda n: pl.BlockSpec((nb, L, n), lambda b, i: (b, i, 0))
    h, c_fin, m_fin = pl.pallas_call(
        _mlstm_body,
        grid=(bsz // nb, nc),
        in_specs=[tok(M_Z_W), tok(LANE), _resident((1, LANE)), _resident((1, M_HEADS * M_DV)),
                  pl.BlockSpec(st, lambda b, i: (b, 0, 0, 0)), pl.BlockSpec(ms, lambda b, i: (b, 0, 0))],
        out_specs=[tok(M_HEADS * M_DV), pl.BlockSpec(st, lambda b, i: (b, 0, 0, 0)), pl.BlockSpec(ms, lambda b, i: (b, 0, 0))],
        out_shape=[jax.ShapeDtypeStruct((bsz, t, M_HEADS * M_DV), F32), jax.ShapeDtypeStruct((bsz,) + st[1:], F32),
                   jax.ShapeDtypeStruct((bsz, 1, LANE), F32)],
        scratch_shapes=[pltpu.VMEM(st, F32), pltpu.VMEM(ms, F32)],
        compiler_params=_params(2),
        name="mlstm_prompt",
    )(z.reshape(bsz, t, -1), zs.reshape(bsz, t, -1), bias, mn, c0, m0)
    return h.reshape(m, -1), c_fin, m_fin


def _mlstm_decode_body(z_ref, zs_ref, bias_ref, mn_ref, c_ref, n_ref, m_ref, h_ref, cn_ref, nn_ref, mo_ref):
    hq = M_HEADS * M_DQK
    rows = SUBLANE
    z = jnp.broadcast_to(z_ref[0], (rows, M_Z_W))
    g = zs_ref[0] + bias_ref[...]
    m_all = m_ref[0]
    m_new_all = m_all
    row0 = _iota((rows, M_DQK), 0) == 0
    for h in range(M_HEADS):
        q = z[:, h * M_DQK:(h + 1) * M_DQK]
        k = z[:, hq + h * M_DQK:hq + (h + 1) * M_DQK] * (M_DQK ** -0.5)
        v = z[:, 2 * hq + h * M_DV:2 * hq + (h + 1) * M_DV]
        og = z[:, 2 * hq + M_HEADS * M_DV + h * M_DV:2 * hq + M_HEADS * M_DV + (h + 1) * M_DV]
        ic = g[:, h:h + 1]
        fl = _logsigmoid(g[:, M_HEADS + h:M_HEADS + h + 1])
        m_old = m_all[:, h:h + 1]
        c_old, n_old = c_ref[0, h], n_ref[0, h:h + 1, :]
        mi = jnp.maximum(fl + m_old, ic)
        w = jnp.exp(ic - mi)
        wi = jnp.exp(fl + m_old - mi)
        s = jnp.sum(q * k, axis=1, keepdims=True) * w
        num = wi * _mm_hp(q, c_old) + s * v
        qn = wi * jnp.sum(q * n_old, axis=1, keepdims=True) + s
        hh = num / jnp.maximum(jnp.abs(qn), jnp.exp(-mi))
        cn_ref[0, h] = wi * c_old + w * _mm_hp(jnp.where(row0, k, 0.0), v, _TN)
        nn_ref[0, h:h + 1, :] = wi * n_old + w * k[0:1]
        m_new_all = jnp.where(_iota((1, LANE), 1) == h, mi, m_new_all)
        h_ref[0, :, h * M_DV:(h + 1) * M_DV] = _mlstm_finish(hh, og, mn_ref[:, h * M_DV:(h + 1) * M_DV])[0:1]
    mo_ref[0] = m_new_all


def _mlstm_decode(z, zs, bias, mn, c0, n0, m0):
    b = z.shape[0]
    cs, ns = (1, M_HEADS, M_DQK, M_DV), (1, M_HEADS, M_DQK)
    m0p = jnp.pad(m0, ((0, 0), (0, LANE - M_HEADS))).reshape(b, 1, LANE)
    row3 = lambda n: pl.BlockSpec((1, 1, n), lambda i: (i, 0, 0))
    h, cn, nn, mo = pl.pallas_call(
        _mlstm_decode_body,
        grid=(b,),
        in_specs=[row3(M_Z_W), row3(LANE), _resident((1, LANE)), _resident((1, M_HEADS * M_DV)),
                  pl.BlockSpec(cs, lambda i: (i, 0, 0, 0)), pl.BlockSpec(ns, lambda i: (i, 0, 0)), row3(LANE)],
        out_specs=[row3(M_HEADS * M_DV), pl.BlockSpec(cs, lambda i: (i, 0, 0, 0)), pl.BlockSpec(ns, lambda i: (i, 0, 0)), row3(LANE)],
        out_shape=[jax.ShapeDtypeStruct((b, 1, M_HEADS * M_DV), F32), jax.ShapeDtypeStruct((b,) + cs[1:], F32),
                   jax.ShapeDtypeStruct((b,) + ns[1:], F32), jax.ShapeDtypeStruct((b, 1, LANE), F32)],
        compiler_params=_params(1),
        name="mlstm_decode",
    )(z.reshape(b, 1, -1), zs.reshape(b, 1, -1), bias, mn, c0, n0, m0p)
    return h.reshape(b, -1), cn, nn, mo[:, 0, :M_HEADS]


def _compress_dense_body(x_ref, pe_ref, w1_ref, w2_ref, o_ref, acc):
    kk = pl.program_id(1)

    @pl.when(kk == 0)
    def _():
        acc[...] = jnp.zeros_like(acc)

    acc[...] += _mm((x_ref[...] + pe_ref[...]).astype(BF16), w1_ref[...])

    @pl.when(kk == pl.num_programs(1) - 1)
    def _():
        o_ref[...] = _mm(_gelu(acc[...]).astype(BF16), w2_ref[...])


def _compress_dense(x, pe, w1, w2, tk=2048):
    r, kdim = x.shape
    tr = r if r <= 512 else 512
    assert r % tr == 0 and kdim % tk == 0
    return pl.pallas_call(
        _compress_dense_body,
        grid=(r // tr, kdim // tk),
        in_specs=[pl.BlockSpec((tr, tk), lambda i, k: (i, k)), pl.BlockSpec((1, tk), lambda i, k: (0, k)),
                  pl.BlockSpec((tk, NSA_KV_W), lambda i, k: (k, 0)), _resident(w2.shape)],
        out_specs=pl.BlockSpec((tr, NSA_KV_W), lambda i, k: (i, 0)),
        out_shape=jax.ShapeDtypeStruct((r, NSA_KV_W), F32),
        scratch_shapes=[pltpu.VMEM((tr, NSA_KV_W), F32)],
        compiler_params=_params(2),
        name="nsa_compress",
    )(x, pe, w1, w2)


_MAX_PAGES_PER_STEP = 32
_D_PER_STEP = 8


def _compress_paged_body(pt_ref, *refs, n_pages, pps):
    page_refs = refs[:pps]
    pe_ref, w1_ref, w2_ref, o_ref, xs = refs[pps:]
    g = pl.program_id(1)
    for p, pr in enumerate(page_refs):
        r0 = pl.multiple_of((g * pps + p) * NSA_KV_W, NSA_KV_W)
        xs[pl.ds(r0, NSA_KV_W), :] = pr[0]

    @pl.when(g == pl.num_programs(1) - 1)
    def _():
        kvd = NSA_KV_HEADS * NSA_DH
        for c in range(2):
            def step(j, acc):
                parts = []
                for dd in range(_D_PER_STEP):
                    d = j * _D_PER_STEP + dd
                    pe_row = pe_ref[pl.ds(c * NSA_DH + d, 1), :]
                    rows = [xs[pl.ds(c * kvd + k * NSA_DH + d, n_pages, stride=NSA_KV_W), :] + pe_row
                            for k in range(NSA_KV_HEADS)]
                    parts.append(jnp.concatenate(rows, axis=0).astype(BF16))
                return acc + _mm(jnp.concatenate(parts, axis=1), w1_ref[c, j])
            hid = lax.fori_loop(0, NSA_DH // _D_PER_STEP, step, jnp.zeros((NSA_KV_HEADS * n_pages, PAGE_SIZE), F32))
            out = _mm(_gelu(hid).astype(BF16), w2_ref[c])
            o_ref[c * NSA_KV_HEADS * n_pages:(c + 1) * NSA_KV_HEADS * n_pages, :] = out


def _compress_paged(pool_t, page_table, pe_t, w1_t, w2_t):
    b, n_pages = page_table.shape
    pps = next(n for n in range(min(_MAX_PAGES_PER_STEP, n_pages), 0, -1) if n_pages % n == 0)
    assert _BLOCKS_PER_PAGE == 2
    rows_out = 2 * NSA_KV_HEADS * n_pages

    def page_spec(p):
        return pl.BlockSpec((1, NSA_KV_W, PAGE_SIZE), lambda i, g, pt: (pt[i, g * pps + p], 0, 0))

    def fixed(a):
        nd = a.ndim
        return pl.BlockSpec(a.shape, lambda i, g, pt: (0,) * nd, pipeline_mode=pl.Buffered(1))

    grid_spec = pltpu.PrefetchScalarGridSpec(
        num_scalar_prefetch=1,
        grid=(b, n_pages // pps),
        in_specs=[page_spec(p) for p in range(pps)] + [fixed(pe_t), fixed(w1_t), fixed(w2_t)],
        out_specs=pl.BlockSpec((rows_out, PAGE_SIZE), lambda i, g, pt: (i, 0)),
        scratch_shapes=[pltpu.VMEM((n_pages * NSA_KV_W, PAGE_SIZE), F32)],
    )
    return pl.pallas_call(
        functools.partial(_compress_paged_body, n_pages=n_pages, pps=pps),
        grid_spec=grid_spec,
        out_shape=jax.ShapeDtypeStruct((b * rows_out, PAGE_SIZE), F32),
        compiler_params=_params(2, VMEM_BIG),
        name="nsa_compress_paged",
    )(page_table, *([pool_t] * pps), pe_t, w1_t, w2_t)


def _gate_rows(zs_ref, gb_ref):
    return _sigmoid(zs_ref[0] + gb_ref[...])


def _gate_row(gates, head, branch):
    r = GATE_OFF + head * NSA_BRANCHES + branch
    return gates[r:r + 1, :]


def _cmp_sel_body(q_ref, kvc_ref, zs_ref, gb_ref, ocmp_ref, nm_ref, sc_scr):
    qi = pl.program_id(1)
    tq = q_ref.shape[2]
    nbp = kvc_ref.shape[0]
    t0 = qi * tq
    tpos = t0 + _iota((nbp, tq), 1)
    blk = _iota((nbp, tq), 0)
    cur = jnp.right_shift(tpos, NSA_BLOCK.bit_length() - 1)
    vis = blk * NSA_BLOCK + (NSA_BLOCK - 1) <= tpos
    forced = (blk == 0) | (blk == cur) | (blk == cur - 1)
    allowed = blk <= cur
    gates = _gate_rows(zs_ref, gb_ref)
    for kh in range(NSA_KV_HEADS):
        kc = kvc_ref[:, kh * NSA_DH:(kh + 1) * NSA_DH].astype(BF16)
        vc = kvc_ref[:, NSA_KV_HEADS * NSA_DH + kh * NSA_DH:NSA_KV_HEADS * NSA_DH + (kh + 1) * NSA_DH].astype(BF16)
        probs = [None] * NSA_GROUP

        def head(gi):
            h = kh * NSA_GROUP + gi
            q = (q_ref[0, h * NSA_DH:(h + 1) * NSA_DH, :] * (NSA_DH ** -0.5)).astype(BF16)
            s = _mm(kc, q)
            yield
            s = jnp.where(vis, s, -jnp.inf)
            m = jnp.max(s, axis=0, keepdims=True)
            m = jnp.where(m == -jnp.inf, 0.0, m)
            e = jnp.exp(s - m)
            p = e / jnp.maximum(jnp.sum(e, axis=0, keepdims=True), TINY)
            probs[gi] = p
            o = _mm(vc, p.astype(BF16), _TN)
            yield
            ocmp_ref[0, h * NSA_DH:(h + 1) * NSA_DH, :] = o * _gate_row(gates, h, 0)

        _round_robin([head(gi) for gi in range(NSA_GROUP)])
        imp = probs[0]
        for p in probs[1:]:
            imp = imp + p
        score = jnp.where(forced, jnp.inf, jnp.where(allowed, imp, -jnp.inf))
        for lc in range(tq // LANE):
            ls = slice(lc * LANE, (lc + 1) * LANE)
            n_live = (t0 + (lc + 1) * LANE - 1) // NSA_BLOCK + 1

            def rank(rows):
                sc_scr[0:rows, :] = score[0:rows, ls]
                blk_rows = _iota((rows, LANE), 0)

                def rank_step(i, cnt):
                    for j in (2 * i, 2 * i + 1):
                        row = sc_scr[pl.ds(j, 1), :]
                        sc = sc_scr[0:rows, :]
                        tie = jnp.where(j < blk_rows, 1.0, 0.0)
                        cnt = cnt + jnp.where(row > sc, 1.0, jnp.where(row == sc, tie, 0.0))
                    return cnt

                cnt = lax.fori_loop(0, n_live // 2, rank_step, jnp.zeros((rows, LANE), F32))
                nm_ref[0, kh * nbp:kh * nbp + rows, ls] = jnp.where(
                    allowed[0:rows, ls], jnp.where(cnt < NSA_TOP_N, 0.0, -MASK_BIG), -MASK_BIG).astype(BF16)

            half = nbp // 2

            @pl.when(n_live <= half)
            def _():
                rank(half)
                nm_ref[0, kh * nbp + half:(kh + 1) * nbp, ls] = jnp.full((nbp - half, LANE), -MASK_BIG, BF16)

            @pl.when(n_live > half)
            def _():
                rank(nbp)


def _cmp_sel(q_t, kvc, zs_t, gb_col, nbp, tq):
    bsz, hd, t = q_t.shape
    tq = min(tq, t)
    nt = t // tq
    tok = lambda n: pl.BlockSpec((1, n, tq), lambda b, i: (b, 0, i))
    return pl.pallas_call(
        _cmp_sel_body,
        grid=(bsz, nt),
        in_specs=[tok(hd), pl.BlockSpec((nbp, NSA_KV_W), lambda b, i: (b, 0)), tok(LANE), _resident((LANE, 1))],
        out_specs=[tok(hd), tok(NSA_KV_HEADS * nbp)],
        out_shape=[jax.ShapeDtypeStruct((bsz, hd, t), F32), jax.ShapeDtypeStruct((bsz, NSA_KV_HEADS * nbp, t), BF16)],
        scratch_shapes=[pltpu.VMEM((nbp, LANE), F32)],
        compiler_params=_params(2),
        name="nsa_cmp_select",
    )(q_t, kvc, zs_t, gb_col)


def _sel_attn_body(q_ref, nm_ref, kv_ref, oh_ref, v_ref, zs_ref, gb_ref, o_ref, *, tk):
    qi = pl.program_id(1)
    tq = q_ref.shape[2]
    nbp = nm_ref.shape[1] // NSA_KV_HEADS
    cols = NSA_GROUP * tq
    t0 = qi * tq
    n_before = t0 // tk
    gates = _gate_rows(zs_ref, gb_ref)
    qas = []
    for kh in range(NSA_KV_HEADS):
        nm = nm_ref[0, kh * nbp:(kh + 1) * nbp, :]
        qas.append(jnp.concatenate(
            [jnp.concatenate([nm, (q_ref[0, h * NSA_DH:(h + 1) * NSA_DH, :] * (NSA_DH ** -0.5 * LOG2E)).astype(BF16)], axis=0)
             for h in range(kh * NSA_GROUP, (kh + 1) * NSA_GROUP)], axis=1))

    def tile(kh, kt, carry, causal, out):
        m_old, acc = carry
        k0 = pl.multiple_of(kt * tk, tk)
        ka = jnp.concatenate([oh_ref[pl.ds(k0, tk), :],
                              kv_ref[pl.ds(k0, tk), kh * NSA_DH:(kh + 1) * NSA_DH].astype(BF16)], axis=1)
        s = _mm(ka, qas[kh])
        yield
        if causal:
            kpos = k0 + _iota((tk, cols), 0)
            tpos = t0 + jnp.bitwise_and(_iota((tk, cols), 1), tq - 1)
            s = jnp.where(kpos <= tpos, s, -jnp.inf)
        m_new = jnp.maximum(m_old, jnp.max(s, axis=0, keepdims=True))
        alpha = jnp.exp2(m_old - m_new)
        p = jnp.exp2(s - m_new)
        yield
        out[kh] = (m_new, alpha * acc + _mm(v_ref[0, kh, :, pl.ds(k0, tk)], p.astype(BF16)))

    def tiles(kt, carries, causal):
        out = [None] * NSA_KV_HEADS
        _round_robin([tile(kh, kt, carries[kh], causal, out) for kh in range(NSA_KV_HEADS)])
        return tuple(out)

    init = (jnp.full((1, cols), -jnp.inf, F32), jnp.zeros((V_AUG, cols), F32))
    carries = lax.fori_loop(0, n_before, lambda kt, c: tiles(kt, c, False), (init,) * NSA_KV_HEADS)
    carries = tiles(n_before, carries, True)
    for kh in range(NSA_KV_HEADS):
        acc = carries[kh][1]
        o = acc[:NSA_DH] / acc[NSA_DH:NSA_DH + 1]
        for gi in range(NSA_GROUP):
            h = kh * NSA_GROUP + gi
            o_ref[0, h * NSA_DH:(h + 1) * NSA_DH, :] = o[:, gi * tq:(gi + 1) * tq] * _gate_row(gates, h, 1)


def _sel_attn(q_t, nm_t, kv, onehot, vs_t, zs_t, gb_col, tq, tk):
    bsz, hd, t = q_t.shape
    tq, tk = min(tq, t), min(tk, t)
    assert t % tq == 0 and t % tk == 0 and tk % tq == 0 and tq & (tq - 1) == 0
    nt = t // tq
    tok = lambda n: pl.BlockSpec((1, n, tq), lambda b, i: (b, 0, i))
    return pl.pallas_call(
        functools.partial(_sel_attn_body, tk=tk),
        grid=(bsz, nt),
        in_specs=[tok(hd), tok(nm_t.shape[1]), pl.BlockSpec((t, NSA_KV_W), lambda b, i: (b, 0)), _resident(onehot.shape),
                  pl.BlockSpec((1,) + vs_t.shape[1:], lambda b, i: (b, 0, 0, 0)), tok(LANE), _resident((LANE, 1))],
        out_specs=tok(hd),
        out_shape=jax.ShapeDtypeStruct((bsz, hd, t), F32),
        compiler_params=_params(2, VMEM_BIG),
        name="nsa_selected",
    )(q_t, nm_t, kv, onehot, vs_t, zs_t, gb_col)


def _win_attn_body(q_ref, k_ref, v_ref, zs_ref, gb_ref, o_ref, bias_scr, *, span):
    qi = pl.program_id(1)
    tq = q_ref.shape[2]
    cols = NSA_GROUP * tq
    t0 = qi * tq
    start = pl.multiple_of(jnp.maximum(t0 + tq - span, 0), tq)
    gates = _gate_rows(zs_ref, gb_ref)

    @pl.when(t0 + tq - span <= 0)
    def _():
        kpos = start + _iota((span, cols), 0)
        tpos = t0 + jnp.bitwise_and(_iota((span, cols), 1), tq - 1)
        bias_scr[...] = jnp.where((kpos <= tpos) & (tpos - kpos < NSA_WINDOW), 0.0, -jnp.inf)

    for kh in range(NSA_KV_HEADS):
        qa = jnp.concatenate([(q_ref[0, h * NSA_DH:(h + 1) * NSA_DH, :] * (NSA_DH ** -0.5 * LOG2E)).astype(BF16)
                              for h in range(kh * NSA_GROUP, (kh + 1) * NSA_GROUP)], axis=1)
        kw = k_ref[pl.ds(start, span), kh * NSA_DH:(kh + 1) * NSA_DH].astype(BF16)
        s = jnp.where(bias_scr[...] == 0.0, _mm(kw, qa), -jnp.inf)
        e = jnp.exp2(s - jnp.max(s, axis=0, keepdims=True))
        oa = _mm(v_ref[0, kh, :, pl.ds(start, span)], e.astype(BF16))
        o = oa[:NSA_DH] / oa[NSA_DH:NSA_DH + 1]
        for gi in range(NSA_GROUP):
            h = kh * NSA_GROUP + gi
            o_ref[0, h * NSA_DH:(h + 1) * NSA_DH, :] = o[:, gi * tq:(gi + 1) * tq] * _gate_row(gates, h, 2)


def _win_attn(q_t, kv, vw_t, zs_t, gb_col, tq):
    bsz, hd, t = q_t.shape
    tq = min(tq, t)
    span = min(NSA_WINDOW + tq, t)
    assert t % tq == 0 and tq & (tq - 1) == 0
    nt = t // tq
    tok = lambda n: pl.BlockSpec((1, n, tq), lambda b, i: (b, 0, i))
    whole = lambda a: pl.BlockSpec((1,) + a.shape[1:], lambda b, i: (b, 0, 0, 0))
    return pl.pallas_call(
        functools.partial(_win_attn_body, span=span),
        grid=(bsz, nt),
        in_specs=[tok(hd), pl.BlockSpec((t, NSA_KV_W), lambda b, i: (b, 0)), whole(vw_t), tok(LANE), _resident((LANE, 1))],
        out_specs=tok(hd),
        out_shape=jax.ShapeDtypeStruct((bsz, hd, t), F32),
        scratch_shapes=[pltpu.VMEM((span, NSA_GROUP * tq), F32)],
        compiler_params=_params(2),
        name="nsa_window",
    )(q_t, kv, vw_t, zs_t, gb_col)


def _decode_forced(n_past_blk):
    cur = n_past_blk
    return sorted({0, cur - 1, cur} - {-1})


def _cmp_decode_body(q_ref, kvc_ref, gz_ref, gb_ref, o_ref, idx_ref, *, past, n_pick):
    n_pages = kvc_ref.shape[0] // (2 * NSA_KV_HEADS)
    nb = _BLOCKS_PER_PAGE * n_pages

    def block_id(shape):
        pos = _iota(shape, 1)
        page = jnp.where(pos >= n_pages, pos - n_pages, pos)
        return page * _BLOCKS_PER_PAGE + jnp.where(pos >= n_pages, 1, 0)

    lane = block_id((NSA_HEADS, nb))
    hrow = _iota((NSA_HEADS, nb), 0)
    vis = lane * NSA_BLOCK + (NSA_BLOCK - 1) <= past
    q = (q_ref[0] * (NSA_DH ** -0.5)).astype(BF16)
    gates = _sigmoid(gz_ref[0] + gb_ref[...])
    cur = past // NSA_BLOCK
    o_all = jnp.zeros((NSA_HEADS, NSA_DH), F32)
    idx_all = jnp.zeros((SUBLANE, LANE), F32)
    orow = _iota((NSA_HEADS, NSA_DH), 0)
    slot_r, slot_c = _iota((SUBLANE, LANE), 0), _iota((SUBLANE, LANE), 1)
    l1 = block_id((1, nb))
    l1f = l1.astype(F32)
    forced = (l1 == 0) | (l1 == cur) | (l1 == cur - 1)
    for kh in range(NSA_KV_HEADS):
        kc = kvc_ref[kh * n_pages:(kh + 1) * n_pages, :].astype(BF16)
        vc = kvc_ref[(NSA_KV_HEADS + kh) * n_pages:(NSA_KV_HEADS + kh + 1) * n_pages, :].astype(BF16)
        s = jnp.concatenate([_mm(q, kc[:, j * NSA_DH:(j + 1) * NSA_DH], _NT) for j in range(_BLOCKS_PER_PAGE)], axis=1)
        s = jnp.where(vis, s, -jnp.inf)
        m = jnp.max(s, axis=1, keepdims=True)
        m = jnp.where(m == -jnp.inf, 0.0, m)
        e = jnp.exp(s - m)
        p = e / jnp.maximum(jnp.sum(e, axis=1, keepdims=True), TINY)
        mine = (hrow >= kh * NSA_GROUP) & (hrow < (kh + 1) * NSA_GROUP)
        pb = p.astype(BF16)
        o_kh = _mm(pb[:, :n_pages], vc[:, :NSA_DH])
        for j in range(1, _BLOCKS_PER_PAGE):
            o_kh = o_kh + _mm(pb[:, j * n_pages:(j + 1) * n_pages], vc[:, j * NSA_DH:(j + 1) * NSA_DH])
        o_all = jnp.where((orow >= kh * NSA_GROUP) & (orow < (kh + 1) * NSA_GROUP), o_kh, o_all)
        imp = jnp.sum(jnp.where(mine, p, 0.0), axis=0, keepdims=True)
        score = jnp.where(forced, -jnp.inf, imp)
        for r in range(n_pick):
            mx = jnp.max(score, axis=1, keepdims=True)
            pick = jnp.min(jnp.where(score == mx, l1f, float(nb)), axis=1, keepdims=True)
            score = jnp.where(l1f == pick, -jnp.inf, score)
            idx_all = jnp.where((slot_r == kh) & (slot_c == r), pick, idx_all)
    o_ref[0] = o_all * gates[:, 0:1]
    idx_ref[0] = idx_all.astype(jnp.int32)


def _cmp_decode(qh, kvc, gz, gb3, past, n_pick):
    b = qh.shape[0]
    rows = kvc.shape[0] // b
    return pl.pallas_call(
        functools.partial(_cmp_decode_body, past=past, n_pick=n_pick),
        grid=(b,),
        in_specs=[pl.BlockSpec((1, NSA_HEADS, NSA_DH), lambda i: (i, 0, 0)), pl.BlockSpec((rows, kvc.shape[1]), lambda i: (i, 0)),
                  pl.BlockSpec((1, NSA_HEADS, NSA_BRANCHES), lambda i: (i, 0, 0)), _resident((NSA_HEADS, NSA_BRANCHES))],
        out_specs=[pl.BlockSpec((1, NSA_HEADS, NSA_DH), lambda i: (i, 0, 0)), pl.BlockSpec((1, SUBLANE, LANE), lambda i: (i, 0, 0))],
        out_shape=[jax.ShapeDtypeStruct((b, NSA_HEADS, NSA_DH), F32), jax.ShapeDtypeStruct((b, SUBLANE, LANE), jnp.int32)],
        compiler_params=_params(1),
        name="nsa_cmp_decode",
    )(qh, kvc, gz, gb3)


def _selwin_decode_body(info_ref, q_ref, ns_ref, nw_ref, win_ref, *refs, n_shared, n_own, first_win_row):
    n_blk = n_shared + NSA_KV_HEADS * n_own
    page_refs = refs[:n_blk]
    gz_ref, gb_ref, o_ref, wout_ref = refs[n_blk:]
    i = pl.program_id(0)
    q = q_ref[0] * (NSA_DH ** -0.5)
    qb = q.astype(BF16)
    gates = _sigmoid(gz_ref[0] + gb_ref[...])
    hrow = _iota((NSA_HEADS, NSA_DH), 0)
    n_win = win_ref.shape[2]
    wlane = _iota((NSA_HEADS, n_win), 1)
    pblk = jnp.right_shift(_iota((NSA_HEADS, PAGE_SIZE), 1), NSA_BLOCK.bit_length() - 1)
    o_all = jnp.zeros((NSA_HEADS, NSA_DH), F32)
    voff = NSA_KV_HEADS * NSA_DH
    for kh in range(NSA_KV_HEADS):
        ksl = slice(kh * NSA_DH, (kh + 1) * NSA_DH)
        vsl = slice(voff + kh * NSA_DH, voff + (kh + 1) * NSA_DH)
        slots = list(range(n_shared)) + list(range(n_shared + kh * n_own, n_shared + (kh + 1) * n_own))
        s_parts = []
        for sl in slots:
            blk_in_page = jnp.bitwise_and(info_ref[i, sl], _BLOCKS_PER_PAGE - 1)
            sp = _mm(qb, page_refs[sl][0, ksl, :].astype(BF16))
            s_parts.append(jnp.where(pblk == blk_in_page, sp, -jnp.inf))
        kn, vn = ns_ref[0, :, ksl], ns_ref[0, :, vsl]
        sn = jnp.sum(q * kn, axis=1, keepdims=True)
        m = sn
        for sp in s_parts:
            m = jnp.maximum(m, jnp.max(sp, axis=1, keepdims=True))
        en = jnp.exp(sn - m)
        l, acc = en, en * vn
        for sl, sp in zip(slots, s_parts):
            e = jnp.exp(sp - m)
            l = l + jnp.sum(e, axis=1, keepdims=True)
            acc = acc + _mm(e.astype(BF16), page_refs[sl][0, vsl, :].astype(BF16), _NT)
        o_sel = acc / l
        sw = jnp.where(wlane >= first_win_row, _mm(qb, win_ref[0, ksl, :].astype(BF16)), -jnp.inf)
        kwn, vwn = nw_ref[0, :, ksl], nw_ref[0, :, vsl]
        swn = jnp.sum(q * kwn, axis=1, keepdims=True)
        mw = jnp.maximum(jnp.max(sw, axis=1, keepdims=True), swn)
        ew, ewn = jnp.exp(sw - mw), jnp.exp(swn - mw)
        o_win = (_mm(ew.astype(BF16), win_ref[0, vsl, :].astype(BF16), _NT) + ewn * vwn) / (jnp.sum(ew, axis=1, keepdims=True) + ewn)
        keep = (hrow >= kh * NSA_GROUP) & (hrow < (kh + 1) * NSA_GROUP)
        o_all = jnp.where(keep, gates[:, 1:2] * o_sel + gates[:, 2:3] * o_win, o_all)
    o_ref[0] = o_all
    r2, c2 = _iota((NSA_KV_W, NSA_KV_W), 0), _iota((NSA_KV_W, NSA_KV_W), 1)
    new_col = jnp.sum(jnp.where(r2 == c2, jnp.broadcast_to(nw_ref[0], (NSA_KV_W, NSA_KV_W)), 0.0), axis=1, keepdims=True)
    shifted = pltpu.roll(win_ref[0], n_win - 1, 1)
    wout_ref[0] = jnp.where(_iota((NSA_KV_W, n_win), 1) == n_win - 1, new_col, shifted)


def _selwin_decode(info, qh, new_s, new_w, win_t, pool_t, gz, gb3, n_shared, n_own, first_win_row):
    b = qh.shape[0]
    n_blk = info.shape[1]
    row3 = lambda n: pl.BlockSpec((1, 1, n), lambda i, r: (i, 0, 0))

    def page_spec(s):
        return pl.BlockSpec((1, NSA_KV_W, PAGE_SIZE), lambda i, r: (jnp.right_shift(r[i, s], _PAGE_SHIFT), 0, 0))

    win_spec = pl.BlockSpec((1,) + win_t.shape[1:], lambda i, r: (i, 0, 0))
    grid_spec = pltpu.PrefetchScalarGridSpec(
        num_scalar_prefetch=1,
        grid=(b,),
        in_specs=[pl.BlockSpec((1, NSA_HEADS, NSA_DH), lambda i, r: (i, 0, 0)), row3(NSA_KV_W), row3(NSA_KV_W), win_spec]
        + [page_spec(s) for s in range(n_blk)]
        + [pl.BlockSpec((1, NSA_HEADS, NSA_BRANCHES), lambda i, r: (i, 0, 0)),
           pl.BlockSpec((NSA_HEADS, NSA_BRANCHES), lambda i, r: (0, 0))],
        out_specs=[pl.BlockSpec((1, NSA_HEADS, NSA_DH), lambda i, r: (i, 0, 0)), win_spec],
    )
    return pl.pallas_call(
        functools.partial(_selwin_decode_body, n_shared=n_shared, n_own=n_own, first_win_row=first_win_row),
        grid_spec=grid_spec,
        out_shape=[jax.ShapeDtypeStruct((b, NSA_HEADS, NSA_DH), F32), jax.ShapeDtypeStruct(win_t.shape, F32)],
        compiler_params=_params(1),
        name="nsa_selwin_decode",
    )(info, qh, new_s.reshape(b, 1, -1), new_w.reshape(b, 1, -1), win_t, *([pool_t] * n_blk), gz, gb3)


def _feature_major(cache):
    n, rows = cache.shape[:2]
    return jnp.transpose(cache, (0, 2, 3, 4, 1)).reshape(n, NSA_KV_W, rows)


def _pad_cols(w, n):
    return jnp.pad(w, ((0, 0), (0, n - w.shape[1])))


def _even_w_in(w):
    sizes = (GLA_HEADS * GLA_DK, GLA_HEADS * GLA_DK, GLA_HEADS * GLA_DV, GLA_HEADS * GLA_DV, GLA_RANK,
             NSA_HEADS * NSA_DH, NSA_BRANCHES * NSA_KV_W, NSA_HEADS * NSA_BRANCHES)
    cuts = [0]
    for s in sizes:
        cuts.append(cuts[-1] + s)
    gq, gk, gv, gg, ga, nq, nkv, ng = (w[:, cuts[i]:cuts[i + 1]] for i in range(len(sizes)))
    small = _pad_cols(jnp.concatenate([ga, ng], axis=1), LANE)
    sample = jnp.concatenate([gq, gk, gv, gg, nq, nkv, small], axis=1).astype(BF16)
    prompt = jnp.concatenate([gq, gk, gv, gg, nkv, small], axis=1).astype(BF16)
    prompt_t = jnp.concatenate([nq, nkv, small], axis=1).T.astype(BF16)
    return sample, prompt, prompt_t


_EVEN_WIDTHS = (GLA_Z_W, NSA_HEADS * NSA_DH, NSA_KV_W, NSA_KV_W, NSA_KV_W, LANE)
_EVEN_WIDTHS_P = (GLA_Z_W, NSA_KV_W, NSA_KV_W, NSA_KV_W, LANE)
_EVEN_WIDTHS_PT = (NSA_HEADS * NSA_DH, NSA_KV_W, NSA_KV_W, NSA_KV_W, LANE)


def _odd_w_in(w):
    main = M_Z_W
    return jnp.concatenate([w[:, :main], _pad_cols(w[:, main:], LANE)], axis=1).astype(BF16)


_ODD_WIDTHS = (M_Z_W, LANE)


def _compress_weights(pe, w1, w2):
    eye_k = jnp.eye(NSA_KV_HEADS, dtype=F32)
    eye_c = jnp.eye(2, dtype=F32)
    n_ck = 2 * NSA_KV_HEADS
    rows = jnp.broadcast_to(w1.transpose(1, 0, 2, 3)[:, :, None], (NSA_BLOCK, 2, NSA_KV_HEADS, NSA_DH, NSA_CMP_HID))
    rows = jnp.tile(rows.reshape(NSA_BLOCK * NSA_KV_W, NSA_CMP_HID), (1, n_ck))
    row_ck = (jnp.arange(NSA_BLOCK * NSA_KV_W) // NSA_DH) % n_ck
    col_ck = jnp.arange(NSA_KV_W) // NSA_CMP_HID
    w1big = jnp.where(row_ck[:, None] == col_ck[None, :], rows, 0.0)
    w2big = jnp.einsum("ced,cx,ky->ckexyd", w2, eye_c, eye_k).reshape(NSA_KV_W, NSA_KV_W)
    pe_flat = jnp.broadcast_to(pe.transpose(1, 0, 2)[:, :, None, :], (NSA_BLOCK, 2, NSA_KV_HEADS, NSA_DH)).reshape(1, -1)
    return pe_flat, w1big.astype(BF16), w2big.astype(BF16)


def _compress_weights_paged(pe, w1, w2):
    eye_b = jnp.eye(_BLOCKS_PER_PAGE, dtype=F32)
    pe_t = jnp.tile(pe.transpose(0, 2, 1), (1, 1, _BLOCKS_PER_PAGE)).reshape(2 * NSA_DH, PAGE_SIZE)
    rows = jnp.broadcast_to(w1.transpose(0, 2, 1, 3)[:, :, None], (2, NSA_DH, _BLOCKS_PER_PAGE, NSA_BLOCK, NSA_CMP_HID))
    rows = jnp.tile(rows.reshape(2, NSA_DH // _D_PER_STEP, _D_PER_STEP * PAGE_SIZE, NSA_CMP_HID), (1, 1, 1, _BLOCKS_PER_PAGE))
    row_blk = (jnp.arange(_D_PER_STEP * PAGE_SIZE) // NSA_BLOCK) % _BLOCKS_PER_PAGE
    col_blk = jnp.arange(_BLOCKS_PER_PAGE * NSA_CMP_HID) // NSA_CMP_HID
    w1_t = jnp.where(row_blk[:, None] == col_blk[None, :], rows, 0.0)
    w2_t = jnp.einsum("ced,hx->chexd", w2, eye_b).reshape(2, _BLOCKS_PER_PAGE * NSA_CMP_HID, _BLOCKS_PER_PAGE * NSA_DH)
    return pe_t, w1_t.astype(BF16), w2_t.astype(BF16)


def _gate_bias_row(gb):
    return jnp.pad(gb, (GATE_OFF, LANE - GATE_OFF - gb.shape[0])).reshape(1, LANE)


def _split_outs(outs, n):
    return outs[:n], outs[n:]


def _values_t(kv_t, rows=NSA_DH):
    b, _, t = kv_t.shape
    v = kv_t.reshape(b, 2, NSA_KV_HEADS, NSA_DH, t)[:, 1].astype(BF16)
    if rows > NSA_DH:
        extra = jnp.zeros((b, NSA_KV_HEADS, rows - NSA_DH, t), BF16).at[:, :, 0].set(1.0)
        v = jnp.concatenate([v, extra], axis=2)
    return v


def _kv_rows(kv_t):
    b, _, t = kv_t.shape
    return jnp.transpose(kv_t.reshape(b, 2, NSA_KV_HEADS, NSA_DH, t), (0, 4, 1, 2, 3))


def _even_layer_prompt(x, bsz, g, prm):
    m = x.shape[0]
    t = m // bsz
    (z, kvc, kvs, kvw, zs), (q_t, kvc_t, kvs_t, kvw_t, zs_t) = _split_outs(
        _norm_proj(x, g, prm["w_in_p"], _EVEN_WIDTHS_P, TM_PROJ, prm["w_in_pt"], _EVEN_WIDTHS_PT, bsz), len(_EVEN_WIDTHS_P))
    o_gla, s_fin = _gla_prompt(z, zs, prm["wa"], prm["ba"], prm["gn"],
                               jnp.zeros((bsz, GLA_HEADS, GLA_DK, GLA_DV), F32), bsz)
    nb = t // NSA_BLOCK
    nbp = -(-nb // LANE) * LANE
    kvcmp = _compress_dense(kvc.reshape(bsz * nb, NSA_BLOCK * NSA_KV_W), prm["pe"], prm["w1"], prm["w2"])
    kvcmp = jnp.pad(kvcmp.reshape(bsz, nb, NSA_KV_W), ((0, 0), (0, nbp - nb), (0, 0))).reshape(bsz * nbp, NSA_KV_W)
    o_cmp, nm = _cmp_sel(q_t, kvcmp, zs_t, prm["gb_col"], nbp, TQ_CMP)
    onehot = (jnp.arange(t)[:, None] // NSA_BLOCK == jnp.arange(nbp)[None, :]).astype(BF16)
    o_sel = _sel_attn(q_t, nm, kvs, onehot, _values_t(kvs_t, V_AUG), zs_t, prm["gb_col"], TQ_ATTN, TK_SEL)
    o_win = _win_attn(q_t, kvw, _values_t(kvw_t, V_AUG), zs_t, prm["gb_col"], TQ_ATTN)
    y = _out_proj(x, [o_gla], (1,), prm["w_out"], TM_OUT, [o_cmp, o_sel, o_win], bsz)
    n_keep = min(NSA_WINDOW, t)
    return y, s_fin, _kv_rows(kvc_t), _kv_rows(kvs_t), _kv_rows(kvw_t[:, :, t - n_keep:])


def _even_layer_sample(x, g, prm, gla_state, cmp_pool, sel_pool, win_buf, page_table):
    b = x.shape[0]
    n_pages = page_table.shape[1]
    past = n_pages * PAGE_SIZE
    n_past_blk = past // NSA_BLOCK
    z, nq, kvc, kvs, kvw, zs = _norm_proj(x, g, prm["w_in"], _EVEN_WIDTHS, b)
    o_gla, s_new = _gla_decode(z, zs, prm["wa"], prm["ba"], prm["gn"], gla_state)
    kvcmp = _compress_paged(_feature_major(cmp_pool), page_table, prm["pe_t"], prm["w1_t"], prm["w2_t"])
    qh = nq.reshape(b, NSA_HEADS, NSA_DH)
    gz = zs[:, GATE_OFF:GATE_OFF + NSA_HEADS * NSA_BRANCHES].reshape(b, NSA_HEADS, NSA_BRANCHES)
    forced = _decode_forced(n_past_blk)
    n_pick = NSA_TOP_N - len(forced)
    assert n_past_blk - len(forced) + 1 >= n_pick
    o_cmp, idx = _cmp_decode(qh, kvcmp, gz, prm["gb3"], past, n_pick)
    shared = [f for f in forced if f < n_past_blk]
    logical = jnp.concatenate([jnp.broadcast_to(jnp.asarray(shared, jnp.int32), (b, len(shared))),
                               idx[:, :NSA_KV_HEADS, :n_pick].reshape(b, NSA_KV_HEADS * n_pick)], axis=1)
    per_page = _BLOCKS_PER_PAGE
    phys = jnp.take_along_axis(page_table, logical // per_page, axis=1) * per_page + logical % per_page
    win_keep = win_buf.shape[1]
    assert win_keep >= 1
    first_win_row = max(win_keep - NSA_WINDOW + 1, 0)
    o_sw, win_new_t = _selwin_decode(phys.astype(jnp.int32), qh, kvs, kvw, _feature_major(win_buf), _feature_major(sel_pool),
                                     gz, prm["gb3"], len(shared), n_pick, first_win_row)
    hd = NSA_HEADS * NSA_DH
    y = _out_proj(x, [o_gla, o_cmp.reshape(b, hd), o_sw.reshape(b, hd)], (1, 2), prm["w_out"], b)
    kv_shape = (b, 1, 2, NSA_KV_HEADS, NSA_DH)
    win_new = jnp.transpose(win_new_t.reshape(b, 2, NSA_KV_HEADS, NSA_DH, win_keep), (0, 4, 1, 2, 3))
    return y, s_new, kvc.reshape(kv_shape), kvs.reshape(kv_shape), win_new


def _odd_layer_prompt(x, bsz, g, prm):
    z, zs = _norm_proj(x, g, prm["w_in"], _ODD_WIDTHS, TM_PROJ)
    h, c_aug, m_fin = _mlstm_prompt(z, zs, prm["bias"], prm["mn"], jnp.zeros((bsz, M_HEADS, M_DQK, M_AUG), F32),
                                    jnp.zeros((bsz, 1, LANE), F32), bsz)
    y = _out_proj(x, [h], (1,), prm["w_out"], TM_OUT)
    return y, c_aug[..., :M_DV], c_aug[..., M_DV], m_fin[:, 0, :M_HEADS]


def _odd_layer_sample(x, g, prm, c0, n0, m0):
    b = x.shape[0]
    z, zs = _norm_proj(x, g, prm["w_in"], _ODD_WIDTHS, b)
    h, cn, nn, mn = _mlstm_decode(z, zs, prm["bias"], prm["mn"], c0, n0, m0)
    return _out_proj(x, [h], (1,), prm["w_out"], b), cn, nn, mn


def kernel(x_prompt, x_sample, cache_cmp_kv, cache_sel_kv, cache_win_kv, state_gla, state_mlstm_c, state_mlstm_n, state_mlstm_m, state_ffn_conv, page_table, norm_mix, norm_ffn, norm_final, even_w_in, even_w_out, gla_w_a2, gla_b_a, gla_norm, nsa_cmp_pe, nsa_cmp_w1, nsa_cmp_w2, nsa_gate_b, odd_w_in, odd_w_out, mlstm_b_i, mlstm_b_f, mlstm_norm, ffn_w_up, ffn_conv_w, ffn_conv_b, ffn_w_down):
    bp, t, d = x_prompt.shape
    bs = x_sample.shape[0]
    assert x_sample.shape[1] == 1
    depth = norm_mix.shape[0]
    f = ffn_conv_w.shape[2]
    xp = x_prompt.reshape(bp * t, d)
    xs = x_sample.reshape(bs, d)
    outs = {k: [] for k in ("cmp_p", "cmp_s", "sel_p", "sel_s", "win_p", "win_s", "gla_p", "gla_s",
                            "mc_p", "mc_s", "mn_p", "mn_s", "mm_p", "mm_s", "cv_p", "cv_s")}
    for l in range(depth):
        if l % 2 == 0:
            e = l // 2
            pe, w1, w2 = _compress_weights(nsa_cmp_pe[e], nsa_cmp_w1[e], nsa_cmp_w2[e])
            pe_t, w1_t, w2_t = _compress_weights_paged(nsa_cmp_pe[e], nsa_cmp_w1[e], nsa_cmp_w2[e])
            w_in_s, w_in_p, w_in_pt = _even_w_in(even_w_in[e])
            prm = dict(pe_t=pe_t, w1_t=w1_t, w2_t=w2_t, w_in=w_in_s, w_in_p=w_in_p, w_in_pt=w_in_pt,
                       w_out=even_w_out[e].astype(BF16),
                       wa=jnp.pad(gla_w_a2[e], ((0, LANE - GLA_RANK), (0, 0))), ba=gla_b_a[e].reshape(1, -1),
                       gn=gla_norm[e].reshape(1, -1), pe=pe, w1=w1, w2=w2, gb_col=_gate_bias_row(nsa_gate_b[e]).reshape(LANE, 1),
                       gb3=nsa_gate_b[e].reshape(NSA_HEADS, NSA_BRANCHES))
            xp, s_, c_, k_, w_ = _even_layer_prompt(xp, bp, norm_mix[l], prm)
            outs["gla_p"].append(s_); outs["cmp_p"].append(c_); outs["sel_p"].append(k_); outs["win_p"].append(w_)
            xs, s_, c_, k_, w_ = _even_layer_sample(xs, norm_mix[l], prm, state_gla[e], cache_cmp_kv[e], cache_sel_kv[e],
                                                    cache_win_kv[e], page_table)
            outs["gla_s"].append(s_); outs["cmp_s"].append(c_); outs["sel_s"].append(k_); outs["win_s"].append(w_)
        else:
            o = l // 2
            bias = jnp.pad(jnp.concatenate([mlstm_b_i[o], mlstm_b_f[o]]), (0, LANE - 2 * M_HEADS)).reshape(1, LANE)
            prm = dict(w_in=_odd_w_in(odd_w_in[o]), w_out=odd_w_out[o].astype(BF16), bias=bias, mn=mlstm_norm[o].reshape(1, -1))
            xp, c_, n_, m_ = _odd_layer_prompt(xp, bp, norm_mix[l], prm)
            outs["mc_p"].append(c_); outs["mn_p"].append(n_); outs["mm_p"].append(m_)
            xs, c_, n_, m_ = _odd_layer_sample(xs, norm_mix[l], prm, state_mlstm_c[o], state_mlstm_n[o], state_mlstm_m[o])
            outs["mc_s"].append(c_); outs["mn_s"].append(n_); outs["mm_s"].append(m_)
        final = l == depth - 1
        wup, wd = ffn_w_up[l].astype(BF16), ffn_w_down[l].astype(BF16)
        xp, cv = _ffn_prompt(xp, bp, norm_ffn[l], wup, ffn_conv_w[l], ffn_conv_b[l], wd,
                             jnp.zeros((bp, CONV_W - 1, f), F32), norm_final, final, TM_FFN)
        outs["cv_p"].append(cv)
        xs, cv = _ffn_decode(xs, norm_ffn[l], wup, ffn_conv_w[l], ffn_conv_b[l], wd, state_ffn_conv[l], norm_final, final)
        outs["cv_s"].append(cv)
    st = lambda k: jnp.stack(outs[k])
    return (xp.reshape(bp, t, d), xs.reshape(bs, 1, d),
            st("cmp_p"), st("cmp_s"), st("sel_p"), st("sel_s"), st("win_p"), st("win_s"), st("gla_p"), st("gla_s"),
            st("mc_p"), st("mc_s"), st("mn_p"), st("mn_s"), st("mm_p"), st("mm_s"), st("cv_p"), st("cv_s"))
```

```python
import functools

import jax
import jax.numpy as jnp
from jax import lax
from jax.experimental import pallas as pl
from jax.experimental.pallas import tpu as pltpu

F32 = jnp.float32
BF16 = jnp.bfloat16

GLA_HEADS, GLA_DK, GLA_DV, GLA_RANK, GLA_TAU, GLA_CHUNK = 4, 64, 128, 16, 16.0, 64
NSA_HEADS, NSA_KV_HEADS, NSA_GROUP, NSA_DH = 8, 2, 4, 64
NSA_BRANCHES, NSA_BLOCK, NSA_TOP_N, NSA_WINDOW, NSA_CMP_HID = 3, 64, 16, 512, 64
M_HEADS, M_DQK, M_DV, M_CHUNK = 4, 128, 256, 64
M_CHUNK_PALLAS = 128
CONV_W = 3
PAGE_SIZE = 128
EPS, TINY = 1e-6, 1e-30
LOG2E = 1.4426950408889634

LANE = 128
SUBLANE = 8
VMEM_BIG = 52 * 1024 * 1024
VMEM_MID = 40 * 1024 * 1024

MASK_BIG = 32768.0

NSA_KV_W = 2 * NSA_KV_HEADS * NSA_DH
GLA_Z_W = 2 * GLA_HEADS * GLA_DK + 2 * GLA_HEADS * GLA_DV
M_Z_W = 2 * M_HEADS * M_DQK + 2 * M_HEADS * M_DV
M_AUG = M_DV + LANE
GATE_OFF = GLA_RANK
TM_PROJ = 512
TM_OUT = 512
TM_FFN = 256
TQ_CMP = 512
TQ_ATTN = 128
TK_SEL = 512
V_AUG = NSA_DH + 16
_BLOCKS_PER_PAGE = PAGE_SIZE // NSA_BLOCK
_PAGE_SHIFT = _BLOCKS_PER_PAGE.bit_length() - 1

_NN = (((1,), (0,)), ((), ()))
_NT = (((1,), (1,)), ((), ()))
_TN = (((0,), (0,)), ((), ()))


def _mm(a, b, dims=_NN):
    return lax.dot_general(a, b, dims, preferred_element_type=F32)


def _split_bf16(x, n):
    parts, r = [], x
    for _ in range(n):
        p = r.astype(BF16)
        parts.append(p)
        r = r - p.astype(F32)
    return parts


def _mm_sel(sel, x, dims=_NN, x_is_rhs=True):
    out = None
    for p in _split_bf16(x, 3):
        t = _mm(sel, p, dims) if x_is_rhs else _mm(p, sel, dims)
        out = t if out is None else out + t
    return out


def _mm_hp(a, b, dims=_NN):
    a1, a2 = _split_bf16(a, 2)
    b1, b2 = _split_bf16(b, 2)
    return _mm(a1, b1, dims) + (_mm(a1, b2, dims) + _mm(a2, b1, dims))


def _gelu(x):
    return 0.5 * x * (1.0 + jnp.tanh(0.7978845608028654 * (x + 0.044715 * (x * x * x))))


def _sigmoid(x):
    return 1.0 / (1.0 + jnp.exp(-x))


def _logsigmoid(x):
    return jnp.minimum(x, 0.0) - jnp.log(1.0 + jnp.exp(-jnp.abs(x)))


def _rms(x, g):
    return x * lax.rsqrt(jnp.mean(x * x, axis=-1, keepdims=True) + EPS) * g


def _iota(shape, dim):
    return lax.broadcasted_iota(jnp.int32, shape, dim)


def _params(n_axes, vmem=VMEM_MID):
    return pltpu.CompilerParams(dimension_semantics=("arbitrary",) * n_axes, vmem_limit_bytes=vmem)


def _resident(shape):
    nd = len(shape)
    return pl.BlockSpec(shape, lambda *_: (0,) * nd, pipeline_mode=pl.Buffered(1))


def _norm_proj_body(x_ref, g_ref, w_ref, *refs, widths, t_widths):
    xb = _rms(x_ref[...], g_ref[...]).astype(BF16)
    o_refs = refs[1:] if t_widths else refs
    off = 0
    for o_ref, n in zip(o_refs[:len(widths)], widths):
        o_ref[...] = _mm(xb, w_ref[:, off:off + n])
        off += n
    off = 0
    for o_ref, n in zip(o_refs[len(widths):], t_widths):
        o_ref[0] = _mm(refs[0][off:off + n, :], xb, _NT)
        off += n


def _norm_proj(x, g, w, widths, tm, wt=None, t_widths=(), bsz=1):
    m, d = x.shape
    tm = min(tm, m)
    nt = m // bsz // tm
    assert m % tm == 0 and sum(widths) == w.shape[1] and (m // bsz) % tm == 0
    t_in = [_resident(wt.shape)] if t_widths else []
    t_args = [wt] if t_widths else []
    return pl.pallas_call(
        functools.partial(_norm_proj_body, widths=tuple(widths), t_widths=tuple(t_widths)),
        grid=(m // tm,),
        in_specs=[pl.BlockSpec((tm, d), lambda i: (i, 0)), _resident((1, d)), _resident(w.shape)] + t_in,
        out_specs=[pl.BlockSpec((tm, n), lambda i: (i, 0)) for n in widths]
        + [pl.BlockSpec((1, n, tm), lambda i: (i // nt, 0, i % nt)) for n in t_widths],
        out_shape=[jax.ShapeDtypeStruct((m, n), F32) for n in widths]
        + [jax.ShapeDtypeStruct((bsz, n, m // bsz), F32) for n in t_widths],
        compiler_params=_params(1),
        name="norm_proj",
    )(x, g.reshape(1, d), w, *t_args)


def _mix_residual(x, h_refs, w_ref, groups, n_t):
    acc = x
    i = off = 0
    for gsz in groups:
        h = h_refs[i][...]
        for j in range(1, gsz):
            h = h + h_refs[i + j][...]
        i += gsz
        n = h.shape[1]
        acc = acc + _mm(h.astype(BF16), w_ref[off:off + n, :])
        off += n
    if n_t:
        ht = h_refs[i][0]
        for j in range(1, n_t):
            ht = ht + h_refs[i + j][0]
        acc = acc + _mm(ht.astype(BF16), w_ref[off:off + ht.shape[0], :], _TN)
    return acc


def _out_proj_body(x_ref, *refs, groups, n_t):
    h_refs, w_ref, o_ref = refs[:-2], refs[-2], refs[-1]
    o_ref[...] = _mix_residual(x_ref[...], h_refs, w_ref, groups, n_t)


def _out_proj(x, hs, groups, w, tm, t_hs=(), bsz=1):
    m, d = x.shape
    tm = min(tm, m)
    nt = m // bsz // tm
    assert m % tm == 0 and (m // bsz) % tm == 0
    return pl.pallas_call(
        functools.partial(_out_proj_body, groups=tuple(groups), n_t=len(t_hs)),
        grid=(m // tm,),
        in_specs=[pl.BlockSpec((tm, d), lambda i: (i, 0))]
        + [pl.BlockSpec((tm, h.shape[1]), lambda i: (i, 0)) for h in hs]
        + [pl.BlockSpec((1, h.shape[1], tm), lambda i: (i // nt, 0, i % nt)) for h in t_hs]
        + [_resident(w.shape)],
        out_specs=pl.BlockSpec((tm, d), lambda i: (i, 0)),
        out_shape=jax.ShapeDtypeStruct((m, d), F32),
        compiler_params=_params(1),
        name="out_proj",
    )(x, *hs, *t_hs, w)


def _ffn_chunks(xn, resid, wup_ref, cw_ref, cb_ref, wd_ref, fc, prev_fn, keep_fn):
    f = cw_ref.shape[1]
    n = f // fc

    def up_proj(c):
        return _mm(xn, wup_ref[:, c * fc:(c + 1) * fc]), _mm(xn, wup_ref[:, f + c * fc:f + (c + 1) * fc])

    acc = resid
    nxt = up_proj(0)
    for c in range(n):
        sl = slice(c * fc, (c + 1) * fc)
        gp, up = nxt
        if c + 1 < n:
            nxt = up_proj(c + 1)
        g2, g1 = prev_fn(gp, sl)
        a = cb_ref[:, sl] + g2 * cw_ref[0:1, sl] + g1 * cw_ref[1:2, sl] + gp * cw_ref[2:3, sl]
        acc = acc + _mm((_gelu(a) * up).astype(BF16), wd_ref[sl, :])
        keep_fn(gp, sl)
    return acc


def _ffn_prompt_body(x_ref, g_ref, wup_ref, cw_ref, cb_ref, wd_ref, st_ref, gf_ref, y_ref, ns_ref, carry, *, final, fc):
    t = pl.program_id(1)
    tm = x_ref.shape[0]

    @pl.when(t == 0)
    def _():
        carry[...] = st_ref[0]

    x = x_ref[...]
    xn = _rms(x, g_ref[...]).astype(BF16)
    row = _iota((tm, fc), 0)

    def prev_fn(gp, sl):
        c0, c1 = carry[0:1, sl], carry[1:2, sl]
        g1 = jnp.where(row == 0, c1, pltpu.roll(gp, 1, 0))
        g2 = jnp.where(row == 0, c0, jnp.where(row == 1, c1, pltpu.roll(gp, 2, 0)))
        return g2, g1

    def keep_fn(gp, sl):
        carry[:, sl] = gp[tm - 2:tm, :]

    acc = _ffn_chunks(xn, x, wup_ref, cw_ref, cb_ref, wd_ref, fc, prev_fn, keep_fn)
    y_ref[...] = _rms(acc, gf_ref[...]) if final else acc

    @pl.when(t == pl.num_programs(1) - 1)
    def _():
        ns_ref[0] = carry[...]


def _ffn_prompt(x, bsz, g, wup, cw, cb, wd, st, gf, final, tm, fc=256):
    m, d = x.shape
    t = m // bsz
    tm = min(tm, t)
    f = cw.shape[1]
    assert t % tm == 0 and f % fc == 0
    nt = t // tm
    rows = lambda n: pl.BlockSpec((tm, n), lambda b, i: (b * nt + i, 0))
    return pl.pallas_call(
        functools.partial(_ffn_prompt_body, final=final, fc=fc),
        grid=(bsz, nt),
        in_specs=[rows(d), _resident((1, d)), _resident(wup.shape), _resident(cw.shape), _resident((1, f)),
                  _resident(wd.shape), pl.BlockSpec((1, CONV_W - 1, f), lambda b, i: (b, 0, 0)), _resident((1, d))],
        out_specs=[rows(d), pl.BlockSpec((1, CONV_W - 1, f), lambda b, i: (b, 0, 0))],
        out_shape=[jax.ShapeDtypeStruct((m, d), F32), jax.ShapeDtypeStruct((bsz, CONV_W - 1, f), F32)],
        scratch_shapes=[pltpu.VMEM((CONV_W - 1, f), F32)],
        compiler_params=_params(2, VMEM_BIG),
        name="ffn_prompt",
    )(x, g.reshape(1, d), wup, cw, cb.reshape(1, f), wd, st, gf.reshape(1, d))


def _ffn_decode_body(x_ref, g_ref, wup_ref, cw_ref, cb_ref, wd_ref, s0_ref, s1_ref, gf_ref, y_ref, gp_ref, *, final, fc):
    x = x_ref[...]
    xn = _rms(x, g_ref[...]).astype(BF16)

    def prev_fn(gp, sl):
        return s0_ref[:, sl], s1_ref[:, sl]

    def keep_fn(gp, sl):
        gp_ref[:, sl] = gp

    acc = _ffn_chunks(xn, x, wup_ref, cw_ref, cb_ref, wd_ref, fc, prev_fn, keep_fn)
    y_ref[...] = _rms(acc, gf_ref[...]) if final else acc


def _ffn_decode(x, g, wup, cw, cb, wd, st, gf, final, fc=256):
    m, d = x.shape
    f = cw.shape[1]
    y, gp = pl.pallas_call(
        functools.partial(_ffn_decode_body, final=final, fc=fc),
        grid=(1,),
        in_specs=[_resident((m, d)), _resident((1, d)), _resident(wup.shape), _resident(cw.shape), _resident((1, f)),
                  _resident(wd.shape), _resident((m, f)), _resident((m, f)), _resident((1, d))],
        out_specs=[pl.BlockSpec((m, d), lambda i: (0, 0)), pl.BlockSpec((m, f), lambda i: (0, 0))],
        out_shape=[jax.ShapeDtypeStruct((m, d), F32), jax.ShapeDtypeStruct((m, f), F32)],
        compiler_params=_params(1, VMEM_BIG),
        name="ffn_decode",
    )(x, g.reshape(1, d), wup, cw, cb.reshape(1, f), wd, st[:, 0], st[:, 1], gf.reshape(1, d))
    return y, jnp.stack([st[:, 1], gp], axis=1)


def _gla_finish(o, gg, gn):
    return _rms(o, gn) * (gg * _sigmoid(gg))


def _gla_body(z_ref, zs_ref, wa_ref, ba_ref, gn_ref, s0_ref, o_ref, sfin_ref, s_scr, s_prev):
    t = pl.program_id(1)
    nb = z_ref.shape[0]

    @pl.when(t == 0)
    def _():
        s_scr[...] = s0_ref[...]

    s_prev[...] = s_scr[...]
    lasts = []
    _round_robin([_gla_chunk(z_ref.at[bi], zs_ref.at[bi], wa_ref, ba_ref, gn_ref, o_ref.at[bi], s_scr.at[bi], lasts)
                  for bi in range(nb)])

    @pl.when(jnp.min(jnp.concatenate(lasts, axis=0)) < -_GLA_FACTORED_RANGE)
    def _():
        for bi in range(nb):
            for h in range(GLA_HEADS):
                _gla_head_exact(h, z_ref.at[bi], zs_ref.at[bi], wa_ref, ba_ref, gn_ref, o_ref.at[bi], s_prev.at[bi])

    @pl.when(t == pl.num_programs(1) - 1)
    def _():
        sfin_ref[...] = s_scr[...]


_GLA_FACTORED_RANGE = 80.0


def _gla_gates(zs_ref, wa_ref, ba_ref):
    c = zs_ref.shape[0]
    la = _logsigmoid(_mm_hp(zs_ref[...], wa_ref[...]) + ba_ref[...]) * (1.0 / GLA_TAU)
    tri = _iota((c, c), 0) >= _iota((c, c), 1)
    return la, _mm_sel(jnp.where(tri, 1.0, 0.0).astype(BF16), la)


def _gla_head_exact(h, z_ref, zs_ref, wa_ref, ba_ref, gn_ref, o_ref, s_prev):
    c = z_ref.shape[0]
    hk = GLA_HEADS * GLA_DK
    ks = slice(h * GLA_DK, (h + 1) * GLA_DK)
    ch = _gla_gates(zs_ref, wa_ref, ba_ref)[1][:, ks]
    q = z_ref[:, ks] * (GLA_DK ** -0.5)
    k = z_ref[:, hk + h * GLA_DK:hk + (h + 1) * GLA_DK]
    v = z_ref[:, 2 * hk + h * GLA_DV:2 * hk + (h + 1) * GLA_DV]
    gg = z_ref[:, 2 * hk + GLA_HEADS * GLA_DV + h * GLA_DV:2 * hk + GLA_HEADS * GLA_DV + (h + 1) * GLA_DV]
    o_inter = _mm((q * jnp.exp(ch)).astype(BF16), s_prev[h].astype(BF16))
    row, rowv = _iota((c, GLA_DK), 0), _iota((c, GLA_DV), 0)

    def token(i, o_acc):
        sel = row == i
        qi = jnp.sum(jnp.where(sel, q, 0.0), axis=0, keepdims=True)
        ci = jnp.sum(jnp.where(sel, ch, 0.0), axis=0, keepdims=True)
        w = jnp.exp(jnp.minimum(ci - ch, 0.0))
        att = jnp.sum(jnp.where(row <= i, qi * k * w, 0.0), axis=1, keepdims=True)
        return jnp.where(rowv == i, jnp.sum(att * v, axis=0, keepdims=True), o_acc)

    o_intra = lax.fori_loop(0, c, token, jnp.zeros((c, GLA_DV), F32))
    o_ref[:, h * GLA_DV:(h + 1) * GLA_DV] = _gla_finish(o_inter + o_intra, gg, gn_ref[...])


def _round_robin(chains):
    chains = list(chains)
    while chains:
        chains = [c for c in chains if next(c, _DONE) is not _DONE]


_DONE = object()


def _gla_chunk(z_ref, zs_ref, wa_ref, ba_ref, gn_ref, o_ref, s_scr, lasts):
    c = z_ref.shape[0]
    hk = GLA_HEADS * GLA_DK
    la = _logsigmoid(_mm_hp(zs_ref[...], wa_ref[...]) + ba_ref[...]) * (1.0 / GLA_TAU)
    yield
    r, cidx = _iota((c, c), 0), _iota((c, c), 1)
    tri = r >= cidx
    cum = _mm_sel(jnp.where(tri, 1.0, 0.0).astype(BF16), la)
    yield
    last = cum[c - 1:c, :]
    lasts.append(last)
    eq, ek, ekl, el = jnp.exp(cum), jnp.exp(-cum), jnp.exp(last - cum), jnp.exp(last)
    gn = gn_ref[...]
    for h in range(GLA_HEADS):
        ks = slice(h * GLA_DK, (h + 1) * GLA_DK)
        vs = slice(2 * hk + h * GLA_DV, 2 * hk + (h + 1) * GLA_DV)
        gs = slice(2 * hk + GLA_HEADS * GLA_DV + h * GLA_DV, 2 * hk + GLA_HEADS * GLA_DV + (h + 1) * GLA_DV)
        q = z_ref[:, ks] * (GLA_DK ** -0.5)
        k = z_ref[:, hk + h * GLA_DK:hk + (h + 1) * GLA_DK]
        v = z_ref[:, vs]
        qt = (q * eq[:, ks]).astype(BF16)
        s_old = s_scr[h]
        att = jnp.where(tri, _mm(qt, (k * ek[:, ks]).astype(BF16), _NT), 0.0)
        vb = v.astype(BF16)
        o_inter = _mm(qt, s_old.astype(BF16))
        kv_new = _mm((k * ekl[:, ks]).astype(BF16), vb, _TN)
        yield
        o = o_inter + _mm(att.astype(BF16), vb)
        ecol = jnp.sum(jnp.where(r == cidx, jnp.broadcast_to(el[:, ks], (c, c)), 0.0), axis=1, keepdims=True)
        s_scr[h] = ecol * s_old + kv_new
        yield
        o_ref[:, h * GLA_DV:(h + 1) * GLA_DV] = _gla_finish(o, z_ref[:, gs], gn)


def _seq_group(bsz):
    return next(n for n in (4, 2, 1) if bsz % n == 0)


def _gla_prompt(z, zs, wa, ba, gn, s0, bsz):
    m = z.shape[0]
    t = m // bsz
    c = GLA_CHUNK
    assert t % c == 0 and GLA_DK == c
    nc = t // c
    hk = GLA_HEADS * GLA_DK
    nb = _seq_group(bsz)
    st = (nb, GLA_HEADS, GLA_DK, GLA_DV)
    tok = lambda n: pl.BlockSpec((nb, c, n), lambda b, i: (b, i, 0))
    o, s_fin = pl.pallas_call(
        _gla_body,
        grid=(bsz // nb, nc),
        in_specs=[tok(GLA_Z_W), tok(LANE), _resident((LANE, hk)), _resident((1, hk)), _resident((1, GLA_DV)),
                  pl.BlockSpec(st, lambda b, i: (b, 0, 0, 0))],
        out_specs=[tok(GLA_HEADS * GLA_DV), pl.BlockSpec(st, lambda b, i: (b, 0, 0, 0))],
        out_shape=[jax.ShapeDtypeStruct((bsz, t, GLA_HEADS * GLA_DV), F32),
                   jax.ShapeDtypeStruct((bsz, GLA_HEADS, GLA_DK, GLA_DV), F32)],
        scratch_shapes=[pltpu.VMEM(st, F32), pltpu.VMEM(st, F32)],
        compiler_params=_params(2),
        name="gla_prompt",
    )(z.reshape(bsz, t, -1), zs.reshape(bsz, t, -1), wa, ba, gn, s0)
    return o.reshape(m, -1), s_fin


def _gla_decode_body(z_ref, zs_ref, wa_ref, ba_ref, gn_ref, s_ref, o_ref, sn_ref):
    hk = GLA_HEADS * GLA_DK
    rows = SUBLANE
    z = jnp.broadcast_to(z_ref[0], (rows, GLA_Z_W))
    ga = jnp.broadcast_to(zs_ref[0], (rows, LANE))
    la = _logsigmoid(_mm_hp(ga, wa_ref[...]) + ba_ref[...]) * (1.0 / GLA_TAU)
    ea = jnp.exp(la)
    r, cidx = _iota((GLA_DK, GLA_DK), 0), _iota((GLA_DK, GLA_DK), 1)
    row0 = _iota((rows, GLA_DK), 0) == 0
    gn = gn_ref[...]
    for h in range(GLA_HEADS):
        ks = slice(h * GLA_DK, (h + 1) * GLA_DK)
        q = z[:, ks] * (GLA_DK ** -0.5)
        k = z[:, hk + h * GLA_DK:hk + (h + 1) * GLA_DK]
        v = z[:, 2 * hk + h * GLA_DV:2 * hk + (h + 1) * GLA_DV]
        gg = z[:, 2 * hk + GLA_HEADS * GLA_DV + h * GLA_DV:2 * hk + GLA_HEADS * GLA_DV + (h + 1) * GLA_DV]
        s_old = s_ref[0, h]
        o = _mm_hp(q * ea[:, ks], s_old) + jnp.sum(q * k, axis=1, keepdims=True) * v
        ecol = jnp.sum(jnp.where(r == cidx, jnp.broadcast_to(ea[0:1, ks], (GLA_DK, GLA_DK)), 0.0), axis=1, keepdims=True)
        sn_ref[0, h] = ecol * s_old + _mm_hp(jnp.where(row0, k, 0.0), v, _TN)
        o_ref[0, :, h * GLA_DV:(h + 1) * GLA_DV] = _gla_finish(o, gg, gn)[0:1]


def _gla_decode(z, zs, wa, ba, gn, s0):
    b = z.shape[0]
    hk = GLA_HEADS * GLA_DK
    st = (1, GLA_HEADS, GLA_DK, GLA_DV)
    o, sn = pl.pallas_call(
        _gla_decode_body,
        grid=(b,),
        in_specs=[pl.BlockSpec((1, 1, GLA_Z_W), lambda i: (i, 0, 0)), pl.BlockSpec((1, 1, LANE), lambda i: (i, 0, 0)),
                  _resident((LANE, hk)), _resident((1, hk)), _resident((1, GLA_DV)),
                  pl.BlockSpec(st, lambda i: (i, 0, 0, 0))],
        out_specs=[pl.BlockSpec((1, 1, GLA_HEADS * GLA_DV), lambda i: (i, 0, 0)), pl.BlockSpec(st, lambda i: (i, 0, 0, 0))],
        out_shape=[jax.ShapeDtypeStruct((b, 1, GLA_HEADS * GLA_DV), F32), jax.ShapeDtypeStruct((b,) + st[1:], F32)],
        compiler_params=_params(1),
        name="gla_decode",
    )(z.reshape(b, 1, -1), zs.reshape(b, 1, -1), wa, ba, gn, s0)
    return o.reshape(b, -1), sn


def _mlstm_finish(hh, og, mn):
    return _rms(hh, mn) * _sigmoid(og)


def _mlstm_body(z_ref, zs_ref, bias_ref, mn_ref, c0_ref, m0_ref, h_ref, cfin_ref, mfin_ref, c_scr, m_scr):
    t = pl.program_id(1)

    @pl.when(t == 0)
    def _():
        c_scr[...] = c0_ref[...]
        m_scr[...] = m0_ref[...]

    _round_robin([_mlstm_chunk(z_ref.at[bi], zs_ref.at[bi], bias_ref, mn_ref, h_ref.at[bi], c_scr.at[bi], m_scr.at[bi])
                  for bi in range(z_ref.shape[0])])

    @pl.when(t == pl.num_programs(1) - 1)
    def _():
        cfin_ref[...] = c_scr[...]
        mfin_ref[...] = m_scr[...]


def _mlstm_chunk(z_ref, zs_ref, bias_ref, mn_ref, h_ref, c_scr, m_scr):
    L = z_ref.shape[0]
    hq = M_HEADS * M_DQK
    g = zs_ref[...] + bias_ref[...]
    lane = _iota((L, LANE), 1)
    gx = jnp.where((lane >= M_HEADS) & (lane < 2 * M_HEADS), _logsigmoid(g), g)
    r, cidx = _iota((L, L), 0), _iota((L, L), 1)
    tri = r >= cidx
    cum_c = _mm_sel(jnp.where(tri, 1.0, 0.0).astype(BF16), gx)
    rows = _mm_sel(jnp.where(r == cidx, 1.0, 0.0).astype(BF16), gx, _TN, x_is_rhs=False)
    cum_r = _mm_sel(jnp.where(r <= cidx, 1.0, 0.0).astype(BF16), gx, _TN, x_is_rhs=False)
    yield
    ones_col = jnp.where(_iota((L, LANE), 1) == 0, 1.0, 0.0).astype(BF16)
    m_all = m_scr[...]
    m_new_all = m_all
    for h in range(M_HEADS):
        q = z_ref[:, h * M_DQK:(h + 1) * M_DQK].astype(BF16)
        k = z_ref[:, hq + h * M_DQK:hq + (h + 1) * M_DQK] * (M_DQK ** -0.5)
        v = z_ref[:, 2 * hq + h * M_DV:2 * hq + (h + 1) * M_DV]
        og = z_ref[:, 2 * hq + M_HEADS * M_DV + h * M_DV:2 * hq + M_HEADS * M_DV + (h + 1) * M_DV]
        va = jnp.concatenate([v.astype(BF16), ones_col], axis=1)
        ic_r, ic_c = rows[h:h + 1, :], gx[:, h:h + 1]
        cr, cc = cum_r[M_HEADS + h:M_HEADS + h + 1, :], cum_c[:, M_HEADS + h:M_HEADS + h + 1]
        m_old = m_all[0:1, h:h + 1]
        dlog = jnp.where(tri, cc - cr + ic_r, -jnp.inf)
        inter = cc + m_old
        mi = jnp.maximum(inter, jnp.max(dlog, axis=1, keepdims=True))
        w = jnp.exp(dlog - mi)
        wi = jnp.exp(inter - mi)
        qk = _mm(q, k.astype(BF16), _NT)
        c_old = c_scr[h]
        qc = _mm(q, c_old.astype(BF16))
        yield
        s = qk * w
        num = wi * qc + _mm(s.astype(BF16), va)
        yield
        qn = num[:, M_DV:M_DV + 1]
        hh = num[:, :M_DV] / jnp.maximum(jnp.abs(qn), jnp.exp(-mi))
        last = cc[L - 1:L, :]
        gl = last - cc + ic_c
        m_new = jnp.maximum(last + m_old, jnp.max(gl, axis=0, keepdims=True))
        wj = jnp.exp(gl - m_new)
        keep = jnp.exp(last + m_old - m_new)
        c_scr[h] = keep * c_old + _mm((wj * k).astype(BF16), va, _TN)
        yield
        m_new_all = jnp.where(_iota((1, LANE), 1) == h, m_new, m_new_all)
        h_ref[:, h * M_DV:(h + 1) * M_DV] = _mlstm_finish(hh, og, mn_ref[:, h * M_DV:(h + 1) * M_DV])
    m_scr[...] = m_new_all


def _mlstm_prompt(z, zs, bias, mn, c0, m0, bsz):
    m = z.shape[0]
    t = m // bsz
    L = next(n for n in (M_CHUNK_PALLAS, M_CHUNK) if t % n == 0)
    nc = t // L
    nb = _seq_group(bsz)
    st, ms = (nb, M_HEADS, M_DQK, M_AUG), (nb, 1, LANE)
    tok = lambda n: pl.BlockSpec((nb, L, n), lambda b, i: (b, i, 0))
    h, c_fin, m_fin = pl.pallas_call(
        _mlstm_body,
        grid=(bsz // nb, nc),
        in_specs=[tok(M_Z_W), tok(LANE), _resident((1, LANE)), _resident((1, M_HEADS * M_DV)),
                  pl.BlockSpec(st, lambda b, i: (b, 0, 0, 0)), pl.BlockSpec(ms, lambda b, i: (b, 0, 0))],
        out_specs=[tok(M_HEADS * M_DV), pl.BlockSpec(st, lambda b, i: (b, 0, 0, 0)), pl.BlockSpec(ms, lambda b, i: (b, 0, 0))],
        out_shape=[jax.ShapeDtypeStruct((bsz, t, M_HEADS * M_DV), F32), jax.ShapeDtypeStruct((bsz,) + st[1:], F32),
                   jax.ShapeDtypeStruct((bsz, 1, LANE), F32)],
        scratch_shapes=[pltpu.VMEM(st, F32), pltpu.VMEM(ms, F32)],
        compiler_params=_params(2),
        name="mlstm_prompt",
    )(z.reshape(bsz, t, -1), zs.reshape(bsz, t, -1), bias, mn, c0, m0)
    return h.reshape(m, -1), c_fin, m_fin


def _mlstm_decode_body(z_ref, zs_ref, bias_ref, mn_ref, c_ref, n_ref, m_ref, h_ref, cn_ref, nn_ref, mo_ref):
    hq = M_HEADS * M_DQK
    rows = SUBLANE
    z = jnp.broadcast_to(z_ref[0], (rows, M_Z_W))
    g = zs_ref[0] + bias_ref[...]
    m_all = m_ref[0]
    m_new_all = m_all
    row0 = _iota((rows, M_DQK), 0) == 0
    for h in range(M_HEADS):
        q = z[:, h * M_DQK:(h + 1) * M_DQK]
        k = z[:, hq + h * M_DQK:hq + (h + 1) * M_DQK] * (M_DQK ** -0.5)
        v = z[:, 2 * hq + h * M_DV:2 * hq + (h + 1) * M_DV]
        og = z[:, 2 * hq + M_HEADS * M_DV + h * M_DV:2 * hq + M_HEADS * M_DV + (h + 1) * M_DV]
        ic = g[:, h:h + 1]
        fl = _logsigmoid(g[:, M_HEADS + h:M_HEADS + h + 1])
        m_old = m_all[:, h:h + 1]
        c_old, n_old = c_ref[0, h], n_ref[0, h:h + 1, :]
        mi = jnp.maximum(fl + m_old, ic)
        w = jnp.exp(ic - mi)
        wi = jnp.exp(fl + m_old - mi)
        s = jnp.sum(q * k, axis=1, keepdims=True) * w
        num = wi * _mm_hp(q, c_old) + s * v
        qn = wi * jnp.sum(q * n_old, axis=1, keepdims=True) + s
        hh = num / jnp.maximum(jnp.abs(qn), jnp.exp(-mi))
        cn_ref[0, h] = wi * c_old + w * _mm_hp(jnp.where(row0, k, 0.0), v, _TN)
        nn_ref[0, h:h + 1, :] = wi * n_old + w * k[0:1]
        m_new_all = jnp.where(_iota((1, LANE), 1) == h, mi, m_new_all)
        h_ref[0, :, h * M_DV:(h + 1) * M_DV] = _mlstm_finish(hh, og, mn_ref[:, h * M_DV:(h + 1) * M_DV])[0:1]
    mo_ref[0] = m_new_all


def _mlstm_decode(z, zs, bias, mn, c0, n0, m0):
    b = z.shape[0]
    cs, ns = (1, M_HEADS, M_DQK, M_DV), (1, M_HEADS, M_DQK)
    m0p = jnp.pad(m0, ((0, 0), (0, LANE - M_HEADS))).reshape(b, 1, LANE)
    row3 = lambda n: pl.BlockSpec((1, 1, n), lambda i: (i, 0, 0))
    h, cn, nn, mo = pl.pallas_call(
        _mlstm_decode_body,
        grid=(b,),
        in_specs=[row3(M_Z_W), row3(LANE), _resident((1, LANE)), _resident((1, M_HEADS * M_DV)),
                  pl.BlockSpec(cs, lambda i: (i, 0, 0, 0)), pl.BlockSpec(ns, lambda i: (i, 0, 0)), row3(LANE)],
        out_specs=[row3(M_HEADS * M_DV), pl.BlockSpec(cs, lambda i: (i, 0, 0, 0)), pl.BlockSpec(ns, lambda i: (i, 0, 0)), row3(LANE)],
        out_shape=[jax.ShapeDtypeStruct((b, 1, M_HEADS * M_DV), F32), jax.ShapeDtypeStruct((b,) + cs[1:], F32),
                   jax.ShapeDtypeStruct((b,) + ns[1:], F32), jax.ShapeDtypeStruct((b, 1, LANE), F32)],
        compiler_params=_params(1),
        name="mlstm_decode",
    )(z.reshape(b, 1, -1), zs.reshape(b, 1, -1), bias, mn, c0, n0, m0p)
    return h.reshape(b, -1), cn, nn, mo[:, 0, :M_HEADS]


def _compress_dense_body(x_ref, pe_ref, w1_ref, w2_ref, o_ref, acc):
    kk = pl.program_id(1)

    @pl.when(kk == 0)
    def _():
        acc[...] = jnp.zeros_like(acc)

    acc[...] += _mm((x_ref[...] + pe_ref[...]).astype(BF16), w1_ref[...])

    @pl.when(kk == pl.num_programs(1) - 1)
    def _():
        o_ref[...] = _mm(_gelu(acc[...]).astype(BF16), w2_ref[...])


def _compress_dense(x, pe, w1, w2, tk=2048):
    r, kdim = x.shape
    tr = r if r <= 512 else 512
    assert r % tr == 0 and kdim % tk == 0
    return pl.pallas_call(
        _compress_dense_body,
        grid=(r // tr, kdim // tk),
        in_specs=[pl.BlockSpec((tr, tk), lambda i, k: (i, k)), pl.BlockSpec((1, tk), lambda i, k: (0, k)),
                  pl.BlockSpec((tk, NSA_KV_W), lambda i, k: (k, 0)), _resident(w2.shape)],
        out_specs=pl.BlockSpec((tr, NSA_KV_W), lambda i, k: (i, 0)),
        out_shape=jax.ShapeDtypeStruct((r, NSA_KV_W), F32),
        scratch_shapes=[pltpu.VMEM((tr, NSA_KV_W), F32)],
        compiler_params=_params(2),
        name="nsa_compress",
    )(x, pe, w1, w2)


_MAX_PAGES_PER_STEP = 32
_D_PER_STEP = 8


def _compress_paged_body(pt_ref, *refs, n_pages, pps):
    page_refs = refs[:pps]
    pe_ref, w1_ref, w2_ref, o_ref, xs = refs[pps:]
    g = pl.program_id(1)
    for p, pr in enumerate(page_refs):
        r0 = pl.multiple_of((g * pps + p) * NSA_KV_W, NSA_KV_W)
        xs[pl.ds(r0, NSA_KV_W), :] = pr[0]

    @pl.when(g == pl.num_programs(1) - 1)
    def _():
        kvd = NSA_KV_HEADS * NSA_DH
        for c in range(2):
            def step(j, acc):
                parts = []
                for dd in range(_D_PER_STEP):
                    d = j * _D_PER_STEP + dd
                    pe_row = pe_ref[pl.ds(c * NSA_DH + d, 1), :]
                    rows = [xs[pl.ds(c * kvd + k * NSA_DH + d, n_pages, stride=NSA_KV_W), :] + pe_row
                            for k in range(NSA_KV_HEADS)]
                    parts.append(jnp.concatenate(rows, axis=0).astype(BF16))
                return acc + _mm(jnp.concatenate(parts, axis=1), w1_ref[c, j])
            hid = lax.fori_loop(0, NSA_DH // _D_PER_STEP, step, jnp.zeros((NSA_KV_HEADS * n_pages, PAGE_SIZE), F32))
            out = _mm(_gelu(hid).astype(BF16), w2_ref[c])
            o_ref[c * NSA_KV_HEADS * n_pages:(c + 1) * NSA_KV_HEADS * n_pages, :] = out


def _compress_paged(pool_t, page_table, pe_t, w1_t, w2_t):
    b, n_pages = page_table.shape
    pps = next(n for n in range(min(_MAX_PAGES_PER_STEP, n_pages), 0, -1) if n_pages % n == 0)
    assert _BLOCKS_PER_PAGE == 2
    rows_out = 2 * NSA_KV_HEADS * n_pages

    def page_spec(p):
        return pl.BlockSpec((1, NSA_KV_W, PAGE_SIZE), lambda i, g, pt: (pt[i, g * pps + p], 0, 0))

    def fixed(a):
        nd = a.ndim
        return pl.BlockSpec(a.shape, lambda i, g, pt: (0,) * nd, pipeline_mode=pl.Buffered(1))

    grid_spec = pltpu.PrefetchScalarGridSpec(
        num_scalar_prefetch=1,
        grid=(b, n_pages // pps),
        in_specs=[page_spec(p) for p in range(pps)] + [fixed(pe_t), fixed(w1_t), fixed(w2_t)],
        out_specs=pl.BlockSpec((rows_out, PAGE_SIZE), lambda i, g, pt: (i, 0)),
        scratch_shapes=[pltpu.VMEM((n_pages * NSA_KV_W, PAGE_SIZE), F32)],
    )
    return pl.pallas_call(
        functools.partial(_compress_paged_body, n_pages=n_pages, pps=pps),
        grid_spec=grid_spec,
        out_shape=jax.ShapeDtypeStruct((b * rows_out, PAGE_SIZE), F32),
        compiler_params=_params(2, VMEM_BIG),
        name="nsa_compress_paged",
    )(page_table, *([pool_t] * pps), pe_t, w1_t, w2_t)


def _gate_rows(zs_ref, gb_ref):
    return _sigmoid(zs_ref[0] + gb_ref[...])


def _gate_row(gates, head, branch):
    r = GATE_OFF + head * NSA_BRANCHES + branch
    return gates[r:r + 1, :]


def _cmp_sel_body(q_ref, kvc_ref, zs_ref, gb_ref, ocmp_ref, nm_ref, sc_scr):
    qi = pl.program_id(1)
    tq = q_ref.shape[2]
    nbp = kvc_ref.shape[0]
    t0 = qi * tq
    tpos = t0 + _iota((nbp, tq), 1)
    blk = _iota((nbp, tq), 0)
    cur = jnp.right_shift(tpos, NSA_BLOCK.bit_length() - 1)
    vis = blk * NSA_BLOCK + (NSA_BLOCK - 1) <= tpos
    forced = (blk == 0) | (blk == cur) | (blk == cur - 1)
    allowed = blk <= cur
    gates = _gate_rows(zs_ref, gb_ref)
    for kh in range(NSA_KV_HEADS):
        kc = kvc_ref[:, kh * NSA_DH:(kh + 1) * NSA_DH].astype(BF16)
        vc = kvc_ref[:, NSA_KV_HEADS * NSA_DH + kh * NSA_DH:NSA_KV_HEADS * NSA_DH + (kh + 1) * NSA_DH].astype(BF16)
        probs = [None] * NSA_GROUP

        def head(gi):
            h = kh * NSA_GROUP + gi
            q = (q_ref[0, h * NSA_DH:(h + 1) * NSA_DH, :] * (NSA_DH ** -0.5)).astype(BF16)
            s = _mm(kc, q)
            yield
            s = jnp.where(vis, s, -jnp.inf)
            m = jnp.max(s, axis=0, keepdims=True)
            m = jnp.where(m == -jnp.inf, 0.0, m)
            e = jnp.exp(s - m)
            p = e / jnp.maximum(jnp.sum(e, axis=0, keepdims=True), TINY)
            probs[gi] = p
            o = _mm(vc, p.astype(BF16), _TN)
            yield
            ocmp_ref[0, h * NSA_DH:(h + 1) * NSA_DH, :] = o * _gate_row(gates, h, 0)

        _round_robin([head(gi) for gi in range(NSA_GROUP)])
        imp = probs[0]
        for p in probs[1:]:
            imp = imp + p
        score = jnp.where(forced, jnp.inf, jnp.where(allowed, imp, -jnp.inf))
        for lc in range(tq // LANE):
            ls = slice(lc * LANE, (lc + 1) * LANE)
            n_live = (t0 + (lc + 1) * LANE - 1) // NSA_BLOCK + 1

            def rank(rows):
                sc_scr[0:rows, :] = score[0:rows, ls]
                blk_rows = _iota((rows, LANE), 0)

                def rank_step(i, cnt):
                    for j in (2 * i, 2 * i + 1):
                        row = sc_scr[pl.ds(j, 1), :]
                        sc = sc_scr[0:rows, :]
                        tie = jnp.where(j < blk_rows, 1.0, 0.0)
                        cnt = cnt + jnp.where(row > sc, 1.0, jnp.where(row == sc, tie, 0.0))
                    return cnt

                cnt = lax.fori_loop(0, n_live // 2, rank_step, jnp.zeros((rows, LANE), F32))
                nm_ref[0, kh * nbp:kh * nbp + rows, ls] = jnp.where(
                    allowed[0:rows, ls], jnp.where(cnt < NSA_TOP_N, 0.0, -MASK_BIG), -MASK_BIG).astype(BF16)

            half = nbp // 2

            @pl.when(n_live <= half)
            def _():
                rank(half)
                nm_ref[0, kh * nbp + half:(kh + 1) * nbp, ls] = jnp.full((nbp - half, LANE), -MASK_BIG, BF16)

            @pl.when(n_live > half)
            def _():
                rank(nbp)


def _cmp_sel(q_t, kvc, zs_t, gb_col, nbp, tq):
    bsz, hd, t = q_t.shape
    tq = min(tq, t)
    nt = t // tq
    tok = lambda n: pl.BlockSpec((1, n, tq), lambda b, i: (b, 0, i))
    return pl.pallas_call(
        _cmp_sel_body,
        grid=(bsz, nt),
        in_specs=[tok(hd), pl.BlockSpec((nbp, NSA_KV_W), lambda b, i: (b, 0)), tok(LANE), _resident((LANE, 1))],
        out_specs=[tok(hd), tok(NSA_KV_HEADS * nbp)],
        out_shape=[jax.ShapeDtypeStruct((bsz, hd, t), F32), jax.ShapeDtypeStruct((bsz, NSA_KV_HEADS * nbp, t), BF16)],
        scratch_shapes=[pltpu.VMEM((nbp, LANE), F32)],
        compiler_params=_params(2),
        name="nsa_cmp_select",
    )(q_t, kvc, zs_t, gb_col)


def _sel_attn_body(q_ref, nm_ref, kv_ref, oh_ref, v_ref, zs_ref, gb_ref, o_ref, *, tk):
    qi = pl.program_id(1)
    tq = q_ref.shape[2]
    nbp = nm_ref.shape[1] // NSA_KV_HEADS
    cols = NSA_GROUP * tq
    t0 = qi * tq
    n_before = t0 // tk
    gates = _gate_rows(zs_ref, gb_ref)
    qas = []
    for kh in range(NSA_KV_HEADS):
        nm = nm_ref[0, kh * nbp:(kh + 1) * nbp, :]
        qas.append(jnp.concatenate(
            [jnp.concatenate([nm, (q_ref[0, h * NSA_DH:(h + 1) * NSA_DH, :] * (NSA_DH ** -0.5 * LOG2E)).astype(BF16)], axis=0)
             for h in range(kh * NSA_GROUP, (kh + 1) * NSA_GROUP)], axis=1))

    def tile(kh, kt, carry, causal, out):
        m_old, acc = carry
        k0 = pl.multiple_of(kt * tk, tk)
        ka = jnp.concatenate([oh_ref[pl.ds(k0, tk), :],
                              kv_ref[pl.ds(k0, tk), kh * NSA_DH:(kh + 1) * NSA_DH].astype(BF16)], axis=1)
        s = _mm(ka, qas[kh])
        yield
        if causal:
            kpos = k0 + _iota((tk, cols), 0)
            tpos = t0 + jnp.bitwise_and(_iota((tk, cols), 1), tq - 1)
            s = jnp.where(kpos <= tpos, s, -jnp.inf)
        m_new = jnp.maximum(m_old, jnp.max(s, axis=0, keepdims=True))
        alpha = jnp.exp2(m_old - m_new)
        p = jnp.exp2(s - m_new)
        yield
        out[kh] = (m_new, alpha * acc + _mm(v_ref[0, kh, :, pl.ds(k0, tk)], p.astype(BF16)))

    def tiles(kt, carries, causal):
        out = [None] * NSA_KV_HEADS
        _round_robin([tile(kh, kt, carries[kh], causal, out) for kh in range(NSA_KV_HEADS)])
        return tuple(out)

    init = (jnp.full((1, cols), -jnp.inf, F32), jnp.zeros((V_AUG, cols), F32))
    carries = lax.fori_loop(0, n_before, lambda kt, c: tiles(kt, c, False), (init,) * NSA_KV_HEADS)
    carries = tiles(n_before, carries, True)
    for kh in range(NSA_KV_HEADS):
        acc = carries[kh][1]
        o = acc[:NSA_DH] / acc[NSA_DH:NSA_DH + 1]
        for gi in range(NSA_GROUP):
            h = kh * NSA_GROUP + gi
            o_ref[0, h * NSA_DH:(h + 1) * NSA_DH, :] = o[:, gi * tq:(gi + 1) * tq] * _gate_row(gates, h, 1)


def _sel_attn(q_t, nm_t, kv, onehot, vs_t, zs_t, gb_col, tq, tk):
    bsz, hd, t = q_t.shape
    tq, tk = min(tq, t), min(tk, t)
    assert t % tq == 0 and t % tk == 0 and tk % tq == 0 and tq & (tq - 1) == 0
    nt = t // tq
    tok = lambda n: pl.BlockSpec((1, n, tq), lambda b, i: (b, 0, i))
    return pl.pallas_call(
        functools.partial(_sel_attn_body, tk=tk),
        grid=(bsz, nt),
        in_specs=[tok(hd), tok(nm_t.shape[1]), pl.BlockSpec((t, NSA_KV_W), lambda b, i: (b, 0)), _resident(onehot.shape),
                  pl.BlockSpec((1,) + vs_t.shape[1:], lambda b, i: (b, 0, 0, 0)), tok(LANE), _resident((LANE, 1))],
        out_specs=tok(hd),
        out_shape=jax.ShapeDtypeStruct((bsz, hd, t), F32),
        compiler_params=_params(2, VMEM_BIG),
        name="nsa_selected",
    )(q_t, nm_t, kv, onehot, vs_t, zs_t, gb_col)


def _win_attn_body(q_ref, k_ref, v_ref, zs_ref, gb_ref, o_ref, bias_scr, *, span):
    qi = pl.program_id(1)
    tq = q_ref.shape[2]
    cols = NSA_GROUP * tq
    t0 = qi * tq
    start = pl.multiple_of(jnp.maximum(t0 + tq - span, 0), tq)
    gates = _gate_rows(zs_ref, gb_ref)

    @pl.when(t0 + tq - span <= 0)
    def _():
        kpos = start + _iota((span, cols), 0)
        tpos = t0 + jnp.bitwise_and(_iota((span, cols), 1), tq - 1)
        bias_scr[...] = jnp.where((kpos <= tpos) & (tpos - kpos < NSA_WINDOW), 0.0, -jnp.inf)

    for kh in range(NSA_KV_HEADS):
        qa = jnp.concatenate([(q_ref[0, h * NSA_DH:(h + 1) * NSA_DH, :] * (NSA_DH ** -0.5 * LOG2E)).astype(BF16)
                              for h in range(kh * NSA_GROUP, (kh + 1) * NSA_GROUP)], axis=1)
        kw = k_ref[pl.ds(start, span), kh * NSA_DH:(kh + 1) * NSA_DH].astype(BF16)
        s = jnp.where(bias_scr[...] == 0.0, _mm(kw, qa), -jnp.inf)
        e = jnp.exp2(s - jnp.max(s, axis=0, keepdims=True))
        oa = _mm(v_ref[0, kh, :, pl.ds(start, span)], e.astype(BF16))
        o = oa[:NSA_DH] / oa[NSA_DH:NSA_DH + 1]
        for gi in range(NSA_GROUP):
            h = kh * NSA_GROUP + gi
            o_ref[0, h * NSA_DH:(h + 1) * NSA_DH, :] = o[:, gi * tq:(gi + 1) * tq] * _gate_row(gates, h, 2)


def _win_attn(q_t, kv, vw_t, zs_t, gb_col, tq):
    bsz, hd, t = q_t.shape
    tq = min(tq, t)
    span = min(NSA_WINDOW + tq, t)
    assert t % tq == 0 and tq & (tq - 1) == 0
    nt = t // tq
    tok = lambda n: pl.BlockSpec((1, n, tq), lambda b, i: (b, 0, i))
    whole = lambda a: pl.BlockSpec((1,) + a.shape[1:], lambda b, i: (b, 0, 0, 0))
    return pl.pallas_call(
        functools.partial(_win_attn_body, span=span),
        grid=(bsz, nt),
        in_specs=[tok(hd), pl.BlockSpec((t, NSA_KV_W), lambda b, i: (b, 0)), whole(vw_t), tok(LANE), _resident((LANE, 1))],
        out_specs=tok(hd),
        out_shape=jax.ShapeDtypeStruct((bsz, hd, t), F32),
        scratch_shapes=[pltpu.VMEM((span, NSA_GROUP * tq), F32)],
        compiler_params=_params(2),
        name="nsa_window",
    )(q_t, kv, vw_t, zs_t, gb_col)


def _decode_forced(n_past_blk):
    cur = n_past_blk
    return sorted({0, cur - 1, cur} - {-1})


def _cmp_decode_body(q_ref, kvc_ref, gz_ref, gb_ref, o_ref, idx_ref, *, past, n_pick):
    n_pages = kvc_ref.shape[0] // (2 * NSA_KV_HEADS)
    nb = _BLOCKS_PER_PAGE * n_pages

    def block_id(shape):
        pos = _iota(shape, 1)
        page = jnp.where(pos >= n_pages, pos - n_pages, pos)
        return page * _BLOCKS_PER_PAGE + jnp.where(pos >= n_pages, 1, 0)

    lane = block_id((NSA_HEADS, nb))
    hrow = _iota((NSA_HEADS, nb), 0)
    vis = lane * NSA_BLOCK + (NSA_BLOCK - 1) <= past
    q = (q_ref[0] * (NSA_DH ** -0.5)).astype(BF16)
    gates = _sigmoid(gz_ref[0] + gb_ref[...])
    cur = past // NSA_BLOCK
    o_all = jnp.zeros((NSA_HEADS, NSA_DH), F32)
    idx_all = jnp.zeros((SUBLANE, LANE), F32)
    orow = _iota((NSA_HEADS, NSA_DH), 0)
    slot_r, slot_c = _iota((SUBLANE, LANE), 0), _iota((SUBLANE, LANE), 1)
    l1 = block_id((1, nb))
    l1f = l1.astype(F32)
    forced = (l1 == 0) | (l1 == cur) | (l1 == cur - 1)
    for kh in range(NSA_KV_HEADS):
        kc = kvc_ref[kh * n_pages:(kh + 1) * n_pages, :].astype(BF16)
        vc = kvc_ref[(NSA_KV_HEADS + kh) * n_pages:(NSA_KV_HEADS + kh + 1) * n_pages, :].astype(BF16)
        s = jnp.concatenate([_mm(q, kc[:, j * NSA_DH:(j + 1) * NSA_DH], _NT) for j in range(_BLOCKS_PER_PAGE)], axis=1)
        s = jnp.where(vis, s, -jnp.inf)
        m = jnp.max(s, axis=1, keepdims=True)
        m = jnp.where(m == -jnp.inf, 0.0, m)
        e = jnp.exp(s - m)
        p = e / jnp.maximum(jnp.sum(e, axis=1, keepdims=True), TINY)
        mine = (hrow >= kh * NSA_GROUP) & (hrow < (kh + 1) * NSA_GROUP)
        pb = p.astype(BF16)
        o_kh = _mm(pb[:, :n_pages], vc[:, :NSA_DH])
        for j in range(1, _BLOCKS_PER_PAGE):
            o_kh = o_kh + _mm(pb[:, j * n_pages:(j + 1) * n_pages], vc[:, j * NSA_DH:(j + 1) * NSA_DH])
        o_all = jnp.where((orow >= kh * NSA_GROUP) & (orow < (kh + 1) * NSA_GROUP), o_kh, o_all)
        imp = jnp.sum(jnp.where(mine, p, 0.0), axis=0, keepdims=True)
        score = jnp.where(forced, -jnp.inf, imp)
        for r in range(n_pick):
            mx = jnp.max(score, axis=1, keepdims=True)
            pick = jnp.min(jnp.where(score == mx, l1f, float(nb)), axis=1, keepdims=True)
            score = jnp.where(l1f == pick, -jnp.inf, score)
            idx_all = jnp.where((slot_r == kh) & (slot_c == r), pick, idx_all)
    o_ref[0] = o_all * gates[:, 0:1]
    idx_ref[0] = idx_all.astype(jnp.int32)


def _cmp_decode(qh, kvc, gz, gb3, past, n_pick):
    b = qh.shape[0]
    rows = kvc.shape[0] // b
    return pl.pallas_call(
        functools.partial(_cmp_decode_body, past=past, n_pick=n_pick),
        grid=(b,),
        in_specs=[pl.BlockSpec((1, NSA_HEADS, NSA_DH), lambda i: (i, 0, 0)), pl.BlockSpec((rows, kvc.shape[1]), lambda i: (i, 0)),
                  pl.BlockSpec((1, NSA_HEADS, NSA_BRANCHES), lambda i: (i, 0, 0)), _resident((NSA_HEADS, NSA_BRANCHES))],
        out_specs=[pl.BlockSpec((1, NSA_HEADS, NSA_DH), lambda i: (i, 0, 0)), pl.BlockSpec((1, SUBLANE, LANE), lambda i: (i, 0, 0))],
        out_shape=[jax.ShapeDtypeStruct((b, NSA_HEADS, NSA_DH), F32), jax.ShapeDtypeStruct((b, SUBLANE, LANE), jnp.int32)],
        compiler_params=_params(1),
        name="nsa_cmp_decode",
    )(qh, kvc, gz, gb3)


def _selwin_decode_body(info_ref, q_ref, ns_ref, nw_ref, win_ref, *refs, n_shared, n_own, first_win_row):
    n_blk = n_shared + NSA_KV_HEADS * n_own
    page_refs = refs[:n_blk]
    gz_ref, gb_ref, o_ref, wout_ref = refs[n_blk:]
    i = pl.program_id(0)
    q = q_ref[0] * (NSA_DH ** -0.5)
    qb = q.astype(BF16)
    gates = _sigmoid(gz_ref[0] + gb_ref[...])
    hrow = _iota((NSA_HEADS, NSA_DH), 0)
    n_win = win_ref.shape[2]
    wlane = _iota((NSA_HEADS, n_win), 1)
    pblk = jnp.right_shift(_iota((NSA_HEADS, PAGE_SIZE), 1), NSA_BLOCK.bit_length() - 1)
    o_all = jnp.zeros((NSA_HEADS, NSA_DH), F32)
    voff = NSA_KV_HEADS * NSA_DH
    for kh in range(NSA_KV_HEADS):
        ksl = slice(kh * NSA_DH, (kh + 1) * NSA_DH)
        vsl = slice(voff + kh * NSA_DH, voff + (kh + 1) * NSA_DH)
        slots = list(range(n_shared)) + list(range(n_shared + kh * n_own, n_shared + (kh + 1) * n_own))
        s_parts = []
        for sl in slots:
            blk_in_page = jnp.bitwise_and(info_ref[i, sl], _BLOCKS_PER_PAGE - 1)
            sp = _mm(qb, page_refs[sl][0, ksl, :].astype(BF16))
            s_parts.append(jnp.where(pblk == blk_in_page, sp, -jnp.inf))
        kn, vn = ns_ref[0, :, ksl], ns_ref[0, :, vsl]
        sn = jnp.sum(q * kn, axis=1, keepdims=True)
        m = sn
        for sp in s_parts:
            m = jnp.maximum(m, jnp.max(sp, axis=1, keepdims=True))
        en = jnp.exp(sn - m)
        l, acc = en, en * vn
        for sl, sp in zip(slots, s_parts):
            e = jnp.exp(sp - m)
            l = l + jnp.sum(e, axis=1, keepdims=True)
            acc = acc + _mm(e.astype(BF16), page_refs[sl][0, vsl, :].astype(BF16), _NT)
        o_sel = acc / l
        sw = jnp.where(wlane >= first_win_row, _mm(qb, win_ref[0, ksl, :].astype(BF16)), -jnp.inf)
        kwn, vwn = nw_ref[0, :, ksl], nw_ref[0, :, vsl]
        swn = jnp.sum(q * kwn, axis=1, keepdims=True)
        mw = jnp.maximum(jnp.max(sw, axis=1, keepdims=True), swn)
        ew, ewn = jnp.exp(sw - mw), jnp.exp(swn - mw)
        o_win = (_mm(ew.astype(BF16), win_ref[0, vsl, :].astype(BF16), _NT) + ewn * vwn) / (jnp.sum(ew, axis=1, keepdims=True) + ewn)
        keep = (hrow >= kh * NSA_GROUP) & (hrow < (kh + 1) * NSA_GROUP)
        o_all = jnp.where(keep, gates[:, 1:2] * o_sel + gates[:, 2:3] * o_win, o_all)
    o_ref[0] = o_all
    r2, c2 = _iota((NSA_KV_W, NSA_KV_W), 0), _iota((NSA_KV_W, NSA_KV_W), 1)
    new_col = jnp.sum(jnp.where(r2 == c2, jnp.broadcast_to(nw_ref[0], (NSA_KV_W, NSA_KV_W)), 0.0), axis=1, keepdims=True)
    shifted = pltpu.roll(win_ref[0], n_win - 1, 1)
    wout_ref[0] = jnp.where(_iota((NSA_KV_W, n_win), 1) == n_win - 1, new_col, shifted)


def _selwin_decode(info, qh, new_s, new_w, win_t, pool_t, gz, gb3, n_shared, n_own, first_win_row):
    b = qh.shape[0]
    n_blk = info.shape[1]
    row3 = lambda n: pl.BlockSpec((1, 1, n), lambda i, r: (i, 0, 0))

    def page_spec(s):
        return pl.BlockSpec((1, NSA_KV_W, PAGE_SIZE), lambda i, r: (jnp.right_shift(r[i, s], _PAGE_SHIFT), 0, 0))

    win_spec = pl.BlockSpec((1,) + win_t.shape[1:], lambda i, r: (i, 0, 0))
    grid_spec = pltpu.PrefetchScalarGridSpec(
        num_scalar_prefetch=1,
        grid=(b,),
        in_specs=[pl.BlockSpec((1, NSA_HEADS, NSA_DH), lambda i, r: (i, 0, 0)), row3(NSA_KV_W), row3(NSA_KV_W), win_spec]
        + [page_spec(s) for s in range(n_blk)]
        + [pl.BlockSpec((1, NSA_HEADS, NSA_BRANCHES), lambda i, r: (i, 0, 0)),
           pl.BlockSpec((NSA_HEADS, NSA_BRANCHES), lambda i, r: (0, 0))],
        out_specs=[pl.BlockSpec((1, NSA_HEADS, NSA_DH), lambda i, r: (i, 0, 0)), win_spec],
    )
    return pl.pallas_call(
        functools.partial(_selwin_decode_body, n_shared=n_shared, n_own=n_own, first_win_row=first_win_row),
        grid_spec=grid_spec,
        out_shape=[jax.ShapeDtypeStruct((b, NSA_HEADS, NSA_DH), F32), jax.ShapeDtypeStruct(win_t.shape, F32)],
        compiler_params=_params(1),
        name="nsa_selwin_decode",
    )(info, qh, new_s.reshape(b, 1, -1), new_w.reshape(b, 1, -1), win_t, *([pool_t] * n_blk), gz, gb3)


def _feature_major(cache):
    n, rows = cache.shape[:2]
    return jnp.transpose(cache, (0, 2, 3, 4, 1)).reshape(n, NSA_KV_W, rows)


def _pad_cols(w, n):
    return jnp.pad(w, ((0, 0), (0, n - w.shape[1])))


def _even_w_in(w):
    sizes = (GLA_HEADS * GLA_DK, GLA_HEADS * GLA_DK, GLA_HEADS * GLA_DV, GLA_HEADS * GLA_DV, GLA_RANK,
             NSA_HEADS * NSA_DH, NSA_BRANCHES * NSA_KV_W, NSA_HEADS * NSA_BRANCHES)
    cuts = [0]
    for s in sizes:
        cuts.append(cuts[-1] + s)
    gq, gk, gv, gg, ga, nq, nkv, ng = (w[:, cuts[i]:cuts[i + 1]] for i in range(len(sizes)))
    small = _pad_cols(jnp.concatenate([ga, ng], axis=1), LANE)
    sample = jnp.concatenate([gq, gk, gv, gg, nq, nkv, small], axis=1).astype(BF16)
    prompt = jnp.concatenate([gq, gk, gv, gg, nkv, small], axis=1).astype(BF16)
    prompt_t = jnp.concatenate([nq, nkv, small], axis=1).T.astype(BF16)
    return sample, prompt, prompt_t


_EVEN_WIDTHS = (GLA_Z_W, NSA_HEADS * NSA_DH, NSA_KV_W, NSA_KV_W, NSA_KV_W, LANE)
_EVEN_WIDTHS_P = (GLA_Z_W, NSA_KV_W, NSA_KV_W, NSA_KV_W, LANE)
_EVEN_WIDTHS_PT = (NSA_HEADS * NSA_DH, NSA_KV_W, NSA_KV_W, NSA_KV_W, LANE)


def _odd_w_in(w):
    main = M_Z_W
    return jnp.concatenate([w[:, :main], _pad_cols(w[:, main:], LANE)], axis=1).astype(BF16)


_ODD_WIDTHS = (M_Z_W, LANE)


def _compress_weights(pe, w1, w2):
    eye_k = jnp.eye(NSA_KV_HEADS, dtype=F32)
    eye_c = jnp.eye(2, dtype=F32)
    n_ck = 2 * NSA_KV_HEADS
    rows = jnp.broadcast_to(w1.transpose(1, 0, 2, 3)[:, :, None], (NSA_BLOCK, 2, NSA_KV_HEADS, NSA_DH, NSA_CMP_HID))
    rows = jnp.tile(rows.reshape(NSA_BLOCK * NSA_KV_W, NSA_CMP_HID), (1, n_ck))
    row_ck = (jnp.arange(NSA_BLOCK * NSA_KV_W) // NSA_DH) % n_ck
    col_ck = jnp.arange(NSA_KV_W) // NSA_CMP_HID
    w1big = jnp.where(row_ck[:, None] == col_ck[None, :], rows, 0.0)
    w2big = jnp.einsum("ced,cx,ky->ckexyd", w2, eye_c, eye_k).reshape(NSA_KV_W, NSA_KV_W)
    pe_flat = jnp.broadcast_to(pe.transpose(1, 0, 2)[:, :, None, :], (NSA_BLOCK, 2, NSA_KV_HEADS, NSA_DH)).reshape(1, -1)
    return pe_flat, w1big.astype(BF16), w2big.astype(BF16)


def _compress_weights_paged(pe, w1, w2):
    eye_b = jnp.eye(_BLOCKS_PER_PAGE, dtype=F32)
    pe_t = jnp.tile(pe.transpose(0, 2, 1), (1, 1, _BLOCKS_PER_PAGE)).reshape(2 * NSA_DH, PAGE_SIZE)
    rows = jnp.broadcast_to(w1.transpose(0, 2, 1, 3)[:, :, None], (2, NSA_DH, _BLOCKS_PER_PAGE, NSA_BLOCK, NSA_CMP_HID))
    rows = jnp.tile(rows.reshape(2, NSA_DH // _D_PER_STEP, _D_PER_STEP * PAGE_SIZE, NSA_CMP_HID), (1, 1, 1, _BLOCKS_PER_PAGE))
    row_blk = (jnp.arange(_D_PER_STEP * PAGE_SIZE) // NSA_BLOCK) % _BLOCKS_PER_PAGE
    col_blk = jnp.arange(_BLOCKS_PER_PAGE * NSA_CMP_HID) // NSA_CMP_HID
    w1_t = jnp.where(row_blk[:, None] == col_blk[None, :], rows, 0.0)
    w2_t = jnp.einsum("ced,hx->chexd", w2, eye_b).reshape(2, _BLOCKS_PER_PAGE * NSA_CMP_HID, _BLOCKS_PER_PAGE * NSA_DH)
    return pe_t, w1_t.astype(BF16), w2_t.astype(BF16)


def _gate_bias_row(gb):
    return jnp.pad(gb, (GATE_OFF, LANE - GATE_OFF - gb.shape[0])).reshape(1, LANE)


def _split_outs(outs, n):
    return outs[:n], outs[n:]


def _values_t(kv_t, rows=NSA_DH):
    b, _, t = kv_t.shape
    v = kv_t.reshape(b, 2, NSA_KV_HEADS, NSA_DH, t)[:, 1].astype(BF16)
    if rows > NSA_DH:
        extra = jnp.zeros((b, NSA_KV_HEADS, rows - NSA_DH, t), BF16).at[:, :, 0].set(1.0)
        v = jnp.concatenate([v, extra], axis=2)
    return v


def _kv_rows(kv_t):
    b, _, t = kv_t.shape
    return jnp.transpose(kv_t.reshape(b, 2, NSA_KV_HEADS, NSA_DH, t), (0, 4, 1, 2, 3))


def _even_layer_prompt(x, bsz, g, prm):
    m = x.shape[0]
    t = m // bsz
    (z, kvc, kvs, kvw, zs), (q_t, kvc_t, kvs_t, kvw_t, zs_t) = _split_outs(
        _norm_proj(x, g, prm["w_in_p"], _EVEN_WIDTHS_P, TM_PROJ, prm["w_in_pt"], _EVEN_WIDTHS_PT, bsz), len(_EVEN_WIDTHS_P))
    o_gla, s_fin = _gla_prompt(z, zs, prm["wa"], prm["ba"], prm["gn"],
                               jnp.zeros((bsz, GLA_HEADS, GLA_DK, GLA_DV), F32), bsz)
    nb = t // NSA_BLOCK
    nbp = -(-nb // LANE) * LANE
    kvcmp = _compress_dense(kvc.reshape(bsz * nb, NSA_BLOCK * NSA_KV_W), prm["pe"], prm["w1"], prm["w2"])
    kvcmp = jnp.pad(kvcmp.reshape(bsz, nb, NSA_KV_W), ((0, 0), (0, nbp - nb), (0, 0))).reshape(bsz * nbp, NSA_KV_W)
    o_cmp, nm = _cmp_sel(q_t, kvcmp, zs_t, prm["gb_col"], nbp, TQ_CMP)
    onehot = (jnp.arange(t)[:, None] // NSA_BLOCK == jnp.arange(nbp)[None, :]).astype(BF16)
    o_sel = _sel_attn(q_t, nm, kvs, onehot, _values_t(kvs_t, V_AUG), zs_t, prm["gb_col"], TQ_ATTN, TK_SEL)
    o_win = _win_attn(q_t, kvw, _values_t(kvw_t, V_AUG), zs_t, prm["gb_col"], TQ_ATTN)
    y = _out_proj(x, [o_gla], (1,), prm["w_out"], TM_OUT, [o_cmp, o_sel, o_win], bsz)
    n_keep = min(NSA_WINDOW, t)
    return y, s_fin, _kv_rows(kvc_t), _kv_rows(kvs_t), _kv_rows(kvw_t[:, :, t - n_keep:])


def _even_layer_sample(x, g, prm, gla_state, cmp_pool, sel_pool, win_buf, page_table):
    b = x.shape[0]
    n_pages = page_table.shape[1]
    past = n_pages * PAGE_SIZE
    n_past_blk = past // NSA_BLOCK
    z, nq, kvc, kvs, kvw, zs = _norm_proj(x, g, prm["w_in"], _EVEN_WIDTHS, b)
    o_gla, s_new = _gla_decode(z, zs, prm["wa"], prm["ba"], prm["gn"], gla_state)
    kvcmp = _compress_paged(_feature_major(cmp_pool), page_table, prm["pe_t"], prm["w1_t"], prm["w2_t"])
    qh = nq.reshape(b, NSA_HEADS, NSA_DH)
    gz = zs[:, GATE_OFF:GATE_OFF + NSA_HEADS * NSA_BRANCHES].reshape(b, NSA_HEADS, NSA_BRANCHES)
    forced = _decode_forced(n_past_blk)
    n_pick = NSA_TOP_N - len(forced)
    assert n_past_blk - len(forced) + 1 >= n_pick
    o_cmp, idx = _cmp_decode(qh, kvcmp, gz, prm["gb3"], past, n_pick)
    shared = [f for f in forced if f < n_past_blk]
    logical = jnp.concatenate([jnp.broadcast_to(jnp.asarray(shared, jnp.int32), (b, len(shared))),
                               idx[:, :NSA_KV_HEADS, :n_pick].reshape(b, NSA_KV_HEADS * n_pick)], axis=1)
    per_page = _BLOCKS_PER_PAGE
    phys = jnp.take_along_axis(page_table, logical // per_page, axis=1) * per_page + logical % per_page
    win_keep = win_buf.shape[1]
    assert win_keep >= 1
    first_win_row = max(win_keep - NSA_WINDOW + 1, 0)
    o_sw, win_new_t = _selwin_decode(phys.astype(jnp.int32), qh, kvs, kvw, _feature_major(win_buf), _feature_major(sel_pool),
                                     gz, prm["gb3"], len(shared), n_pick, first_win_row)
    hd = NSA_HEADS * NSA_DH
    y = _out_proj(x, [o_gla, o_cmp.reshape(b, hd), o_sw.reshape(b, hd)], (1, 2), prm["w_out"], b)
    kv_shape = (b, 1, 2, NSA_KV_HEADS, NSA_DH)
    win_new = jnp.transpose(win_new_t.reshape(b, 2, NSA_KV_HEADS, NSA_DH, win_keep), (0, 4, 1, 2, 3))
    return y, s_new, kvc.reshape(kv_shape), kvs.reshape(kv_shape), win_new


def _odd_layer_prompt(x, bsz, g, prm):
    z, zs = _norm_proj(x, g, prm["w_in"], _ODD_WIDTHS, TM_PROJ)
    h, c_aug, m_fin = _mlstm_prompt(z, zs, prm["bias"], prm["mn"], jnp.zeros((bsz, M_HEADS, M_DQK, M_AUG), F32),
                                    jnp.zeros((bsz, 1, LANE), F32), bsz)
    y = _out_proj(x, [h], (1,), prm["w_out"], TM_OUT)
    return y, c_aug[..., :M_DV], c_aug[..., M_DV], m_fin[:, 0, :M_HEADS]


def _odd_layer_sample(x, g, prm, c0, n0, m0):
    b = x.shape[0]
    z, zs = _norm_proj(x, g, prm["w_in"], _ODD_WIDTHS, b)
    h, cn, nn, mn = _mlstm_decode(z, zs, prm["bias"], prm["mn"], c0, n0, m0)
    return _out_proj(x, [h], (1,), prm["w_out"], b), cn, nn, mn


def kernel(x_prompt, x_sample, cache_cmp_kv, cache_sel_kv, cache_win_kv, state_gla, state_mlstm_c, state_mlstm_n, state_mlstm_m, state_ffn_conv, page_table, norm_mix, norm_ffn, norm_final, even_w_in, even_w_out, gla_w_a2, gla_b_a, gla_norm, nsa_cmp_pe, nsa_cmp_w1, nsa_cmp_w2, nsa_gate_b, odd_w_in, odd_w_out, mlstm_b_i, mlstm_b_f, mlstm_norm, ffn_w_up, ffn_conv_w, ffn_conv_b, ffn_w_down):
    bp, t, d = x_prompt.shape
    bs = x_sample.shape[0]
    assert x_sample.shape[1] == 1
    depth = norm_mix.shape[0]
    f = ffn_conv_w.shape[2]
    xp = x_prompt.reshape(bp * t, d)
    xs = x_sample.reshape(bs, d)
    outs = {k: [] for k in ("cmp_p", "cmp_s", "sel_p", "sel_s", "win_p", "win_s", "gla_p", "gla_s",
                            "mc_p", "mc_s", "mn_p", "mn_s", "mm_p", "mm_s", "cv_p", "cv_s")}
    for l in range(depth):
        if l % 2 == 0:
            e = l // 2
            pe, w1, w2 = _compress_weights(nsa_cmp_pe[e], nsa_cmp_w1[e], nsa_cmp_w2[e])
            pe_t, w1_t, w2_t = _compress_weights_paged(nsa_cmp_pe[e], nsa_cmp_w1[e], nsa_cmp_w2[e])
            w_in_s, w_in_p, w_in_pt = _even_w_in(even_w_in[e])
            prm = dict(pe_t=pe_t, w1_t=w1_t, w2_t=w2_t, w_in=w_in_s, w_in_p=w_in_p, w_in_pt=w_in_pt,
                       w_out=even_w_out[e].astype(BF16),
                       wa=jnp.pad(gla_w_a2[e], ((0, LANE - GLA_RANK), (0, 0))), ba=gla_b_a[e].reshape(1, -1),
                       gn=gla_norm[e].reshape(1, -1), pe=pe, w1=w1, w2=w2, gb_col=_gate_bias_row(nsa_gate_b[e]).reshape(LANE, 1),
                       gb3=nsa_gate_b[e].reshape(NSA_HEADS, NSA_BRANCHES))
            xp, s_, c_, k_, w_ = _even_layer_prompt(xp, bp, norm_mix[l], prm)
            outs["gla_p"].append(s_); outs["cmp_p"].append(c_); outs["sel_p"].append(k_); outs["win_p"].append(w_)
            xs, s_, c_, k_, w_ = _even_layer_sample(xs, norm_mix[l], prm, state_gla[e], cache_cmp_kv[e], cache_sel_kv[e],
                                                    cache_win_kv[e], page_table)
            outs["gla_s"].append(s_); outs["cmp_s"].append(c_); outs["sel_s"].append(k_); outs["win_s"].append(w_)
        else:
            o = l // 2
            bias = jnp.pad(jnp.concatenate([mlstm_b_i[o], mlstm_b_f[o]]), (0, LANE - 2 * M_HEADS)).reshape(1, LANE)
            prm = dict(w_in=_odd_w_in(odd_w_in[o]), w_out=odd_w_out[o].astype(BF16), bias=bias, mn=mlstm_norm[o].reshape(1, -1))
            xp, c_, n_, m_ = _odd_layer_prompt(xp, bp, norm_mix[l], prm)
            outs["mc_p"].append(c_); outs["mn_p"].append(n_); outs["mm_p"].append(m_)
            xs, c_, n_, m_ = _odd_layer_sample(xs, norm_mix[l], prm, state_mlstm_c[o], state_mlstm_n[o], state_mlstm_m[o])
            outs["mc_s"].append(c_); outs["mn_s"].append(n_); outs["mm_s"].append(m_)
        final = l == depth - 1
        wup, wd = ffn_w_up[l].astype(BF16), ffn_w_down[l].astype(BF16)
        xp, cv = _ffn_prompt(xp, bp, norm_ffn[l], wup, ffn_conv_w[l], ffn_conv_b[l], wd,
                             jnp.zeros((bp, CONV_W - 1, f), F32), norm_final, final, TM_FFN)
        outs["cv_p"].append(cv)
        xs, cv = _ffn_decode(xs, norm_ffn[l], wup, ffn_conv_w[l], ffn_conv_b[l], wd, state_ffn_conv[l], norm_final, final)
        outs["cv_s"].append(cv)
    st = lambda k: jnp.stack(outs[k])
    return (xp.reshape(bp, t, d), xs.reshape(bs, 1, d),
            st("cmp_p"), st("cmp_s"), st("sel_p"), st("sel_s"), st("win_p"), st("win_s"), st("gla_p"), st("gla_s"),
            st("mc_p"), st("mc_s"), st("mn_p"), st("mn_s"), st("mm_p"), st("mm_s"), st("cv_p"), st("cv_s"))
```
